```python
import jax, jax.numpy as jnp
from jax import lax
import numpy as np

D_MODEL = 1024
BATCH = 2
SEQ = 8192
DEPTH = 1

GRID_W = 64
NA_HEAD_DIM = 64
NA_WIDTH = D_MODEL // 2
NA_HEADS = NA_WIDTH // NA_HEAD_DIM
NA_KH = 8
NA_KW = 16
GLA_HEADS = 4
GLA_VAL_WIDTH = D_MODEL - NA_WIDTH
GLA_DV = GLA_VAL_WIDTH // GLA_HEADS
GLA_DK = GLA_DV // 2
GLA_KEY_WIDTH = GLA_HEADS * GLA_DK
GLA_GATE_RANK = 16
GLA_GATE_NORM = 16.0
GLA_CHUNK = 64
D_MIX = NA_WIDTH + GLA_VAL_WIDTH
D_FF = 4 * D_MODEL
IN_SPLITS = [NA_WIDTH, NA_WIDTH, NA_WIDTH,
             GLA_KEY_WIDTH, GLA_KEY_WIDTH, GLA_VAL_WIDTH, GLA_VAL_WIDTH,
             GLA_GATE_RANK, GLA_GATE_RANK]
D_IN = sum(IN_SPLITS)
EPS = 1e-6

kernel_name = "hybrid_natten_gla_encoder_block"


def rmsnorm(x, g):
    xf = x.astype(jnp.float32)
    y = xf * lax.rsqrt(jnp.mean(xf * xf, axis=-1, keepdims=True) + EPS)
    return (y * g.astype(jnp.float32)).astype(x.dtype)


def neighbourhood_attention(q, k, v, rpb):
    B, T, H, dh = q.shape
    R = T // GRID_W
    W = GRID_W
    KH = min(NA_KH, R)
    KW = NA_KW
    grid = lambda t: t.reshape(B, R, W, H, dh).transpose(0, 3, 1, 2, 4)
    qg, kg, vg = grid(q), grid(k), grid(v)
    rows = jnp.arange(R)
    row_start = jnp.clip(rows - KH // 2, 0, R - KH)
    row_idx = row_start[:, None] + jnp.arange(KH)[None, :]
    k_rows = kg[:, :, row_idx]
    v_rows = vg[:, :, row_idx]
    scores = jnp.einsum('bhrqd,bhrikd->bhrqik', qg, k_rows).astype(jnp.float32)
    scores = scores * (dh ** -0.5)
    dr_idx = row_idx - rows[:, None] + (NA_KH - 1)
    cols = jnp.arange(W)
    dc = cols[None, :] - cols[:, None]
    dc_idx = jnp.clip(dc, -(KW - 1), KW - 1) + (KW - 1)
    bias = rpb.astype(jnp.float32)[:, dr_idx[:, None, :, None], dc_idx[None, :, None, :]]
    col_start = jnp.clip(cols - KW // 2, 0, W - KW)
    in_win = (cols[None, :] >= col_start[:, None]) & (cols[None, :] < col_start[:, None] + KW)
    scores = jnp.where(in_win[None, None, None, :, None, :], scores + bias[None], -jnp.inf)
    p = jax.nn.softmax(scores.reshape(B, H, R, W, KH * W), axis=-1).reshape(B, H, R, W, KH, W)
    out = jnp.einsum('bhrqik,bhrikd->bhrqd', p.astype(v.dtype), v_rows)
    return out.transpose(0, 2, 3, 1, 4).reshape(B, T, H * dh)


def gla_chunked(q, k, v, log_a, strict):
    B, H, T, dk = q.shape
    dv = v.shape[-1]
    C = GLA_CHUNK
    N = T // C
    q = q.reshape(B, H, N, C, dk)
    k = k.reshape(B, H, N, C, dk)
    v = v.reshape(B, H, N, C, dv)
    b = jnp.cumsum(log_a.reshape(B, H, N, C, dk), axis=-2)
    q_dec = q * jnp.exp(b)
    k_inv = k * jnp.exp(-b)
    mask = jnp.tril(jnp.ones((C, C), dtype=bool), k=-1 if strict else 0)
    A = jnp.where(mask, jnp.einsum('bhnid,bhnjd->bhnij', q_dec, k_inv), 0.0)
    o_intra = jnp.einsum('bhnij,bhnje->bhnie', A, v)
    b_last = b[..., -1:, :]
    contrib = jnp.einsum('bhncd,bhnce->bhnde', k * jnp.exp(b_last - b), v)
    decay = jnp.exp(b_last[..., 0, :])

    def step(S, inp):
        g, c = inp
        return g[..., None] * S + c, S

    S0 = jnp.zeros((B, H, dk, dv), dtype=contrib.dtype)
    _, S_prev = lax.scan(step, S0, (jnp.moveaxis(decay, 2, 0), jnp.moveaxis(contrib, 2, 0)))
    S_prev = jnp.moveaxis(S_prev, 0, 2)
    o = o_intra + jnp.einsum('bhncd,bhnde->bhnce', q_dec, S_prev)
    return o.reshape(B, H, T, dv)


def bidirectional_gla(q, k, v, r, z_f, z_b, gu_f, gb_f, gu_b, gb_b, norm_g):
    B, T, _ = q.shape
    bhtd = lambda t, d: t.reshape(B, T, GLA_HEADS, d).transpose(0, 2, 1, 3)
    qh = bhtd(q, GLA_DK) * (GLA_DK ** -0.5)
    kh = bhtd(k, GLA_DK)
    vh = bhtd(v, GLA_DV)
    log_a_f = jax.nn.log_sigmoid((z_f @ gu_f + gb_f).astype(jnp.float32)) / GLA_GATE_NORM
    log_a_b = jax.nn.log_sigmoid((z_b @ gu_b + gb_b).astype(jnp.float32)) / GLA_GATE_NORM
    la_f = bhtd(log_a_f, GLA_DK)
    la_b = bhtd(log_a_b, GLA_DK)
    fwd = gla_chunked(qh, kh, vh, la_f, strict=False)
    flip = lambda t: jnp.flip(t, axis=2)
    bwd = flip(gla_chunked(flip(qh), flip(kh), flip(vh), flip(la_b), strict=True))
    o = (fwd + bwd).astype(v.dtype).transpose(0, 2, 1, 3)
    o = rmsnorm(o, norm_g) * jax.nn.silu(r.reshape(B, T, GLA_HEADS, GLA_DV))
    return o.reshape(B, T, GLA_VAL_WIDTH)


def hybrid_mixer(h, ln_g, w_in, rpb, gu_f, gb_f, gu_b, gb_b, norm_g, w_out):
    B, T, _ = h.shape
    n = rmsnorm(h, ln_g)
    proj = n @ w_in
    offsets = [int(o) for o in np.cumsum(IN_SPLITS)[:-1]]
    qa, ka, va, qg, kg, vg, rg, zf, zb = jnp.split(proj, offsets, axis=-1)
    na_heads = lambda t: t.reshape(B, T, NA_HEADS, NA_HEAD_DIM)
    y_na = neighbourhood_attention(na_heads(qa), na_heads(ka), na_heads(va), rpb)
    y_gla = bidirectional_gla(qg, kg, vg, rg, zf, zb, gu_f, gb_f, gu_b, gb_b, norm_g)
    return jnp.concatenate([y_na, y_gla], axis=-1) @ w_out


def sqrelu_mlp(h, ln_g, w1, w2):
    u = rmsnorm(h, ln_g) @ w1
    return jnp.square(jax.nn.relu(u)) @ w2


def setup_inputs(seed: int = 0) -> dict:
    key = jax.random.key(seed)
    ks = jax.random.split(key, 16)
    nrm = lambda k, shape, s: jax.random.normal(k, shape, jnp.float32) * s
    L = DEPTH
    return {
        "x": nrm(ks[0], (BATCH, SEQ, D_MODEL), 1.0),
        "ln_mix_g": 1.0 + nrm(ks[1], (L, D_MODEL), 0.02),
        "w_in": nrm(ks[2], (L, D_MODEL, D_IN), D_MODEL ** -0.5),
        "na_rpb": nrm(ks[3], (L, NA_HEADS, 2 * NA_KH - 1, 2 * NA_KW - 1), 0.02),
        "gla_gate_up_fwd": nrm(ks[4], (L, GLA_GATE_RANK, GLA_KEY_WIDTH), GLA_GATE_RANK ** -0.5),
        "gla_gate_bias_fwd": nrm(ks[5], (L, GLA_KEY_WIDTH), 0.1),
        "gla_gate_up_bwd": nrm(ks[6], (L, GLA_GATE_RANK, GLA_KEY_WIDTH), GLA_GATE_RANK ** -0.5),
        "gla_gate_bias_bwd": nrm(ks[7], (L, GLA_KEY_WIDTH), 0.1),
        "gla_norm_g": 1.0 + nrm(ks[8], (L, GLA_DV), 0.02),
        "w_out": nrm(ks[9], (L, D_MIX, D_MODEL), D_MIX ** -0.5),
        "ln_ff_g": 1.0 + nrm(ks[10], (L, D_MODEL), 0.02),
        "w_ff1": nrm(ks[11], (L, D_MODEL, D_FF), D_MODEL ** -0.5),
        "w_ff2": nrm(ks[12], (L, D_FF, D_MODEL), D_FF ** -0.5),
        "ln_final_g": 1.0 + nrm(ks[13], (D_MODEL,), 0.02),
    }


def reference(x, ln_mix_g, w_in, na_rpb, gla_gate_up_fwd, gla_gate_bias_fwd,
              gla_gate_up_bwd, gla_gate_bias_bwd, gla_norm_g, w_out,
              ln_ff_g, w_ff1, w_ff2, ln_final_g):
    h = x
    for l in range(DEPTH):
        h = h + hybrid_mixer(h, ln_mix_g[l], w_in[l], na_rpb[l],
                             gla_gate_up_fwd[l], gla_gate_bias_fwd[l],
                             gla_gate_up_bwd[l], gla_gate_bias_bwd[l],
                             gla_norm_g[l], w_out[l])
        h = h + sqrelu_mlp(h, ln_ff_g[l], w_ff1[l], w_ff2[l])
    return rmsnorm(h, ln_final_g)
```

```python
import functools

import jax
import jax.numpy as jnp
from jax import lax
from jax.experimental import pallas as pl
from jax.experimental.pallas import tpu as pltpu

F32 = jnp.float32
BF16 = jnp.bfloat16

EPS = 1e-6
GRID_W = 64
NA_HEADS = 8
NA_HEAD_DIM = 64
NA_KH = 8
NA_KW = 16
GLA_HEADS = 4
GLA_DK = 64
GLA_DV = 128
GLA_GATE_RANK = 16
GLA_GATE_NORM = 16.0
GLA_CHUNK = 64

LANES = 128
NA_GROUP_ROWS = 4
NA_GROUP = NA_GROUP_ROWS * GRID_W
NA_KEY_ROWS = 12
NA_KEYS = NA_KEY_ROWS * GRID_W
VMEM_LIMIT = 56 * 1024 * 1024


def _dot(a, b):
    return jnp.dot(a, b, preferred_element_type=F32)


def _dot_nt(a, b):
    return lax.dot_general(a, b, (((1,), (1,)), ((), ())), preferred_element_type=F32)


def _dot_tn(a, b):
    return lax.dot_general(a, b, (((0,), (0,)), ((), ())), preferred_element_type=F32)


def _rms(x, g):
    return x * lax.rsqrt(jnp.mean(x * x, axis=-1, keepdims=True) + EPS) * g


def _inproj_kernel(x_ref, g_ref, wm_ref, wkt_ref, wz_ref, gu_ref, gb_ref,
                   qa_ref, kat_ref, va_ref, qg_ref, kg_ref, vg_ref, sr_ref, la_ref):
    n = _rms(x_ref[...], g_ref[...]).astype(BF16)
    qa_ref[...] = (_dot(n, wm_ref[:, 0:512]) * (NA_HEAD_DIM ** -0.5)).astype(BF16)
    kat_ref[...] = _dot_nt(wkt_ref[...], n).astype(BF16)
    va_ref[...] = _dot(n, wm_ref[:, 1024:1536]).astype(BF16)
    qg_ref[...] = (_dot(n, wm_ref[:, 1536:1792]) * (GLA_DK ** -0.5)).astype(BF16)
    kg_ref[...] = _dot(n, wm_ref[:, 1792:2048]).astype(BF16)
    vg_ref[...] = _dot(n, wm_ref[:, 2048:2560]).astype(BF16)
    r = _dot(n, wm_ref[:, 2560:3072])
    sr_ref[...] = (r * jax.nn.sigmoid(r)).astype(BF16)
    z = _dot(n, wz_ref[...]).astype(BF16)
    pre = _dot(z, gu_ref[...]) + gb_ref[...]
    log_sig = jnp.minimum(pre, 0.0) - jnp.log1p(jnp.exp(-jnp.abs(pre)))
    la_ref[...] = log_sig * (1.0 / GLA_GATE_NORM)


def _inproj(x, g, wm, wkt, wz, gu, gb, *, tm):
    B, T, D = x.shape
    tok = lambda w: pl.BlockSpec((None, tm, w), lambda b, i: (b, i, 0))
    const = lambda a: pl.BlockSpec(a.shape, lambda b, i: (0,) * a.ndim,
                                   pipeline_mode=pl.Buffered(1))
    outs = [
        (jax.ShapeDtypeStruct((B, T, 512), BF16), tok(512)),
        (jax.ShapeDtypeStruct((B, 512, T), BF16),
         pl.BlockSpec((None, 512, tm), lambda b, i: (b, 0, i))),
        (jax.ShapeDtypeStruct((B, T, 512), BF16), tok(512)),
        (jax.ShapeDtypeStruct((B, T, 256), BF16), tok(256)),
        (jax.ShapeDtypeStruct((B, T, 256), BF16), tok(256)),
        (jax.ShapeDtypeStruct((B, T, 512), BF16), tok(512)),
        (jax.ShapeDtypeStruct((B, T, 512), BF16), tok(512)),
        (jax.ShapeDtypeStruct((B, T, 512), F32), tok(512)),
    ]
    return pl.pallas_call(
        _inproj_kernel,
        grid=(B, T // tm),
        in_specs=[tok(D), const(g), const(wm), const(wkt), const(wz), const(gu), const(gb)],
        out_specs=[s for _, s in outs],
        out_shape=[s for s, _ in outs],
        compiler_params=pltpu.CompilerParams(
            dimension_semantics=("parallel", "parallel"), vmem_limit_bytes=VMEM_LIMIT),
        name="inproj",
    )(x, g, wm, wkt, wz, gu, gb)


def _na_bias_tables(rpb):
    H = rpb.shape[0]
    g = jnp.arange(NA_GROUP_ROWS)[:, None]
    i = jnp.arange(NA_KEY_ROWS)[None, :]
    zero = jnp.zeros_like(g)
    i_start = jnp.stack([zero, g, zero + (NA_KEY_ROWS - NA_KH)])
    dr = jnp.stack([i - g, i - g - NA_KH // 2, i - g - NA_KH])
    valid_r = (i[None] >= i_start) & (i[None] < i_start + NA_KH)
    dr_idx = jnp.clip(dr + NA_KH - 1, 0, 2 * NA_KH - 2)
    cols = jnp.arange(GRID_W)
    dc_idx = jnp.clip(cols[None, :] - cols[:, None], -(NA_KW - 1), NA_KW - 1) + NA_KW - 1
    col_start = jnp.clip(cols - NA_KW // 2, 0, GRID_W - NA_KW)
    in_win = (cols[None, :] >= col_start[:, None]) & (cols[None, :] < col_start[:, None] + NA_KW)
    bias = rpb.astype(F32)[:, dr_idx[:, :, None, :, None], dc_idx[None, None, :, None, :]]
    mask = valid_r[:, :, None, :, None] & in_win[None, None, :, None, :]
    bias = jnp.where(mask[None], bias, -jnp.inf)
    return bias.reshape(H, 3, NA_GROUP, NA_KEYS)


def _na_kernel(q_ref, kt_ref, v_ref, bias_ref, o_ref, *, groups_per_step, n_groups):
    j = pl.program_id(2)
    lane = lax.broadcasted_iota(jnp.int32, (NA_GROUP, LANES), 1)
    first_head = lane < NA_HEAD_DIM
    for gi in range(groups_per_step):
        u = j * groups_per_step + gi
        key0 = pl.multiple_of(jnp.clip(u - 1, 0, n_groups - 3) * NA_GROUP, NA_GROUP)
        table = jnp.where(u == 0, 0, jnp.where(u == n_groups - 1, 2, 1))
        qq = q_ref[gi * NA_GROUP:(gi + 1) * NA_GROUP, :]
        kt = kt_ref[:, pl.ds(key0, NA_KEYS)]
        vv = v_ref[pl.ds(key0, NA_KEYS), :]
        outs = []
        for hh in range(2):
            qm = jnp.where(first_head if hh == 0 else ~first_head, qq, jnp.zeros_like(qq))
            s = _dot(qm, kt) + bias_ref[hh, table]
            m = jnp.max(s, axis=-1, keepdims=True)
            p = jnp.exp(s - m)
            l = jnp.sum(p, axis=-1, keepdims=True)
            outs.append(_dot(p.astype(BF16), vv) * (1.0 / l))
        o_ref[gi * NA_GROUP:(gi + 1) * NA_GROUP, :] = jnp.where(
            first_head, outs[0], outs[1]).astype(o_ref.dtype)


def _na(qa, kat, va, bias, *, tq):
    B, T, _ = qa.shape
    n_groups = T // NA_GROUP
    n_pairs = NA_HEADS // 2
    kern = functools.partial(_na_kernel, groups_per_step=tq // NA_GROUP, n_groups=n_groups)
    return pl.pallas_call(
        kern,
        grid=(B, n_pairs, T // tq),
        in_specs=[
            pl.BlockSpec((None, tq, LANES), lambda b, p, j: (b, j, p)),
            pl.BlockSpec((None, LANES, T), lambda b, p, j: (b, p, 0)),
            pl.BlockSpec((None, T, LANES), lambda b, p, j: (b, 0, p)),
            pl.BlockSpec((2, 3, NA_GROUP, NA_KEYS), lambda b, p, j: (p, 0, 0, 0)),
        ],
        out_specs=pl.BlockSpec((None, tq, LANES), lambda b, p, j: (b, j, p)),
        out_shape=jax.ShapeDtypeStruct((B, T, NA_HEADS * NA_HEAD_DIM), BF16),
        compiler_params=pltpu.CompilerParams(
            dimension_semantics=("parallel", "parallel", "parallel"),
            vmem_limit_bytes=VMEM_LIMIT),
        name="natten",
    )(qa, kat, va, bias)


def _gla_direction(q_ref, k_ref, v_ref, la_ref, o_ref, st_ref, r0, tri, keep, last_row, first_head):
    C = GLA_CHUNK
    rows = pl.ds(r0, C)
    la = la_ref[rows, :]
    la_hi = la.astype(BF16)
    la_lo = (la - la_hi.astype(F32)).astype(BF16)
    b = _dot(tri, la_hi) + _dot(tri, la_lo)
    b_last = b[last_row:last_row + 1, :]
    q = q_ref[rows, :].astype(F32)
    k = k_ref[rows, :].astype(F32)
    q_dec = (q * jnp.exp(b)).astype(BF16)
    k_inv = (k * jnp.exp(-b)).astype(BF16)
    k_dec = (k * jnp.exp(b_last - b)).astype(BF16)
    decay = jnp.exp(b_last)
    for p in range(GLA_HEADS // 2):
        sl = slice(p * LANES, (p + 1) * LANES)
        qd_p, ki_p, kd_p = q_dec[:, sl], k_inv[:, sl], k_dec[:, sl]
        st = st_ref[p]
        st_bf = st.astype(BF16)
        contrib = []
        for hh in range(2):
            h = 2 * p + hh
            qm = jnp.where(first_head if hh == 0 else ~first_head, qd_p, jnp.zeros_like(qd_p))
            a = jnp.where(keep, _dot_nt(qm, ki_p), 0.0).astype(BF16)
            v_h = v_ref[rows, h * GLA_DV:(h + 1) * GLA_DV]
            o_ref[rows, h * GLA_DV:(h + 1) * GLA_DV] = _dot(a, v_h) + _dot_nt(qm, st_bf)
            contrib.append(_dot_tn(v_h, kd_p))
        lane = lax.broadcasted_iota(jnp.int32, (GLA_DV, LANES), 1)
        st_ref[p] = st * decay[:, sl] + jnp.where(lane < GLA_DK, contrib[0], contrib[1])


def _gla_kernel(qf_ref, kf_ref, vf_ref, laf_ref, qb_ref, kb_ref, vb_ref, lab_ref,
                of_ref, ob_ref, sf_ref, sb_ref, *, n_chunks):
    @pl.when(pl.program_id(1) == 0)
    def _():
        sf_ref[...] = jnp.zeros_like(sf_ref)
        sb_ref[...] = jnp.zeros_like(sb_ref)

    C = GLA_CHUNK
    row = lax.broadcasted_iota(jnp.int32, (C, C), 0)
    col = lax.broadcasted_iota(jnp.int32, (C, C), 1)
    tri_f = jnp.where(col <= row, 1.0, 0.0).astype(BF16)
    tri_b = jnp.where(col >= row, 1.0, 0.0).astype(BF16)
    first_head = lax.broadcasted_iota(jnp.int32, (C, LANES), 1) < GLA_DK

    def chunk(c, carry):
        rf = pl.multiple_of(c * C, C)
        rb = pl.multiple_of((n_chunks - 1 - c) * C, C)
        _gla_direction(qf_ref, kf_ref, vf_ref, laf_ref, of_ref, sf_ref, rf,
                       tri_f, col <= row, C - 1, first_head)
        _gla_direction(qb_ref, kb_ref, vb_ref, lab_ref, ob_ref, sb_ref, rb,
                       tri_b, col > row, 0, first_head)
        return carry

    lax.fori_loop(0, n_chunks, chunk, 0)


def _gla(qg, kg, vg, la, *, tb):
    B, T, _ = qg.shape
    nb = T // tb
    kw = GLA_HEADS * GLA_DK
    vw = GLA_HEADS * GLA_DV
    fwd = lambda w, blk=0: pl.BlockSpec((None, tb, w), lambda b, s: (b, s, blk))
    bwd = lambda w, blk=0: pl.BlockSpec((None, tb, w), lambda b, s: (b, nb - 1 - s, blk))
    state = pltpu.VMEM((GLA_HEADS // 2, GLA_DV, LANES), F32)
    return pl.pallas_call(
        functools.partial(_gla_kernel, n_chunks=tb // GLA_CHUNK),
        grid=(B, nb),
        in_specs=[fwd(kw), fwd(kw), fwd(vw), fwd(kw, 0), bwd(kw), bwd(kw), bwd(vw), bwd(kw, 1)],
        out_specs=[fwd(vw), bwd(vw)],
        out_shape=[jax.ShapeDtypeStruct((B, T, vw), F32)] * 2,
        scratch_shapes=[state, state],
        compiler_params=pltpu.CompilerParams(
            dimension_semantics=("parallel", "arbitrary"), vmem_limit_bytes=VMEM_LIMIT),
        name="gla",
    )(qg, kg, vg, la, qg, kg, vg, la)


def _mix_ffn_kernel(x_ref, yna_ref, of_ref, ob_ref, sr_ref, gng_ref, wout_ref, gff_ref,
                    w1_ref, w2_ref, gfin_ref, out_ref, *, final_norm, ff_chunk):
    o = of_ref[...] + ob_ref[...]
    parts = [_rms(o[:, h * GLA_DV:(h + 1) * GLA_DV], gng_ref[...]) for h in range(GLA_HEADS)]
    y_gla = (jnp.concatenate(parts, axis=-1) * sr_ref[...].astype(F32)).astype(BF16)
    na_w = yna_ref.shape[-1]
    mix = _dot(yna_ref[...], wout_ref[0:na_w, :]) + _dot(y_gla, wout_ref[na_w:, :])
    h1 = x_ref[...] + mix
    n2 = _rms(h1, gff_ref[...]).astype(BF16)
    ffn = None
    for f in range(w1_ref.shape[1] // ff_chunk):
        u = _dot(n2, w1_ref[:, f * ff_chunk:(f + 1) * ff_chunk])
        a = jnp.square(jnp.maximum(u, 0.0)).astype(BF16)
        d = _dot(a, w2_ref[f * ff_chunk:(f + 1) * ff_chunk, :])
        ffn = d if ffn is None else ffn + d
    h2 = h1 + ffn
    out_ref[...] = _rms(h2, gfin_ref[...]) if final_norm else h2


def _mix_ffn(x, yna, of, ob, sr, gng, wout, gff, w1, w2, gfin, *, tm, final_norm):
    B, T, D = x.shape
    tok = lambda w: pl.BlockSpec((None, tm, w), lambda b, i: (b, i, 0))
    const = lambda a: pl.BlockSpec(a.shape, lambda b, i: (0,) * a.ndim,
                                   pipeline_mode=pl.Buffered(1))
    return pl.pallas_call(
        functools.partial(_mix_ffn_kernel, final_norm=final_norm, ff_chunk=1024),
        grid=(B, T // tm),
        in_specs=[tok(D), tok(yna.shape[-1]), tok(of.shape[-1]), tok(ob.shape[-1]),
                  tok(sr.shape[-1]), const(gng), const(wout), const(gff), const(w1),
                  const(w2), const(gfin)],
        out_specs=tok(D),
        out_shape=jax.ShapeDtypeStruct((B, T, D), F32),
        compiler_params=pltpu.CompilerParams(
            dimension_semantics=("parallel", "parallel"), vmem_limit_bytes=VMEM_LIMIT),
        name="mix_ffn",
    )(x, yna, of, ob, sr, gng, wout, gff, w1, w2, gfin)


def kernel(x, ln_mix_g, w_in, na_rpb, gla_gate_up_fwd, gla_gate_bias_fwd, gla_gate_up_bwd,
           gla_gate_bias_bwd, gla_norm_g, w_out, ln_ff_g, w_ff1, w_ff2, ln_final_g):
    B, T, D = x.shape
    depth = w_in.shape[0]
    assert T % NA_GROUP == 0 and T // NA_GROUP >= 3 and T % 512 == 0
    n_main = w_in.shape[-1] - 2 * GLA_GATE_RANK
    kw = GLA_HEADS * GLA_DK
    row = lambda v: v.reshape(1, -1).astype(F32)
    h = x
    for l in range(depth):
        w = w_in[l]
        wm = w[:, :n_main].astype(BF16)
        wkt = w[:, 512:1024].T.astype(BF16)
        wz = w[:, n_main:].astype(BF16)
        zeros = jnp.zeros((GLA_GATE_RANK, kw), F32)
        gu = jnp.concatenate([
            jnp.concatenate([gla_gate_up_fwd[l], zeros], axis=1),
            jnp.concatenate([zeros, gla_gate_up_bwd[l]], axis=1)], axis=0).astype(BF16)
        gb = jnp.concatenate([gla_gate_bias_fwd[l], gla_gate_bias_bwd[l]]).reshape(1, -1)
        qa, kat, va, qg, kg, vg, sr, la = _inproj(
            h, row(ln_mix_g[l]), wm, wkt, wz, gu, gb.astype(F32), tm=512)
        y_na = _na(qa, kat, va, _na_bias_tables(na_rpb[l]), tq=1024)
        o_f, o_b = _gla(qg, kg, vg, la, tb=512)
        h = _mix_ffn(h, y_na, o_f, o_b, sr, row(gla_norm_g[l]), w_out[l].astype(BF16),
                     row(ln_ff_g[l]), w_ff1[l].astype(BF16), w_ff2[l].astype(BF16),
                     row(ln_final_g), tm=512, final_norm=(l == depth - 1))
    return h
```

```python
import functools

import jax
import jax.numpy as jnp
from jax import lax
from jax.experimental import pallas as pl
from jax.experimental.pallas import tpu as pltpu

F32 = jnp.float32
BF16 = jnp.bfloat16

EPS = 1e-6
GRID_W = 64
NA_HEADS = 8
NA_HEAD_DIM = 64
NA_KH = 8
NA_KW = 16
GLA_HEADS = 4
GLA_DK = 64
GLA_DV = 128
GLA_GATE_RANK = 16
GLA_GATE_NORM = 16.0
GLA_CHUNK = 64

LANES = 128
NA_GROUP_ROWS = 4
NA_GROUP = NA_GROUP_ROWS * GRID_W
NA_KEY_ROWS = 12
NA_KEYS = NA_KEY_ROWS * GRID_W
VMEM_LIMIT = 56 * 1024 * 1024


def _dot(a, b):
    return jnp.dot(a, b, preferred_element_type=F32)


def _dot_nt(a, b):
    return lax.dot_general(a, b, (((1,), (1,)), ((), ())), preferred_element_type=F32)


def _dot_tn(a, b):
    return lax.dot_general(a, b, (((0,), (0,)), ((), ())), preferred_element_type=F32)


def _rms(x, g):
    return x * lax.rsqrt(jnp.mean(x * x, axis=-1, keepdims=True) + EPS) * g


def _inproj_kernel(x_ref, g_ref, wm_ref, wkt_ref, wz_ref, gu_ref, gb_ref,
                   qa_ref, kat_ref, va_ref, qg_ref, kg_ref, vg_ref, sr_ref, la_ref):
    n = _rms(x_ref[...], g_ref[...]).astype(BF16)
    qa_ref[...] = (_dot(n, wm_ref[:, 0:512]) * (NA_HEAD_DIM ** -0.5)).astype(BF16)
    kat_ref[...] = _dot_nt(wkt_ref[...], n).astype(BF16)
    va_ref[...] = _dot(n, wm_ref[:, 1024:1536]).astype(BF16)
    qg_ref[...] = (_dot(n, wm_ref[:, 1536:1792]) * (GLA_DK ** -0.5)).astype(BF16)
    kg_ref[...] = _dot(n, wm_ref[:, 1792:2048]).astype(BF16)
    vg_ref[...] = _dot(n, wm_ref[:, 2048:2560]).astype(BF16)
    r = _dot(n, wm_ref[:, 2560:3072])
    sr_ref[...] = (r * jax.nn.sigmoid(r)).astype(BF16)
    z = _dot(n, wz_ref[...]).astype(BF16)
    pre = _dot(z, gu_ref[...]) + gb_ref[...]
    log_sig = jnp.minimum(pre, 0.0) - jnp.log1p(jnp.exp(-jnp.abs(pre)))
    la_ref[...] = log_sig * (1.0 / GLA_GATE_NORM)


def _inproj(x, g, wm, wkt, wz, gu, gb, *, tm):
    B, T, D = x.shape
    tok = lambda w: pl.BlockSpec((None, tm, w), lambda b, i: (b, i, 0))
    const = lambda a: pl.BlockSpec(a.shape, lambda b, i: (0,) * a.ndim,
                                   pipeline_mode=pl.Buffered(1))
    outs = [
        (jax.ShapeDtypeStruct((B, T, 512), BF16), tok(512)),
        (jax.ShapeDtypeStruct((B, 512, T), BF16),
         pl.BlockSpec((None, 512, tm), lambda b, i: (b, 0, i))),
        (jax.ShapeDtypeStruct((B, T, 512), BF16), tok(512)),
        (jax.ShapeDtypeStruct((B, T, 256), BF16), tok(256)),
        (jax.ShapeDtypeStruct((B, T, 256), BF16), tok(256)),
        (jax.ShapeDtypeStruct((B, T, 512), BF16), tok(512)),
        (jax.ShapeDtypeStruct((B, T, 512), BF16), tok(512)),
        (jax.ShapeDtypeStruct((B, T, 512), F32), tok(512)),
    ]
    return pl.pallas_call(
        _inproj_kernel,
        grid=(B, T // tm),
        in_specs=[tok(D), const(g), const(wm), const(wkt), const(wz), const(gu), const(gb)],
        out_specs=[s for _, s in outs],
        out_shape=[s for s, _ in outs],
        compiler_params=pltpu.CompilerParams(
            dimension_semantics=("parallel", "parallel"), vmem_limit_bytes=VMEM_LIMIT),
        name="inproj",
    )(x, g, wm, wkt, wz, gu, gb)


def _na_bias_rows(rpb):
    n_dc = 2 * NA_KW - 1
    padded = jnp.pad(rpb.astype(F32), ((0, 0), (4, 5), (0, GRID_W - n_dc)))
    return jnp.concatenate([padded[:, :-1], padded[:, 1:]], axis=-1)


def _na_build_tables(rows_ref, tab_ref):
    qc = lax.broadcasted_iota(jnp.int32, (GRID_W, LANES), 0)
    lane = lax.broadcasted_iota(jnp.int32, (GRID_W, LANES), 1)
    kc = lane & (GRID_W - 1)
    col_start = jnp.clip(qc - NA_KW // 2, 0, GRID_W - NA_KW)
    in_win = (kc >= col_start) & (kc < col_start + NA_KW)
    valid = {(True, True): in_win,
             (True, False): in_win & (lane < GRID_W),
             (False, True): in_win & (lane >= GRID_W)}
    neg = jnp.full((GRID_W, LANES), -jnp.inf, F32)
    row_offset = (0, -(NA_KH // 2), -NA_KH)
    for hh in range(2):
        for t in range(3):
            for g in range(NA_GROUP_ROWS):
                i_start = (0, g, NA_KEY_ROWS - NA_KH)[t]
                for m in range(NA_KEY_ROWS // 2):
                    halves = tuple(i_start <= i < i_start + NA_KH for i in (2 * m, 2 * m + 1))
                    if any(halves):
                        a = 2 * m - g + row_offset[t] + NA_KH - 1 + 4
                        src = jnp.broadcast_to(rows_ref[hh, a:a + 1, :], (GRID_W, LANES))
                        rot = pltpu.roll(src, LANES - (NA_KW - 1), 1, stride=1, stride_axis=0)
                        block = jnp.where(valid[halves], rot, neg)
                    else:
                        block = neg
                    tab_ref[hh, t, g * GRID_W:(g + 1) * GRID_W, m * LANES:(m + 1) * LANES] = block


def _na_kernel(q_ref, kt_ref, v_ref, rows_ref, o_ref, bias_ref, *, groups_per_step, n_groups):
    j = pl.program_id(2)

    @pl.when(j == 0)
    def _():
        _na_build_tables(rows_ref, bias_ref)

    lane = lax.broadcasted_iota(jnp.int32, (NA_GROUP, LANES), 1)
    first_head = lane < NA_HEAD_DIM
    for gi in range(groups_per_step):
        u = j * groups_per_step + gi
        key0 = pl.multiple_of(jnp.clip(u - 1, 0, n_groups - 3) * NA_GROUP, NA_GROUP)
        table = jnp.where(u == 0, 0, jnp.where(u == n_groups - 1, 2, 1))
        qq = q_ref[gi * NA_GROUP:(gi + 1) * NA_GROUP, :]
        kt = kt_ref[:, pl.ds(key0, NA_KEYS)]
        vv = v_ref[pl.ds(key0, NA_KEYS), :]
        outs = []
        for hh in range(2):
            qm = jnp.where(first_head if hh == 0 else ~first_head, qq, jnp.zeros_like(qq))
            s = _dot(qm, kt) + bias_ref[hh, table]
            m = jnp.max(s, axis=-1, keepdims=True)
            p = jnp.exp(s - m)
            l = jnp.sum(p, axis=-1, keepdims=True)
            outs.append(_dot(p.astype(BF16), vv) * (1.0 / l))
        o_ref[gi * NA_GROUP:(gi + 1) * NA_GROUP, :] = jnp.where(
            first_head, outs[0], outs[1]).astype(o_ref.dtype)


def _na(qa, kat, va, rows, *, tq):
    B, T, _ = qa.shape
    n_groups = T // NA_GROUP
    n_pairs = NA_HEADS // 2
    kern = functools.partial(_na_kernel, groups_per_step=tq // NA_GROUP, n_groups=n_groups)
    return pl.pallas_call(
        kern,
        grid=(B, n_pairs, T // tq),
        in_specs=[
            pl.BlockSpec((None, tq, LANES), lambda b, p, j: (b, j, p)),
            pl.BlockSpec((None, LANES, T), lambda b, p, j: (b, p, 0)),
            pl.BlockSpec((None, T, LANES), lambda b, p, j: (b, 0, p)),
            pl.BlockSpec((2,) + rows.shape[1:], lambda b, p, j: (p, 0, 0)),
        ],
        out_specs=pl.BlockSpec((None, tq, LANES), lambda b, p, j: (b, j, p)),
        out_shape=jax.ShapeDtypeStruct((B, T, NA_HEADS * NA_HEAD_DIM), BF16),
        scratch_shapes=[pltpu.VMEM((2, 3, NA_GROUP, NA_KEYS), F32)],
        compiler_params=pltpu.CompilerParams(
            dimension_semantics=("parallel", "parallel", "arbitrary"),
            vmem_limit_bytes=VMEM_LIMIT),
        name="natten",
    )(qa, kat, va, rows)


def _gla_direction(q_ref, k_ref, v_ref, la_ref, o_ref, st_ref, r0, tri, keep, last_row, first_head):
    C = GLA_CHUNK
    rows = pl.ds(r0, C)
    la = la_ref[rows, :]
    la_hi = la.astype(BF16)
    la_lo = (la - la_hi.astype(F32)).astype(BF16)
    b = _dot(tri, la_hi) + _dot(tri, la_lo)
    b_last = b[last_row:last_row + 1, :]
    q = q_ref[rows, :].astype(F32)
    k = k_ref[rows, :].astype(F32)
    q_dec = (q * jnp.exp(b)).astype(BF16)
    k_inv = (k * jnp.exp(-b)).astype(BF16)
    k_dec = (k * jnp.exp(b_last - b)).astype(BF16)
    decay = jnp.exp(b_last)
    for p in range(GLA_HEADS // 2):
        sl = slice(p * LANES, (p + 1) * LANES)
        qd_p, ki_p, kd_p = q_dec[:, sl], k_inv[:, sl], k_dec[:, sl]
        st = st_ref[p]
        st_bf = st.astype(BF16)
        contrib = []
        for hh in range(2):
            h = 2 * p + hh
            qm = jnp.where(first_head if hh == 0 else ~first_head, qd_p, jnp.zeros_like(qd_p))
            a = jnp.where(keep, _dot_nt(qm, ki_p), 0.0).astype(BF16)
            v_h = v_ref[rows, h * GLA_DV:(h + 1) * GLA_DV]
            o_ref[rows, h * GLA_DV:(h + 1) * GLA_DV] = _dot(a, v_h) + _dot_nt(qm, st_bf)
            contrib.append(_dot_tn(v_h, kd_p))
        lane = lax.broadcasted_iota(jnp.int32, (GLA_DV, LANES), 1)
        st_ref[p] = st * decay[:, sl] + jnp.where(lane < GLA_DK, contrib[0], contrib[1])


def _gla_kernel(qf_ref, kf_ref, vf_ref, laf_ref, qb_ref, kb_ref, vb_ref, lab_ref,
                of_ref, ob_ref, sf_ref, sb_ref, *, n_chunks):
    @pl.when(pl.program_id(1) == 0)
    def _():
        sf_ref[...] = jnp.zeros_like(sf_ref)
        sb_ref[...] = jnp.zeros_like(sb_ref)

    C = GLA_CHUNK
    row = lax.broadcasted_iota(jnp.int32, (C, C), 0)
    col = lax.broadcasted_iota(jnp.int32, (C, C), 1)
    tri_f = jnp.where(col <= row, 1.0, 0.0).astype(BF16)
    tri_b = jnp.where(col >= row, 1.0, 0.0).astype(BF16)
    first_head = lax.broadcasted_iota(jnp.int32, (C, LANES), 1) < GLA_DK

    def chunk(c, carry):
        rf = pl.multiple_of(c * C, C)
        rb = pl.multiple_of((n_chunks - 1 - c) * C, C)
        _gla_direction(qf_ref, kf_ref, vf_ref, laf_ref, of_ref, sf_ref, rf,
                       tri_f, col <= row, C - 1, first_head)
        _gla_direction(qb_ref, kb_ref, vb_ref, lab_ref, ob_ref, sb_ref, rb,
                       tri_b, col > row, 0, first_head)
        return carry

    lax.fori_loop(0, n_chunks, chunk, 0)


def _gla(qg, kg, vg, la, *, tb):
    B, T, _ = qg.shape
    nb = T // tb
    kw = GLA_HEADS * GLA_DK
    vw = GLA_HEADS * GLA_DV
    fwd = lambda w, blk=0: pl.BlockSpec((None, tb, w), lambda b, s: (b, s, blk))
    bwd = lambda w, blk=0: pl.BlockSpec((None, tb, w), lambda b, s: (b, nb - 1 - s, blk))
    state = pltpu.VMEM((GLA_HEADS // 2, GLA_DV, LANES), F32)
    return pl.pallas_call(
        functools.partial(_gla_kernel, n_chunks=tb // GLA_CHUNK),
        grid=(B, nb),
        in_specs=[fwd(kw), fwd(kw), fwd(vw), fwd(kw, 0), bwd(kw), bwd(kw), bwd(vw), bwd(kw, 1)],
        out_specs=[fwd(vw), bwd(vw)],
        out_shape=[jax.ShapeDtypeStruct((B, T, vw), F32)] * 2,
        scratch_shapes=[state, state],
        compiler_params=pltpu.CompilerParams(
            dimension_semantics=("parallel", "arbitrary"), vmem_limit_bytes=VMEM_LIMIT),
        name="gla",
    )(qg, kg, vg, la, qg, kg, vg, la)


def _mix_ffn_kernel(x_ref, yna_ref, of_ref, ob_ref, sr_ref, gng_ref, wout_ref, gff_ref,
                    w1_ref, w2_ref, gfin_ref, out_ref, *, final_norm, ff_chunk):
    o = of_ref[...] + ob_ref[...]
    parts = [_rms(o[:, h * GLA_DV:(h + 1) * GLA_DV], gng_ref[...]) for h in range(GLA_HEADS)]
    y_gla = (jnp.concatenate(parts, axis=-1) * sr_ref[...].astype(F32)).astype(BF16)
    na_w = yna_ref.shape[-1]
    mix = _dot(yna_ref[...], wout_ref[0:na_w, :]) + _dot(y_gla, wout_ref[na_w:, :])
    h1 = x_ref[...] + mix
    n2 = _rms(h1, gff_ref[...]).astype(BF16)
    ffn = None
    for f in range(w1_ref.shape[1] // ff_chunk):
        u = _dot(n2, w1_ref[:, f * ff_chunk:(f + 1) * ff_chunk])
        a = jnp.square(jnp.maximum(u, 0.0)).astype(BF16)
        d = _dot(a, w2_ref[f * ff_chunk:(f + 1) * ff_chunk, :])
        ffn = d if ffn is None else ffn + d
    h2 = h1 + ffn
    out_ref[...] = _rms(h2, gfin_ref[...]) if final_norm else h2


def _mix_ffn(x, yna, of, ob, sr, gng, wout, gff, w1, w2, gfin, *, tm, final_norm):
    B, T, D = x.shape
    tok = lambda w: pl.BlockSpec((None, tm, w), lambda b, i: (b, i, 0))
    const = lambda a: pl.BlockSpec(a.shape, lambda b, i: (0,) * a.ndim,
                                   pipeline_mode=pl.Buffered(1))
    return pl.pallas_call(
        functools.partial(_mix_ffn_kernel, final_norm=final_norm, ff_chunk=1024),
        grid=(B, T // tm),
        in_specs=[tok(D), tok(yna.shape[-1]), tok(of.shape[-1]), tok(ob.shape[-1]),
                  tok(sr.shape[-1]), const(gng), const(wout), const(gff), const(w1),
                  const(w2), const(gfin)],
        out_specs=tok(D),
        out_shape=jax.ShapeDtypeStruct((B, T, D), F32),
        compiler_params=pltpu.CompilerParams(
            dimension_semantics=("parallel", "parallel"), vmem_limit_bytes=VMEM_LIMIT),
        name="mix_ffn",
    )(x, yna, of, ob, sr, gng, wout, gff, w1, w2, gfin)


def kernel(x, ln_mix_g, w_in, na_rpb, gla_gate_up_fwd, gla_gate_bias_fwd, gla_gate_up_bwd,
           gla_gate_bias_bwd, gla_norm_g, w_out, ln_ff_g, w_ff1, w_ff2, ln_final_g):
    B, T, D = x.shape
    depth = w_in.shape[0]
    assert T % NA_GROUP == 0 and T // NA_GROUP >= 3 and T % 512 == 0
    n_main = w_in.shape[-1] - 2 * GLA_GATE_RANK
    kw = GLA_HEADS * GLA_DK
    row = lambda v: v.reshape(1, -1).astype(F32)
    h = x
    for l in range(depth):
        w = w_in[l]
        wm = w[:, :n_main].astype(BF16)
        wkt = w[:, 512:1024].T.astype(BF16)
        wz = w[:, n_main:].astype(BF16)
        zeros = jnp.zeros((GLA_GATE_RANK, kw), F32)
        gu = jnp.concatenate([
            jnp.concatenate([gla_gate_up_fwd[l], zeros], axis=1),
            jnp.concatenate([zeros, gla_gate_up_bwd[l]], axis=1)], axis=0).astype(BF16)
        gb = jnp.concatenate([gla_gate_bias_fwd[l], gla_gate_bias_bwd[l]]).reshape(1, -1)
        qa, kat, va, qg, kg, vg, sr, la = _inproj(
            h, row(ln_mix_g[l]), wm, wkt, wz, gu, gb.astype(F32), tm=512)
        y_na = _na(qa, kat, va, _na_bias_rows(na_rpb[l]), tq=1024)
        o_f, o_b = _gla(qg, kg, vg, la, tb=512)
        h = _mix_ffn(h, y_na, o_f, o_b, sr, row(gla_norm_g[l]), w_out[l].astype(BF16),
                     row(ln_ff_g[l]), w_ff1[l].astype(BF16), w_ff2[l].astype(BF16),
                     row(ln_final_g), tm=512, final_norm=(l == depth - 1))
    return h
```

```python
import functools

import jax
import jax.numpy as jnp
from jax import lax
from jax.experimental import pallas as pl
from jax.experimental.pallas import tpu as pltpu

F32 = jnp.float32
BF16 = jnp.bfloat16

EPS = 1e-6
GRID_W = 64
NA_HEADS = 8
NA_HEAD_DIM = 64
NA_KH = 8
NA_KW = 16
GLA_HEADS = 4
GLA_DK = 64
GLA_DV = 128
GLA_GATE_RANK = 16
GLA_GATE_NORM = 16.0
GLA_CHUNK = 64

LANES = 128
NA_GROUP_ROWS = 4
NA_GROUP = NA_GROUP_ROWS * GRID_W
NA_KEY_ROWS = 12
NA_KEYS = NA_KEY_ROWS * GRID_W
VMEM_LIMIT = 56 * 1024 * 1024


def _dot(a, b):
    return jnp.dot(a, b, preferred_element_type=F32)


def _dot_nt(a, b):
    return lax.dot_general(a, b, (((1,), (1,)), ((), ())), preferred_element_type=F32)


def _dot_tn(a, b):
    return lax.dot_general(a, b, (((0,), (0,)), ((), ())), preferred_element_type=F32)


def _rms(x, g):
    return x * lax.rsqrt(jnp.mean(x * x, axis=-1, keepdims=True) + EPS) * g


def _inproj_kernel(x_ref, g_ref, wm_ref, wkt_ref, wz_ref, gu_ref, gb_ref,
                   qa_ref, kat_ref, va_ref, qg_ref, kg_ref, vg_ref, sr_ref, la_ref):
    n = _rms(x_ref[...], g_ref[...]).astype(BF16)
    qa_ref[...] = (_dot(n, wm_ref[:, 0:512]) * (NA_HEAD_DIM ** -0.5)).astype(BF16)
    kat_ref[...] = _dot_nt(wkt_ref[...], n).astype(BF16)
    va_ref[...] = _dot(n, wm_ref[:, 1024:1536]).astype(BF16)
    qg_ref[...] = (_dot(n, wm_ref[:, 1536:1792]) * (GLA_DK ** -0.5)).astype(BF16)
    kg_ref[...] = _dot(n, wm_ref[:, 1792:2048]).astype(BF16)
    vg_ref[...] = _dot(n, wm_ref[:, 2048:2560]).astype(BF16)
    r = _dot(n, wm_ref[:, 2560:3072])
    sr_ref[...] = (r * jax.nn.sigmoid(r)).astype(BF16)
    z = _dot(n, wz_ref[...]).astype(BF16)
    pre = _dot(z, gu_ref[...]) + gb_ref[...]
    log_sig = jnp.minimum(pre, 0.0) - jnp.log1p(jnp.exp(-jnp.abs(pre)))
    la_ref[...] = log_sig * (1.0 / GLA_GATE_NORM)


def _inproj(x, g, wm, wkt, wz, gu, gb, *, tm):
    B, T, D = x.shape
    tok = lambda w: pl.BlockSpec((None, tm, w), lambda b, i: (b, i, 0))
    const = lambda a: pl.BlockSpec(a.shape, lambda b, i: (0,) * a.ndim,
                                   pipeline_mode=pl.Buffered(1))
    outs = [
        (jax.ShapeDtypeStruct((B, T, 512), BF16), tok(512)),
        (jax.ShapeDtypeStruct((B, 512, T), BF16),
         pl.BlockSpec((None, 512, tm), lambda b, i: (b, 0, i))),
        (jax.ShapeDtypeStruct((B, T, 512), BF16), tok(512)),
        (jax.ShapeDtypeStruct((B, T, 256), BF16), tok(256)),
        (jax.ShapeDtypeStruct((B, T, 256), BF16), tok(256)),
        (jax.ShapeDtypeStruct((B, T, 512), BF16), tok(512)),
        (jax.ShapeDtypeStruct((B, T, 512), BF16), tok(512)),
        (jax.ShapeDtypeStruct((B, T, 512), F32), tok(512)),
    ]
    return pl.pallas_call(
        _inproj_kernel,
        grid=(B, T // tm),
        in_specs=[tok(D), const(g), const(wm), const(wkt), const(wz), const(gu), const(gb)],
        out_specs=[s for _, s in outs],
        out_shape=[s for s, _ in outs],
        compiler_params=pltpu.CompilerParams(
            dimension_semantics=("parallel", "parallel"), vmem_limit_bytes=VMEM_LIMIT),
        name="inproj",
    )(x, g, wm, wkt, wz, gu, gb)


def _na_bias_rows(rpb):
    n_dc = 2 * NA_KW - 1
    padded = jnp.pad(rpb.astype(F32), ((0, 0), (4, 5), (0, GRID_W - n_dc)))
    return jnp.concatenate([padded[:, :-1], padded[:, 1:]], axis=-1)


def _na_build_tables(rows_ref, tab_ref):
    qc = lax.broadcasted_iota(jnp.int32, (GRID_W, LANES), 0)
    lane = lax.broadcasted_iota(jnp.int32, (GRID_W, LANES), 1)
    kc = lane & (GRID_W - 1)
    col_start = jnp.clip(qc - NA_KW // 2, 0, GRID_W - NA_KW)
    in_win = (kc >= col_start) & (kc < col_start + NA_KW)
    valid = {(True, True): in_win,
             (True, False): in_win & (lane < GRID_W),
             (False, True): in_win & (lane >= GRID_W)}
    neg = jnp.full((GRID_W, LANES), -jnp.inf, F32)
    row_offset = (0, -(NA_KH // 2), -NA_KH)
    for hh in range(2):
        for t in range(3):
            for g in range(NA_GROUP_ROWS):
                i_start = (0, g, NA_KEY_ROWS - NA_KH)[t]
                for m in range(NA_KEY_ROWS // 2):
                    halves = tuple(i_start <= i < i_start + NA_KH for i in (2 * m, 2 * m + 1))
                    if any(halves):
                        a = 2 * m - g + row_offset[t] + NA_KH - 1 + 4
                        src = jnp.broadcast_to(rows_ref[hh, a:a + 1, :], (GRID_W, LANES))
                        rot = pltpu.roll(src, LANES - (NA_KW - 1), 1, stride=1, stride_axis=0)
                        block = jnp.where(valid[halves], rot, neg)
                    else:
                        block = neg
                    tab_ref[hh, t, g * GRID_W:(g + 1) * GRID_W, m * LANES:(m + 1) * LANES] = block


def _na_kernel(q_ref, kt_ref, v_ref, rows_ref, o_ref, bias_ref, *, groups_per_step, n_groups):
    j = pl.program_id(2)

    @pl.when(j == 0)
    def _():
        _na_build_tables(rows_ref, bias_ref)

    lane = lax.broadcasted_iota(jnp.int32, (NA_GROUP, LANES), 1)
    first_head = lane < NA_HEAD_DIM
    for gi in range(groups_per_step):
        u = j * groups_per_step + gi
        key0 = pl.multiple_of(jnp.clip(u - 1, 0, n_groups - 3) * NA_GROUP, NA_GROUP)
        table = jnp.where(u == 0, 0, jnp.where(u == n_groups - 1, 2, 1))
        qq = q_ref[gi * NA_GROUP:(gi + 1) * NA_GROUP, :]
        kt = kt_ref[:, pl.ds(key0, NA_KEYS)]
        vv = v_ref[pl.ds(key0, NA_KEYS), :]
        outs = []
        for hh in range(2):
            qm = jnp.where(first_head if hh == 0 else ~first_head, qq, jnp.zeros_like(qq))
            s = _dot(qm, kt) + bias_ref[hh, table]
            m = jnp.max(s, axis=-1, keepdims=True)
            p = jnp.exp(s - m)
            l = jnp.sum(p, axis=-1, keepdims=True)
            outs.append(_dot(p.astype(BF16), vv) * (1.0 / l))
        o_ref[gi * NA_GROUP:(gi + 1) * NA_GROUP, :] = jnp.where(
            first_head, outs[0], outs[1]).astype(o_ref.dtype)


def _na(qa, kat, va, rows, *, tq):
    B, T, _ = qa.shape
    n_groups = T // NA_GROUP
    n_pairs = NA_HEADS // 2
    kern = functools.partial(_na_kernel, groups_per_step=tq // NA_GROUP, n_groups=n_groups)
    return pl.pallas_call(
        kern,
        grid=(B, n_pairs, T // tq),
        in_specs=[
            pl.BlockSpec((None, tq, LANES), lambda b, p, j: (b, j, p)),
            pl.BlockSpec((None, LANES, T), lambda b, p, j: (b, p, 0)),
            pl.BlockSpec((None, T, LANES), lambda b, p, j: (b, 0, p)),
            pl.BlockSpec((2,) + rows.shape[1:], lambda b, p, j: (p, 0, 0)),
        ],
        out_specs=pl.BlockSpec((None, tq, LANES), lambda b, p, j: (b, j, p)),
        out_shape=jax.ShapeDtypeStruct((B, T, NA_HEADS * NA_HEAD_DIM), BF16),
        scratch_shapes=[pltpu.VMEM((2, 3, NA_GROUP, NA_KEYS), F32)],
        compiler_params=pltpu.CompilerParams(
            dimension_semantics=("parallel", "parallel", "arbitrary"),
            vmem_limit_bytes=VMEM_LIMIT),
        name="natten",
    )(qa, kat, va, rows)


def _segmented_cumsum(x, reverse):
    n = x.shape[0]
    pos = lax.broadcasted_iota(jnp.int32, x.shape, 0) & (GLA_CHUNK - 1)
    step = 1
    while step < GLA_CHUNK:
        if reverse:
            shifted, ok = pltpu.roll(x, n - step, 0), pos < GLA_CHUNK - step
        else:
            shifted, ok = pltpu.roll(x, step, 0), pos >= step
        x = x + jnp.where(ok, shifted, 0.0)
        step *= 2
    return x


class _GlaDir:
    def __init__(self, reverse, q, k, v, la, o, state, qd, ki, kd, dec, a, contrib, prev):
        self.reverse = reverse
        self.q, self.k, self.v, self.la, self.o, self.state = q, k, v, la, o, state
        self.qd, self.ki, self.kd, self.dec, self.a, self.contrib, self.prev = (
            qd, ki, kd, dec, a, contrib, prev)


def _gla_decays(d, n_chunks):
    C = GLA_CHUNK
    b = _segmented_cumsum(d.la[...], d.reverse)
    b3 = b.reshape(n_chunks, C, b.shape[-1])
    last = 0 if d.reverse else C - 1
    b_last = b3[:, last:last + 1, :]
    q = d.q[...].astype(F32)
    k = d.k[...].astype(F32)
    d.qd[...] = (q * jnp.exp(b)).astype(BF16)
    d.ki[...] = (k * jnp.exp(-b)).astype(BF16)
    d.kd[...] = (k.reshape(b3.shape) * jnp.exp(b_last - b3)).reshape(b.shape).astype(BF16)
    d.dec[...] = jnp.exp(b_last)


def _gla_intra(d, c):
    C = GLA_CHUNK
    rows = pl.ds(pl.multiple_of(c * C, C), C)
    qd = d.qd[rows, :]
    lane = lax.broadcasted_iota(jnp.int32, qd.shape, 1)
    q_heads = jnp.concatenate(
        [jnp.where((lane >= h * GLA_DK) & (lane < (h + 1) * GLA_DK), qd, jnp.zeros_like(qd))
         for h in range(GLA_HEADS)], axis=0)
    scores = _dot_nt(q_heads, d.ki[rows, :])
    i = lax.broadcasted_iota(jnp.int32, scores.shape, 0) & (C - 1)
    j = lax.broadcasted_iota(jnp.int32, scores.shape, 1)
    keep = (j > i) if d.reverse else (j <= i)
    d.a[c] = jnp.where(keep, scores, 0.0).astype(BF16)
    first_head = lax.broadcasted_iota(jnp.int32, (GLA_DV, LANES), 1) < GLA_DK
    for p in range(GLA_HEADS // 2):
        kd_p = d.kd[rows, p * LANES:(p + 1) * LANES]
        c0, c1 = (_dot_tn(d.v[rows, h * GLA_DV:(h + 1) * GLA_DV], kd_p) for h in (2 * p, 2 * p + 1))
        d.contrib[c, p] = jnp.where(first_head, c0, c1)


def _gla_scan(d, n_chunks):
    order = range(n_chunks - 1, -1, -1) if d.reverse else range(n_chunks)
    for p in range(GLA_HEADS // 2):
        st = d.state[p]
        for c in order:
            d.prev[c, p] = st.astype(BF16)
            st = st * d.dec[c][:, p * LANES:(p + 1) * LANES] + d.contrib[c, p]
        d.state[p] = st


def _gla_outputs(d, c):
    C = GLA_CHUNK
    rows = pl.ds(pl.multiple_of(c * C, C), C)
    first_head = lax.broadcasted_iota(jnp.int32, (C, LANES), 1) < GLA_DK
    for p in range(GLA_HEADS // 2):
        qd_p = d.qd[rows, p * LANES:(p + 1) * LANES]
        zero = jnp.zeros_like(qd_p)
        q_pair = jnp.concatenate([jnp.where(first_head, qd_p, zero),
                                  jnp.where(first_head, zero, qd_p)], axis=0)
        inter = _dot_nt(q_pair, d.prev[c, p])
        for hh in range(2):
            h = 2 * p + hh
            cols = slice(h * GLA_DV, (h + 1) * GLA_DV)
            d.o[rows, cols] = _dot(d.a[c, h * C:(h + 1) * C, :], d.v[rows, cols]) + inter[hh * C:(hh + 1) * C]


def _gla_kernel(qf_ref, kf_ref, vf_ref, laf_ref, qb_ref, kb_ref, vb_ref, lab_ref,
                of_ref, ob_ref, sf_ref, sb_ref, *scratch, n_chunks):
    @pl.when(pl.program_id(1) == 0)
    def _():
        sf_ref[...] = jnp.zeros_like(sf_ref)
        sb_ref[...] = jnp.zeros_like(sb_ref)

    n = len(scratch) // 2
    dirs = (_GlaDir(False, qf_ref, kf_ref, vf_ref, laf_ref, of_ref, sf_ref, *scratch[:n]),
            _GlaDir(True, qb_ref, kb_ref, vb_ref, lab_ref, ob_ref, sb_ref, *scratch[n:]))
    for d in dirs:
        _gla_decays(d, n_chunks)

    def intra(c, carry):
        for d in dirs:
            _gla_intra(d, c)
        return carry

    lax.fori_loop(0, n_chunks, intra, 0, unroll=2)
    for d in dirs:
        _gla_scan(d, n_chunks)

    def outputs(c, carry):
        for d in dirs:
            _gla_outputs(d, c)
        return carry

    lax.fori_loop(0, n_chunks, outputs, 0, unroll=2)


def _gla(qg, kg, vg, la, *, tb):
    B, T, _ = qg.shape
    nb = T // tb
    kw = GLA_HEADS * GLA_DK
    vw = GLA_HEADS * GLA_DV
    fwd = lambda w, blk=0: pl.BlockSpec((None, tb, w), lambda b, s: (b, s, blk))
    bwd = lambda w, blk=0: pl.BlockSpec((None, tb, w), lambda b, s: (b, nb - 1 - s, blk))
    nc = tb // GLA_CHUNK
    n_pairs = GLA_HEADS // 2
    state = pltpu.VMEM((n_pairs, GLA_DV, LANES), F32)
    per_dir = [pltpu.VMEM((tb, kw), BF16)] * 3 + [
        pltpu.VMEM((nc, 1, kw), F32),
        pltpu.VMEM((nc, GLA_HEADS * GLA_CHUNK, GLA_CHUNK), BF16),
        pltpu.VMEM((nc, n_pairs, GLA_DV, LANES), F32),
        pltpu.VMEM((nc, n_pairs, GLA_DV, LANES), BF16)]
    return pl.pallas_call(
        functools.partial(_gla_kernel, n_chunks=nc),
        grid=(B, nb),
        in_specs=[fwd(kw), fwd(kw), fwd(vw), fwd(kw, 0), bwd(kw), bwd(kw), bwd(vw), bwd(kw, 1)],
        out_specs=[fwd(vw), bwd(vw)],
        out_shape=[jax.ShapeDtypeStruct((B, T, vw), F32)] * 2,
        scratch_shapes=[state, state] + per_dir * 2,
        compiler_params=pltpu.CompilerParams(
            dimension_semantics=("parallel", "arbitrary"), vmem_limit_bytes=VMEM_LIMIT),
        name="gla",
    )(qg, kg, vg, la, qg, kg, vg, la)


def _mix_ffn_kernel(x_ref, yna_ref, of_ref, ob_ref, sr_ref, gng_ref, wout_ref, gff_ref,
                    w1_ref, w2_ref, gfin_ref, out_ref, *, final_norm, ff_chunk):
    o = of_ref[...] + ob_ref[...]
    parts = [_rms(o[:, h * GLA_DV:(h + 1) * GLA_DV], gng_ref[...]) for h in range(GLA_HEADS)]
    y_gla = (jnp.concatenate(parts, axis=-1) * sr_ref[...].astype(F32)).astype(BF16)
    na_w = yna_ref.shape[-1]
    mix = _dot(yna_ref[...], wout_ref[0:na_w, :]) + _dot(y_gla, wout_ref[na_w:, :])
    h1 = x_ref[...] + mix
    n2 = _rms(h1, gff_ref[...]).astype(BF16)
    ffn = None
    for f in range(w1_ref.shape[1] // ff_chunk):
        u = _dot(n2, w1_ref[:, f * ff_chunk:(f + 1) * ff_chunk])
        a = jnp.square(jnp.maximum(u, 0.0)).astype(BF16)
        d = _dot(a, w2_ref[f * ff_chunk:(f + 1) * ff_chunk, :])
        ffn = d if ffn is None else ffn + d
    h2 = h1 + ffn
    out_ref[...] = _rms(h2, gfin_ref[...]) if final_norm else h2


def _mix_ffn(x, yna, of, ob, sr, gng, wout, gff, w1, w2, gfin, *, tm, final_norm):
    B, T, D = x.shape
    tok = lambda w: pl.BlockSpec((None, tm, w), lambda b, i: (b, i, 0))
    const = lambda a: pl.BlockSpec(a.shape, lambda b, i: (0,) * a.ndim,
                                   pipeline_mode=pl.Buffered(1))
    return pl.pallas_call(
        functools.partial(_mix_ffn_kernel, final_norm=final_norm, ff_chunk=1024),
        grid=(B, T // tm),
        in_specs=[tok(D), tok(yna.shape[-1]), tok(of.shape[-1]), tok(ob.shape[-1]),
                  tok(sr.shape[-1]), const(gng), const(wout), const(gff), const(w1),
                  const(w2), const(gfin)],
        out_specs=tok(D),
        out_shape=jax.ShapeDtypeStruct((B, T, D), F32),
        compiler_params=pltpu.CompilerParams(
            dimension_semantics=("parallel", "parallel"), vmem_limit_bytes=VMEM_LIMIT),
        name="mix_ffn",
    )(x, yna, of, ob, sr, gng, wout, gff, w1, w2, gfin)


def kernel(x, ln_mix_g, w_in, na_rpb, gla_gate_up_fwd, gla_gate_bias_fwd, gla_gate_up_bwd,
           gla_gate_bias_bwd, gla_norm_g, w_out, ln_ff_g, w_ff1, w_ff2, ln_final_g):
    B, T, D = x.shape
    depth = w_in.shape[0]
    assert T % NA_GROUP == 0 and T // NA_GROUP >= 3 and T % 512 == 0
    n_main = w_in.shape[-1] - 2 * GLA_GATE_RANK
    kw = GLA_HEADS * GLA_DK
    row = lambda v: v.reshape(1, -1).astype(F32)
    h = x
    for l in range(depth):
        w = w_in[l]
        wm = w[:, :n_main].astype(BF16)
        wkt = w[:, 512:1024].T.astype(BF16)
        wz = w[:, n_main:].astype(BF16)
        zeros = jnp.zeros((GLA_GATE_RANK, kw), F32)
        gu = jnp.concatenate([
            jnp.concatenate([gla_gate_up_fwd[l], zeros], axis=1),
            jnp.concatenate([zeros, gla_gate_up_bwd[l]], axis=1)], axis=0).astype(BF16)
        gb = jnp.concatenate([gla_gate_bias_fwd[l], gla_gate_bias_bwd[l]]).reshape(1, -1)
        qa, kat, va, qg, kg, vg, sr, la = _inproj(
            h, row(ln_mix_g[l]), wm, wkt, wz, gu, gb.astype(F32), tm=512)
        y_na = _na(qa, kat, va, _na_bias_rows(na_rpb[l]), tq=1024)
        o_f, o_b = _gla(qg, kg, vg, la, tb=512)
        h = _mix_ffn(h, y_na, o_f, o_b, sr, row(gla_norm_g[l]), w_out[l].astype(BF16),
                     row(ln_ff_g[l]), w_ff1[l].astype(BF16), w_ff2[l].astype(BF16),
                     row(ln_final_g), tm=512, final_norm=(l == depth - 1))
    return h
```

```python
import functools

import jax
import jax.numpy as jnp
from jax import lax
from jax.experimental import pallas as pl
from jax.experimental.pallas import tpu as pltpu

F32 = jnp.float32
BF16 = jnp.bfloat16

EPS = 1e-6
GRID_W = 64
NA_HEADS = 8
NA_HEAD_DIM = 64
NA_KH = 8
NA_KW = 16
GLA_HEADS = 4
GLA_DK = 64
GLA_DV = 128
GLA_GATE_RANK = 16
GLA_GATE_NORM = 16.0
GLA_CHUNK = 64

LANES = 128
NA_GROUP_ROWS = 4
NA_GROUP = NA_GROUP_ROWS * GRID_W
NA_KEY_ROWS = 12
NA_KEYS = NA_KEY_ROWS * GRID_W
VMEM_LIMIT = 56 * 1024 * 1024


def _dot(a, b):
    return jnp.dot(a, b, preferred_element_type=F32)


def _dot_nt(a, b):
    return lax.dot_general(a, b, (((1,), (1,)), ((), ())), preferred_element_type=F32)


def _dot_tn(a, b):
    return lax.dot_general(a, b, (((0,), (0,)), ((), ())), preferred_element_type=F32)


def _rms(x, g):
    return x * lax.rsqrt(jnp.mean(x * x, axis=-1, keepdims=True) + EPS) * g


def _segmented_cumsum(x, reverse):
    n = x.shape[0]
    pos = lax.broadcasted_iota(jnp.int32, x.shape, 0) & (GLA_CHUNK - 1)
    step = 1
    while step < GLA_CHUNK:
        if reverse:
            shifted, ok = pltpu.roll(x, n - step, 0), pos < GLA_CHUNK - step
        else:
            shifted, ok = pltpu.roll(x, step, 0), pos >= step
        x = x + jnp.where(ok, shifted, 0.0)
        step *= 2
    return x


def _gla_operands(q, k, log_a, reverse):
    C = GLA_CHUNK
    b = _segmented_cumsum(log_a, reverse)
    b3 = b.reshape(-1, C, b.shape[-1])
    last = 0 if reverse else C - 1
    b_last = b3[:, last:last + 1, :]
    k_dec = (k.reshape(b3.shape) * jnp.exp(b_last - b3)).reshape(b.shape)
    ops = jnp.concatenate([q * jnp.exp(b), k * jnp.exp(-b), k_dec], axis=-1)
    return ops.astype(BF16), jnp.exp(b_last)


def _inproj_kernel(x_ref, g_ref, wm_ref, wkt_ref, wz_ref, gu_ref, gb_ref,
                   qa_ref, kat_ref, va_ref, vg_ref, sr_ref, gf_ref, gbw_ref, dec_ref):
    kw = GLA_HEADS * GLA_DK
    n = _rms(x_ref[...], g_ref[...]).astype(BF16)
    z = _dot(n, wz_ref[...]).astype(BF16)
    pre = _dot(z, gu_ref[...]) + gb_ref[...]
    log_sig = jnp.minimum(pre, 0.0) - jnp.log1p(jnp.exp(-jnp.abs(pre)))
    log_a = log_sig * (1.0 / GLA_GATE_NORM)
    qg = _dot(n, wm_ref[:, 1536:1792]) * (GLA_DK ** -0.5)
    kg = _dot(n, wm_ref[:, 1792:2048])
    gf_ref[...], dec_f = _gla_operands(qg, kg, log_a[:, :kw], False)
    gbw_ref[...], dec_b = _gla_operands(qg, kg, log_a[:, kw:], True)
    dec_ref[...] = jnp.concatenate([dec_f, dec_b], axis=-1)
    qa_ref[...] = (_dot(n, wm_ref[:, 0:512]) * (NA_HEAD_DIM ** -0.5)).astype(BF16)
    kat_ref[...] = _dot_nt(wkt_ref[...], n).astype(BF16)
    va_ref[...] = _dot(n, wm_ref[:, 1024:1536]).astype(BF16)
    vg_ref[...] = _dot(n, wm_ref[:, 2048:2560]).astype(BF16)
    r = _dot(n, wm_ref[:, 2560:3072])
    sr_ref[...] = (r * jax.nn.sigmoid(r)).astype(BF16)


def _inproj(x, g, wm, wkt, wz, gu, gb, *, tm):
    B, T, D = x.shape
    kw = GLA_HEADS * GLA_DK
    nc = tm // GLA_CHUNK
    tok = lambda w: pl.BlockSpec((None, tm, w), lambda b, i: (b, i, 0))
    const = lambda a: pl.BlockSpec(a.shape, lambda b, i: (0,) * a.ndim,
                                   pipeline_mode=pl.Buffered(1))
    outs = [
        (jax.ShapeDtypeStruct((B, T, 512), BF16), tok(512)),
        (jax.ShapeDtypeStruct((B, 512, T), BF16),
         pl.BlockSpec((None, 512, tm), lambda b, i: (b, 0, i))),
        (jax.ShapeDtypeStruct((B, T, 512), BF16), tok(512)),
        (jax.ShapeDtypeStruct((B, T, 512), BF16), tok(512)),
        (jax.ShapeDtypeStruct((B, T, 512), BF16), tok(512)),
        (jax.ShapeDtypeStruct((B, T, 3 * kw), BF16), tok(3 * kw)),
        (jax.ShapeDtypeStruct((B, T, 3 * kw), BF16), tok(3 * kw)),
        (jax.ShapeDtypeStruct((B, T // GLA_CHUNK, 1, 2 * kw), F32),
         pl.BlockSpec((None, nc, 1, 2 * kw), lambda b, i: (b, i, 0, 0))),
    ]
    return pl.pallas_call(
        _inproj_kernel,
        grid=(B, T // tm),
        in_specs=[tok(D), const(g), const(wm), const(wkt), const(wz), const(gu), const(gb)],
        out_specs=[s for _, s in outs],
        out_shape=[s for s, _ in outs],
        compiler_params=pltpu.CompilerParams(
            dimension_semantics=("parallel", "parallel"), vmem_limit_bytes=VMEM_LIMIT),
        name="inproj",
    )(x, g, wm, wkt, wz, gu, gb)


def _na_bias_rows(rpb):
    n_dc = 2 * NA_KW - 1
    padded = jnp.pad(rpb.astype(F32), ((0, 0), (4, 5), (0, GRID_W - n_dc)))
    return jnp.concatenate([padded[:, :-1], padded[:, 1:]], axis=-1)


def _na_build_tables(rows_ref, tab_ref):
    qc = lax.broadcasted_iota(jnp.int32, (GRID_W, LANES), 0)
    lane = lax.broadcasted_iota(jnp.int32, (GRID_W, LANES), 1)
    kc = lane & (GRID_W - 1)
    col_start = jnp.clip(qc - NA_KW // 2, 0, GRID_W - NA_KW)
    in_win = (kc >= col_start) & (kc < col_start + NA_KW)
    valid = {(True, True): in_win,
             (True, False): in_win & (lane < GRID_W),
             (False, True): in_win & (lane >= GRID_W)}
    neg = jnp.full((GRID_W, LANES), -jnp.inf, F32)
    row_offset = (0, -(NA_KH // 2), -NA_KH)
    for hh in range(2):
        for t in range(3):
            for g in range(NA_GROUP_ROWS):
                i_start = (0, g, NA_KEY_ROWS - NA_KH)[t]
                for m in range(NA_KEY_ROWS // 2):
                    halves = tuple(i_start <= i < i_start + NA_KH for i in (2 * m, 2 * m + 1))
                    if any(halves):
                        a = 2 * m - g + row_offset[t] + NA_KH - 1 + 4
                        src = jnp.broadcast_to(rows_ref[hh, a:a + 1, :], (GRID_W, LANES))
                        rot = pltpu.roll(src, LANES - (NA_KW - 1), 1, stride=1, stride_axis=0)
                        block = jnp.where(valid[halves], rot, neg)
                    else:
                        block = neg
                    tab_ref[hh, t, g * GRID_W:(g + 1) * GRID_W, m * LANES:(m + 1) * LANES] = block


def _na_kernel(q_ref, kt_ref, v_ref, rows_ref, o_ref, bias_ref, *, groups_per_step, n_groups):
    j = pl.program_id(2)

    @pl.when(j == 0)
    def _():
        _na_build_tables(rows_ref, bias_ref)

    lane = lax.broadcasted_iota(jnp.int32, (NA_GROUP, LANES), 1)
    first_head = lane < NA_HEAD_DIM
    for gi in range(groups_per_step):
        u = j * groups_per_step + gi
        key0 = pl.multiple_of(jnp.clip(u - 1, 0, n_groups - 3) * NA_GROUP, NA_GROUP)
        table = jnp.where(u == 0, 0, jnp.where(u == n_groups - 1, 2, 1))
        qq = q_ref[gi * NA_GROUP:(gi + 1) * NA_GROUP, :]
        kt = kt_ref[:, pl.ds(key0, NA_KEYS)]
        vv = v_ref[pl.ds(key0, NA_KEYS), :]
        outs = []
        for hh in range(2):
            qm = jnp.where(first_head if hh == 0 else ~first_head, qq, jnp.zeros_like(qq))
            s = _dot(qm, kt) + bias_ref[hh, table]
            m = jnp.max(s, axis=-1, keepdims=True)
            p = jnp.exp(s - m)
            l = jnp.sum(p, axis=-1, keepdims=True)
            outs.append(_dot(p.astype(BF16), vv) * (1.0 / l))
        o_ref[gi * NA_GROUP:(gi + 1) * NA_GROUP, :] = jnp.where(
            first_head, outs[0], outs[1]).astype(o_ref.dtype)


def _na(qa, kat, va, rows, *, tq):
    B, T, _ = qa.shape
    n_groups = T // NA_GROUP
    n_pairs = NA_HEADS // 2
    kern = functools.partial(_na_kernel, groups_per_step=tq // NA_GROUP, n_groups=n_groups)
    return pl.pallas_call(
        kern,
        grid=(B, n_pairs, T // tq),
        in_specs=[
            pl.BlockSpec((None, tq, LANES), lambda b, p, j: (b, j, p)),
            pl.BlockSpec((None, LANES, T), lambda b, p, j: (b, p, 0)),
            pl.BlockSpec((None, T, LANES), lambda b, p, j: (b, 0, p)),
            pl.BlockSpec((2,) + rows.shape[1:], lambda b, p, j: (p, 0, 0)),
        ],
        out_specs=pl.BlockSpec((None, tq, LANES), lambda b, p, j: (b, j, p)),
        out_shape=jax.ShapeDtypeStruct((B, T, NA_HEADS * NA_HEAD_DIM), BF16),
        scratch_shapes=[pltpu.VMEM((2, 3, NA_GROUP, NA_KEYS), F32)],
        compiler_params=pltpu.CompilerParams(
            dimension_semantics=("parallel", "parallel", "arbitrary"),
            vmem_limit_bytes=VMEM_LIMIT),
        name="natten",
    )(qa, kat, va, rows)


class _GlaDir:
    def __init__(self, reverse, ops, v, dec, o, state, a, contrib, prev):
        kw = GLA_HEADS * GLA_DK
        self.reverse = reverse
        self.qd, self.ki, self.kd = (ops.at[:, i * kw:(i + 1) * kw] for i in range(3))
        self.v, self.dec, self.o, self.state = v, dec, o, state
        self.a, self.contrib, self.prev = a, contrib, prev


def _gla_intra(d, c):
    C = GLA_CHUNK
    rows = pl.ds(pl.multiple_of(c * C, C), C)
    qd = d.qd[rows, :]
    lane = lax.broadcasted_iota(jnp.int32, qd.shape, 1)
    q_heads = jnp.concatenate(
        [jnp.where((lane >= h * GLA_DK) & (lane < (h + 1) * GLA_DK), qd, jnp.zeros_like(qd))
         for h in range(GLA_HEADS)], axis=0)
    scores = _dot_nt(q_heads, d.ki[rows, :])
    i = lax.broadcasted_iota(jnp.int32, scores.shape, 0) & (C - 1)
    j = lax.broadcasted_iota(jnp.int32, scores.shape, 1)
    keep = (j > i) if d.reverse else (j <= i)
    d.a[c] = jnp.where(keep, scores, 0.0).astype(BF16)
    first_head = lax.broadcasted_iota(jnp.int32, (GLA_DV, LANES), 1) < GLA_DK
    for p in range(GLA_HEADS // 2):
        kd_p = d.kd[rows, p * LANES:(p + 1) * LANES]
        c0, c1 = (_dot_tn(d.v[rows, h * GLA_DV:(h + 1) * GLA_DV], kd_p) for h in (2 * p, 2 * p + 1))
        d.contrib[c, p] = jnp.where(first_head, c0, c1)


def _gla_scan(d, n_chunks):
    order = range(n_chunks - 1, -1, -1) if d.reverse else range(n_chunks)
    for p in range(GLA_HEADS // 2):
        st = d.state[p]
        for c in order:
            d.prev[c, p] = st.astype(BF16)
            st = st * d.dec[c][:, p * LANES:(p + 1) * LANES] + d.contrib[c, p]
        d.state[p] = st


def _gla_outputs(d, c):
    C = GLA_CHUNK
    rows = pl.ds(pl.multiple_of(c * C, C), C)
    first_head = lax.broadcasted_iota(jnp.int32, (C, LANES), 1) < GLA_DK
    for p in range(GLA_HEADS // 2):
        qd_p = d.qd[rows, p * LANES:(p + 1) * LANES]
        zero = jnp.zeros_like(qd_p)
        q_pair = jnp.concatenate([jnp.where(first_head, qd_p, zero),
                                  jnp.where(first_head, zero, qd_p)], axis=0)
        inter = _dot_nt(q_pair, d.prev[c, p])
        for hh in range(2):
            h = 2 * p + hh
            cols = slice(h * GLA_DV, (h + 1) * GLA_DV)
            d.o[rows, cols] = _dot(d.a[c, h * C:(h + 1) * C, :], d.v[rows, cols]) + inter[hh * C:(hh + 1) * C]


def _gla_kernel(gf_ref, vf_ref, decf_ref, gb_ref, vb_ref, decb_ref,
                of_ref, ob_ref, sf_ref, sb_ref, *scratch, n_chunks):
    @pl.when(pl.program_id(1) == 0)
    def _():
        sf_ref[...] = jnp.zeros_like(sf_ref)
        sb_ref[...] = jnp.zeros_like(sb_ref)

    n = len(scratch) // 2
    dirs = (_GlaDir(False, gf_ref, vf_ref, decf_ref, of_ref, sf_ref, *scratch[:n]),
            _GlaDir(True, gb_ref, vb_ref, decb_ref, ob_ref, sb_ref, *scratch[n:]))

    def intra(c, carry):
        for d in dirs:
            _gla_intra(d, c)
        return carry

    lax.fori_loop(0, n_chunks, intra, 0, unroll=2)
    for d in dirs:
        _gla_scan(d, n_chunks)

    def outputs(c, carry):
        for d in dirs:
            _gla_outputs(d, c)
        return carry

    lax.fori_loop(0, n_chunks, outputs, 0, unroll=2)


def _gla(gf, gb, vg, dec, *, tb):
    B, T, vw = vg.shape
    nb = T // tb
    nc = tb // GLA_CHUNK
    kw = GLA_HEADS * GLA_DK
    n_pairs = GLA_HEADS // 2
    fwd = lambda w: pl.BlockSpec((None, tb, w), lambda b, s: (b, s, 0))
    bwd = lambda w: pl.BlockSpec((None, tb, w), lambda b, s: (b, nb - 1 - s, 0))
    dec_fwd = pl.BlockSpec((None, nc, 1, kw), lambda b, s: (b, s, 0, 0))
    dec_bwd = pl.BlockSpec((None, nc, 1, kw), lambda b, s: (b, nb - 1 - s, 0, 1))
    state = pltpu.VMEM((n_pairs, GLA_DV, LANES), F32)
    per_dir = [
        pltpu.VMEM((nc, GLA_HEADS * GLA_CHUNK, GLA_CHUNK), BF16),
        pltpu.VMEM((nc, n_pairs, GLA_DV, LANES), F32),
        pltpu.VMEM((nc, n_pairs, GLA_DV, LANES), BF16)]
    return pl.pallas_call(
        functools.partial(_gla_kernel, n_chunks=nc),
        grid=(B, nb),
        in_specs=[fwd(3 * kw), fwd(vw), dec_fwd, bwd(3 * kw), bwd(vw), dec_bwd],
        out_specs=[fwd(vw), bwd(vw)],
        out_shape=[jax.ShapeDtypeStruct((B, T, vw), F32)] * 2,
        scratch_shapes=[state, state] + per_dir * 2,
        compiler_params=pltpu.CompilerParams(
            dimension_semantics=("parallel", "arbitrary"), vmem_limit_bytes=VMEM_LIMIT),
        name="gla",
    )(gf, vg, dec, gb, vg, dec)


def _mix_ffn_kernel(x_ref, yna_ref, of_ref, ob_ref, sr_ref, gng_ref, wout_ref, gff_ref,
                    w1_ref, w2_ref, gfin_ref, out_ref, *, final_norm, ff_chunk):
    o = of_ref[...] + ob_ref[...]
    parts = [_rms(o[:, h * GLA_DV:(h + 1) * GLA_DV], gng_ref[...]) for h in range(GLA_HEADS)]
    y_gla = (jnp.concatenate(parts, axis=-1) * sr_ref[...].astype(F32)).astype(BF16)
    na_w = yna_ref.shape[-1]
    mix = _dot(yna_ref[...], wout_ref[0:na_w, :]) + _dot(y_gla, wout_ref[na_w:, :])
    h1 = x_ref[...] + mix
    n2 = _rms(h1, gff_ref[...]).astype(BF16)
    ffn = None
    for f in range(w1_ref.shape[1] // ff_chunk):
        u = _dot(n2, w1_ref[:, f * ff_chunk:(f + 1) * ff_chunk])
        a = jnp.square(jnp.maximum(u, 0.0)).astype(BF16)
        d = _dot(a, w2_ref[f * ff_chunk:(f + 1) * ff_chunk, :])
        ffn = d if ffn is None else ffn + d
    h2 = h1 + ffn
    out_ref[...] = _rms(h2, gfin_ref[...]) if final_norm else h2


def _mix_ffn(x, yna, of, ob, sr, gng, wout, gff, w1, w2, gfin, *, tm, final_norm):
    B, T, D = x.shape
    tok = lambda w: pl.BlockSpec((None, tm, w), lambda b, i: (b, i, 0))
    const = lambda a: pl.BlockSpec(a.shape, lambda b, i: (0,) * a.ndim,
                                   pipeline_mode=pl.Buffered(1))
    return pl.pallas_call(
        functools.partial(_mix_ffn_kernel, final_norm=final_norm, ff_chunk=1024),
        grid=(B, T // tm),
        in_specs=[tok(D), tok(yna.shape[-1]), tok(of.shape[-1]), tok(ob.shape[-1]),
                  tok(sr.shape[-1]), const(gng), const(wout), const(gff), const(w1),
                  const(w2), const(gfin)],
        out_specs=tok(D),
        out_shape=jax.ShapeDtypeStruct((B, T, D), F32),
        compiler_params=pltpu.CompilerParams(
            dimension_semantics=("parallel", "parallel"), vmem_limit_bytes=VMEM_LIMIT),
        name="mix_ffn",
    )(x, yna, of, ob, sr, gng, wout, gff, w1, w2, gfin)


def kernel(x, ln_mix_g, w_in, na_rpb, gla_gate_up_fwd, gla_gate_bias_fwd, gla_gate_up_bwd,
           gla_gate_bias_bwd, gla_norm_g, w_out, ln_ff_g, w_ff1, w_ff2, ln_final_g):
    B, T, D = x.shape
    depth = w_in.shape[0]
    assert T % NA_GROUP == 0 and T // NA_GROUP >= 3 and T % 512 == 0
    n_main = w_in.shape[-1] - 2 * GLA_GATE_RANK
    kw = GLA_HEADS * GLA_DK
    row = lambda v: v.reshape(1, -1).astype(F32)
    h = x
    for l in range(depth):
        w = w_in[l]
        wm = w[:, :n_main].astype(BF16)
        wkt = w[:, 512:1024].T.astype(BF16)
        wz = w[:, n_main:].astype(BF16)
        zeros = jnp.zeros((GLA_GATE_RANK, kw), F32)
        gu = jnp.concatenate([
            jnp.concatenate([gla_gate_up_fwd[l], zeros], axis=1),
            jnp.concatenate([zeros, gla_gate_up_bwd[l]], axis=1)], axis=0).astype(BF16)
        gb = jnp.concatenate([gla_gate_bias_fwd[l], gla_gate_bias_bwd[l]]).reshape(1, -1)
        qa, kat, va, vg, sr, gla_f, gla_b, dec = _inproj(
            h, row(ln_mix_g[l]), wm, wkt, wz, gu, gb.astype(F32), tm=512)
        y_na = _na(qa, kat, va, _na_bias_rows(na_rpb[l]), tq=1024)
        o_f, o_b = _gla(gla_f, gla_b, vg, dec, tb=512)
        h = _mix_ffn(h, y_na, o_f, o_b, sr, row(gla_norm_g[l]), w_out[l].astype(BF16),
                     row(ln_ff_g[l]), w_ff1[l].astype(BF16), w_ff2[l].astype(BF16),
                     row(ln_final_g), tm=512, final_norm=(l == depth - 1))
    return h
```

```python
import functools

import jax
import jax.numpy as jnp
from jax import lax
from jax.experimental import pallas as pl
from jax.experimental.pallas import tpu as pltpu

F32 = jnp.float32
BF16 = jnp.bfloat16

EPS = 1e-6
GRID_W = 64
NA_HEADS = 8
NA_HEAD_DIM = 64
NA_KH = 8
NA_KW = 16
GLA_HEADS = 4
GLA_DK = 64
GLA_DV = 128
GLA_GATE_RANK = 16
GLA_GATE_NORM = 16.0
GLA_CHUNK = 64

LANES = 128
NA_GROUP_ROWS = 4
NA_GROUP = NA_GROUP_ROWS * GRID_W
NA_KEY_ROWS = 12
NA_KEYS = NA_KEY_ROWS * GRID_W
LOG2E = 1.4426950408889634
VMEM_LIMIT = 56 * 1024 * 1024


def _dot(a, b):
    return jnp.dot(a, b, preferred_element_type=F32)


def _dot_nt(a, b):
    return lax.dot_general(a, b, (((1,), (1,)), ((), ())), preferred_element_type=F32)


def _dot_tn(a, b):
    return lax.dot_general(a, b, (((0,), (0,)), ((), ())), preferred_element_type=F32)


def _rms(x, g):
    return x * lax.rsqrt(jnp.mean(x * x, axis=-1, keepdims=True) + EPS) * g


def _segmented_cumsum(x, reverse):
    n = x.shape[0]
    pos = lax.broadcasted_iota(jnp.int32, x.shape, 0) & (GLA_CHUNK - 1)
    step = 1
    while step < GLA_CHUNK:
        if reverse:
            shifted, ok = pltpu.roll(x, n - step, 0), pos < GLA_CHUNK - step
        else:
            shifted, ok = pltpu.roll(x, step, 0), pos >= step
        x = x + jnp.where(ok, shifted, 0.0)
        step *= 2
    return x


def _gla_operands(q, k, log_a, reverse):
    C = GLA_CHUNK
    b = _segmented_cumsum(log_a, reverse)
    b3 = b.reshape(-1, C, b.shape[-1])
    last = 0 if reverse else C - 1
    b_last = b3[:, last:last + 1, :]
    k_dec = (k.reshape(b3.shape) * jnp.exp(b_last - b3)).reshape(b.shape)
    ops = jnp.concatenate([q * jnp.exp(b), k * jnp.exp(-b), k_dec], axis=-1)
    return ops.astype(BF16), jnp.exp(b_last)


def _inproj_kernel(x_ref, g_ref, wm_ref, wkt_ref, wz_ref, gu_ref, gb_ref,
                   qa_ref, kat_ref, va_ref, vg_ref, sr_ref, gf_ref, gbw_ref, dec_ref):
    kw = GLA_HEADS * GLA_DK
    n = _rms(x_ref[...], g_ref[...]).astype(BF16)
    def log_decay(cols):
        pre = _dot(z, gu_ref[:, cols]) + gb_ref[:, cols]
        log_sig = jnp.minimum(pre, 0.0) - jnp.log1p(jnp.exp(-jnp.abs(pre)))
        return log_sig * (1.0 / GLA_GATE_NORM)

    z = _dot(n, wz_ref[...]).astype(BF16)
    qg = _dot(n, wm_ref[:, 1536:1792]) * (GLA_DK ** -0.5)
    kg = _dot(n, wm_ref[:, 1792:2048])
    gf_ref[...], dec_f = _gla_operands(qg, kg, log_decay(slice(0, kw)), False)
    qa_ref[...] = (_dot(n, wm_ref[:, 0:512]) * (NA_HEAD_DIM ** -0.5 * LOG2E)).astype(BF16)
    kat_ref[...] = _dot_nt(wkt_ref[...], n).astype(BF16)
    gbw_ref[...], dec_b = _gla_operands(qg, kg, log_decay(slice(kw, 2 * kw)), True)
    dec_ref[...] = jnp.concatenate([dec_f, dec_b], axis=-1)
    va_ref[...] = _dot(n, wm_ref[:, 1024:1536]).astype(BF16)
    vg_ref[...] = _dot(n, wm_ref[:, 2048:2560]).astype(BF16)
    r = _dot(n, wm_ref[:, 2560:3072])
    sr_ref[...] = (r * jax.nn.sigmoid(r)).astype(BF16)


def _inproj(x, g, wm, wkt, wz, gu, gb, *, tm):
    B, T, D = x.shape
    kw = GLA_HEADS * GLA_DK
    nc = tm // GLA_CHUNK
    tok = lambda w: pl.BlockSpec((None, tm, w), lambda b, i: (b, i, 0))
    const = lambda a: pl.BlockSpec(a.shape, lambda b, i: (0,) * a.ndim,
                                   pipeline_mode=pl.Buffered(1))
    outs = [
        (jax.ShapeDtypeStruct((B, T, 512), BF16), tok(512)),
        (jax.ShapeDtypeStruct((B, 512, T), BF16),
         pl.BlockSpec((None, 512, tm), lambda b, i: (b, 0, i))),
        (jax.ShapeDtypeStruct((B, T, 512), BF16), tok(512)),
        (jax.ShapeDtypeStruct((B, T, 512), BF16), tok(512)),
        (jax.ShapeDtypeStruct((B, T, 512), BF16), tok(512)),
        (jax.ShapeDtypeStruct((B, T, 3 * kw), BF16), tok(3 * kw)),
        (jax.ShapeDtypeStruct((B, T, 3 * kw), BF16), tok(3 * kw)),
        (jax.ShapeDtypeStruct((B, T // GLA_CHUNK, 1, 2 * kw), F32),
         pl.BlockSpec((None, nc, 1, 2 * kw), lambda b, i: (b, i, 0, 0))),
    ]
    return pl.pallas_call(
        _inproj_kernel,
        grid=(B, T // tm),
        in_specs=[tok(D), const(g), const(wm), const(wkt), const(wz), const(gu), const(gb)],
        out_specs=[s for _, s in outs],
        out_shape=[s for s, _ in outs],
        compiler_params=pltpu.CompilerParams(
            dimension_semantics=("parallel", "parallel"), vmem_limit_bytes=VMEM_LIMIT),
        name="inproj",
    )(x, g, wm, wkt, wz, gu, gb)


def _na_bias_rows(rpb):
    n_dc = 2 * NA_KW - 1
    padded = jnp.pad(rpb.astype(F32), ((0, 0), (4, 5), (0, GRID_W - n_dc)))
    return jnp.concatenate([padded[:, :-1], padded[:, 1:]], axis=-1)


def _na_tiles(table, g):
    i_start = (0, g, NA_KEY_ROWS - NA_KH)[table]
    return range(i_start // 2, (i_start + NA_KH + 1) // 2)


def _na_build_tables(rows_ref, tab_ref):
    qc = lax.broadcasted_iota(jnp.int32, (GRID_W, LANES), 0)
    lane = lax.broadcasted_iota(jnp.int32, (GRID_W, LANES), 1)
    kc = lane & (GRID_W - 1)
    col_start = jnp.clip(qc - NA_KW // 2, 0, GRID_W - NA_KW)
    in_win = (kc >= col_start) & (kc < col_start + NA_KW)
    valid = {(True, True): in_win,
             (True, False): in_win & (lane < GRID_W),
             (False, True): in_win & (lane >= GRID_W)}
    row_offset = (0, -(NA_KH // 2), -NA_KH)
    for hh in range(2):
        for t in range(3):
            for g in range(NA_GROUP_ROWS):
                i_start = (0, g, NA_KEY_ROWS - NA_KH)[t]
                for m in _na_tiles(t, g):
                    halves = tuple(i_start <= i < i_start + NA_KH for i in (2 * m, 2 * m + 1))
                    a = 2 * m - g + row_offset[t] + NA_KH - 1 + 4
                    src = jnp.broadcast_to(rows_ref[hh, a:a + 1, :], (GRID_W, LANES))
                    rot = pltpu.roll(src, LANES - (NA_KW - 1), 1, stride=1, stride_axis=0)
                    tab_ref[hh, t, g * GRID_W:(g + 1) * GRID_W, m * LANES:(m + 1) * LANES] = (
                        jnp.where(valid[halves], rot * LOG2E, -jnp.inf))


class _NaRefs:
    def __init__(self, q, kt, v, bias, o, s, p, inv_l, n_groups):
        self.q, self.kt, self.v, self.bias, self.o = q, kt, v, bias, o
        self.s, self.p, self.inv_l, self.n_groups = s, p, inv_l, n_groups


def _na_rows(u):
    return pl.ds(pl.multiple_of(u * NA_GROUP, NA_GROUP), NA_GROUP)


def _na_key0(r, u):
    return pl.multiple_of(jnp.clip(u - 1, 0, r.n_groups - 3) * NA_GROUP, NA_GROUP)


def _na_scores(r, u):
    qq = r.q[_na_rows(u), :]
    first_head = lax.broadcasted_iota(jnp.int32, qq.shape, 1) < NA_HEAD_DIM
    zero = jnp.zeros_like(qq)
    q_heads = jnp.concatenate([jnp.where(first_head, qq, zero),
                               jnp.where(first_head, zero, qq)], axis=0)
    return _dot(q_heads, r.kt[:, pl.ds(_na_key0(r, u), NA_KEYS)])


def _na_softmax(r, s, table):
    p_tiles, inv_l = [], []
    for hh in range(2):
        for g in range(NA_GROUP_ROWS):
            rows = slice(hh * NA_GROUP + g * GRID_W, hh * NA_GROUP + (g + 1) * GRID_W)
            sb = [s[rows, c * LANES:(c + 1) * LANES]
                  + r.bias[hh, table, g * GRID_W:(g + 1) * GRID_W, c * LANES:(c + 1) * LANES]
                  for c in _na_tiles(table, g)]
            m = jnp.max(functools.reduce(jnp.maximum, sb), axis=-1, keepdims=True)
            p = [jnp.exp2(x - m) for x in sb]
            inv_l.append(1.0 / jnp.sum(functools.reduce(jnp.add, p), axis=-1, keepdims=True))
            p_tiles.append([x.astype(BF16) for x in p])
    return p_tiles, inv_l


def _na_store_output(r, u, o):
    first_head = lax.broadcasted_iota(jnp.int32, (NA_GROUP, LANES), 1) < NA_HEAD_DIM
    r.o[_na_rows(u), :] = jnp.where(first_head, o[:NA_GROUP], o[NA_GROUP:]).astype(r.o.dtype)


def _na_edge_group(r, u, table):
    tiles = _na_tiles(table, 0)
    p_tiles, inv_l = _na_softmax(r, _na_scores(r, u), table)
    p = jnp.concatenate([jnp.concatenate(t, axis=1) for t in p_tiles], axis=0)
    keys = pl.ds(_na_key0(r, u) + tiles.start * LANES, len(tiles) * LANES)
    _na_store_output(r, u, _dot(p, r.v[keys, :]) * jnp.concatenate(inv_l, axis=0))


def _na_block_rows(block):
    return slice(block * GRID_W, (block + 1) * GRID_W)


def _na_stage(r, u, slot):
    r.s[1 - slot][...] = _na_scores(r, jnp.minimum(u + 1, r.n_groups - 2))
    _na_stage_softmax(r, slot)
    _na_stage_output(r, u - 1, 1 - slot)


def _na_stage_softmax(r, slot):
    p_tiles, inv_l = _na_softmax(r, r.s[slot], 1)
    for block, (tiles, inv) in enumerate(zip(p_tiles, inv_l)):
        r.inv_l[slot][_na_block_rows(block), :] = inv
        for c, tile in zip(_na_tiles(1, block % NA_GROUP_ROWS), tiles):
            r.p[slot][_na_block_rows(block), c * LANES:(c + 1) * LANES] = tile


def _na_stage_output(r, u, slot):
    o = _dot(r.p[slot][...], r.v[pl.ds(_na_key0(r, u), NA_KEYS), :]) * r.inv_l[slot][...]
    _na_store_output(r, u, o)


def _na_kernel(q_ref, kt_ref, v_ref, rows_ref, o_ref, bias_ref, *slots, n_groups):
    _na_build_tables(rows_ref, bias_ref)
    r = _NaRefs(q_ref, kt_ref, v_ref, bias_ref, o_ref, slots[0:2], slots[2:4], slots[4:6], n_groups)
    last = n_groups - 1
    _na_edge_group(r, 0, 0)
    _na_edge_group(r, last, 2)
    for slot in range(2):
        for block in range(2 * NA_GROUP_ROWS):
            for c in set(range(NA_KEYS // LANES)) - set(_na_tiles(1, block % NA_GROUP_ROWS)):
                r.p[slot][_na_block_rows(block), c * LANES:(c + 1) * LANES] = (
                    jnp.zeros((GRID_W, LANES), BF16))
    r.s[1][...] = _na_scores(r, 1)
    _na_stage_softmax(r, 1)
    r.s[0][...] = _na_scores(r, 2)
    _na_stage(r, 2, 0)

    def two_stages(i, carry):
        _na_stage(r, 3 + 2 * i, 1)
        _na_stage(r, 4 + 2 * i, 0)
        return carry

    lax.fori_loop(0, (n_groups - 4) // 2, two_stages, 0)
    _na_stage_output(r, last - 1, (last - 1) % 2)


def _na(qa, kat, va, rows):
    B, T, _ = qa.shape
    n_pairs = NA_HEADS // 2
    return pl.pallas_call(
        functools.partial(_na_kernel, n_groups=T // NA_GROUP),
        grid=(B, n_pairs),
        in_specs=[
            pl.BlockSpec((None, T, LANES), lambda b, p: (b, 0, p)),
            pl.BlockSpec((None, LANES, T), lambda b, p: (b, p, 0)),
            pl.BlockSpec((None, T, LANES), lambda b, p: (b, 0, p)),
            pl.BlockSpec((2,) + rows.shape[1:], lambda b, p: (p, 0, 0)),
        ],
        out_specs=pl.BlockSpec((None, T, LANES), lambda b, p: (b, 0, p)),
        out_shape=jax.ShapeDtypeStruct((B, T, NA_HEADS * NA_HEAD_DIM), BF16),
        scratch_shapes=[pltpu.VMEM((2, 3, NA_GROUP, NA_KEYS), F32)
                        ] + [pltpu.VMEM((2 * NA_GROUP, NA_KEYS), F32)] * 2
                        + [pltpu.VMEM((2 * NA_GROUP, NA_KEYS), BF16)] * 2
                        + [pltpu.VMEM((2 * NA_GROUP, 1), F32)] * 2,
        compiler_params=pltpu.CompilerParams(
            dimension_semantics=("parallel", "parallel"), vmem_limit_bytes=VMEM_LIMIT),
        name="natten",
    )(qa, kat, va, rows)


class _GlaDir:
    def __init__(self, reverse, ops, v, dec, o, state, a, contrib, prev):
        kw = GLA_HEADS * GLA_DK
        self.reverse = reverse
        self.qd, self.ki, self.kd = (ops.at[:, i * kw:(i + 1) * kw] for i in range(3))
        self.v, self.dec, self.o, self.state = v, dec, o, state
        self.a, self.contrib, self.prev = a, contrib, prev


def _gla_intra(d, c):
    C = GLA_CHUNK
    rows = pl.ds(pl.multiple_of(c * C, C), C)
    qd = d.qd[rows, :]
    lane = lax.broadcasted_iota(jnp.int32, qd.shape, 1)
    q_heads = jnp.concatenate(
        [jnp.where((lane >= h * GLA_DK) & (lane < (h + 1) * GLA_DK), qd, jnp.zeros_like(qd))
         for h in range(GLA_HEADS)], axis=0)
    scores = _dot_nt(q_heads, d.ki[rows, :])
    i = lax.broadcasted_iota(jnp.int32, scores.shape, 0) & (C - 1)
    j = lax.broadcasted_iota(jnp.int32, scores.shape, 1)
    keep = (j > i) if d.reverse else (j <= i)
    d.a[c] = jnp.where(keep, scores, 0.0).astype(BF16)
    first_head = lax.broadcasted_iota(jnp.int32, (GLA_DV, LANES), 1) < GLA_DK
    for p in range(GLA_HEADS // 2):
        kd_p = d.kd[rows, p * LANES:(p + 1) * LANES]
        c0, c1 = (_dot_tn(d.v[rows, h * GLA_DV:(h + 1) * GLA_DV], kd_p) for h in (2 * p, 2 * p + 1))
        d.contrib[c, p] = jnp.where(first_head, c0, c1)


def _gla_scan(d, n_chunks):
    order = range(n_chunks - 1, -1, -1) if d.reverse else range(n_chunks)
    for p in range(GLA_HEADS // 2):
        st = d.state[p]
        for c in order:
            d.prev[c, p] = st.astype(BF16)
            st = st * d.dec[c][:, p * LANES:(p + 1) * LANES] + d.contrib[c, p]
        d.state[p] = st


def _gla_outputs(d, c):
    C = GLA_CHUNK
    rows = pl.ds(pl.multiple_of(c * C, C), C)
    first_head = lax.broadcasted_iota(jnp.int32, (C, LANES), 1) < GLA_DK
    for p in range(GLA_HEADS // 2):
        qd_p = d.qd[rows, p * LANES:(p + 1) * LANES]
        zero = jnp.zeros_like(qd_p)
        q_pair = jnp.concatenate([jnp.where(first_head, qd_p, zero),
                                  jnp.where(first_head, zero, qd_p)], axis=0)
        inter = _dot_nt(q_pair, d.prev[c, p])
        for hh in range(2):
            h = 2 * p + hh
            cols = slice(h * GLA_DV, (h + 1) * GLA_DV)
            d.o[rows, cols] = _dot(d.a[c, h * C:(h + 1) * C, :], d.v[rows, cols]) + inter[hh * C:(hh + 1) * C]


def _gla_kernel(gf_ref, vf_ref, decf_ref, gb_ref, vb_ref, decb_ref,
                of_ref, ob_ref, sf_ref, sb_ref, *scratch, n_chunks):
    @pl.when(pl.program_id(1) == 0)
    def _():
        sf_ref[...] = jnp.zeros_like(sf_ref)
        sb_ref[...] = jnp.zeros_like(sb_ref)

    n = len(scratch) // 2
    dirs = (_GlaDir(False, gf_ref, vf_ref, decf_ref, of_ref, sf_ref, *scratch[:n]),
            _GlaDir(True, gb_ref, vb_ref, decb_ref, ob_ref, sb_ref, *scratch[n:]))

    def intra(c, carry):
        for d in dirs:
            _gla_intra(d, c)
        return carry

    lax.fori_loop(0, n_chunks, intra, 0, unroll=2)
    for d in dirs:
        _gla_scan(d, n_chunks)

    def outputs(c, carry):
        for d in dirs:
            _gla_outputs(d, c)
        return carry

    lax.fori_loop(0, n_chunks, outputs, 0, unroll=2)


def _gla(gf, gb, vg, dec, *, tb):
    B, T, vw = vg.shape
    nb = T // tb
    nc = tb // GLA_CHUNK
    kw = GLA_HEADS * GLA_DK
    n_pairs = GLA_HEADS // 2
    fwd = lambda w: pl.BlockSpec((None, tb, w), lambda b, s: (b, s, 0))
    bwd = lambda w: pl.BlockSpec((None, tb, w), lambda b, s: (b, nb - 1 - s, 0))
    dec_fwd = pl.BlockSpec((None, nc, 1, kw), lambda b, s: (b, s, 0, 0))
    dec_bwd = pl.BlockSpec((None, nc, 1, kw), lambda b, s: (b, nb - 1 - s, 0, 1))
    state = pltpu.VMEM((n_pairs, GLA_DV, LANES), F32)
    per_dir = [
        pltpu.VMEM((nc, GLA_HEADS * GLA_CHUNK, GLA_CHUNK), BF16),
        pltpu.VMEM((nc, n_pairs, GLA_DV, LANES), F32),
        pltpu.VMEM((nc, n_pairs, GLA_DV, LANES), BF16)]
    return pl.pallas_call(
        functools.partial(_gla_kernel, n_chunks=nc),
        grid=(B, nb),
        in_specs=[fwd(3 * kw), fwd(vw), dec_fwd, bwd(3 * kw), bwd(vw), dec_bwd],
        out_specs=[fwd(vw), bwd(vw)],
        out_shape=[jax.ShapeDtypeStruct((B, T, vw), F32)] * 2,
        scratch_shapes=[state, state] + per_dir * 2,
        compiler_params=pltpu.CompilerParams(
            dimension_semantics=("parallel", "arbitrary"), vmem_limit_bytes=VMEM_LIMIT),
        name="gla",
    )(gf, vg, dec, gb, vg, dec)


def _mix_ffn_kernel(x_ref, yna_ref, of_ref, ob_ref, sr_ref, gng_ref, wout_ref, gff_ref,
                    w1_ref, w2_ref, gfin_ref, out_ref, *, final_norm, ff_chunk):
    o = of_ref[...] + ob_ref[...]
    parts = [_rms(o[:, h * GLA_DV:(h + 1) * GLA_DV], gng_ref[...]) for h in range(GLA_HEADS)]
    y_gla = (jnp.concatenate(parts, axis=-1) * sr_ref[...].astype(F32)).astype(BF16)
    na_w = yna_ref.shape[-1]
    mix = _dot(yna_ref[...], wout_ref[0:na_w, :]) + _dot(y_gla, wout_ref[na_w:, :])
    h1 = x_ref[...] + mix
    n2 = _rms(h1, gff_ref[...]).astype(BF16)
    ffn = None
    for f in range(w1_ref.shape[1] // ff_chunk):
        u = _dot(n2, w1_ref[:, f * ff_chunk:(f + 1) * ff_chunk])
        a = jnp.square(jnp.maximum(u, 0.0)).astype(BF16)
        d = _dot(a, w2_ref[f * ff_chunk:(f + 1) * ff_chunk, :])
        ffn = d if ffn is None else ffn + d
    h2 = h1 + ffn
    out_ref[...] = _rms(h2, gfin_ref[...]) if final_norm else h2


def _mix_ffn(x, yna, of, ob, sr, gng, wout, gff, w1, w2, gfin, *, tm, final_norm):
    B, T, D = x.shape
    tok = lambda w: pl.BlockSpec((None, tm, w), lambda b, i: (b, i, 0))
    const = lambda a: pl.BlockSpec(a.shape, lambda b, i: (0,) * a.ndim,
                                   pipeline_mode=pl.Buffered(1))
    return pl.pallas_call(
        functools.partial(_mix_ffn_kernel, final_norm=final_norm, ff_chunk=1024),
        grid=(B, T // tm),
        in_specs=[tok(D), tok(yna.shape[-1]), tok(of.shape[-1]), tok(ob.shape[-1]),
                  tok(sr.shape[-1]), const(gng), const(wout), const(gff), const(w1),
                  const(w2), const(gfin)],
        out_specs=tok(D),
        out_shape=jax.ShapeDtypeStruct((B, T, D), F32),
        compiler_params=pltpu.CompilerParams(
            dimension_semantics=("parallel", "parallel"), vmem_limit_bytes=VMEM_LIMIT),
        name="mix_ffn",
    )(x, yna, of, ob, sr, gng, wout, gff, w1, w2, gfin)


def kernel(x, ln_mix_g, w_in, na_rpb, gla_gate_up_fwd, gla_gate_bias_fwd, gla_gate_up_bwd,
           gla_gate_bias_bwd, gla_norm_g, w_out, ln_ff_g, w_ff1, w_ff2, ln_final_g):
    B, T, D = x.shape
    depth = w_in.shape[0]
    assert T % NA_GROUP == 0 and T // NA_GROUP >= 3 and T % 512 == 0
    n_main = w_in.shape[-1] - 2 * GLA_GATE_RANK
    kw = GLA_HEADS * GLA_DK
    row = lambda v: v.reshape(1, -1).astype(F32)
    h = x
    for l in range(depth):
        w = w_in[l]
        wm = w[:, :n_main].astype(BF16)
        wkt = w[:, 512:1024].T.astype(BF16)
        wz = w[:, n_main:].astype(BF16)
        zeros = jnp.zeros((GLA_GATE_RANK, kw), F32)
        gu = jnp.concatenate([
            jnp.concatenate([gla_gate_up_fwd[l], zeros], axis=1),
            jnp.concatenate([zeros, gla_gate_up_bwd[l]], axis=1)], axis=0).astype(BF16)
        gb = jnp.concatenate([gla_gate_bias_fwd[l], gla_gate_bias_bwd[l]]).reshape(1, -1)
        qa, kat, va, vg, sr, gla_f, gla_b, dec = _inproj(
            h, row(ln_mix_g[l]), wm, wkt, wz, gu, gb.astype(F32), tm=512)
        y_na = _na(qa, kat, va, _na_bias_rows(na_rpb[l]))
        o_f, o_b = _gla(gla_f, gla_b, vg, dec, tb=512)
        h = _mix_ffn(h, y_na, o_f, o_b, sr, row(gla_norm_g[l]), w_out[l].astype(BF16),
                     row(ln_ff_g[l]), w_ff1[l].astype(BF16), w_ff2[l].astype(BF16),
                     row(ln_final_g), tm=512, final_norm=(l == depth - 1))
    return h
```

```python
import functools

import jax
import jax.numpy as jnp
from jax import lax
from jax.experimental import pallas as pl
from jax.experimental.pallas import tpu as pltpu

F32 = jnp.float32
BF16 = jnp.bfloat16

EPS = 1e-6
GRID_W = 64
NA_HEADS = 8
NA_HEAD_DIM = 64
NA_KH = 8
NA_KW = 16
GLA_HEADS = 4
GLA_DK = 64
GLA_DV = 128
GLA_GATE_RANK = 16
GLA_GATE_NORM = 16.0
GLA_CHUNK = 64

LANES = 128
NA_GROUP_ROWS = 4
NA_GROUP = NA_GROUP_ROWS * GRID_W
NA_KEY_ROWS = 12
NA_KEYS = NA_KEY_ROWS * GRID_W
LOG2E = 1.4426950408889634
VMEM_LIMIT = 56 * 1024 * 1024


def _dot(a, b):
    return jnp.dot(a, b, preferred_element_type=F32)


def _dot_nt(a, b):
    return lax.dot_general(a, b, (((1,), (1,)), ((), ())), preferred_element_type=F32)


def _dot_tn(a, b):
    return lax.dot_general(a, b, (((0,), (0,)), ((), ())), preferred_element_type=F32)


def _rms(x, g):
    return x * lax.rsqrt(jnp.mean(x * x, axis=-1, keepdims=True) + EPS) * g


def _segmented_cumsum(x, reverse):
    n = x.shape[0]
    pos = lax.broadcasted_iota(jnp.int32, x.shape, 0) & (GLA_CHUNK - 1)
    step = 1
    while step < GLA_CHUNK:
        if reverse:
            shifted, ok = pltpu.roll(x, n - step, 0), pos < GLA_CHUNK - step
        else:
            shifted, ok = pltpu.roll(x, step, 0), pos >= step
        x = x + jnp.where(ok, shifted, 0.0)
        step *= 2
    return x


def _gla_operands(q, k, log_a, reverse):
    C = GLA_CHUNK
    b = _segmented_cumsum(log_a, reverse)
    b3 = b.reshape(-1, C, b.shape[-1])
    last = 0 if reverse else C - 1
    b_last = b3[:, last:last + 1, :]
    k_dec = (k.reshape(b3.shape) * jnp.exp(b_last - b3)).reshape(b.shape)
    ops = jnp.concatenate([q * jnp.exp(b), k * jnp.exp(-b), k_dec], axis=-1)
    return ops.astype(BF16), jnp.exp(b_last)


def _inproj_kernel(x_ref, g_ref, wm_ref, wkt_ref, wz_ref, gu_ref, gb_ref,
                   qa_ref, kat_ref, va_ref, vg_ref, sr_ref, gf_ref, gbw_ref, dec_ref):
    kw = GLA_HEADS * GLA_DK
    n = _rms(x_ref[...], g_ref[...]).astype(BF16)
    def log_decay(cols):
        pre = _dot(z, gu_ref[:, cols]) + gb_ref[:, cols]
        log_sig = jnp.minimum(pre, 0.0) - jnp.log1p(jnp.exp(-jnp.abs(pre)))
        return log_sig * (1.0 / GLA_GATE_NORM)

    z = _dot(n, wz_ref[...]).astype(BF16)
    qg = _dot(n, wm_ref[:, 1536:1792]) * (GLA_DK ** -0.5)
    kg = _dot(n, wm_ref[:, 1792:2048])
    gf_ref[...], dec_f = _gla_operands(qg, kg, log_decay(slice(0, kw)), False)
    qa_ref[...] = (_dot(n, wm_ref[:, 0:512]) * (NA_HEAD_DIM ** -0.5 * LOG2E)).astype(BF16)
    kat_ref[...] = _dot_nt(wkt_ref[...], n).astype(BF16)
    gbw_ref[...], dec_b = _gla_operands(qg, kg, log_decay(slice(kw, 2 * kw)), True)
    dec_ref[...] = jnp.concatenate([dec_f, dec_b], axis=-1)
    va_ref[...] = _dot(n, wm_ref[:, 1024:1536]).astype(BF16)
    vg_ref[...] = _dot(n, wm_ref[:, 2048:2560]).astype(BF16)
    r = _dot(n, wm_ref[:, 2560:3072])
    sr_ref[...] = (r * jax.nn.sigmoid(r)).astype(BF16)


def _inproj(x, g, wm, wkt, wz, gu, gb, *, tm):
    B, T, D = x.shape
    kw = GLA_HEADS * GLA_DK
    nc = tm // GLA_CHUNK
    tok = lambda w: pl.BlockSpec((None, tm, w), lambda b, i: (b, i, 0))
    const = lambda a: pl.BlockSpec(a.shape, lambda b, i: (0,) * a.ndim,
                                   pipeline_mode=pl.Buffered(1))
    outs = [
        (jax.ShapeDtypeStruct((B, T, 512), BF16), tok(512)),
        (jax.ShapeDtypeStruct((B, 512, T), BF16),
         pl.BlockSpec((None, 512, tm), lambda b, i: (b, 0, i))),
        (jax.ShapeDtypeStruct((B, T, 512), BF16), tok(512)),
        (jax.ShapeDtypeStruct((B, T, 512), BF16), tok(512)),
        (jax.ShapeDtypeStruct((B, T, 512), BF16), tok(512)),
        (jax.ShapeDtypeStruct((B, T, 3 * kw), BF16), tok(3 * kw)),
        (jax.ShapeDtypeStruct((B, T, 3 * kw), BF16), tok(3 * kw)),
        (jax.ShapeDtypeStruct((B, T // GLA_CHUNK, 1, 2 * kw), F32),
         pl.BlockSpec((None, nc, 1, 2 * kw), lambda b, i: (b, i, 0, 0))),
    ]
    return pl.pallas_call(
        _inproj_kernel,
        grid=(B, T // tm),
        in_specs=[tok(D), const(g), const(wm), const(wkt), const(wz), const(gu), const(gb)],
        out_specs=[s for _, s in outs],
        out_shape=[s for s, _ in outs],
        compiler_params=pltpu.CompilerParams(
            dimension_semantics=("parallel", "parallel"), vmem_limit_bytes=VMEM_LIMIT),
        name="inproj",
    )(x, g, wm, wkt, wz, gu, gb)


def _na_bias_rows(rpb):
    n_dc = 2 * NA_KW - 1
    padded = jnp.pad(rpb.astype(F32), ((0, 0), (4, 5), (0, GRID_W - n_dc)))
    return jnp.concatenate([padded[:, :-1], padded[:, 1:]], axis=-1)


def _na_tiles(table, g):
    i_start = (0, g, NA_KEY_ROWS - NA_KH)[table]
    return range(i_start // 2, (i_start + NA_KH + 1) // 2)


def _na_build_tables(rows_ref, tab_ref):
    qc = lax.broadcasted_iota(jnp.int32, (GRID_W, LANES), 0)
    lane = lax.broadcasted_iota(jnp.int32, (GRID_W, LANES), 1)
    kc = lane & (GRID_W - 1)
    col_start = jnp.clip(qc - NA_KW // 2, 0, GRID_W - NA_KW)
    in_win = (kc >= col_start) & (kc < col_start + NA_KW)
    valid = {(True, True): in_win,
             (True, False): in_win & (lane < GRID_W),
             (False, True): in_win & (lane >= GRID_W)}
    row_offset = (0, -(NA_KH // 2), -NA_KH)
    for hh in range(2):
        for t in range(3):
            for g in range(NA_GROUP_ROWS):
                i_start = (0, g, NA_KEY_ROWS - NA_KH)[t]
                for m in _na_tiles(t, g):
                    halves = tuple(i_start <= i < i_start + NA_KH for i in (2 * m, 2 * m + 1))
                    a = 2 * m - g + row_offset[t] + NA_KH - 1 + 4
                    src = jnp.broadcast_to(rows_ref[hh, a:a + 1, :], (GRID_W, LANES))
                    rot = pltpu.roll(src, LANES - (NA_KW - 1), 1, stride=1, stride_axis=0)
                    tab_ref[hh, t, g * GRID_W:(g + 1) * GRID_W, m * LANES:(m + 1) * LANES] = (
                        jnp.where(valid[halves], rot * LOG2E, -jnp.inf))


class _NaRefs:
    def __init__(self, q, kt, v, bias, o, s, p, inv_l, n_groups):
        self.q, self.kt, self.v, self.bias, self.o = q, kt, v, bias, o
        self.s, self.p, self.inv_l, self.n_groups = s, p, inv_l, n_groups


def _na_rows(u):
    return pl.ds(pl.multiple_of(u * NA_GROUP, NA_GROUP), NA_GROUP)


def _na_key0(r, u):
    return pl.multiple_of(jnp.clip(u - 1, 0, r.n_groups - 3) * NA_GROUP, NA_GROUP)


def _na_scores(r, u):
    qq = r.q[_na_rows(u), :]
    first_head = lax.broadcasted_iota(jnp.int32, qq.shape, 1) < NA_HEAD_DIM
    zero = jnp.zeros_like(qq)
    q_heads = jnp.concatenate([jnp.where(first_head, qq, zero),
                               jnp.where(first_head, zero, qq)], axis=0)
    return _dot(q_heads, r.kt[:, pl.ds(_na_key0(r, u), NA_KEYS)])


def _na_softmax(r, s, table):
    p_tiles, inv_l = [], []
    for hh in range(2):
        for g in range(NA_GROUP_ROWS):
            rows = slice(hh * NA_GROUP + g * GRID_W, hh * NA_GROUP + (g + 1) * GRID_W)
            sb = [s[rows, c * LANES:(c + 1) * LANES]
                  + r.bias[hh, table, g * GRID_W:(g + 1) * GRID_W, c * LANES:(c + 1) * LANES]
                  for c in _na_tiles(table, g)]
            m = jnp.max(functools.reduce(jnp.maximum, sb), axis=-1, keepdims=True)
            p = [jnp.exp2(x - m) for x in sb]
            inv_l.append(1.0 / jnp.sum(functools.reduce(jnp.add, p), axis=-1, keepdims=True))
            p_tiles.append([x.astype(BF16) for x in p])
    return p_tiles, inv_l


def _na_store_output(r, u, o):
    first_head = lax.broadcasted_iota(jnp.int32, (NA_GROUP, LANES), 1) < NA_HEAD_DIM
    r.o[_na_rows(u), :] = jnp.where(first_head, o[:NA_GROUP], o[NA_GROUP:]).astype(r.o.dtype)


def _na_edge_group(r, u, table):
    tiles = _na_tiles(table, 0)
    p_tiles, inv_l = _na_softmax(r, _na_scores(r, u), table)
    p = jnp.concatenate([jnp.concatenate(t, axis=1) for t in p_tiles], axis=0)
    keys = pl.ds(_na_key0(r, u) + tiles.start * LANES, len(tiles) * LANES)
    _na_store_output(r, u, _dot(p, r.v[keys, :]) * jnp.concatenate(inv_l, axis=0))


def _na_block_rows(block):
    return slice(block * GRID_W, (block + 1) * GRID_W)


def _na_stage(r, u, slot):
    r.s[1 - slot][...] = _na_scores(r, jnp.minimum(u + 1, r.n_groups - 2))
    _na_stage_softmax(r, slot)
    _na_stage_output(r, u - 1, 1 - slot)


def _na_stage_softmax(r, slot):
    p_tiles, inv_l = _na_softmax(r, r.s[slot], 1)
    for block, (tiles, inv) in enumerate(zip(p_tiles, inv_l)):
        r.inv_l[slot][_na_block_rows(block), :] = inv
        for c, tile in zip(_na_tiles(1, block % NA_GROUP_ROWS), tiles):
            r.p[slot][_na_block_rows(block), c * LANES:(c + 1) * LANES] = tile


def _na_stage_output(r, u, slot):
    o = _dot(r.p[slot][...], r.v[pl.ds(_na_key0(r, u), NA_KEYS), :]) * r.inv_l[slot][...]
    _na_store_output(r, u, o)


def _na_kernel(q_ref, kt_ref, v_ref, rows_ref, o_ref, bias_ref, *slots, n_groups):
    _na_build_tables(rows_ref, bias_ref)
    r = _NaRefs(q_ref, kt_ref, v_ref, bias_ref, o_ref, slots[0:2], slots[2:4], slots[4:6], n_groups)
    last = n_groups - 1
    _na_edge_group(r, 0, 0)
    _na_edge_group(r, last, 2)
    for slot in range(2):
        for block in range(2 * NA_GROUP_ROWS):
            for c in set(range(NA_KEYS // LANES)) - set(_na_tiles(1, block % NA_GROUP_ROWS)):
                r.p[slot][_na_block_rows(block), c * LANES:(c + 1) * LANES] = (
                    jnp.zeros((GRID_W, LANES), BF16))
    r.s[1][...] = _na_scores(r, 1)
    _na_stage_softmax(r, 1)
    r.s[0][...] = _na_scores(r, 2)
    _na_stage(r, 2, 0)

    def two_stages(i, carry):
        _na_stage(r, 3 + 2 * i, 1)
        _na_stage(r, 4 + 2 * i, 0)
        return carry

    lax.fori_loop(0, (n_groups - 4) // 2, two_stages, 0)
    _na_stage_output(r, last - 1, (last - 1) % 2)


def _na(qa, kat, va, rows):
    B, T, _ = qa.shape
    n_pairs = NA_HEADS // 2
    return pl.pallas_call(
        functools.partial(_na_kernel, n_groups=T // NA_GROUP),
        grid=(B, n_pairs),
        in_specs=[
            pl.BlockSpec((None, T, LANES), lambda b, p: (b, 0, p)),
            pl.BlockSpec((None, LANES, T), lambda b, p: (b, p, 0)),
            pl.BlockSpec((None, T, LANES), lambda b, p: (b, 0, p)),
            pl.BlockSpec((2,) + rows.shape[1:], lambda b, p: (p, 0, 0)),
        ],
        out_specs=pl.BlockSpec((None, T, LANES), lambda b, p: (b, 0, p)),
        out_shape=jax.ShapeDtypeStruct((B, T, NA_HEADS * NA_HEAD_DIM), BF16),
        scratch_shapes=[pltpu.VMEM((2, 3, NA_GROUP, NA_KEYS), F32)
                        ] + [pltpu.VMEM((2 * NA_GROUP, NA_KEYS), F32)] * 2
                        + [pltpu.VMEM((2 * NA_GROUP, NA_KEYS), BF16)] * 2
                        + [pltpu.VMEM((2 * NA_GROUP, 1), F32)] * 2,
        compiler_params=pltpu.CompilerParams(
            dimension_semantics=("parallel", "parallel"), vmem_limit_bytes=VMEM_LIMIT),
        name="natten",
    )(qa, kat, va, rows)


class _GlaDir:
    def __init__(self, reverse, ops, v, dec, o, state, a, contrib, prev):
        kw = GLA_HEADS * GLA_DK
        self.reverse = reverse
        self.qd, self.ki, self.kd = (ops.at[:, i * kw:(i + 1) * kw] for i in range(3))
        self.v, self.dec, self.o, self.state = v, dec, o, state
        self.a, self.contrib, self.prev = a, contrib, prev


def _gla_intra(d, c):
    C = GLA_CHUNK
    rows = slice(c * C, (c + 1) * C)
    qd = d.qd[rows, :]
    lane = lax.broadcasted_iota(jnp.int32, qd.shape, 1)
    q_heads = jnp.concatenate(
        [jnp.where((lane >= h * GLA_DK) & (lane < (h + 1) * GLA_DK), qd, jnp.zeros_like(qd))
         for h in range(GLA_HEADS)], axis=0)
    scores = _dot_nt(q_heads, d.ki[rows, :])
    i = lax.broadcasted_iota(jnp.int32, scores.shape, 0) & (C - 1)
    j = lax.broadcasted_iota(jnp.int32, scores.shape, 1)
    keep = (j > i) if d.reverse else (j <= i)
    d.a[c] = jnp.where(keep, scores, 0.0).astype(BF16)
    first_head = lax.broadcasted_iota(jnp.int32, (GLA_DV, LANES), 1) < GLA_DK
    for p in range(GLA_HEADS // 2):
        kd_p = d.kd[rows, p * LANES:(p + 1) * LANES]
        c0, c1 = (_dot_tn(d.v[rows, h * GLA_DV:(h + 1) * GLA_DV], kd_p) for h in (2 * p, 2 * p + 1))
        d.contrib[c, p] = jnp.where(first_head, c0, c1)


def _gla_scan(d, n_chunks):
    order = range(n_chunks - 1, -1, -1) if d.reverse else range(n_chunks)
    for p in range(GLA_HEADS // 2):
        st = d.state[p]
        for c in order:
            d.prev[c, p] = st.astype(BF16)
            st = st * d.dec[c][:, p * LANES:(p + 1) * LANES] + d.contrib[c, p]
        d.state[p] = st


def _gla_outputs(d, c):
    C = GLA_CHUNK
    rows = slice(c * C, (c + 1) * C)
    first_head = lax.broadcasted_iota(jnp.int32, (C, LANES), 1) < GLA_DK
    for p in range(GLA_HEADS // 2):
        qd_p = d.qd[rows, p * LANES:(p + 1) * LANES]
        zero = jnp.zeros_like(qd_p)
        q_pair = jnp.concatenate([jnp.where(first_head, qd_p, zero),
                                  jnp.where(first_head, zero, qd_p)], axis=0)
        inter = _dot_nt(q_pair, d.prev[c, p])
        for hh in range(2):
            h = 2 * p + hh
            cols = slice(h * GLA_DV, (h + 1) * GLA_DV)
            d.o[rows, cols] = _dot(d.a[c, h * C:(h + 1) * C, :], d.v[rows, cols]) + inter[hh * C:(hh + 1) * C]


def _gla_kernel(gf_ref, vf_ref, decf_ref, gb_ref, vb_ref, decb_ref,
                of_ref, ob_ref, sf_ref, sb_ref, *scratch, n_chunks):
    @pl.when(pl.program_id(1) == 0)
    def _():
        sf_ref[...] = jnp.zeros_like(sf_ref)
        sb_ref[...] = jnp.zeros_like(sb_ref)

    n = len(scratch) // 2
    dirs = (_GlaDir(False, gf_ref, vf_ref, decf_ref, of_ref, sf_ref, *scratch[:n]),
            _GlaDir(True, gb_ref, vb_ref, decb_ref, ob_ref, sb_ref, *scratch[n:]))

    for c in range(n_chunks):
        for d in dirs:
            _gla_intra(d, c)
    for d in dirs:
        _gla_scan(d, n_chunks)
    for c in range(n_chunks):
        for d in dirs:
            _gla_outputs(d, c)


def _gla(gf, gb, vg, dec, *, tb):
    B, T, vw = vg.shape
    nb = T // tb
    nc = tb // GLA_CHUNK
    kw = GLA_HEADS * GLA_DK
    n_pairs = GLA_HEADS // 2
    fwd = lambda w: pl.BlockSpec((None, tb, w), lambda b, s: (b, s, 0))
    bwd = lambda w: pl.BlockSpec((None, tb, w), lambda b, s: (b, nb - 1 - s, 0))
    dec_fwd = pl.BlockSpec((None, nc, 1, kw), lambda b, s: (b, s, 0, 0))
    dec_bwd = pl.BlockSpec((None, nc, 1, kw), lambda b, s: (b, nb - 1 - s, 0, 1))
    state = pltpu.VMEM((n_pairs, GLA_DV, LANES), F32)
    per_dir = [
        pltpu.VMEM((nc, GLA_HEADS * GLA_CHUNK, GLA_CHUNK), BF16),
        pltpu.VMEM((nc, n_pairs, GLA_DV, LANES), F32),
        pltpu.VMEM((nc, n_pairs, GLA_DV, LANES), BF16)]
    return pl.pallas_call(
        functools.partial(_gla_kernel, n_chunks=nc),
        grid=(B, nb),
        in_specs=[fwd(3 * kw), fwd(vw), dec_fwd, bwd(3 * kw), bwd(vw), dec_bwd],
        out_specs=[fwd(vw), bwd(vw)],
        out_shape=[jax.ShapeDtypeStruct((B, T, vw), F32)] * 2,
        scratch_shapes=[state, state] + per_dir * 2,
        compiler_params=pltpu.CompilerParams(
            dimension_semantics=("parallel", "arbitrary"), vmem_limit_bytes=VMEM_LIMIT),
        name="gla",
    )(gf, vg, dec, gb, vg, dec)


def _mix_ffn_kernel(x_ref, yna_ref, of_ref, ob_ref, sr_ref, gng_ref, wout_ref, gff_ref,
                    w1_ref, w2_ref, gfin_ref, out_ref, *, final_norm, ff_chunk):
    o = of_ref[...] + ob_ref[...]
    parts = [_rms(o[:, h * GLA_DV:(h + 1) * GLA_DV], gng_ref[...]) for h in range(GLA_HEADS)]
    y_gla = (jnp.concatenate(parts, axis=-1) * sr_ref[...].astype(F32)).astype(BF16)
    na_w = yna_ref.shape[-1]
    mix = _dot(yna_ref[...], wout_ref[0:na_w, :]) + _dot(y_gla, wout_ref[na_w:, :])
    h1 = x_ref[...] + mix
    n2 = _rms(h1, gff_ref[...]).astype(BF16)
    ffn = None
    for f in range(w1_ref.shape[1] // ff_chunk):
        u = _dot(n2, w1_ref[:, f * ff_chunk:(f + 1) * ff_chunk])
        a = jnp.square(jnp.maximum(u, 0.0)).astype(BF16)
        d = _dot(a, w2_ref[f * ff_chunk:(f + 1) * ff_chunk, :])
        ffn = d if ffn is None else ffn + d
    h2 = h1 + ffn
    out_ref[...] = _rms(h2, gfin_ref[...]) if final_norm else h2


def _mix_ffn(x, yna, of, ob, sr, gng, wout, gff, w1, w2, gfin, *, tm, final_norm):
    B, T, D = x.shape
    tok = lambda w: pl.BlockSpec((None, tm, w), lambda b, i: (b, i, 0))
    const = lambda a: pl.BlockSpec(a.shape, lambda b, i: (0,) * a.ndim,
                                   pipeline_mode=pl.Buffered(1))
    return pl.pallas_call(
        functools.partial(_mix_ffn_kernel, final_norm=final_norm, ff_chunk=1024),
        grid=(B, T // tm),
        in_specs=[tok(D), tok(yna.shape[-1]), tok(of.shape[-1]), tok(ob.shape[-1]),
                  tok(sr.shape[-1]), const(gng), const(wout), const(gff), const(w1),
                  const(w2), const(gfin)],
        out_specs=tok(D),
        out_shape=jax.ShapeDtypeStruct((B, T, D), F32),
        compiler_params=pltpu.CompilerParams(
            dimension_semantics=("parallel", "parallel"), vmem_limit_bytes=VMEM_LIMIT),
        name="mix_ffn",
    )(x, yna, of, ob, sr, gng, wout, gff, w1, w2, gfin)


def kernel(x, ln_mix_g, w_in, na_rpb, gla_gate_up_fwd, gla_gate_bias_fwd, gla_gate_up_bwd,
           gla_gate_bias_bwd, gla_norm_g, w_out, ln_ff_g, w_ff1, w_ff2, ln_final_g):
    B, T, D = x.shape
    depth = w_in.shape[0]
    assert T % NA_GROUP == 0 and T // NA_GROUP >= 3 and T % 512 == 0
    n_main = w_in.shape[-1] - 2 * GLA_GATE_RANK
    kw = GLA_HEADS * GLA_DK
    row = lambda v: v.reshape(1, -1).astype(F32)
    h = x
    for l in range(depth):
        w = w_in[l]
        wm = w[:, :n_main].astype(BF16)
        wkt = w[:, 512:1024].T.astype(BF16)
        wz = w[:, n_main:].astype(BF16)
        zeros = jnp.zeros((GLA_GATE_RANK, kw), F32)
        gu = jnp.concatenate([
            jnp.concatenate([gla_gate_up_fwd[l], zeros], axis=1),
            jnp.concatenate([zeros, gla_gate_up_bwd[l]], axis=1)], axis=0).astype(BF16)
        gb = jnp.concatenate([gla_gate_bias_fwd[l], gla_gate_bias_bwd[l]]).reshape(1, -1)
        qa, kat, va, vg, sr, gla_f, gla_b, dec = _inproj(
            h, row(ln_mix_g[l]), wm, wkt, wz, gu, gb.astype(F32), tm=512)
        y_na = _na(qa, kat, va, _na_bias_rows(na_rpb[l]))
        o_f, o_b = _gla(gla_f, gla_b, vg, dec, tb=512)
        h = _mix_ffn(h, y_na, o_f, o_b, sr, row(gla_norm_g[l]), w_out[l].astype(BF16),
                     row(ln_ff_g[l]), w_ff1[l].astype(BF16), w_ff2[l].astype(BF16),
                     row(ln_final_g), tm=512, final_norm=(l == depth - 1))
    return h
```

```python
import functools

import jax
import jax.numpy as jnp
from jax import lax
from jax.experimental import pallas as pl
from jax.experimental.pallas import tpu as pltpu

F32 = jnp.float32
BF16 = jnp.bfloat16

EPS = 1e-6
GRID_W = 64
NA_HEADS = 8
NA_HEAD_DIM = 64
NA_KH = 8
NA_KW = 16
GLA_HEADS = 4
GLA_DK = 64
GLA_DV = 128
GLA_GATE_RANK = 16
GLA_GATE_NORM = 16.0
GLA_CHUNK = 64

LANES = 128
NA_GROUP_ROWS = 4
NA_GROUP = NA_GROUP_ROWS * GRID_W
NA_KEY_ROWS = 12
NA_KEYS = NA_KEY_ROWS * GRID_W
LOG2E = 1.4426950408889634
VMEM_LIMIT = 56 * 1024 * 1024


def _dot(a, b):
    return jnp.dot(a, b, preferred_element_type=F32)


def _dot_nt(a, b):
    return lax.dot_general(a, b, (((1,), (1,)), ((), ())), preferred_element_type=F32)


def _dot_tn(a, b):
    return lax.dot_general(a, b, (((0,), (0,)), ((), ())), preferred_element_type=F32)


def _rms(x, g):
    return x * lax.rsqrt(jnp.mean(x * x, axis=-1, keepdims=True) + EPS) * g


def _segmented_cumsum(x, reverse):
    n = x.shape[0]
    pos = lax.broadcasted_iota(jnp.int32, x.shape, 0) & (GLA_CHUNK - 1)
    step = 1
    while step < GLA_CHUNK:
        if reverse:
            shifted, ok = pltpu.roll(x, n - step, 0), pos < GLA_CHUNK - step
        else:
            shifted, ok = pltpu.roll(x, step, 0), pos >= step
        x = x + jnp.where(ok, shifted, 0.0)
        step *= 2
    return x


def _gla_cumdecay(log_a, reverse):
    b = _segmented_cumsum(log_a, reverse)
    b3 = b.reshape(-1, GLA_CHUNK, b.shape[-1])
    last = 0 if reverse else GLA_CHUNK - 1
    return b, b3[:, last:last + 1, :]


def _gla_operands(q, k, b, b_last):
    b3 = b.reshape(-1, GLA_CHUNK, b.shape[-1])
    k_dec = (k.reshape(b3.shape) * jnp.exp2(b_last - b3)).reshape(b.shape)
    ops = jnp.concatenate([q * jnp.exp2(b), k * jnp.exp2(-b), k_dec], axis=-1)
    return ops.astype(BF16), jnp.exp2(b_last)


def _inproj_kernel(x_ref, g_ref, wm_ref, wkt_ref, wz_ref, gu_ref, gb_ref,
                   qa_ref, kat_ref, va_ref, vg_ref, sr_ref, gf_ref, gbw_ref, dec_ref):
    kw = GLA_HEADS * GLA_DK
    n = _rms(x_ref[...], g_ref[...]).astype(BF16)

    def log_decay(cols):
        pre = _dot(z, gu_ref[:, cols]) + gb_ref[:, cols]
        softplus2 = jnp.log2(1.0 + jnp.exp2(jnp.abs(pre) * -LOG2E))
        return jnp.minimum(pre * (LOG2E / GLA_GATE_NORM), 0.0) - softplus2 * (1.0 / GLA_GATE_NORM)

    z = _dot(n, wz_ref[...]).astype(BF16)
    b_f, last_f = _gla_cumdecay(log_decay(slice(0, kw)), False)
    qa_ref[...] = (_dot(n, wm_ref[:, 0:512]) * (NA_HEAD_DIM ** -0.5 * LOG2E)).astype(BF16)
    b_b, last_b = _gla_cumdecay(log_decay(slice(kw, 2 * kw)), True)
    kat_ref[...] = _dot_nt(wkt_ref[...], n).astype(BF16)
    qg = _dot(n, wm_ref[:, 1536:1792]) * (GLA_DK ** -0.5)
    kg = _dot(n, wm_ref[:, 1792:2048])
    gf_ref[...], dec_f = _gla_operands(qg, kg, b_f, last_f)
    va_ref[...] = _dot(n, wm_ref[:, 1024:1536]).astype(BF16)
    gbw_ref[...], dec_b = _gla_operands(qg, kg, b_b, last_b)
    dec_ref[...] = jnp.concatenate([dec_f, dec_b], axis=-1)
    vg_ref[...] = _dot(n, wm_ref[:, 2048:2560]).astype(BF16)
    r = _dot(n, wm_ref[:, 2560:3072])
    sr_ref[...] = (r * jax.nn.sigmoid(r)).astype(BF16)


def _inproj(x, g, wm, wkt, wz, gu, gb, *, tm):
    B, T, D = x.shape
    kw = GLA_HEADS * GLA_DK
    nc = tm // GLA_CHUNK
    tok = lambda w: pl.BlockSpec((None, tm, w), lambda b, i: (b, i, 0))
    const = lambda a: pl.BlockSpec(a.shape, lambda b, i: (0,) * a.ndim,
                                   pipeline_mode=pl.Buffered(1))
    outs = [
        (jax.ShapeDtypeStruct((B, T, 512), BF16), tok(512)),
        (jax.ShapeDtypeStruct((B, 512, T), BF16),
         pl.BlockSpec((None, 512, tm), lambda b, i: (b, 0, i))),
        (jax.ShapeDtypeStruct((B, T, 512), BF16), tok(512)),
        (jax.ShapeDtypeStruct((B, T, 512), BF16), tok(512)),
        (jax.ShapeDtypeStruct((B, T, 512), BF16), tok(512)),
        (jax.ShapeDtypeStruct((B, T, 3 * kw), BF16), tok(3 * kw)),
        (jax.ShapeDtypeStruct((B, T, 3 * kw), BF16), tok(3 * kw)),
        (jax.ShapeDtypeStruct((B, T // GLA_CHUNK, 1, 2 * kw), F32),
         pl.BlockSpec((None, nc, 1, 2 * kw), lambda b, i: (b, i, 0, 0))),
    ]
    return pl.pallas_call(
        _inproj_kernel,
        grid=(B, T // tm),
        in_specs=[tok(D), const(g), const(wm), const(wkt), const(wz), const(gu), const(gb)],
        out_specs=[s for _, s in outs],
        out_shape=[s for s, _ in outs],
        compiler_params=pltpu.CompilerParams(
            dimension_semantics=("parallel", "parallel"), vmem_limit_bytes=VMEM_LIMIT),
        name="inproj",
    )(x, g, wm, wkt, wz, gu, gb)


def _na_bias_rows(rpb):
    n_dc = 2 * NA_KW - 1
    padded = jnp.pad(rpb.astype(F32), ((0, 0), (4, 5), (0, GRID_W - n_dc)))
    return jnp.concatenate([padded[:, :-1], padded[:, 1:]], axis=-1)


def _na_tiles(table, g):
    i_start = (0, g, NA_KEY_ROWS - NA_KH)[table]
    return range(i_start // 2, (i_start + NA_KH + 1) // 2)


def _na_build_tables(rows_ref, tab_ref):
    qc = lax.broadcasted_iota(jnp.int32, (GRID_W, LANES), 0)
    lane = lax.broadcasted_iota(jnp.int32, (GRID_W, LANES), 1)
    kc = lane & (GRID_W - 1)
    col_start = jnp.clip(qc - NA_KW // 2, 0, GRID_W - NA_KW)
    in_win = (kc >= col_start) & (kc < col_start + NA_KW)
    valid = {(True, True): in_win,
             (True, False): in_win & (lane < GRID_W),
             (False, True): in_win & (lane >= GRID_W)}
    row_offset = (0, -(NA_KH // 2), -NA_KH)
    for hh in range(2):
        for t in range(3):
            for g in range(NA_GROUP_ROWS):
                i_start = (0, g, NA_KEY_ROWS - NA_KH)[t]
                for m in _na_tiles(t, g):
                    halves = tuple(i_start <= i < i_start + NA_KH for i in (2 * m, 2 * m + 1))
                    a = 2 * m - g + row_offset[t] + NA_KH - 1 + 4
                    src = jnp.broadcast_to(rows_ref[hh, a:a + 1, :], (GRID_W, LANES))
                    rot = pltpu.roll(src, LANES - (NA_KW - 1), 1, stride=1, stride_axis=0)
                    tab_ref[hh, t, g * GRID_W:(g + 1) * GRID_W, m * LANES:(m + 1) * LANES] = (
                        jnp.where(valid[halves], rot * LOG2E, -jnp.inf))


class _NaRefs:
    def __init__(self, q, kt, v, bias, o, s, p, inv_l, n_groups):
        self.q, self.kt, self.v, self.bias, self.o = q, kt, v, bias, o
        self.s, self.p, self.inv_l, self.n_groups = s, p, inv_l, n_groups


def _na_rows(u):
    return pl.ds(pl.multiple_of(u * NA_GROUP, NA_GROUP), NA_GROUP)


def _na_key0(r, u):
    return pl.multiple_of(jnp.clip(u - 1, 0, r.n_groups - 3) * NA_GROUP, NA_GROUP)


def _na_scores(r, u):
    qq = r.q[_na_rows(u), :]
    first_head = lax.broadcasted_iota(jnp.int32, qq.shape, 1) < NA_HEAD_DIM
    zero = jnp.zeros_like(qq)
    q_heads = jnp.concatenate([jnp.where(first_head, qq, zero),
                               jnp.where(first_head, zero, qq)], axis=0)
    return _dot(q_heads, r.kt[:, pl.ds(_na_key0(r, u), NA_KEYS)])


def _na_softmax(r, s, table):
    p_tiles, inv_l = [], []
    for hh in range(2):
        for g in range(NA_GROUP_ROWS):
            rows = slice(hh * NA_GROUP + g * GRID_W, hh * NA_GROUP + (g + 1) * GRID_W)
            sb = [s[rows, c * LANES:(c + 1) * LANES]
                  + r.bias[hh, table, g * GRID_W:(g + 1) * GRID_W, c * LANES:(c + 1) * LANES]
                  for c in _na_tiles(table, g)]
            m = jnp.max(functools.reduce(jnp.maximum, sb), axis=-1, keepdims=True)
            p = [jnp.exp2(x - m) for x in sb]
            inv_l.append(1.0 / jnp.sum(functools.reduce(jnp.add, p), axis=-1, keepdims=True))
            p_tiles.append([x.astype(BF16) for x in p])
    return p_tiles, inv_l


def _na_store_output(r, u, o):
    first_head = lax.broadcasted_iota(jnp.int32, (NA_GROUP, LANES), 1) < NA_HEAD_DIM
    r.o[_na_rows(u), :] = jnp.where(first_head, o[:NA_GROUP], o[NA_GROUP:]).astype(r.o.dtype)


def _na_edge_group(r, u, table):
    tiles = _na_tiles(table, 0)
    p_tiles, inv_l = _na_softmax(r, _na_scores(r, u), table)
    p = jnp.concatenate([jnp.concatenate(t, axis=1) for t in p_tiles], axis=0)
    keys = pl.ds(_na_key0(r, u) + tiles.start * LANES, len(tiles) * LANES)
    _na_store_output(r, u, _dot(p, r.v[keys, :]) * jnp.concatenate(inv_l, axis=0))


def _na_block_rows(block):
    return slice(block * GRID_W, (block + 1) * GRID_W)


def _na_stage(r, u, slot):
    r.s[1 - slot][...] = _na_scores(r, jnp.minimum(u + 1, r.n_groups - 2))
    _na_stage_softmax(r, slot)
    _na_stage_output(r, u - 1, 1 - slot)


def _na_stage_softmax(r, slot):
    p_tiles, inv_l = _na_softmax(r, r.s[slot], 1)
    for block, (tiles, inv) in enumerate(zip(p_tiles, inv_l)):
        r.inv_l[slot][_na_block_rows(block), :] = inv
        for c, tile in zip(_na_tiles(1, block % NA_GROUP_ROWS), tiles):
            r.p[slot][_na_block_rows(block), c * LANES:(c + 1) * LANES] = tile


def _na_stage_output(r, u, slot):
    o = _dot(r.p[slot][...], r.v[pl.ds(_na_key0(r, u), NA_KEYS), :]) * r.inv_l[slot][...]
    _na_store_output(r, u, o)


def _na_kernel(q_ref, kt_ref, v_ref, rows_ref, o_ref, bias_ref, *slots, n_groups):
    _na_build_tables(rows_ref, bias_ref)
    r = _NaRefs(q_ref, kt_ref, v_ref, bias_ref, o_ref, slots[0:2], slots[2:4], slots[4:6], n_groups)
    last = n_groups - 1
    _na_edge_group(r, 0, 0)
    _na_edge_group(r, last, 2)
    for slot in range(2):
        for block in range(2 * NA_GROUP_ROWS):
            for c in set(range(NA_KEYS // LANES)) - set(_na_tiles(1, block % NA_GROUP_ROWS)):
                r.p[slot][_na_block_rows(block), c * LANES:(c + 1) * LANES] = (
                    jnp.zeros((GRID_W, LANES), BF16))
    r.s[1][...] = _na_scores(r, 1)
    _na_stage_softmax(r, 1)
    r.s[0][...] = _na_scores(r, 2)
    _na_stage(r, 2, 0)

    def two_stages(i, carry):
        _na_stage(r, 3 + 2 * i, 1)
        _na_stage(r, 4 + 2 * i, 0)
        return carry

    lax.fori_loop(0, (n_groups - 4) // 2, two_stages, 0)
    _na_stage_output(r, last - 1, (last - 1) % 2)


def _na(qa, kat, va, rows):
    B, T, _ = qa.shape
    n_pairs = NA_HEADS // 2
    return pl.pallas_call(
        functools.partial(_na_kernel, n_groups=T // NA_GROUP),
        grid=(B, n_pairs),
        in_specs=[
            pl.BlockSpec((None, T, LANES), lambda b, p: (b, 0, p)),
            pl.BlockSpec((None, LANES, T), lambda b, p: (b, p, 0)),
            pl.BlockSpec((None, T, LANES), lambda b, p: (b, 0, p)),
            pl.BlockSpec((2,) + rows.shape[1:], lambda b, p: (p, 0, 0)),
        ],
        out_specs=pl.BlockSpec((None, T, LANES), lambda b, p: (b, 0, p)),
        out_shape=jax.ShapeDtypeStruct((B, T, NA_HEADS * NA_HEAD_DIM), BF16),
        scratch_shapes=[pltpu.VMEM((2, 3, NA_GROUP, NA_KEYS), F32)
                        ] + [pltpu.VMEM((2 * NA_GROUP, NA_KEYS), F32)] * 2
                        + [pltpu.VMEM((2 * NA_GROUP, NA_KEYS), BF16)] * 2
                        + [pltpu.VMEM((2 * NA_GROUP, 1), F32)] * 2,
        compiler_params=pltpu.CompilerParams(
            dimension_semantics=("parallel", "parallel"), vmem_limit_bytes=VMEM_LIMIT),
        name="natten",
    )(qa, kat, va, rows)


class _GlaDir:
    def __init__(self, reverse, ops, v, dec, o, state, a, contrib, prev):
        kw = GLA_HEADS * GLA_DK
        self.reverse = reverse
        self.qd, self.ki, self.kd = (ops.at[:, i * kw:(i + 1) * kw] for i in range(3))
        self.v, self.dec, self.o, self.state = v, dec, o, state
        self.a, self.contrib, self.prev = a, contrib, prev


def _gla_intra(d, c):
    C = GLA_CHUNK
    rows = slice(c * C, (c + 1) * C)
    qd = d.qd[rows, :]
    lane = lax.broadcasted_iota(jnp.int32, qd.shape, 1)
    q_heads = jnp.concatenate(
        [jnp.where((lane >= h * GLA_DK) & (lane < (h + 1) * GLA_DK), qd, jnp.zeros_like(qd))
         for h in range(GLA_HEADS)], axis=0)
    scores = _dot_nt(q_heads, d.ki[rows, :])
    i = lax.broadcasted_iota(jnp.int32, scores.shape, 0) & (C - 1)
    j = lax.broadcasted_iota(jnp.int32, scores.shape, 1)
    keep = (j > i) if d.reverse else (j <= i)
    d.a[c] = jnp.where(keep, scores, 0.0).astype(BF16)
    first_head = lax.broadcasted_iota(jnp.int32, (GLA_DV, LANES), 1) < GLA_DK
    for p in range(GLA_HEADS // 2):
        kd_p = d.kd[rows, p * LANES:(p + 1) * LANES]
        c0, c1 = (_dot_tn(d.v[rows, h * GLA_DV:(h + 1) * GLA_DV], kd_p) for h in (2 * p, 2 * p + 1))
        d.contrib[c, p] = jnp.where(first_head, c0, c1)


def _gla_scan(d, n_chunks):
    order = range(n_chunks - 1, -1, -1) if d.reverse else range(n_chunks)
    for p in range(GLA_HEADS // 2):
        st = d.state[p]
        for c in order:
            d.prev[c, p] = st.astype(BF16)
            st = st * d.dec[c][:, p * LANES:(p + 1) * LANES] + d.contrib[c, p]
        d.state[p] = st


def _gla_outputs(d, c):
    C = GLA_CHUNK
    rows = slice(c * C, (c + 1) * C)
    first_head = lax.broadcasted_iota(jnp.int32, (C, LANES), 1) < GLA_DK
    for p in range(GLA_HEADS // 2):
        qd_p = d.qd[rows, p * LANES:(p + 1) * LANES]
        zero = jnp.zeros_like(qd_p)
        q_pair = jnp.concatenate([jnp.where(first_head, qd_p, zero),
                                  jnp.where(first_head, zero, qd_p)], axis=0)
        inter = _dot_nt(q_pair, d.prev[c, p])
        for hh in range(2):
            h = 2 * p + hh
            cols = slice(h * GLA_DV, (h + 1) * GLA_DV)
            d.o[rows, cols] = _dot(d.a[c, h * C:(h + 1) * C, :], d.v[rows, cols]) + inter[hh * C:(hh + 1) * C]


def _gla_kernel(gf_ref, vf_ref, decf_ref, gb_ref, vb_ref, decb_ref,
                of_ref, ob_ref, sf_ref, sb_ref, *scratch, n_chunks):
    @pl.when(pl.program_id(1) == 0)
    def _():
        sf_ref[...] = jnp.zeros_like(sf_ref)
        sb_ref[...] = jnp.zeros_like(sb_ref)

    n = len(scratch) // 2
    dirs = (_GlaDir(False, gf_ref, vf_ref, decf_ref, of_ref, sf_ref, *scratch[:n]),
            _GlaDir(True, gb_ref, vb_ref, decb_ref, ob_ref, sb_ref, *scratch[n:]))

    for c in range(n_chunks):
        for d in dirs:
            _gla_intra(d, c)
    for d in dirs:
        _gla_scan(d, n_chunks)
    for c in range(n_chunks):
        for d in dirs:
            _gla_outputs(d, c)


def _gla(gf, gb, vg, dec, *, tb):
    B, T, vw = vg.shape
    nb = T // tb
    nc = tb // GLA_CHUNK
    kw = GLA_HEADS * GLA_DK
    n_pairs = GLA_HEADS // 2
    fwd = lambda w: pl.BlockSpec((None, tb, w), lambda b, s: (b, s, 0))
    bwd = lambda w: pl.BlockSpec((None, tb, w), lambda b, s: (b, nb - 1 - s, 0))
    dec_fwd = pl.BlockSpec((None, nc, 1, kw), lambda b, s: (b, s, 0, 0))
    dec_bwd = pl.BlockSpec((None, nc, 1, kw), lambda b, s: (b, nb - 1 - s, 0, 1))
    state = pltpu.VMEM((n_pairs, GLA_DV, LANES), F32)
    per_dir = [
        pltpu.VMEM((nc, GLA_HEADS * GLA_CHUNK, GLA_CHUNK), BF16),
        pltpu.VMEM((nc, n_pairs, GLA_DV, LANES), F32),
        pltpu.VMEM((nc, n_pairs, GLA_DV, LANES), BF16)]
    return pl.pallas_call(
        functools.partial(_gla_kernel, n_chunks=nc),
        grid=(B, nb),
        in_specs=[fwd(3 * kw), fwd(vw), dec_fwd, bwd(3 * kw), bwd(vw), dec_bwd],
        out_specs=[fwd(vw), bwd(vw)],
        out_shape=[jax.ShapeDtypeStruct((B, T, vw), F32)] * 2,
        scratch_shapes=[state, state] + per_dir * 2,
        compiler_params=pltpu.CompilerParams(
            dimension_semantics=("parallel", "arbitrary"), vmem_limit_bytes=VMEM_LIMIT),
        name="gla",
    )(gf, vg, dec, gb, vg, dec)


def _mix_ffn_kernel(x_ref, yna_ref, of_ref, ob_ref, sr_ref, gng_ref, wout_ref, gff_ref,
                    w1_ref, w2_ref, gfin_ref, out_ref, *, final_norm, ff_chunk):
    o = of_ref[...] + ob_ref[...]
    parts = [_rms(o[:, h * GLA_DV:(h + 1) * GLA_DV], gng_ref[...]) for h in range(GLA_HEADS)]
    y_gla = (jnp.concatenate(parts, axis=-1) * sr_ref[...].astype(F32)).astype(BF16)
    na_w = yna_ref.shape[-1]
    mix = _dot(yna_ref[...], wout_ref[0:na_w, :]) + _dot(y_gla, wout_ref[na_w:, :])
    h1 = x_ref[...] + mix
    n2 = _rms(h1, gff_ref[...]).astype(BF16)
    ffn = None
    for f in range(w1_ref.shape[1] // ff_chunk):
        u = _dot(n2, w1_ref[:, f * ff_chunk:(f + 1) * ff_chunk])
        a = jnp.square(jnp.maximum(u, 0.0)).astype(BF16)
        d = _dot(a, w2_ref[f * ff_chunk:(f + 1) * ff_chunk, :])
        ffn = d if ffn is None else ffn + d
    h2 = h1 + ffn
    out_ref[...] = _rms(h2, gfin_ref[...]) if final_norm else h2


def _mix_ffn(x, yna, of, ob, sr, gng, wout, gff, w1, w2, gfin, *, tm, final_norm):
    B, T, D = x.shape
    tok = lambda w: pl.BlockSpec((None, tm, w), lambda b, i: (b, i, 0))
    const = lambda a: pl.BlockSpec(a.shape, lambda b, i: (0,) * a.ndim,
                                   pipeline_mode=pl.Buffered(1))
    return pl.pallas_call(
        functools.partial(_mix_ffn_kernel, final_norm=final_norm, ff_chunk=1024),
        grid=(B, T // tm),
        in_specs=[tok(D), tok(yna.shape[-1]), tok(of.shape[-1]), tok(ob.shape[-1]),
                  tok(sr.shape[-1]), const(gng), const(wout), const(gff), const(w1),
                  const(w2), const(gfin)],
        out_specs=tok(D),
        out_shape=jax.ShapeDtypeStruct((B, T, D), F32),
        compiler_params=pltpu.CompilerParams(
            dimension_semantics=("parallel", "parallel"), vmem_limit_bytes=VMEM_LIMIT),
        name="mix_ffn",
    )(x, yna, of, ob, sr, gng, wout, gff, w1, w2, gfin)


def kernel(x, ln_mix_g, w_in, na_rpb, gla_gate_up_fwd, gla_gate_bias_fwd, gla_gate_up_bwd,
           gla_gate_bias_bwd, gla_norm_g, w_out, ln_ff_g, w_ff1, w_ff2, ln_final_g):
    B, T, D = x.shape
    depth = w_in.shape[0]
    assert T % NA_GROUP == 0 and T // NA_GROUP >= 3 and T % 512 == 0
    n_main = w_in.shape[-1] - 2 * GLA_GATE_RANK
    kw = GLA_HEADS * GLA_DK
    row = lambda v: v.reshape(1, -1).astype(F32)
    h = x
    for l in range(depth):
        w = w_in[l]
        wm = w[:, :n_main].astype(BF16)
        wkt = w[:, 512:1024].T.astype(BF16)
        wz = w[:, n_main:].astype(BF16)
        zeros = jnp.zeros((GLA_GATE_RANK, kw), F32)
        gu = jnp.concatenate([
            jnp.concatenate([gla_gate_up_fwd[l], zeros], axis=1),
            jnp.concatenate([zeros, gla_gate_up_bwd[l]], axis=1)], axis=0).astype(BF16)
        gb = jnp.concatenate([gla_gate_bias_fwd[l], gla_gate_bias_bwd[l]]).reshape(1, -1)
        qa, kat, va, vg, sr, gla_f, gla_b, dec = _inproj(
            h, row(ln_mix_g[l]), wm, wkt, wz, gu, gb.astype(F32), tm=512)
        y_na = _na(qa, kat, va, _na_bias_rows(na_rpb[l]))
        o_f, o_b = _gla(gla_f, gla_b, vg, dec, tb=512)
        h = _mix_ffn(h, y_na, o_f, o_b, sr, row(gla_norm_g[l]), w_out[l].astype(BF16),
                     row(ln_ff_g[l]), w_ff1[l].astype(BF16), w_ff2[l].astype(BF16),
                     row(ln_final_g), tm=512, final_norm=(l == depth - 1))
    return h
```

```python
import functools

import jax
import jax.numpy as jnp
from jax import lax
from jax.experimental import pallas as pl
from jax.experimental.pallas import tpu as pltpu

F32 = jnp.float32
BF16 = jnp.bfloat16

EPS = 1e-6
GRID_W = 64
NA_HEADS = 8
NA_HEAD_DIM = 64
NA_KH = 8
NA_KW = 16
GLA_HEADS = 4
GLA_DK = 64
GLA_DV = 128
GLA_GATE_RANK = 16
GLA_GATE_NORM = 16.0
GLA_CHUNK = 64

LANES = 128
NA_GROUP_ROWS = 4
NA_GROUP = NA_GROUP_ROWS * GRID_W
NA_KEY_ROWS = 12
NA_KEYS = NA_KEY_ROWS * GRID_W
NA_UNROLL = 4
LOG2E = 1.4426950408889634
VMEM_LIMIT = 56 * 1024 * 1024


def _dot(a, b):
    return jnp.dot(a, b, preferred_element_type=F32)


def _dot_nt(a, b):
    return lax.dot_general(a, b, (((1,), (1,)), ((), ())), preferred_element_type=F32)


def _dot_tn(a, b):
    return lax.dot_general(a, b, (((0,), (0,)), ((), ())), preferred_element_type=F32)


def _rms(x, g):
    return x * lax.rsqrt(jnp.mean(x * x, axis=-1, keepdims=True) + EPS) * g


def _segmented_cumsum(x, reverse):
    n = x.shape[0]
    pos = lax.broadcasted_iota(jnp.int32, x.shape, 0) & (GLA_CHUNK - 1)
    step = 1
    while step < GLA_CHUNK:
        if reverse:
            shifted, ok = pltpu.roll(x, n - step, 0), pos < GLA_CHUNK - step
        else:
            shifted, ok = pltpu.roll(x, step, 0), pos >= step
        x = x + jnp.where(ok, shifted, 0.0)
        step *= 2
    return x


def _gla_cumdecay(log_a, reverse):
    b = _segmented_cumsum(log_a, reverse)
    b3 = b.reshape(-1, GLA_CHUNK, b.shape[-1])
    last = 0 if reverse else GLA_CHUNK - 1
    return b, b3[:, last:last + 1, :]


def _gla_operands(q, k, b, b_last):
    b3 = b.reshape(-1, GLA_CHUNK, b.shape[-1])
    k_dec = (k.reshape(b3.shape) * jnp.exp2(b_last - b3)).reshape(b.shape)
    ops = jnp.concatenate([q * jnp.exp2(b), k * jnp.exp2(-b), k_dec], axis=-1)
    return ops.astype(BF16), jnp.exp2(b_last)


def _inproj_kernel(x_ref, g_ref, wm_ref, wkt_ref, wz_ref, gu_ref, gb_ref,
                   qa_ref, kat_ref, va_ref, vg_ref, sr_ref, gf_ref, gbw_ref, dec_ref):
    kw = GLA_HEADS * GLA_DK
    n = _rms(x_ref[...], g_ref[...]).astype(BF16)

    def log_decay(cols):
        pre = _dot(z, gu_ref[:, cols]) + gb_ref[:, cols]
        softplus2 = jnp.log2(1.0 + jnp.exp2(jnp.abs(pre) * -LOG2E))
        return jnp.minimum(pre * (LOG2E / GLA_GATE_NORM), 0.0) - softplus2 * (1.0 / GLA_GATE_NORM)

    z = _dot(n, wz_ref[...]).astype(BF16)
    b_f, last_f = _gla_cumdecay(log_decay(slice(0, kw)), False)
    qa_ref[...] = (_dot(n, wm_ref[:, 0:512]) * (NA_HEAD_DIM ** -0.5 * LOG2E)).astype(BF16)
    b_b, last_b = _gla_cumdecay(log_decay(slice(kw, 2 * kw)), True)
    kat_ref[...] = _dot_nt(wkt_ref[...], n).astype(BF16)
    qg = _dot(n, wm_ref[:, 1536:1792]) * (GLA_DK ** -0.5)
    kg = _dot(n, wm_ref[:, 1792:2048])
    gf_ref[...], dec_f = _gla_operands(qg, kg, b_f, last_f)
    va_ref[...] = _dot(n, wm_ref[:, 1024:1536]).astype(BF16)
    gbw_ref[...], dec_b = _gla_operands(qg, kg, b_b, last_b)
    dec_ref[...] = jnp.concatenate([dec_f, dec_b], axis=-1)
    vg_ref[...] = _dot(n, wm_ref[:, 2048:2560]).astype(BF16)
    r = _dot(n, wm_ref[:, 2560:3072])
    sr_ref[...] = (r * jax.nn.sigmoid(r)).astype(BF16)


def _inproj(x, g, wm, wkt, wz, gu, gb, *, tm):
    B, T, D = x.shape
    kw = GLA_HEADS * GLA_DK
    nc = tm // GLA_CHUNK
    tok = lambda w: pl.BlockSpec((None, tm, w), lambda b, i: (b, i, 0))
    const = lambda a: pl.BlockSpec(a.shape, lambda b, i: (0,) * a.ndim,
                                   pipeline_mode=pl.Buffered(1))
    outs = [
        (jax.ShapeDtypeStruct((B, T, 512), BF16), tok(512)),
        (jax.ShapeDtypeStruct((B, 512, T), BF16),
         pl.BlockSpec((None, 512, tm), lambda b, i: (b, 0, i))),
        (jax.ShapeDtypeStruct((B, T, 512), BF16), tok(512)),
        (jax.ShapeDtypeStruct((B, T, 512), BF16), tok(512)),
        (jax.ShapeDtypeStruct((B, T, 512), BF16), tok(512)),
        (jax.ShapeDtypeStruct((B, T, 3 * kw), BF16), tok(3 * kw)),
        (jax.ShapeDtypeStruct((B, T, 3 * kw), BF16), tok(3 * kw)),
        (jax.ShapeDtypeStruct((B, T // GLA_CHUNK, 1, 2 * kw), F32),
         pl.BlockSpec((None, nc, 1, 2 * kw), lambda b, i: (b, i, 0, 0))),
    ]
    return pl.pallas_call(
        _inproj_kernel,
        grid=(B, T // tm),
        in_specs=[tok(D), const(g), const(wm), const(wkt), const(wz), const(gu), const(gb)],
        out_specs=[s for _, s in outs],
        out_shape=[s for s, _ in outs],
        compiler_params=pltpu.CompilerParams(
            dimension_semantics=("parallel", "parallel"), vmem_limit_bytes=VMEM_LIMIT),
        name="inproj",
    )(x, g, wm, wkt, wz, gu, gb)


def _na_bias_rows(rpb):
    n_dc = 2 * NA_KW - 1
    padded = jnp.pad(rpb.astype(F32), ((0, 0), (4, 5), (0, GRID_W - n_dc)))
    return jnp.concatenate([padded[:, :-1], padded[:, 1:]], axis=-1)


def _na_tiles(table, g):
    i_start = (0, g, NA_KEY_ROWS - NA_KH)[table]
    return range(i_start // 2, (i_start + NA_KH + 1) // 2)


def _na_build_tables(rows_ref, tab_ref):
    qc = lax.broadcasted_iota(jnp.int32, (GRID_W, LANES), 0)
    lane = lax.broadcasted_iota(jnp.int32, (GRID_W, LANES), 1)
    kc = lane & (GRID_W - 1)
    col_start = jnp.clip(qc - NA_KW // 2, 0, GRID_W - NA_KW)
    in_win = (kc >= col_start) & (kc < col_start + NA_KW)
    valid = {(True, True): in_win,
             (True, False): in_win & (lane < GRID_W),
             (False, True): in_win & (lane >= GRID_W)}
    row_offset = (0, -(NA_KH // 2), -NA_KH)
    for hh in range(2):
        for t in range(3):
            for g in range(NA_GROUP_ROWS):
                i_start = (0, g, NA_KEY_ROWS - NA_KH)[t]
                for m in _na_tiles(t, g):
                    halves = tuple(i_start <= i < i_start + NA_KH for i in (2 * m, 2 * m + 1))
                    a = 2 * m - g + row_offset[t] + NA_KH - 1 + 4
                    src = jnp.broadcast_to(rows_ref[hh, a:a + 1, :], (GRID_W, LANES))
                    rot = pltpu.roll(src, LANES - (NA_KW - 1), 1, stride=1, stride_axis=0)
                    tab_ref[hh, t, g * GRID_W:(g + 1) * GRID_W, m * LANES:(m + 1) * LANES] = (
                        jnp.where(valid[halves], rot * LOG2E, -jnp.inf))


class _NaRefs:
    def __init__(self, q, kt, v, bias, o, s, p, inv_l, n_groups):
        self.q, self.kt, self.v, self.bias, self.o = q, kt, v, bias, o
        self.s, self.p, self.inv_l, self.n_groups = s, p, inv_l, n_groups


def _na_rows(u):
    return pl.ds(pl.multiple_of(u * NA_GROUP, NA_GROUP), NA_GROUP)


def _na_key0(r, u):
    return pl.multiple_of(jnp.clip(u - 1, 0, r.n_groups - 3) * NA_GROUP, NA_GROUP)


def _na_scores(r, u):
    qq = r.q[_na_rows(u), :]
    first_head = lax.broadcasted_iota(jnp.int32, qq.shape, 1) < NA_HEAD_DIM
    zero = jnp.zeros_like(qq)
    q_heads = jnp.concatenate([jnp.where(first_head, qq, zero),
                               jnp.where(first_head, zero, qq)], axis=0)
    return _dot(q_heads, r.kt[:, pl.ds(_na_key0(r, u), NA_KEYS)])


def _na_softmax(r, s, table):
    p_tiles, inv_l = [], []
    for hh in range(2):
        for g in range(NA_GROUP_ROWS):
            rows = slice(hh * NA_GROUP + g * GRID_W, hh * NA_GROUP + (g + 1) * GRID_W)
            sb = [s[rows, c * LANES:(c + 1) * LANES]
                  + r.bias[hh, table, g * GRID_W:(g + 1) * GRID_W, c * LANES:(c + 1) * LANES]
                  for c in _na_tiles(table, g)]
            m = jnp.max(functools.reduce(jnp.maximum, sb), axis=-1, keepdims=True)
            p = [jnp.exp2(x - m) for x in sb]
            inv_l.append(1.0 / jnp.sum(functools.reduce(jnp.add, p), axis=-1, keepdims=True))
            p_tiles.append([x.astype(BF16) for x in p])
    return p_tiles, inv_l


def _na_store_output(r, u, o):
    first_head = lax.broadcasted_iota(jnp.int32, (NA_GROUP, LANES), 1) < NA_HEAD_DIM
    r.o[_na_rows(u), :] = jnp.where(first_head, o[:NA_GROUP], o[NA_GROUP:]).astype(r.o.dtype)


def _na_edge_group(r, u, table):
    tiles = _na_tiles(table, 0)
    p_tiles, inv_l = _na_softmax(r, _na_scores(r, u), table)
    p = jnp.concatenate([jnp.concatenate(t, axis=1) for t in p_tiles], axis=0)
    keys = pl.ds(_na_key0(r, u) + tiles.start * LANES, len(tiles) * LANES)
    _na_store_output(r, u, _dot(p, r.v[keys, :]) * jnp.concatenate(inv_l, axis=0))


def _na_block_rows(block):
    return slice(block * GRID_W, (block + 1) * GRID_W)


def _na_stage(r, u, slot):
    r.s[1 - slot][...] = _na_scores(r, jnp.minimum(u + 1, r.n_groups - 2))
    _na_stage_softmax(r, slot)
    _na_stage_output(r, u - 1, 1 - slot)


def _na_stage_softmax(r, slot):
    p_tiles, inv_l = _na_softmax(r, r.s[slot], 1)
    for block, (tiles, inv) in enumerate(zip(p_tiles, inv_l)):
        r.inv_l[slot][_na_block_rows(block), :] = inv
        for c, tile in zip(_na_tiles(1, block % NA_GROUP_ROWS), tiles):
            r.p[slot][_na_block_rows(block), c * LANES:(c + 1) * LANES] = tile


def _na_stage_output(r, u, slot):
    o = _dot(r.p[slot][...], r.v[pl.ds(_na_key0(r, u), NA_KEYS), :]) * r.inv_l[slot][...]
    _na_store_output(r, u, o)


def _na_kernel(q_ref, kt_ref, v_ref, rows_ref, o_ref, bias_ref, *slots, n_groups):
    _na_build_tables(rows_ref, bias_ref)
    r = _NaRefs(q_ref, kt_ref, v_ref, bias_ref, o_ref, slots[0:2], slots[2:4], slots[4:6], n_groups)
    last = n_groups - 1
    _na_edge_group(r, 0, 0)
    _na_edge_group(r, last, 2)
    for slot in range(2):
        for block in range(2 * NA_GROUP_ROWS):
            for c in set(range(NA_KEYS // LANES)) - set(_na_tiles(1, block % NA_GROUP_ROWS)):
                r.p[slot][_na_block_rows(block), c * LANES:(c + 1) * LANES] = (
                    jnp.zeros((GRID_W, LANES), BF16))
    r.s[1][...] = _na_scores(r, 1)
    _na_stage_softmax(r, 1)
    r.s[0][...] = _na_scores(r, 2)
    _na_stage(r, 2, 0)

    def stages(first, count):
        for k in range(count):
            _na_stage(r, first + k, (3 + k) % 2)

    def unrolled_stages(i, carry):
        stages(3 + NA_UNROLL * i, NA_UNROLL)
        return carry

    n_loops, n_tail = divmod(n_groups - 4, NA_UNROLL)
    lax.fori_loop(0, n_loops, unrolled_stages, 0)
    stages(3 + NA_UNROLL * n_loops, n_tail)
    _na_stage_output(r, last - 1, (last - 1) % 2)


def _na(qa, kat, va, rows):
    B, T, _ = qa.shape
    n_pairs = NA_HEADS // 2
    return pl.pallas_call(
        functools.partial(_na_kernel, n_groups=T // NA_GROUP),
        grid=(B, n_pairs),
        in_specs=[
            pl.BlockSpec((None, T, LANES), lambda b, p: (b, 0, p)),
            pl.BlockSpec((None, LANES, T), lambda b, p: (b, p, 0)),
            pl.BlockSpec((None, T, LANES), lambda b, p: (b, 0, p)),
            pl.BlockSpec((2,) + rows.shape[1:], lambda b, p: (p, 0, 0)),
        ],
        out_specs=pl.BlockSpec((None, T, LANES), lambda b, p: (b, 0, p)),
        out_shape=jax.ShapeDtypeStruct((B, T, NA_HEADS * NA_HEAD_DIM), BF16),
        scratch_shapes=[pltpu.VMEM((2, 3, NA_GROUP, NA_KEYS), F32)
                        ] + [pltpu.VMEM((2 * NA_GROUP, NA_KEYS), F32)] * 2
                        + [pltpu.VMEM((2 * NA_GROUP, NA_KEYS), BF16)] * 2
                        + [pltpu.VMEM((2 * NA_GROUP, 1), F32)] * 2,
        compiler_params=pltpu.CompilerParams(
            dimension_semantics=("parallel", "parallel"), vmem_limit_bytes=VMEM_LIMIT),
        name="natten",
    )(qa, kat, va, rows)


class _GlaDir:
    def __init__(self, reverse, ops, v, dec, o, state, a, contrib, prev):
        kw = GLA_HEADS * GLA_DK
        self.reverse = reverse
        self.qd, self.ki, self.kd = (ops.at[:, i * kw:(i + 1) * kw] for i in range(3))
        self.v, self.dec, self.o, self.state = v, dec, o, state
        self.a, self.contrib, self.prev = a, contrib, prev


def _gla_intra(d, c):
    C = GLA_CHUNK
    rows = slice(c * C, (c + 1) * C)
    qd = d.qd[rows, :]
    lane = lax.broadcasted_iota(jnp.int32, qd.shape, 1)
    q_heads = jnp.concatenate(
        [jnp.where((lane >= h * GLA_DK) & (lane < (h + 1) * GLA_DK), qd, jnp.zeros_like(qd))
         for h in range(GLA_HEADS)], axis=0)
    scores = _dot_nt(q_heads, d.ki[rows, :])
    i = lax.broadcasted_iota(jnp.int32, scores.shape, 0) & (C - 1)
    j = lax.broadcasted_iota(jnp.int32, scores.shape, 1)
    keep = (j > i) if d.reverse else (j <= i)
    d.a[c] = jnp.where(keep, scores, 0.0).astype(BF16)
    first_head = lax.broadcasted_iota(jnp.int32, (GLA_DV, LANES), 1) < GLA_DK
    for p in range(GLA_HEADS // 2):
        kd_p = d.kd[rows, p * LANES:(p + 1) * LANES]
        c0, c1 = (_dot_tn(d.v[rows, h * GLA_DV:(h + 1) * GLA_DV], kd_p) for h in (2 * p, 2 * p + 1))
        d.contrib[c, p] = jnp.where(first_head, c0, c1)


def _gla_scan(d, n_chunks):
    order = range(n_chunks - 1, -1, -1) if d.reverse else range(n_chunks)
    for p in range(GLA_HEADS // 2):
        st = d.state[p]
        for c in order:
            d.prev[c, p] = st.astype(BF16)
            st = st * d.dec[c][:, p * LANES:(p + 1) * LANES] + d.contrib[c, p]
        d.state[p] = st


def _gla_outputs(d, c):
    C = GLA_CHUNK
    rows = slice(c * C, (c + 1) * C)
    first_head = lax.broadcasted_iota(jnp.int32, (C, LANES), 1) < GLA_DK
    for p in range(GLA_HEADS // 2):
        qd_p = d.qd[rows, p * LANES:(p + 1) * LANES]
        zero = jnp.zeros_like(qd_p)
        q_pair = jnp.concatenate([jnp.where(first_head, qd_p, zero),
                                  jnp.where(first_head, zero, qd_p)], axis=0)
        inter = _dot_nt(q_pair, d.prev[c, p])
        for hh in range(2):
            h = 2 * p + hh
            cols = slice(h * GLA_DV, (h + 1) * GLA_DV)
            d.o[rows, cols] = _dot(d.a[c, h * C:(h + 1) * C, :], d.v[rows, cols]) + inter[hh * C:(hh + 1) * C]


def _gla_kernel(gf_ref, vf_ref, decf_ref, gb_ref, vb_ref, decb_ref,
                of_ref, ob_ref, sf_ref, sb_ref, *scratch, n_chunks):
    @pl.when(pl.program_id(1) == 0)
    def _():
        sf_ref[...] = jnp.zeros_like(sf_ref)
        sb_ref[...] = jnp.zeros_like(sb_ref)

    n = len(scratch) // 2
    dirs = (_GlaDir(False, gf_ref, vf_ref, decf_ref, of_ref, sf_ref, *scratch[:n]),
            _GlaDir(True, gb_ref, vb_ref, decb_ref, ob_ref, sb_ref, *scratch[n:]))

    for c in range(n_chunks):
        for d in dirs:
            _gla_intra(d, c)
    for d in dirs:
        _gla_scan(d, n_chunks)
    for c in range(n_chunks):
        for d in dirs:
            _gla_outputs(d, c)


def _gla(gf, gb, vg, dec, *, tb):
    B, T, vw = vg.shape
    nb = T // tb
    nc = tb // GLA_CHUNK
    kw = GLA_HEADS * GLA_DK
    n_pairs = GLA_HEADS // 2
    fwd = lambda w: pl.BlockSpec((None, tb, w), lambda b, s: (b, s, 0))
    bwd = lambda w: pl.BlockSpec((None, tb, w), lambda b, s: (b, nb - 1 - s, 0))
    dec_fwd = pl.BlockSpec((None, nc, 1, kw), lambda b, s: (b, s, 0, 0))
    dec_bwd = pl.BlockSpec((None, nc, 1, kw), lambda b, s: (b, nb - 1 - s, 0, 1))
    state = pltpu.VMEM((n_pairs, GLA_DV, LANES), F32)
    per_dir = [
        pltpu.VMEM((nc, GLA_HEADS * GLA_CHUNK, GLA_CHUNK), BF16),
        pltpu.VMEM((nc, n_pairs, GLA_DV, LANES), F32),
        pltpu.VMEM((nc, n_pairs, GLA_DV, LANES), BF16)]
    return pl.pallas_call(
        functools.partial(_gla_kernel, n_chunks=nc),
        grid=(B, nb),
        in_specs=[fwd(3 * kw), fwd(vw), dec_fwd, bwd(3 * kw), bwd(vw), dec_bwd],
        out_specs=[fwd(vw), bwd(vw)],
        out_shape=[jax.ShapeDtypeStruct((B, T, vw), F32)] * 2,
        scratch_shapes=[state, state] + per_dir * 2,
        compiler_params=pltpu.CompilerParams(
            dimension_semantics=("parallel", "arbitrary"), vmem_limit_bytes=VMEM_LIMIT),
        name="gla",
    )(gf, vg, dec, gb, vg, dec)


def _mix_ffn_kernel(x_ref, yna_ref, of_ref, ob_ref, sr_ref, gng_ref, wout_ref, gff_ref,
                    w1_ref, w2_ref, gfin_ref, out_ref, *, final_norm, ff_chunk):
    o = of_ref[...] + ob_ref[...]
    parts = [_rms(o[:, h * GLA_DV:(h + 1) * GLA_DV], gng_ref[...]) for h in range(GLA_HEADS)]
    y_gla = (jnp.concatenate(parts, axis=-1) * sr_ref[...].astype(F32)).astype(BF16)
    na_w = yna_ref.shape[-1]
    mix = _dot(yna_ref[...], wout_ref[0:na_w, :]) + _dot(y_gla, wout_ref[na_w:, :])
    h1 = x_ref[...] + mix
    n2 = _rms(h1, gff_ref[...]).astype(BF16)
    ffn = None
    for f in range(w1_ref.shape[1] // ff_chunk):
        u = _dot(n2, w1_ref[:, f * ff_chunk:(f + 1) * ff_chunk])
        a = jnp.square(jnp.maximum(u, 0.0)).astype(BF16)
        d = _dot(a, w2_ref[f * ff_chunk:(f + 1) * ff_chunk, :])
        ffn = d if ffn is None else ffn + d
    h2 = h1 + ffn
    out_ref[...] = _rms(h2, gfin_ref[...]) if final_norm else h2


def _mix_ffn(x, yna, of, ob, sr, gng, wout, gff, w1, w2, gfin, *, tm, final_norm):
    B, T, D = x.shape
    tok = lambda w: pl.BlockSpec((None, tm, w), lambda b, i: (b, i, 0))
    const = lambda a: pl.BlockSpec(a.shape, lambda b, i: (0,) * a.ndim,
                                   pipeline_mode=pl.Buffered(1))
    return pl.pallas_call(
        functools.partial(_mix_ffn_kernel, final_norm=final_norm, ff_chunk=1024),
        grid=(B, T // tm),
        in_specs=[tok(D), tok(yna.shape[-1]), tok(of.shape[-1]), tok(ob.shape[-1]),
                  tok(sr.shape[-1]), const(gng), const(wout), const(gff), const(w1),
                  const(w2), const(gfin)],
        out_specs=tok(D),
        out_shape=jax.ShapeDtypeStruct((B, T, D), F32),
        compiler_params=pltpu.CompilerParams(
            dimension_semantics=("parallel", "parallel"), vmem_limit_bytes=VMEM_LIMIT),
        name="mix_ffn",
    )(x, yna, of, ob, sr, gng, wout, gff, w1, w2, gfin)


def kernel(x, ln_mix_g, w_in, na_rpb, gla_gate_up_fwd, gla_gate_bias_fwd, gla_gate_up_bwd,
           gla_gate_bias_bwd, gla_norm_g, w_out, ln_ff_g, w_ff1, w_ff2, ln_final_g):
    B, T, D = x.shape
    depth = w_in.shape[0]
    assert T % NA_GROUP == 0 and T // NA_GROUP >= 3 and T % 512 == 0
    n_main = w_in.shape[-1] - 2 * GLA_GATE_RANK
    kw = GLA_HEADS * GLA_DK
    row = lambda v: v.reshape(1, -1).astype(F32)
    h = x
    for l in range(depth):
        w = w_in[l]
        wm = w[:, :n_main].astype(BF16)
        wkt = w[:, 512:1024].T.astype(BF16)
        wz = w[:, n_main:].astype(BF16)
        zeros = jnp.zeros((GLA_GATE_RANK, kw), F32)
        gu = jnp.concatenate([
            jnp.concatenate([gla_gate_up_fwd[l], zeros], axis=1),
            jnp.concatenate([zeros, gla_gate_up_bwd[l]], axis=1)], axis=0).astype(BF16)
        gb = jnp.concatenate([gla_gate_bias_fwd[l], gla_gate_bias_bwd[l]]).reshape(1, -1)
        qa, kat, va, vg, sr, gla_f, gla_b, dec = _inproj(
            h, row(ln_mix_g[l]), wm, wkt, wz, gu, gb.astype(F32), tm=512)
        y_na = _na(qa, kat, va, _na_bias_rows(na_rpb[l]))
        o_f, o_b = _gla(gla_f, gla_b, vg, dec, tb=512)
        h = _mix_ffn(h, y_na, o_f, o_b, sr, row(gla_norm_g[l]), w_out[l].astype(BF16),
                     row(ln_ff_g[l]), w_ff1[l].astype(BF16), w_ff2[l].astype(BF16),
                     row(ln_final_g), tm=512, final_norm=(l == depth - 1))
    return h
```

```python
import functools

import jax
import jax.numpy as jnp
from jax import lax
from jax.experimental import pallas as pl
from jax.experimental.pallas import tpu as pltpu

F32 = jnp.float32
BF16 = jnp.bfloat16

EPS = 1e-6
GRID_W = 64
NA_HEADS = 8
NA_HEAD_DIM = 64
NA_KH = 8
NA_KW = 16
GLA_HEADS = 4
GLA_DK = 64
GLA_DV = 128
GLA_GATE_RANK = 16
GLA_GATE_NORM = 16.0
GLA_CHUNK = 64

LANES = 128
NA_GROUP_ROWS = 4
NA_GROUP = NA_GROUP_ROWS * GRID_W
NA_KEY_ROWS = 12
NA_KEYS = NA_KEY_ROWS * GRID_W
NA_UNROLL = 4
LOG2E = 1.4426950408889634
VMEM_LIMIT = 56 * 1024 * 1024


def _dot(a, b):
    return jnp.dot(a, b, preferred_element_type=F32)


def _dot_nt(a, b):
    return lax.dot_general(a, b, (((1,), (1,)), ((), ())), preferred_element_type=F32)


def _dot_tn(a, b):
    return lax.dot_general(a, b, (((0,), (0,)), ((), ())), preferred_element_type=F32)


def _rms(x, g):
    return x * lax.rsqrt(jnp.mean(x * x, axis=-1, keepdims=True) + EPS) * g


def _segmented_cumsum(x, reverse):
    n = x.shape[0]
    pos = lax.broadcasted_iota(jnp.int32, x.shape, 0) & (GLA_CHUNK - 1)
    step = 1
    while step < GLA_CHUNK:
        if reverse:
            shifted, ok = pltpu.roll(x, n - step, 0), pos < GLA_CHUNK - step
        else:
            shifted, ok = pltpu.roll(x, step, 0), pos >= step
        x = x + jnp.where(ok, shifted, 0.0)
        step *= 2
    return x


def _gla_cumdecay(log_a, reverse):
    b = _segmented_cumsum(log_a, reverse)
    b3 = b.reshape(-1, GLA_CHUNK, b.shape[-1])
    last = 0 if reverse else GLA_CHUNK - 1
    return b, b3[:, last:last + 1, :]


def _gla_operands(q, k, b, b_last):
    b3 = b.reshape(-1, GLA_CHUNK, b.shape[-1])
    k_dec = (k.reshape(b3.shape) * jnp.exp2(b_last - b3)).reshape(b.shape)
    ops = jnp.concatenate([q * jnp.exp2(b), k * jnp.exp2(-b), k_dec], axis=-1)
    return ops.astype(BF16), jnp.exp2(b_last)


def _inproj_kernel(x_ref, g_ref, wm_ref, wkt_ref, wz_ref, gu_ref, gb_ref,
                   qa_ref, kat_ref, va_ref, vg_ref, sr_ref, gf_ref, gbw_ref, dec_ref):
    kw = GLA_HEADS * GLA_DK
    n = _rms(x_ref[...], g_ref[...]).astype(BF16)

    def log_decay(cols):
        pre = _dot(z, gu_ref[:, cols]) + gb_ref[:, cols]
        softplus2 = jnp.log2(1.0 + jnp.exp2(jnp.abs(pre) * -LOG2E))
        return jnp.minimum(pre * (LOG2E / GLA_GATE_NORM), 0.0) - softplus2 * (1.0 / GLA_GATE_NORM)

    z = _dot(n, wz_ref[...]).astype(BF16)
    b_f, last_f = _gla_cumdecay(log_decay(slice(0, kw)), False)
    qa_ref[...] = (_dot(n, wm_ref[:, 0:512]) * (NA_HEAD_DIM ** -0.5 * LOG2E)).astype(BF16)
    b_b, last_b = _gla_cumdecay(log_decay(slice(kw, 2 * kw)), True)
    kat_ref[...] = _dot_nt(wkt_ref[...], n).astype(BF16)
    qg = _dot(n, wm_ref[:, 1536:1792]) * (GLA_DK ** -0.5)
    kg = _dot(n, wm_ref[:, 1792:2048])
    gf_ref[...], dec_f = _gla_operands(qg, kg, b_f, last_f)
    va_ref[...] = _dot(n, wm_ref[:, 1024:1536]).astype(BF16)
    gbw_ref[...], dec_b = _gla_operands(qg, kg, b_b, last_b)
    dec_ref[...] = jnp.concatenate([dec_f, dec_b], axis=-1)
    vg_ref[...] = _dot(n, wm_ref[:, 2048:2560]).astype(BF16)
    r = _dot(n, wm_ref[:, 2560:3072])
    sr_ref[...] = (r * jax.nn.sigmoid(r)).astype(BF16)


def _inproj(x, g, wm, wkt, wz, gu, gb, *, tm):
    B, T, D = x.shape
    kw = GLA_HEADS * GLA_DK
    nc = tm // GLA_CHUNK
    tok = lambda w: pl.BlockSpec((None, tm, w), lambda b, i: (b, i, 0))
    const = lambda a: pl.BlockSpec(a.shape, lambda b, i: (0,) * a.ndim,
                                   pipeline_mode=pl.Buffered(1))
    outs = [
        (jax.ShapeDtypeStruct((B, T, 512), BF16), tok(512)),
        (jax.ShapeDtypeStruct((B, 512, T), BF16),
         pl.BlockSpec((None, 512, tm), lambda b, i: (b, 0, i))),
        (jax.ShapeDtypeStruct((B, T, 512), BF16), tok(512)),
        (jax.ShapeDtypeStruct((B, T, 512), BF16), tok(512)),
        (jax.ShapeDtypeStruct((B, T, 512), BF16), tok(512)),
        (jax.ShapeDtypeStruct((B, T, 3 * kw), BF16), tok(3 * kw)),
        (jax.ShapeDtypeStruct((B, T, 3 * kw), BF16), tok(3 * kw)),
        (jax.ShapeDtypeStruct((B, T // GLA_CHUNK, 1, 2 * kw), F32),
         pl.BlockSpec((None, nc, 1, 2 * kw), lambda b, i: (b, i, 0, 0))),
    ]
    return pl.pallas_call(
        _inproj_kernel,
        grid=(B, T // tm),
        in_specs=[tok(D), const(g), const(wm), const(wkt), const(wz), const(gu), const(gb)],
        out_specs=[s for _, s in outs],
        out_shape=[s for s, _ in outs],
        compiler_params=pltpu.CompilerParams(
            dimension_semantics=("parallel", "parallel"), vmem_limit_bytes=VMEM_LIMIT),
        name="inproj",
    )(x, g, wm, wkt, wz, gu, gb)


def _na_bias_rows(rpb):
    n_dc = 2 * NA_KW - 1
    padded = jnp.pad(rpb.astype(F32), ((0, 0), (4, 5), (0, GRID_W - n_dc)))
    return jnp.concatenate([padded[:, :-1], padded[:, 1:]], axis=-1)


def _na_tiles(table, g):
    i_start = (0, g, NA_KEY_ROWS - NA_KH)[table]
    return range(i_start // 2, (i_start + NA_KH + 1) // 2)


def _na_build_tables(rows_ref, tab_ref):
    qc = lax.broadcasted_iota(jnp.int32, (GRID_W, LANES), 0)
    lane = lax.broadcasted_iota(jnp.int32, (GRID_W, LANES), 1)
    kc = lane & (GRID_W - 1)
    col_start = jnp.clip(qc - NA_KW // 2, 0, GRID_W - NA_KW)
    in_win = (kc >= col_start) & (kc < col_start + NA_KW)
    valid = {(True, True): in_win,
             (True, False): in_win & (lane < GRID_W),
             (False, True): in_win & (lane >= GRID_W)}
    row_offset = (0, -(NA_KH // 2), -NA_KH)
    for hh in range(2):
        for t in range(3):
            for g in range(NA_GROUP_ROWS):
                i_start = (0, g, NA_KEY_ROWS - NA_KH)[t]
                for m in _na_tiles(t, g):
                    halves = tuple(i_start <= i < i_start + NA_KH for i in (2 * m, 2 * m + 1))
                    a = 2 * m - g + row_offset[t] + NA_KH - 1 + 4
                    src = jnp.broadcast_to(rows_ref[hh, a:a + 1, :], (GRID_W, LANES))
                    rot = pltpu.roll(src, LANES - (NA_KW - 1), 1, stride=1, stride_axis=0)
                    tab_ref[hh, t, g * GRID_W:(g + 1) * GRID_W, m * LANES:(m + 1) * LANES] = (
                        jnp.where(valid[halves], rot * LOG2E, -jnp.inf))


class _NaRefs:
    def __init__(self, q, kt, v, bias, o, n_groups):
        self.q, self.kt, self.v, self.bias, self.o, self.n_groups = q, kt, v, bias, o, n_groups


def _na_rows(u):
    return pl.ds(pl.multiple_of(u * NA_GROUP, NA_GROUP), NA_GROUP)


def _na_key0(r, u):
    return pl.multiple_of(jnp.clip(u - 1, 0, r.n_groups - 3) * NA_GROUP, NA_GROUP)


def _na_scores(r, u):
    qq = r.q[_na_rows(u), :]
    first_head = lax.broadcasted_iota(jnp.int32, qq.shape, 1) < NA_HEAD_DIM
    zero = jnp.zeros_like(qq)
    q_heads = jnp.concatenate([jnp.where(first_head, qq, zero),
                               jnp.where(first_head, zero, qq)], axis=0)
    return _dot(q_heads, r.kt[:, pl.ds(_na_key0(r, u), NA_KEYS)])


def _na_softmax(r, s, table):
    p_tiles, inv_l = [], []
    for hh in range(2):
        for g in range(NA_GROUP_ROWS):
            rows = slice(hh * NA_GROUP + g * GRID_W, hh * NA_GROUP + (g + 1) * GRID_W)
            sb = [s[rows, c * LANES:(c + 1) * LANES]
                  + r.bias[hh, table, g * GRID_W:(g + 1) * GRID_W, c * LANES:(c + 1) * LANES]
                  for c in _na_tiles(table, g)]
            m = jnp.max(functools.reduce(jnp.maximum, sb), axis=-1, keepdims=True)
            p = [jnp.exp2(x - m) for x in sb]
            inv_l.append(1.0 / jnp.sum(functools.reduce(jnp.add, p), axis=-1, keepdims=True))
            p_tiles.append([x.astype(BF16) for x in p])
    return p_tiles, inv_l


def _na_store_output(r, u, o):
    first_head = lax.broadcasted_iota(jnp.int32, (NA_GROUP, LANES), 1) < NA_HEAD_DIM
    r.o[_na_rows(u), :] = jnp.where(first_head, o[:NA_GROUP], o[NA_GROUP:]).astype(r.o.dtype)


def _na_edge_group(r, u, table):
    tiles = _na_tiles(table, 0)
    p_tiles, inv_l = _na_softmax(r, _na_scores(r, u), table)
    p = jnp.concatenate([jnp.concatenate(t, axis=1) for t in p_tiles], axis=0)
    keys = pl.ds(_na_key0(r, u) + tiles.start * LANES, len(tiles) * LANES)
    _na_store_output(r, u, _dot(p, r.v[keys, :]) * jnp.concatenate(inv_l, axis=0))


def _na_interior_probs(r, s):
    p_tiles, inv_l = _na_softmax(r, s, 1)
    zero_tile = jnp.zeros((GRID_W, LANES), BF16)
    p_rows = []
    for block, tiles in enumerate(p_tiles):
        valid = _na_tiles(1, block % NA_GROUP_ROWS)
        p_rows.append(jnp.concatenate(
            [zero_tile] * valid.start + tiles + [zero_tile] * (NA_KEYS // LANES - valid.stop), axis=1))
    return jnp.concatenate(p_rows, axis=0), jnp.concatenate(inv_l, axis=0)


def _na_interior_output(r, u, p, inv_l):
    _na_store_output(r, u, _dot(p, r.v[pl.ds(_na_key0(r, u), NA_KEYS), :]) * inv_l)


def _na_stage(r, u, carry):
    s_cur, p_prev, l_prev = carry
    s_next = _na_scores(r, jnp.minimum(u + 1, r.n_groups - 2))
    p_cur, l_cur = _na_interior_probs(r, s_cur)
    _na_interior_output(r, u - 1, p_prev, l_prev)
    return s_next, p_cur, l_cur


def _na_kernel(q_ref, kt_ref, v_ref, rows_ref, o_ref, bias_ref, s_ref, p_ref, l_ref, *, n_groups):
    _na_build_tables(rows_ref, bias_ref)
    r = _NaRefs(q_ref, kt_ref, v_ref, bias_ref, o_ref, n_groups)
    last = n_groups - 1
    _na_edge_group(r, 0, 0)
    _na_edge_group(r, last, 2)

    def load_carry():
        return s_ref[...], p_ref[...], l_ref[...]

    def store_carry(carry):
        s_ref[...], p_ref[...], l_ref[...] = carry

    def stages(first, count, carry):
        for k in range(count):
            carry = _na_stage(r, first + k, carry)
        return carry

    def unrolled_stages(i, c):
        store_carry(stages(2 + NA_UNROLL * i, NA_UNROLL, load_carry()))
        return c

    store_carry((_na_scores(r, 2),) + _na_interior_probs(r, _na_scores(r, 1)))
    n_loops, n_tail = divmod(n_groups - 3, NA_UNROLL)
    lax.fori_loop(0, n_loops, unrolled_stages, 0)
    _, p_prev, l_prev = stages(2 + NA_UNROLL * n_loops, n_tail, load_carry())
    _na_interior_output(r, last - 1, p_prev, l_prev)


def _na(qa, kat, va, rows):
    B, T, _ = qa.shape
    n_pairs = NA_HEADS // 2
    return pl.pallas_call(
        functools.partial(_na_kernel, n_groups=T // NA_GROUP),
        grid=(B, n_pairs),
        in_specs=[
            pl.BlockSpec((None, T, LANES), lambda b, p: (b, 0, p)),
            pl.BlockSpec((None, LANES, T), lambda b, p: (b, p, 0)),
            pl.BlockSpec((None, T, LANES), lambda b, p: (b, 0, p)),
            pl.BlockSpec((2,) + rows.shape[1:], lambda b, p: (p, 0, 0)),
        ],
        out_specs=pl.BlockSpec((None, T, LANES), lambda b, p: (b, 0, p)),
        out_shape=jax.ShapeDtypeStruct((B, T, NA_HEADS * NA_HEAD_DIM), BF16),
        scratch_shapes=[pltpu.VMEM((2, 3, NA_GROUP, NA_KEYS), F32),
                        pltpu.VMEM((2 * NA_GROUP, NA_KEYS), F32),
                        pltpu.VMEM((2 * NA_GROUP, NA_KEYS), BF16),
                        pltpu.VMEM((2 * NA_GROUP, 1), F32)],
        compiler_params=pltpu.CompilerParams(
            dimension_semantics=("parallel", "parallel"), vmem_limit_bytes=VMEM_LIMIT),
        name="natten",
    )(qa, kat, va, rows)


class _GlaDir:
    def __init__(self, reverse, ops, v, dec, o, state, a, contrib, prev):
        kw = GLA_HEADS * GLA_DK
        self.reverse = reverse
        self.qd, self.ki, self.kd = (ops.at[:, i * kw:(i + 1) * kw] for i in range(3))
        self.v, self.dec, self.o, self.state = v, dec, o, state
        self.a, self.contrib, self.prev = a, contrib, prev


def _gla_intra(d, c):
    C = GLA_CHUNK
    rows = slice(c * C, (c + 1) * C)
    qd = d.qd[rows, :]
    lane = lax.broadcasted_iota(jnp.int32, qd.shape, 1)
    q_heads = jnp.concatenate(
        [jnp.where((lane >= h * GLA_DK) & (lane < (h + 1) * GLA_DK), qd, jnp.zeros_like(qd))
         for h in range(GLA_HEADS)], axis=0)
    scores = _dot_nt(q_heads, d.ki[rows, :])
    i = lax.broadcasted_iota(jnp.int32, scores.shape, 0) & (C - 1)
    j = lax.broadcasted_iota(jnp.int32, scores.shape, 1)
    keep = (j > i) if d.reverse else (j <= i)
    d.a[c] = jnp.where(keep, scores, 0.0).astype(BF16)
    first_head = lax.broadcasted_iota(jnp.int32, (GLA_DV, LANES), 1) < GLA_DK
    for p in range(GLA_HEADS // 2):
        kd_p = d.kd[rows, p * LANES:(p + 1) * LANES]
        c0, c1 = (_dot_tn(d.v[rows, h * GLA_DV:(h + 1) * GLA_DV], kd_p) for h in (2 * p, 2 * p + 1))
        d.contrib[c, p] = jnp.where(first_head, c0, c1)


def _gla_scan(d, n_chunks):
    order = range(n_chunks - 1, -1, -1) if d.reverse else range(n_chunks)
    for p in range(GLA_HEADS // 2):
        st = d.state[p]
        for c in order:
            d.prev[c, p] = st.astype(BF16)
            st = st * d.dec[c][:, p * LANES:(p + 1) * LANES] + d.contrib[c, p]
        d.state[p] = st


def _gla_outputs(d, c):
    C = GLA_CHUNK
    rows = slice(c * C, (c + 1) * C)
    first_head = lax.broadcasted_iota(jnp.int32, (C, LANES), 1) < GLA_DK
    for p in range(GLA_HEADS // 2):
        qd_p = d.qd[rows, p * LANES:(p + 1) * LANES]
        zero = jnp.zeros_like(qd_p)
        q_pair = jnp.concatenate([jnp.where(first_head, qd_p, zero),
                                  jnp.where(first_head, zero, qd_p)], axis=0)
        inter = _dot_nt(q_pair, d.prev[c, p])
        for hh in range(2):
            h = 2 * p + hh
            cols = slice(h * GLA_DV, (h + 1) * GLA_DV)
            d.o[rows, cols] = _dot(d.a[c, h * C:(h + 1) * C, :], d.v[rows, cols]) + inter[hh * C:(hh + 1) * C]


def _gla_kernel(gf_ref, vf_ref, decf_ref, gb_ref, vb_ref, decb_ref,
                of_ref, ob_ref, sf_ref, sb_ref, *scratch, n_chunks):
    @pl.when(pl.program_id(1) == 0)
    def _():
        sf_ref[...] = jnp.zeros_like(sf_ref)
        sb_ref[...] = jnp.zeros_like(sb_ref)

    n = len(scratch) // 2
    dirs = (_GlaDir(False, gf_ref, vf_ref, decf_ref, of_ref, sf_ref, *scratch[:n]),
            _GlaDir(True, gb_ref, vb_ref, decb_ref, ob_ref, sb_ref, *scratch[n:]))

    for c in range(n_chunks):
        for d in dirs:
            _gla_intra(d, c)
    for d in dirs:
        _gla_scan(d, n_chunks)
    for c in range(n_chunks):
        for d in dirs:
            _gla_outputs(d, c)


def _gla(gf, gb, vg, dec, *, tb):
    B, T, vw = vg.shape
    nb = T // tb
    nc = tb // GLA_CHUNK
    kw = GLA_HEADS * GLA_DK
    n_pairs = GLA_HEADS // 2
    fwd = lambda w: pl.BlockSpec((None, tb, w), lambda b, s: (b, s, 0))
    bwd = lambda w: pl.BlockSpec((None, tb, w), lambda b, s: (b, nb - 1 - s, 0))
    dec_fwd = pl.BlockSpec((None, nc, 1, kw), lambda b, s: (b, s, 0, 0))
    dec_bwd = pl.BlockSpec((None, nc, 1, kw), lambda b, s: (b, nb - 1 - s, 0, 1))
    state = pltpu.VMEM((n_pairs, GLA_DV, LANES), F32)
    per_dir = [
        pltpu.VMEM((nc, GLA_HEADS * GLA_CHUNK, GLA_CHUNK), BF16),
        pltpu.VMEM((nc, n_pairs, GLA_DV, LANES), F32),
        pltpu.VMEM((nc, n_pairs, GLA_DV, LANES), BF16)]
    return pl.pallas_call(
        functools.partial(_gla_kernel, n_chunks=nc),
        grid=(B, nb),
        in_specs=[fwd(3 * kw), fwd(vw), dec_fwd, bwd(3 * kw), bwd(vw), dec_bwd],
        out_specs=[fwd(vw), bwd(vw)],
        out_shape=[jax.ShapeDtypeStruct((B, T, vw), F32)] * 2,
        scratch_shapes=[state, state] + per_dir * 2,
        compiler_params=pltpu.CompilerParams(
            dimension_semantics=("parallel", "arbitrary"), vmem_limit_bytes=VMEM_LIMIT),
        name="gla",
    )(gf, vg, dec, gb, vg, dec)


def _mix_ffn_kernel(x_ref, yna_ref, of_ref, ob_ref, sr_ref, gng_ref, wout_ref, gff_ref,
                    w1_ref, w2_ref, gfin_ref, out_ref, *, final_norm, ff_chunk):
    o = of_ref[...] + ob_ref[...]
    parts = [_rms(o[:, h * GLA_DV:(h + 1) * GLA_DV], gng_ref[...]) for h in range(GLA_HEADS)]
    y_gla = (jnp.concatenate(parts, axis=-1) * sr_ref[...].astype(F32)).astype(BF16)
    na_w = yna_ref.shape[-1]
    mix = _dot(yna_ref[...], wout_ref[0:na_w, :]) + _dot(y_gla, wout_ref[na_w:, :])
    h1 = x_ref[...] + mix
    n2 = _rms(h1, gff_ref[...]).astype(BF16)
    ffn = None
    for f in range(w1_ref.shape[1] // ff_chunk):
        u = _dot(n2, w1_ref[:, f * ff_chunk:(f + 1) * ff_chunk])
        a = jnp.square(jnp.maximum(u, 0.0)).astype(BF16)
        d = _dot(a, w2_ref[f * ff_chunk:(f + 1) * ff_chunk, :])
        ffn = d if ffn is None else ffn + d
    h2 = h1 + ffn
    out_ref[...] = _rms(h2, gfin_ref[...]) if final_norm else h2


def _mix_ffn(x, yna, of, ob, sr, gng, wout, gff, w1, w2, gfin, *, tm, final_norm):
    B, T, D = x.shape
    tok = lambda w: pl.BlockSpec((None, tm, w), lambda b, i: (b, i, 0))
    const = lambda a: pl.BlockSpec(a.shape, lambda b, i: (0,) * a.ndim,
                                   pipeline_mode=pl.Buffered(1))
    return pl.pallas_call(
        functools.partial(_mix_ffn_kernel, final_norm=final_norm, ff_chunk=1024),
        grid=(B, T // tm),
        in_specs=[tok(D), tok(yna.shape[-1]), tok(of.shape[-1]), tok(ob.shape[-1]),
                  tok(sr.shape[-1]), const(gng), const(wout), const(gff), const(w1),
                  const(w2), const(gfin)],
        out_specs=tok(D),
        out_shape=jax.ShapeDtypeStruct((B, T, D), F32),
        compiler_params=pltpu.CompilerParams(
            dimension_semantics=("parallel", "parallel"), vmem_limit_bytes=VMEM_LIMIT),
        name="mix_ffn",
    )(x, yna, of, ob, sr, gng, wout, gff, w1, w2, gfin)


def kernel(x, ln_mix_g, w_in, na_rpb, gla_gate_up_fwd, gla_gate_bias_fwd, gla_gate_up_bwd,
           gla_gate_bias_bwd, gla_norm_g, w_out, ln_ff_g, w_ff1, w_ff2, ln_final_g):
    B, T, D = x.shape
    depth = w_in.shape[0]
    assert T % NA_GROUP == 0 and T // NA_GROUP >= 3 and T % 512 == 0
    n_main = w_in.shape[-1] - 2 * GLA_GATE_RANK
    kw = GLA_HEADS * GLA_DK
    row = lambda v: v.reshape(1, -1).astype(F32)
    h = x
    for l in range(depth):
        w = w_in[l]
        wm = w[:, :n_main].astype(BF16)
        wkt = w[:, 512:1024].T.astype(BF16)
        wz = w[:, n_main:].astype(BF16)
        zeros = jnp.zeros((GLA_GATE_RANK, kw), F32)
        gu = jnp.concatenate([
            jnp.concatenate([gla_gate_up_fwd[l], zeros], axis=1),
            jnp.concatenate([zeros, gla_gate_up_bwd[l]], axis=1)], axis=0).astype(BF16)
        gb = jnp.concatenate([gla_gate_bias_fwd[l], gla_gate_bias_bwd[l]]).reshape(1, -1)
        qa, kat, va, vg, sr, gla_f, gla_b, dec = _inproj(
            h, row(ln_mix_g[l]), wm, wkt, wz, gu, gb.astype(F32), tm=512)
        y_na = _na(qa, kat, va, _na_bias_rows(na_rpb[l]))
        o_f, o_b = _gla(gla_f, gla_b, vg, dec, tb=512)
        h = _mix_ffn(h, y_na, o_f, o_b, sr, row(gla_norm_g[l]), w_out[l].astype(BF16),
                     row(ln_ff_g[l]), w_ff1[l].astype(BF16), w_ff2[l].astype(BF16),
                     row(ln_final_g), tm=512, final_norm=(l == depth - 1))
    return h
```

```python
import functools

import jax
import jax.numpy as jnp
from jax import lax
from jax.experimental import pallas as pl
from jax.experimental.pallas import tpu as pltpu

F32 = jnp.float32
BF16 = jnp.bfloat16

EPS = 1e-6
GRID_W = 64
NA_HEADS = 8
NA_HEAD_DIM = 64
NA_KH = 8
NA_KW = 16
GLA_HEADS = 4
GLA_DK = 64
GLA_DV = 128
GLA_GATE_RANK = 16
GLA_GATE_NORM = 16.0
GLA_CHUNK = 64

LANES = 128
NA_GROUP_ROWS = 4
NA_GROUP = NA_GROUP_ROWS * GRID_W
NA_KEY_ROWS = 12
NA_KEYS = NA_KEY_ROWS * GRID_W
NA_UNROLL = 4
LOG2E = 1.4426950408889634
VMEM_LIMIT = 56 * 1024 * 1024


def _dot(a, b):
    return jnp.dot(a, b, preferred_element_type=F32)


def _dot_nt(a, b):
    return lax.dot_general(a, b, (((1,), (1,)), ((), ())), preferred_element_type=F32)


def _dot_tn(a, b):
    return lax.dot_general(a, b, (((0,), (0,)), ((), ())), preferred_element_type=F32)


def _rms(x, g):
    return x * lax.rsqrt(jnp.mean(x * x, axis=-1, keepdims=True) + EPS) * g


def _segmented_cumsum(x, reverse):
    n = x.shape[0]
    pos = lax.broadcasted_iota(jnp.int32, x.shape, 0) & (GLA_CHUNK - 1)
    step = 1
    while step < GLA_CHUNK:
        if reverse:
            shifted, ok = pltpu.roll(x, n - step, 0), pos < GLA_CHUNK - step
        else:
            shifted, ok = pltpu.roll(x, step, 0), pos >= step
        x = x + jnp.where(ok, shifted, 0.0)
        step *= 2
    return x


def _gla_cumdecay(log_a, reverse):
    b = _segmented_cumsum(log_a, reverse)
    b3 = b.reshape(-1, GLA_CHUNK, b.shape[-1])
    last = 0 if reverse else GLA_CHUNK - 1
    return b, b3[:, last:last + 1, :]


def _gla_operands(q, k, b, b_last):
    b3 = b.reshape(-1, GLA_CHUNK, b.shape[-1])
    k_dec = (k.reshape(b3.shape) * jnp.exp2(b_last - b3)).reshape(b.shape)
    ops = jnp.concatenate([q * jnp.exp2(b), k * jnp.exp2(-b), k_dec], axis=-1)
    return ops.astype(BF16), jnp.exp2(b_last)


def _inproj_kernel(x_ref, g_ref, w_ref, gu_ref, gb_ref,
                   qa_ref, kat_ref, va_ref, vg_ref, sr_ref, gf_ref, gbw_ref, dec_ref,
                   wm_ref, wkt_ref):
    kw = GLA_HEADS * GLA_DK
    n_main = w_ref.shape[1] - 2 * GLA_GATE_RANK

    @pl.when((pl.program_id(0) == 0) & (pl.program_id(1) == 0))
    def _():
        wm_ref[...] = w_ref[...].astype(BF16)
        wkt_ref[...] = w_ref[:, 512:1024].T.astype(BF16)

    n = _rms(x_ref[...], g_ref[...]).astype(BF16)

    def log_decay(cols):
        pre = _dot(z, gu_ref[:, cols]) + gb_ref[:, cols]
        softplus2 = jnp.log2(1.0 + jnp.exp2(jnp.abs(pre) * -LOG2E))
        return jnp.minimum(pre * (LOG2E / GLA_GATE_NORM), 0.0) - softplus2 * (1.0 / GLA_GATE_NORM)

    z = _dot(n, wm_ref[:, n_main:]).astype(BF16)
    b_f, last_f = _gla_cumdecay(log_decay(slice(0, kw)), False)
    qa_ref[...] = (_dot(n, wm_ref[:, 0:512]) * (NA_HEAD_DIM ** -0.5 * LOG2E)).astype(BF16)
    b_b, last_b = _gla_cumdecay(log_decay(slice(kw, 2 * kw)), True)
    kat_ref[...] = _dot_nt(wkt_ref[...], n).astype(BF16)
    qg = _dot(n, wm_ref[:, 1536:1792]) * (GLA_DK ** -0.5)
    kg = _dot(n, wm_ref[:, 1792:2048])
    gf_ref[...], dec_f = _gla_operands(qg, kg, b_f, last_f)
    va_ref[...] = _dot(n, wm_ref[:, 1024:1536]).astype(BF16)
    gbw_ref[...], dec_b = _gla_operands(qg, kg, b_b, last_b)
    dec_ref[...] = jnp.concatenate([dec_f, dec_b], axis=-1)
    vg_ref[...] = _dot(n, wm_ref[:, 2048:2560]).astype(BF16)
    r = _dot(n, wm_ref[:, 2560:3072])
    sr_ref[...] = (r * jax.nn.sigmoid(r)).astype(BF16)


def _inproj(x, g, w, gu, gb, *, tm):
    B, T, D = x.shape
    kw = GLA_HEADS * GLA_DK
    nc = tm // GLA_CHUNK
    tok = lambda w: pl.BlockSpec((None, tm, w), lambda b, i: (b, i, 0))
    const = lambda a: pl.BlockSpec(a.shape, lambda b, i: (0,) * a.ndim,
                                   pipeline_mode=pl.Buffered(1))
    outs = [
        (jax.ShapeDtypeStruct((B, T, 512), BF16), tok(512)),
        (jax.ShapeDtypeStruct((B, 512, T), BF16),
         pl.BlockSpec((None, 512, tm), lambda b, i: (b, 0, i))),
        (jax.ShapeDtypeStruct((B, T, 512), BF16), tok(512)),
        (jax.ShapeDtypeStruct((B, T, 512), BF16), tok(512)),
        (jax.ShapeDtypeStruct((B, T, 512), BF16), tok(512)),
        (jax.ShapeDtypeStruct((B, T, 3 * kw), BF16), tok(3 * kw)),
        (jax.ShapeDtypeStruct((B, T, 3 * kw), BF16), tok(3 * kw)),
        (jax.ShapeDtypeStruct((B, T // GLA_CHUNK, 1, 2 * kw), F32),
         pl.BlockSpec((None, nc, 1, 2 * kw), lambda b, i: (b, i, 0, 0))),
    ]
    return pl.pallas_call(
        _inproj_kernel,
        grid=(B, T // tm),
        in_specs=[tok(D), const(g), const(w), const(gu), const(gb)],
        out_specs=[s for _, s in outs],
        out_shape=[s for s, _ in outs],
        scratch_shapes=[pltpu.VMEM(w.shape, BF16), pltpu.VMEM((512, D), BF16)],
        compiler_params=pltpu.CompilerParams(
            dimension_semantics=("arbitrary", "arbitrary"), vmem_limit_bytes=VMEM_LIMIT),
        name="inproj",
    )(x, g, w, gu, gb)


def _na_bias_rows(rpb):
    n_dc = 2 * NA_KW - 1
    padded = jnp.pad(rpb.astype(F32), ((0, 0), (4, 5), (0, GRID_W - n_dc)))
    return jnp.concatenate([padded[:, :-1], padded[:, 1:]], axis=-1)


def _na_tiles(table, g):
    i_start = (0, g, NA_KEY_ROWS - NA_KH)[table]
    return range(i_start // 2, (i_start + NA_KH + 1) // 2)


def _na_build_tables(rows_ref, tab_ref):
    qc = lax.broadcasted_iota(jnp.int32, (GRID_W, LANES), 0)
    lane = lax.broadcasted_iota(jnp.int32, (GRID_W, LANES), 1)
    kc = lane & (GRID_W - 1)
    col_start = jnp.clip(qc - NA_KW // 2, 0, GRID_W - NA_KW)
    in_win = (kc >= col_start) & (kc < col_start + NA_KW)
    valid = {(True, True): in_win,
             (True, False): in_win & (lane < GRID_W),
             (False, True): in_win & (lane >= GRID_W)}
    row_offset = (0, -(NA_KH // 2), -NA_KH)
    for hh in range(2):
        for t in range(3):
            for g in range(NA_GROUP_ROWS):
                i_start = (0, g, NA_KEY_ROWS - NA_KH)[t]
                for m in _na_tiles(t, g):
                    halves = tuple(i_start <= i < i_start + NA_KH for i in (2 * m, 2 * m + 1))
                    a = 2 * m - g + row_offset[t] + NA_KH - 1 + 4
                    src = jnp.broadcast_to(rows_ref[hh, a:a + 1, :], (GRID_W, LANES))
                    rot = pltpu.roll(src, LANES - (NA_KW - 1), 1, stride=1, stride_axis=0)
                    tab_ref[hh, t, g * GRID_W:(g + 1) * GRID_W, m * LANES:(m + 1) * LANES] = (
                        jnp.where(valid[halves], rot * LOG2E, -jnp.inf))


class _NaRefs:
    def __init__(self, q, kt, v, bias, o, n_groups):
        self.q, self.kt, self.v, self.bias, self.o, self.n_groups = q, kt, v, bias, o, n_groups


def _na_rows(u):
    return pl.ds(pl.multiple_of(u * NA_GROUP, NA_GROUP), NA_GROUP)


def _na_key0(r, u):
    return pl.multiple_of(jnp.clip(u - 1, 0, r.n_groups - 3) * NA_GROUP, NA_GROUP)


def _na_scores(r, u):
    qq = r.q[_na_rows(u), :]
    first_head = lax.broadcasted_iota(jnp.int32, qq.shape, 1) < NA_HEAD_DIM
    zero = jnp.zeros_like(qq)
    q_heads = jnp.concatenate([jnp.where(first_head, qq, zero),
                               jnp.where(first_head, zero, qq)], axis=0)
    return _dot(q_heads, r.kt[:, pl.ds(_na_key0(r, u), NA_KEYS)])


def _na_softmax(r, s, table):
    p_tiles, inv_l = [], []
    for hh in range(2):
        for g in range(NA_GROUP_ROWS):
            rows = slice(hh * NA_GROUP + g * GRID_W, hh * NA_GROUP + (g + 1) * GRID_W)
            sb = [s[rows, c * LANES:(c + 1) * LANES]
                  + r.bias[hh, table, g * GRID_W:(g + 1) * GRID_W, c * LANES:(c + 1) * LANES]
                  for c in _na_tiles(table, g)]
            m = jnp.max(functools.reduce(jnp.maximum, sb), axis=-1, keepdims=True)
            p = [jnp.exp2(x - m) for x in sb]
            inv_l.append(1.0 / jnp.sum(functools.reduce(jnp.add, p), axis=-1, keepdims=True))
            p_tiles.append([x.astype(BF16) for x in p])
    return p_tiles, inv_l


def _na_store_output(r, u, o):
    first_head = lax.broadcasted_iota(jnp.int32, (NA_GROUP, LANES), 1) < NA_HEAD_DIM
    r.o[_na_rows(u), :] = jnp.where(first_head, o[:NA_GROUP], o[NA_GROUP:]).astype(r.o.dtype)


def _na_edge_group(r, u, table):
    tiles = _na_tiles(table, 0)
    p_tiles, inv_l = _na_softmax(r, _na_scores(r, u), table)
    p = jnp.concatenate([jnp.concatenate(t, axis=1) for t in p_tiles], axis=0)
    keys = pl.ds(_na_key0(r, u) + tiles.start * LANES, len(tiles) * LANES)
    _na_store_output(r, u, _dot(p, r.v[keys, :]) * jnp.concatenate(inv_l, axis=0))


def _na_interior_probs(r, s):
    p_tiles, inv_l = _na_softmax(r, s, 1)
    zero_tile = jnp.zeros((GRID_W, LANES), BF16)
    p_rows = []
    for block, tiles in enumerate(p_tiles):
        valid = _na_tiles(1, block % NA_GROUP_ROWS)
        p_rows.append(jnp.concatenate(
            [zero_tile] * valid.start + tiles + [zero_tile] * (NA_KEYS // LANES - valid.stop), axis=1))
    return jnp.concatenate(p_rows, axis=0), jnp.concatenate(inv_l, axis=0)


def _na_interior_output(r, u, p, inv_l):
    _na_store_output(r, u, _dot(p, r.v[pl.ds(_na_key0(r, u), NA_KEYS), :]) * inv_l)


def _na_stage(r, u, carry):
    s_cur, p_prev, l_prev = carry
    s_next = _na_scores(r, jnp.minimum(u + 1, r.n_groups - 2))
    p_cur, l_cur = _na_interior_probs(r, s_cur)
    _na_interior_output(r, u - 1, p_prev, l_prev)
    return s_next, p_cur, l_cur


def _na_kernel(q_ref, kt_ref, v_ref, rows_ref, o_ref, bias_ref, s_ref, p_ref, l_ref, *, n_groups):
    _na_build_tables(rows_ref, bias_ref)
    r = _NaRefs(q_ref, kt_ref, v_ref, bias_ref, o_ref, n_groups)
    last = n_groups - 1
    _na_edge_group(r, 0, 0)
    _na_edge_group(r, last, 2)

    def load_carry():
        return s_ref[...], p_ref[...], l_ref[...]

    def store_carry(carry):
        s_ref[...], p_ref[...], l_ref[...] = carry

    def stages(first, count, carry):
        for k in range(count):
            carry = _na_stage(r, first + k, carry)
        return carry

    def unrolled_stages(i, c):
        store_carry(stages(2 + NA_UNROLL * i, NA_UNROLL, load_carry()))
        return c

    store_carry((_na_scores(r, 2),) + _na_interior_probs(r, _na_scores(r, 1)))
    n_loops, n_tail = divmod(n_groups - 3, NA_UNROLL)
    lax.fori_loop(0, n_loops, unrolled_stages, 0)
    _, p_prev, l_prev = stages(2 + NA_UNROLL * n_loops, n_tail, load_carry())
    _na_interior_output(r, last - 1, p_prev, l_prev)


def _na(qa, kat, va, rows):
    B, T, _ = qa.shape
    n_pairs = NA_HEADS // 2
    return pl.pallas_call(
        functools.partial(_na_kernel, n_groups=T // NA_GROUP),
        grid=(B, n_pairs),
        in_specs=[
            pl.BlockSpec((None, T, LANES), lambda b, p: (b, 0, p)),
            pl.BlockSpec((None, LANES, T), lambda b, p: (b, p, 0)),
            pl.BlockSpec((None, T, LANES), lambda b, p: (b, 0, p)),
            pl.BlockSpec((2,) + rows.shape[1:], lambda b, p: (p, 0, 0)),
        ],
        out_specs=pl.BlockSpec((None, T, LANES), lambda b, p: (b, 0, p)),
        out_shape=jax.ShapeDtypeStruct((B, T, NA_HEADS * NA_HEAD_DIM), BF16),
        scratch_shapes=[pltpu.VMEM((2, 3, NA_GROUP, NA_KEYS), F32),
                        pltpu.VMEM((2 * NA_GROUP, NA_KEYS), F32),
                        pltpu.VMEM((2 * NA_GROUP, NA_KEYS), BF16),
                        pltpu.VMEM((2 * NA_GROUP, 1), F32)],
        compiler_params=pltpu.CompilerParams(
            dimension_semantics=("parallel", "parallel"), vmem_limit_bytes=VMEM_LIMIT),
        name="natten",
    )(qa, kat, va, rows)


class _GlaDir:
    def __init__(self, reverse, ops, v, dec, o, state, a, contrib, prev):
        kw = GLA_HEADS * GLA_DK
        self.reverse = reverse
        self.qd, self.ki, self.kd = (ops.at[:, i * kw:(i + 1) * kw] for i in range(3))
        self.v, self.dec, self.o, self.state = v, dec, o, state
        self.a, self.contrib, self.prev = a, contrib, prev


def _gla_intra(d, c):
    C = GLA_CHUNK
    rows = slice(c * C, (c + 1) * C)
    qd = d.qd[rows, :]
    lane = lax.broadcasted_iota(jnp.int32, qd.shape, 1)
    q_heads = jnp.concatenate(
        [jnp.where((lane >= h * GLA_DK) & (lane < (h + 1) * GLA_DK), qd, jnp.zeros_like(qd))
         for h in range(GLA_HEADS)], axis=0)
    scores = _dot_nt(q_heads, d.ki[rows, :])
    i = lax.broadcasted_iota(jnp.int32, scores.shape, 0) & (C - 1)
    j = lax.broadcasted_iota(jnp.int32, scores.shape, 1)
    keep = (j > i) if d.reverse else (j <= i)
    d.a[c] = jnp.where(keep, scores, 0.0).astype(BF16)
    first_head = lax.broadcasted_iota(jnp.int32, (GLA_DV, LANES), 1) < GLA_DK
    for p in range(GLA_HEADS // 2):
        kd_p = d.kd[rows, p * LANES:(p + 1) * LANES]
        c0, c1 = (_dot_tn(d.v[rows, h * GLA_DV:(h + 1) * GLA_DV], kd_p) for h in (2 * p, 2 * p + 1))
        d.contrib[c, p] = jnp.where(first_head, c0, c1)


def _gla_scan(d, n_chunks):
    order = range(n_chunks - 1, -1, -1) if d.reverse else range(n_chunks)
    for p in range(GLA_HEADS // 2):
        st = d.state[p]
        for c in order:
            d.prev[c, p] = st.astype(BF16)
            st = st * d.dec[c][:, p * LANES:(p + 1) * LANES] + d.contrib[c, p]
        d.state[p] = st


def _gla_outputs(d, c):
    C = GLA_CHUNK
    rows = slice(c * C, (c + 1) * C)
    first_head = lax.broadcasted_iota(jnp.int32, (C, LANES), 1) < GLA_DK
    for p in range(GLA_HEADS // 2):
        qd_p = d.qd[rows, p * LANES:(p + 1) * LANES]
        zero = jnp.zeros_like(qd_p)
        q_pair = jnp.concatenate([jnp.where(first_head, qd_p, zero),
                                  jnp.where(first_head, zero, qd_p)], axis=0)
        inter = _dot_nt(q_pair, d.prev[c, p])
        for hh in range(2):
            h = 2 * p + hh
            cols = slice(h * GLA_DV, (h + 1) * GLA_DV)
            d.o[rows, cols] = _dot(d.a[c, h * C:(h + 1) * C, :], d.v[rows, cols]) + inter[hh * C:(hh + 1) * C]


def _gla_kernel(gf_ref, vf_ref, decf_ref, gb_ref, vb_ref, decb_ref,
                of_ref, ob_ref, sf_ref, sb_ref, *scratch, n_chunks):
    @pl.when(pl.program_id(1) == 0)
    def _():
        sf_ref[...] = jnp.zeros_like(sf_ref)
        sb_ref[...] = jnp.zeros_like(sb_ref)

    n = len(scratch) // 2
    dirs = (_GlaDir(False, gf_ref, vf_ref, decf_ref, of_ref, sf_ref, *scratch[:n]),
            _GlaDir(True, gb_ref, vb_ref, decb_ref, ob_ref, sb_ref, *scratch[n:]))

    for c in range(n_chunks):
        for d in dirs:
            _gla_intra(d, c)
    for d in dirs:
        _gla_scan(d, n_chunks)
    for c in range(n_chunks):
        for d in dirs:
            _gla_outputs(d, c)


def _gla(gf, gb, vg, dec, *, tb):
    B, T, vw = vg.shape
    nb = T // tb
    nc = tb // GLA_CHUNK
    kw = GLA_HEADS * GLA_DK
    n_pairs = GLA_HEADS // 2
    fwd = lambda w: pl.BlockSpec((None, tb, w), lambda b, s: (b, s, 0))
    bwd = lambda w: pl.BlockSpec((None, tb, w), lambda b, s: (b, nb - 1 - s, 0))
    dec_fwd = pl.BlockSpec((None, nc, 1, kw), lambda b, s: (b, s, 0, 0))
    dec_bwd = pl.BlockSpec((None, nc, 1, kw), lambda b, s: (b, nb - 1 - s, 0, 1))
    state = pltpu.VMEM((n_pairs, GLA_DV, LANES), F32)
    per_dir = [
        pltpu.VMEM((nc, GLA_HEADS * GLA_CHUNK, GLA_CHUNK), BF16),
        pltpu.VMEM((nc, n_pairs, GLA_DV, LANES), F32),
        pltpu.VMEM((nc, n_pairs, GLA_DV, LANES), BF16)]
    return pl.pallas_call(
        functools.partial(_gla_kernel, n_chunks=nc),
        grid=(B, nb),
        in_specs=[fwd(3 * kw), fwd(vw), dec_fwd, bwd(3 * kw), bwd(vw), dec_bwd],
        out_specs=[fwd(vw), bwd(vw)],
        out_shape=[jax.ShapeDtypeStruct((B, T, vw), F32)] * 2,
        scratch_shapes=[state, state] + per_dir * 2,
        compiler_params=pltpu.CompilerParams(
            dimension_semantics=("parallel", "arbitrary"), vmem_limit_bytes=VMEM_LIMIT),
        name="gla",
    )(gf, vg, dec, gb, vg, dec)


def _mix_ffn_kernel(x_ref, yna_ref, of_ref, ob_ref, sr_ref, gng_ref, wout_ref, gff_ref,
                    w1_ref, w2_ref, gfin_ref, out_ref, *, final_norm, ff_chunk):
    o = of_ref[...] + ob_ref[...]
    parts = [_rms(o[:, h * GLA_DV:(h + 1) * GLA_DV], gng_ref[...]) for h in range(GLA_HEADS)]
    y_gla = (jnp.concatenate(parts, axis=-1) * sr_ref[...].astype(F32)).astype(BF16)
    na_w = yna_ref.shape[-1]
    mix = _dot(yna_ref[...], wout_ref[0:na_w, :]) + _dot(y_gla, wout_ref[na_w:, :])
    h1 = x_ref[...] + mix
    n2 = _rms(h1, gff_ref[...]).astype(BF16)
    ffn = None
    for f in range(w1_ref.shape[1] // ff_chunk):
        u = _dot(n2, w1_ref[:, f * ff_chunk:(f + 1) * ff_chunk])
        a = jnp.square(jnp.maximum(u, 0.0)).astype(BF16)
        d = _dot(a, w2_ref[f * ff_chunk:(f + 1) * ff_chunk, :])
        ffn = d if ffn is None else ffn + d
    h2 = h1 + ffn
    out_ref[...] = _rms(h2, gfin_ref[...]) if final_norm else h2


def _mix_ffn(x, yna, of, ob, sr, gng, wout, gff, w1, w2, gfin, *, tm, final_norm):
    B, T, D = x.shape
    tok = lambda w: pl.BlockSpec((None, tm, w), lambda b, i: (b, i, 0))
    const = lambda a: pl.BlockSpec(a.shape, lambda b, i: (0,) * a.ndim,
                                   pipeline_mode=pl.Buffered(1))
    return pl.pallas_call(
        functools.partial(_mix_ffn_kernel, final_norm=final_norm, ff_chunk=1024),
        grid=(B, T // tm),
        in_specs=[tok(D), tok(yna.shape[-1]), tok(of.shape[-1]), tok(ob.shape[-1]),
                  tok(sr.shape[-1]), const(gng), const(wout), const(gff), const(w1),
                  const(w2), const(gfin)],
        out_specs=tok(D),
        out_shape=jax.ShapeDtypeStruct((B, T, D), F32),
        compiler_params=pltpu.CompilerParams(
            dimension_semantics=("parallel", "parallel"), vmem_limit_bytes=VMEM_LIMIT),
        name="mix_ffn",
    )(x, yna, of, ob, sr, gng, wout, gff, w1, w2, gfin)


def kernel(x, ln_mix_g, w_in, na_rpb, gla_gate_up_fwd, gla_gate_bias_fwd, gla_gate_up_bwd,
           gla_gate_bias_bwd, gla_norm_g, w_out, ln_ff_g, w_ff1, w_ff2, ln_final_g):
    B, T, D = x.shape
    depth = w_in.shape[0]
    assert T % NA_GROUP == 0 and T // NA_GROUP >= 3 and T % 512 == 0
    kw = GLA_HEADS * GLA_DK
    row = lambda v: v.reshape(1, -1).astype(F32)
    h = x
    for l in range(depth):
        zeros = jnp.zeros((GLA_GATE_RANK, kw), F32)
        gu = jnp.concatenate([
            jnp.concatenate([gla_gate_up_fwd[l], zeros], axis=1),
            jnp.concatenate([zeros, gla_gate_up_bwd[l]], axis=1)], axis=0).astype(BF16)
        gb = jnp.concatenate([gla_gate_bias_fwd[l], gla_gate_bias_bwd[l]]).reshape(1, -1)
        qa, kat, va, vg, sr, gla_f, gla_b, dec = _inproj(
            h, row(ln_mix_g[l]), w_in[l], gu, gb.astype(F32), tm=512)
        y_na = _na(qa, kat, va, _na_bias_rows(na_rpb[l]))
        o_f, o_b = _gla(gla_f, gla_b, vg, dec, tb=512)
        h = _mix_ffn(h, y_na, o_f, o_b, sr, row(gla_norm_g[l]), w_out[l].astype(BF16),
                     row(ln_ff_g[l]), w_ff1[l].astype(BF16), w_ff2[l].astype(BF16),
                     row(ln_final_g), tm=512, final_norm=(l == depth - 1))
    return h
```

```python
import functools

import jax
import jax.numpy as jnp
from jax import lax
from jax.experimental import pallas as pl
from jax.experimental.pallas import tpu as pltpu

F32 = jnp.float32
BF16 = jnp.bfloat16

EPS = 1e-6
GRID_W = 64
NA_HEADS = 8
NA_HEAD_DIM = 64
NA_KH = 8
NA_KW = 16
GLA_HEADS = 4
GLA_DK = 64
GLA_DV = 128
GLA_GATE_RANK = 16
GLA_GATE_NORM = 16.0
GLA_CHUNK = 64

LANES = 128
NA_GROUP_ROWS = 4
NA_GROUP = NA_GROUP_ROWS * GRID_W
NA_KEY_ROWS = 12
NA_KEYS = NA_KEY_ROWS * GRID_W
NA_UNROLL = 4
LOG2E = 1.4426950408889634
VMEM_LIMIT = 56 * 1024 * 1024


def _dot(a, b):
    return jnp.dot(a, b, preferred_element_type=F32)


def _dot_nt(a, b):
    return lax.dot_general(a, b, (((1,), (1,)), ((), ())), preferred_element_type=F32)


def _dot_tn(a, b):
    return lax.dot_general(a, b, (((0,), (0,)), ((), ())), preferred_element_type=F32)


def _rms(x, g):
    return x * lax.rsqrt(jnp.mean(x * x, axis=-1, keepdims=True) + EPS) * g


def _segmented_cumsum(x, reverse):
    n = x.shape[0]
    pos = lax.broadcasted_iota(jnp.int32, x.shape, 0) & (GLA_CHUNK - 1)
    step = 1
    while step < GLA_CHUNK:
        if reverse:
            shifted, ok = pltpu.roll(x, n - step, 0), pos < GLA_CHUNK - step
        else:
            shifted, ok = pltpu.roll(x, step, 0), pos >= step
        x = x + jnp.where(ok, shifted, 0.0)
        step *= 2
    return x


def _gla_cumdecay(log_a, reverse):
    b = _segmented_cumsum(log_a, reverse)
    b3 = b.reshape(-1, GLA_CHUNK, b.shape[-1])
    last = 0 if reverse else GLA_CHUNK - 1
    return b, b3[:, last:last + 1, :]


def _gla_operands(q, k, b, b_last):
    b3 = b.reshape(-1, GLA_CHUNK, b.shape[-1])
    k_dec = (k.reshape(b3.shape) * jnp.exp2(b_last - b3)).reshape(b.shape)
    ops = jnp.concatenate([q * jnp.exp2(b), k * jnp.exp2(-b), k_dec], axis=-1)
    return ops.astype(BF16), jnp.exp2(b_last)


def _inproj_kernel(x_ref, g_ref, wt_ref, gu_ref, gb_ref,
                   qa_ref, kat_ref, va_ref, vg_ref, sr_ref, gf_ref, gbw_ref, dec_ref, wb_ref):
    kw = GLA_HEADS * GLA_DK
    n_main = wt_ref.shape[0] - 2 * GLA_GATE_RANK

    @pl.when((pl.program_id(0) == 0) & (pl.program_id(1) == 0))
    def _():
        wb_ref[...] = wt_ref[...].astype(BF16)

    def proj(rows):
        return _dot_nt(n, wb_ref[rows, :])

    n = _rms(x_ref[...], g_ref[...]).astype(BF16)

    def log_decay(cols):
        pre = _dot(z, gu_ref[:, cols]) + gb_ref[:, cols]
        softplus2 = jnp.log2(1.0 + jnp.exp2(jnp.abs(pre) * -LOG2E))
        return jnp.minimum(pre * (LOG2E / GLA_GATE_NORM), 0.0) - softplus2 * (1.0 / GLA_GATE_NORM)

    z = proj(slice(n_main, None)).astype(BF16)
    b_f, last_f = _gla_cumdecay(log_decay(slice(0, kw)), False)
    qa_ref[...] = (proj(slice(0, 512)) * (NA_HEAD_DIM ** -0.5 * LOG2E)).astype(BF16)
    b_b, last_b = _gla_cumdecay(log_decay(slice(kw, 2 * kw)), True)
    kat_ref[...] = _dot_nt(wb_ref[512:1024, :], n).astype(BF16)
    qg = proj(slice(1536, 1792)) * (GLA_DK ** -0.5)
    kg = proj(slice(1792, 2048))
    gf_ref[...], dec_f = _gla_operands(qg, kg, b_f, last_f)
    va_ref[...] = proj(slice(1024, 1536)).astype(BF16)
    gbw_ref[...], dec_b = _gla_operands(qg, kg, b_b, last_b)
    dec_ref[...] = jnp.concatenate([dec_f, dec_b], axis=-1)
    vg_ref[...] = proj(slice(2048, 2560)).astype(BF16)
    r = proj(slice(2560, 3072))
    sr_ref[...] = (r * jax.nn.sigmoid(r)).astype(BF16)


def _inproj(x, g, wt, gu, gb, *, tm):
    B, T, D = x.shape
    kw = GLA_HEADS * GLA_DK
    nc = tm // GLA_CHUNK
    tok = lambda w: pl.BlockSpec((None, tm, w), lambda b, i: (b, i, 0))
    const = lambda a: pl.BlockSpec(a.shape, lambda b, i: (0,) * a.ndim,
                                   pipeline_mode=pl.Buffered(1))
    outs = [
        (jax.ShapeDtypeStruct((B, T, 512), BF16), tok(512)),
        (jax.ShapeDtypeStruct((B, 512, T), BF16),
         pl.BlockSpec((None, 512, tm), lambda b, i: (b, 0, i))),
        (jax.ShapeDtypeStruct((B, T, 512), BF16), tok(512)),
        (jax.ShapeDtypeStruct((B, T, 512), BF16), tok(512)),
        (jax.ShapeDtypeStruct((B, T, 512), BF16), tok(512)),
        (jax.ShapeDtypeStruct((B, T, 3 * kw), BF16), tok(3 * kw)),
        (jax.ShapeDtypeStruct((B, T, 3 * kw), BF16), tok(3 * kw)),
        (jax.ShapeDtypeStruct((B, T // GLA_CHUNK, 1, 2 * kw), F32),
         pl.BlockSpec((None, nc, 1, 2 * kw), lambda b, i: (b, i, 0, 0))),
    ]
    return pl.pallas_call(
        _inproj_kernel,
        grid=(B, T // tm),
        in_specs=[tok(D), const(g), const(wt), const(gu), const(gb)],
        out_specs=[s for _, s in outs],
        out_shape=[s for s, _ in outs],
        scratch_shapes=[pltpu.VMEM(wt.shape, BF16)],
        compiler_params=pltpu.CompilerParams(
            dimension_semantics=("arbitrary", "arbitrary"), vmem_limit_bytes=VMEM_LIMIT),
        name="inproj",
    )(x, g, wt, gu, gb)


def _na_bias_rows(rpb):
    n_dc = 2 * NA_KW - 1
    padded = jnp.pad(rpb.astype(F32), ((0, 0), (4, 5), (0, GRID_W - n_dc)))
    return jnp.concatenate([padded[:, :-1], padded[:, 1:]], axis=-1)


def _na_tiles(table, g):
    i_start = (0, g, NA_KEY_ROWS - NA_KH)[table]
    return range(i_start // 2, (i_start + NA_KH + 1) // 2)


def _na_build_tables(rows_ref, tab_ref):
    qc = lax.broadcasted_iota(jnp.int32, (GRID_W, LANES), 0)
    lane = lax.broadcasted_iota(jnp.int32, (GRID_W, LANES), 1)
    kc = lane & (GRID_W - 1)
    col_start = jnp.clip(qc - NA_KW // 2, 0, GRID_W - NA_KW)
    in_win = (kc >= col_start) & (kc < col_start + NA_KW)
    valid = {(True, True): in_win,
             (True, False): in_win & (lane < GRID_W),
             (False, True): in_win & (lane >= GRID_W)}
    row_offset = (0, -(NA_KH // 2), -NA_KH)
    for hh in range(2):
        for t in range(3):
            for g in range(NA_GROUP_ROWS):
                i_start = (0, g, NA_KEY_ROWS - NA_KH)[t]
                for m in _na_tiles(t, g):
                    halves = tuple(i_start <= i < i_start + NA_KH for i in (2 * m, 2 * m + 1))
                    a = 2 * m - g + row_offset[t] + NA_KH - 1 + 4
                    src = jnp.broadcast_to(rows_ref[hh, a:a + 1, :], (GRID_W, LANES))
                    rot = pltpu.roll(src, LANES - (NA_KW - 1), 1, stride=1, stride_axis=0)
                    tab_ref[hh, t, g * GRID_W:(g + 1) * GRID_W, m * LANES:(m + 1) * LANES] = (
                        jnp.where(valid[halves], rot * LOG2E, -jnp.inf))


class _NaRefs:
    def __init__(self, q, kt, v, bias, o, n_groups):
        self.q, self.kt, self.v, self.bias, self.o, self.n_groups = q, kt, v, bias, o, n_groups


def _na_rows(u):
    return pl.ds(pl.multiple_of(u * NA_GROUP, NA_GROUP), NA_GROUP)


def _na_key0(r, u):
    return pl.multiple_of(jnp.clip(u - 1, 0, r.n_groups - 3) * NA_GROUP, NA_GROUP)


def _na_scores(r, u):
    qq = r.q[_na_rows(u), :]
    first_head = lax.broadcasted_iota(jnp.int32, qq.shape, 1) < NA_HEAD_DIM
    zero = jnp.zeros_like(qq)
    q_heads = jnp.concatenate([jnp.where(first_head, qq, zero),
                               jnp.where(first_head, zero, qq)], axis=0)
    return _dot(q_heads, r.kt[:, pl.ds(_na_key0(r, u), NA_KEYS)])


def _na_softmax(r, s, table):
    p_tiles, inv_l = [], []
    for hh in range(2):
        for g in range(NA_GROUP_ROWS):
            rows = slice(hh * NA_GROUP + g * GRID_W, hh * NA_GROUP + (g + 1) * GRID_W)
            sb = [s[rows, c * LANES:(c + 1) * LANES]
                  + r.bias[hh, table, g * GRID_W:(g + 1) * GRID_W, c * LANES:(c + 1) * LANES]
                  for c in _na_tiles(table, g)]
            m = jnp.max(functools.reduce(jnp.maximum, sb), axis=-1, keepdims=True)
            p = [jnp.exp2(x - m) for x in sb]
            inv_l.append(1.0 / jnp.sum(functools.reduce(jnp.add, p), axis=-1, keepdims=True))
            p_tiles.append([x.astype(BF16) for x in p])
    return p_tiles, inv_l


def _na_store_output(r, u, o):
    first_head = lax.broadcasted_iota(jnp.int32, (NA_GROUP, LANES), 1) < NA_HEAD_DIM
    r.o[_na_rows(u), :] = jnp.where(first_head, o[:NA_GROUP], o[NA_GROUP:]).astype(r.o.dtype)


def _na_edge_group(r, u, table):
    tiles = _na_tiles(table, 0)
    p_tiles, inv_l = _na_softmax(r, _na_scores(r, u), table)
    p = jnp.concatenate([jnp.concatenate(t, axis=1) for t in p_tiles], axis=0)
    keys = pl.ds(_na_key0(r, u) + tiles.start * LANES, len(tiles) * LANES)
    _na_store_output(r, u, _dot(p, r.v[keys, :]) * jnp.concatenate(inv_l, axis=0))


def _na_interior_probs(r, s):
    p_tiles, inv_l = _na_softmax(r, s, 1)
    zero_tile = jnp.zeros((GRID_W, LANES), BF16)
    p_rows = []
    for block, tiles in enumerate(p_tiles):
        valid = _na_tiles(1, block % NA_GROUP_ROWS)
        p_rows.append(jnp.concatenate(
            [zero_tile] * valid.start + tiles + [zero_tile] * (NA_KEYS // LANES - valid.stop), axis=1))
    return jnp.concatenate(p_rows, axis=0), jnp.concatenate(inv_l, axis=0)


def _na_interior_output(r, u, p, inv_l):
    _na_store_output(r, u, _dot(p, r.v[pl.ds(_na_key0(r, u), NA_KEYS), :]) * inv_l)


def _na_stage(r, u, carry):
    s_cur, p_prev, l_prev = carry
    s_next = _na_scores(r, jnp.minimum(u + 1, r.n_groups - 2))
    p_cur, l_cur = _na_interior_probs(r, s_cur)
    _na_interior_output(r, u - 1, p_prev, l_prev)
    return s_next, p_cur, l_cur


def _na_kernel(q_ref, kt_ref, v_ref, rows_ref, o_ref, bias_ref, s_ref, p_ref, l_ref, *, n_groups):
    _na_build_tables(rows_ref, bias_ref)
    r = _NaRefs(q_ref, kt_ref, v_ref, bias_ref, o_ref, n_groups)
    last = n_groups - 1
    _na_edge_group(r, 0, 0)
    _na_edge_group(r, last, 2)

    def load_carry():
        return s_ref[...], p_ref[...], l_ref[...]

    def store_carry(carry):
        s_ref[...], p_ref[...], l_ref[...] = carry

    def stages(first, count, carry):
        for k in range(count):
            carry = _na_stage(r, first + k, carry)
        return carry

    def unrolled_stages(i, c):
        store_carry(stages(2 + NA_UNROLL * i, NA_UNROLL, load_carry()))
        return c

    store_carry((_na_scores(r, 2),) + _na_interior_probs(r, _na_scores(r, 1)))
    n_loops, n_tail = divmod(n_groups - 3, NA_UNROLL)
    lax.fori_loop(0, n_loops, unrolled_stages, 0)
    _, p_prev, l_prev = stages(2 + NA_UNROLL * n_loops, n_tail, load_carry())
    _na_interior_output(r, last - 1, p_prev, l_prev)


def _na(qa, kat, va, rows):
    B, T, _ = qa.shape
    n_pairs = NA_HEADS // 2
    return pl.pallas_call(
        functools.partial(_na_kernel, n_groups=T // NA_GROUP),
        grid=(B, n_pairs),
        in_specs=[
            pl.BlockSpec((None, T, LANES), lambda b, p: (b, 0, p)),
            pl.BlockSpec((None, LANES, T), lambda b, p: (b, p, 0)),
            pl.BlockSpec((None, T, LANES), lambda b, p: (b, 0, p)),
            pl.BlockSpec((2,) + rows.shape[1:], lambda b, p: (p, 0, 0)),
        ],
        out_specs=pl.BlockSpec((None, T, LANES), lambda b, p: (b, 0, p)),
        out_shape=jax.ShapeDtypeStruct((B, T, NA_HEADS * NA_HEAD_DIM), BF16),
        scratch_shapes=[pltpu.VMEM((2, 3, NA_GROUP, NA_KEYS), F32),
                        pltpu.VMEM((2 * NA_GROUP, NA_KEYS), F32),
                        pltpu.VMEM((2 * NA_GROUP, NA_KEYS), BF16),
                        pltpu.VMEM((2 * NA_GROUP, 1), F32)],
        compiler_params=pltpu.CompilerParams(
            dimension_semantics=("parallel", "parallel"), vmem_limit_bytes=VMEM_LIMIT),
        name="natten",
    )(qa, kat, va, rows)


class _GlaDir:
    def __init__(self, reverse, ops, v, dec, o, state, a, contrib, prev):
        kw = GLA_HEADS * GLA_DK
        self.reverse = reverse
        self.qd, self.ki, self.kd = (ops.at[:, i * kw:(i + 1) * kw] for i in range(3))
        self.v, self.dec, self.o, self.state = v, dec, o, state
        self.a, self.contrib, self.prev = a, contrib, prev


def _gla_intra(d, c):
    C = GLA_CHUNK
    rows = slice(c * C, (c + 1) * C)
    qd = d.qd[rows, :]
    lane = lax.broadcasted_iota(jnp.int32, qd.shape, 1)
    q_heads = jnp.concatenate(
        [jnp.where((lane >= h * GLA_DK) & (lane < (h + 1) * GLA_DK), qd, jnp.zeros_like(qd))
         for h in range(GLA_HEADS)], axis=0)
    scores = _dot_nt(q_heads, d.ki[rows, :])
    i = lax.broadcasted_iota(jnp.int32, scores.shape, 0) & (C - 1)
    j = lax.broadcasted_iota(jnp.int32, scores.shape, 1)
    keep = (j > i) if d.reverse else (j <= i)
    d.a[c] = jnp.where(keep, scores, 0.0).astype(BF16)
    first_head = lax.broadcasted_iota(jnp.int32, (GLA_DV, LANES), 1) < GLA_DK
    for p in range(GLA_HEADS // 2):
        kd_p = d.kd[rows, p * LANES:(p + 1) * LANES]
        c0, c1 = (_dot_tn(d.v[rows, h * GLA_DV:(h + 1) * GLA_DV], kd_p) for h in (2 * p, 2 * p + 1))
        d.contrib[c, p] = jnp.where(first_head, c0, c1)


def _gla_scan(d, n_chunks):
    order = range(n_chunks - 1, -1, -1) if d.reverse else range(n_chunks)
    for p in range(GLA_HEADS // 2):
        st = d.state[p]
        for c in order:
            d.prev[c, p] = st.astype(BF16)
            st = st * d.dec[c][:, p * LANES:(p + 1) * LANES] + d.contrib[c, p]
        d.state[p] = st


def _gla_outputs(d, c):
    C = GLA_CHUNK
    rows = slice(c * C, (c + 1) * C)
    first_head = lax.broadcasted_iota(jnp.int32, (C, LANES), 1) < GLA_DK
    for p in range(GLA_HEADS // 2):
        qd_p = d.qd[rows, p * LANES:(p + 1) * LANES]
        zero = jnp.zeros_like(qd_p)
        q_pair = jnp.concatenate([jnp.where(first_head, qd_p, zero),
                                  jnp.where(first_head, zero, qd_p)], axis=0)
        inter = _dot_nt(q_pair, d.prev[c, p])
        for hh in range(2):
            h = 2 * p + hh
            cols = slice(h * GLA_DV, (h + 1) * GLA_DV)
            d.o[rows, cols] = _dot(d.a[c, h * C:(h + 1) * C, :], d.v[rows, cols]) + inter[hh * C:(hh + 1) * C]


def _gla_kernel(gf_ref, vf_ref, decf_ref, gb_ref, vb_ref, decb_ref,
                of_ref, ob_ref, sf_ref, sb_ref, *scratch, n_chunks):
    @pl.when(pl.program_id(1) == 0)
    def _():
        sf_ref[...] = jnp.zeros_like(sf_ref)
        sb_ref[...] = jnp.zeros_like(sb_ref)

    n = len(scratch) // 2
    dirs = (_GlaDir(False, gf_ref, vf_ref, decf_ref, of_ref, sf_ref, *scratch[:n]),
            _GlaDir(True, gb_ref, vb_ref, decb_ref, ob_ref, sb_ref, *scratch[n:]))

    for c in range(n_chunks):
        for d in dirs:
            _gla_intra(d, c)
    for d in dirs:
        _gla_scan(d, n_chunks)
    for c in range(n_chunks):
        for d in dirs:
            _gla_outputs(d, c)


def _gla(gf, gb, vg, dec, *, tb):
    B, T, vw = vg.shape
    nb = T // tb
    nc = tb // GLA_CHUNK
    kw = GLA_HEADS * GLA_DK
    n_pairs = GLA_HEADS // 2
    fwd = lambda w: pl.BlockSpec((None, tb, w), lambda b, s: (b, s, 0))
    bwd = lambda w: pl.BlockSpec((None, tb, w), lambda b, s: (b, nb - 1 - s, 0))
    dec_fwd = pl.BlockSpec((None, nc, 1, kw), lambda b, s: (b, s, 0, 0))
    dec_bwd = pl.BlockSpec((None, nc, 1, kw), lambda b, s: (b, nb - 1 - s, 0, 1))
    state = pltpu.VMEM((n_pairs, GLA_DV, LANES), F32)
    per_dir = [
        pltpu.VMEM((nc, GLA_HEADS * GLA_CHUNK, GLA_CHUNK), BF16),
        pltpu.VMEM((nc, n_pairs, GLA_DV, LANES), F32),
        pltpu.VMEM((nc, n_pairs, GLA_DV, LANES), BF16)]
    return pl.pallas_call(
        functools.partial(_gla_kernel, n_chunks=nc),
        grid=(B, nb),
        in_specs=[fwd(3 * kw), fwd(vw), dec_fwd, bwd(3 * kw), bwd(vw), dec_bwd],
        out_specs=[fwd(vw), bwd(vw)],
        out_shape=[jax.ShapeDtypeStruct((B, T, vw), F32)] * 2,
        scratch_shapes=[state, state] + per_dir * 2,
        compiler_params=pltpu.CompilerParams(
            dimension_semantics=("parallel", "arbitrary"), vmem_limit_bytes=VMEM_LIMIT),
        name="gla",
    )(gf, vg, dec, gb, vg, dec)


def _mix_ffn_kernel(x_ref, yna_ref, of_ref, ob_ref, sr_ref, gng_ref, wout_ref, gff_ref,
                    w1_ref, w2_ref, gfin_ref, out_ref, *, final_norm, ff_chunk):
    o = of_ref[...] + ob_ref[...]
    parts = [_rms(o[:, h * GLA_DV:(h + 1) * GLA_DV], gng_ref[...]) for h in range(GLA_HEADS)]
    y_gla = (jnp.concatenate(parts, axis=-1) * sr_ref[...].astype(F32)).astype(BF16)
    na_w = yna_ref.shape[-1]
    mix = _dot(yna_ref[...], wout_ref[0:na_w, :]) + _dot(y_gla, wout_ref[na_w:, :])
    h1 = x_ref[...] + mix
    n2 = _rms(h1, gff_ref[...]).astype(BF16)
    ffn = None
    for f in range(w1_ref.shape[1] // ff_chunk):
        u = _dot(n2, w1_ref[:, f * ff_chunk:(f + 1) * ff_chunk])
        a = jnp.square(jnp.maximum(u, 0.0)).astype(BF16)
        d = _dot(a, w2_ref[f * ff_chunk:(f + 1) * ff_chunk, :])
        ffn = d if ffn is None else ffn + d
    h2 = h1 + ffn
    out_ref[...] = _rms(h2, gfin_ref[...]) if final_norm else h2


def _mix_ffn(x, yna, of, ob, sr, gng, wout, gff, w1, w2, gfin, *, tm, final_norm):
    B, T, D = x.shape
    tok = lambda w: pl.BlockSpec((None, tm, w), lambda b, i: (b, i, 0))
    const = lambda a: pl.BlockSpec(a.shape, lambda b, i: (0,) * a.ndim,
                                   pipeline_mode=pl.Buffered(1))
    return pl.pallas_call(
        functools.partial(_mix_ffn_kernel, final_norm=final_norm, ff_chunk=1024),
        grid=(B, T // tm),
        in_specs=[tok(D), tok(yna.shape[-1]), tok(of.shape[-1]), tok(ob.shape[-1]),
                  tok(sr.shape[-1]), const(gng), const(wout), const(gff), const(w1),
                  const(w2), const(gfin)],
        out_specs=tok(D),
        out_shape=jax.ShapeDtypeStruct((B, T, D), F32),
        compiler_params=pltpu.CompilerParams(
            dimension_semantics=("parallel", "parallel"), vmem_limit_bytes=VMEM_LIMIT),
        name="mix_ffn",
    )(x, yna, of, ob, sr, gng, wout, gff, w1, w2, gfin)


def kernel(x, ln_mix_g, w_in, na_rpb, gla_gate_up_fwd, gla_gate_bias_fwd, gla_gate_up_bwd,
           gla_gate_bias_bwd, gla_norm_g, w_out, ln_ff_g, w_ff1, w_ff2, ln_final_g):
    B, T, D = x.shape
    depth = w_in.shape[0]
    assert T % NA_GROUP == 0 and T // NA_GROUP >= 3 and T % 512 == 0
    kw = GLA_HEADS * GLA_DK
    row = lambda v: v.reshape(1, -1).astype(F32)
    h = x
    for l in range(depth):
        zeros = jnp.zeros((GLA_GATE_RANK, kw), F32)
        gu = jnp.concatenate([
            jnp.concatenate([gla_gate_up_fwd[l], zeros], axis=1),
            jnp.concatenate([zeros, gla_gate_up_bwd[l]], axis=1)], axis=0).astype(BF16)
        gb = jnp.concatenate([gla_gate_bias_fwd[l], gla_gate_bias_bwd[l]]).reshape(1, -1)
        qa, kat, va, vg, sr, gla_f, gla_b, dec = _inproj(
            h, row(ln_mix_g[l]), w_in[l].T, gu, gb.astype(F32), tm=512)
        y_na = _na(qa, kat, va, _na_bias_rows(na_rpb[l]))
        o_f, o_b = _gla(gla_f, gla_b, vg, dec, tb=512)
        h = _mix_ffn(h, y_na, o_f, o_b, sr, row(gla_norm_g[l]), w_out[l].astype(BF16),
                     row(ln_ff_g[l]), w_ff1[l].astype(BF16), w_ff2[l].astype(BF16),
                     row(ln_final_g), tm=512, final_norm=(l == depth - 1))
    return h
```

```python
import functools

import jax
import jax.numpy as jnp
from jax import lax
from jax.experimental import pallas as pl
from jax.experimental.pallas import tpu as pltpu

F32 = jnp.float32
BF16 = jnp.bfloat16

EPS = 1e-6
GRID_W = 64
NA_HEADS = 8
NA_HEAD_DIM = 64
NA_KH = 8
NA_KW = 16
GLA_HEADS = 4
GLA_DK = 64
GLA_DV = 128
GLA_GATE_RANK = 16
GLA_GATE_NORM = 16.0
GLA_CHUNK = 64

LANES = 128
NA_GROUP_ROWS = 4
NA_GROUP = NA_GROUP_ROWS * GRID_W
NA_KEY_ROWS = 12
NA_KEYS = NA_KEY_ROWS * GRID_W
NA_UNROLL = 4
LOG2E = 1.4426950408889634
VMEM_LIMIT = 56 * 1024 * 1024


def _dot(a, b):
    return jnp.dot(a, b, preferred_element_type=F32)


def _dot_nt(a, b):
    return lax.dot_general(a, b, (((1,), (1,)), ((), ())), preferred_element_type=F32)


def _dot_tn(a, b):
    return lax.dot_general(a, b, (((0,), (0,)), ((), ())), preferred_element_type=F32)


def _rms(x, g):
    return x * lax.rsqrt(jnp.mean(x * x, axis=-1, keepdims=True) + EPS) * g


def _segmented_cumsum(x, reverse):
    n = x.shape[0]
    pos = lax.broadcasted_iota(jnp.int32, x.shape, 0) & (GLA_CHUNK - 1)
    step = 1
    while step < GLA_CHUNK:
        if reverse:
            shifted, ok = pltpu.roll(x, n - step, 0), pos < GLA_CHUNK - step
        else:
            shifted, ok = pltpu.roll(x, step, 0), pos >= step
        x = x + jnp.where(ok, shifted, 0.0)
        step *= 2
    return x


def _gla_cumdecay(log_a, reverse):
    b = _segmented_cumsum(log_a, reverse)
    b3 = b.reshape(-1, GLA_CHUNK, b.shape[-1])
    last = 0 if reverse else GLA_CHUNK - 1
    return b, b3[:, last:last + 1, :]


def _gla_operands(q, k, b, b_last):
    b3 = b.reshape(-1, GLA_CHUNK, b.shape[-1])
    k_dec = (k.reshape(b3.shape) * jnp.exp2(b_last - b3)).reshape(b.shape)
    ops = jnp.concatenate([q * jnp.exp2(b), k * jnp.exp2(-b), k_dec], axis=-1)
    return ops.astype(BF16), jnp.exp2(b_last)


def _inproj_kernel(x_ref, g_ref, wt_ref, gu_ref, gb_ref,
                   qa_ref, kat_ref, va_ref, vg_ref, sr_ref, gf_ref, gbw_ref, dec_ref, wb_ref):
    kw = GLA_HEADS * GLA_DK
    n_main = wt_ref.shape[0] - 2 * GLA_GATE_RANK

    @pl.when((pl.program_id(0) == 0) & (pl.program_id(1) == 0))
    def _():
        wb_ref[...] = wt_ref[...].astype(BF16)

    def proj(rows):
        return _dot_nt(n, wb_ref[rows, :])

    n = _rms(x_ref[...], g_ref[...]).astype(BF16)

    def log_decay(cols):
        pre = _dot(z, gu_ref[:, cols]) + gb_ref[:, cols]
        softplus2 = jnp.log2(1.0 + jnp.exp2(jnp.abs(pre) * -LOG2E))
        return jnp.minimum(pre * (LOG2E / GLA_GATE_NORM), 0.0) - softplus2 * (1.0 / GLA_GATE_NORM)

    z = proj(slice(n_main, None)).astype(BF16)
    b_f, last_f = _gla_cumdecay(log_decay(slice(0, kw)), False)
    qa_ref[...] = (proj(slice(0, 512)) * (NA_HEAD_DIM ** -0.5 * LOG2E)).astype(BF16)
    b_b, last_b = _gla_cumdecay(log_decay(slice(kw, 2 * kw)), True)
    kat_ref[...] = _dot_nt(wb_ref[512:1024, :], n).astype(BF16)
    qg = proj(slice(1536, 1792)) * (GLA_DK ** -0.5)
    kg = proj(slice(1792, 2048))
    gf_ref[...], dec_f = _gla_operands(qg, kg, b_f, last_f)
    va_ref[...] = proj(slice(1024, 1536)).astype(BF16)
    gbw_ref[...], dec_b = _gla_operands(qg, kg, b_b, last_b)
    dec_ref[...] = jnp.concatenate([dec_f, dec_b], axis=-1)
    vg_ref[...] = proj(slice(2048, 2560)).astype(BF16)
    r = proj(slice(2560, 3072))
    sr_ref[...] = (r * jax.nn.sigmoid(r)).astype(BF16)


def _inproj(x, g, wt, gu, gb, *, tm):
    B, T, D = x.shape
    kw = GLA_HEADS * GLA_DK
    nc = tm // GLA_CHUNK
    tok = lambda w: pl.BlockSpec((None, tm, w), lambda b, i: (b, i, 0))
    const = lambda a: pl.BlockSpec(a.shape, lambda b, i: (0,) * a.ndim,
                                   pipeline_mode=pl.Buffered(1))
    outs = [
        (jax.ShapeDtypeStruct((B, T, 512), BF16), tok(512)),
        (jax.ShapeDtypeStruct((B, 512, T), BF16),
         pl.BlockSpec((None, 512, tm), lambda b, i: (b, 0, i))),
        (jax.ShapeDtypeStruct((B, T, 512), BF16), tok(512)),
        (jax.ShapeDtypeStruct((B, T, 512), BF16), tok(512)),
        (jax.ShapeDtypeStruct((B, T, 512), BF16), tok(512)),
        (jax.ShapeDtypeStruct((B, T, 3 * kw), BF16), tok(3 * kw)),
        (jax.ShapeDtypeStruct((B, T, 3 * kw), BF16), tok(3 * kw)),
        (jax.ShapeDtypeStruct((B, T // GLA_CHUNK, 1, 2 * kw), F32),
         pl.BlockSpec((None, nc, 1, 2 * kw), lambda b, i: (b, i, 0, 0))),
    ]
    return pl.pallas_call(
        _inproj_kernel,
        grid=(B, T // tm),
        in_specs=[tok(D), const(g), const(wt), const(gu), const(gb)],
        out_specs=[s for _, s in outs],
        out_shape=[s for s, _ in outs],
        scratch_shapes=[pltpu.VMEM(wt.shape, BF16)],
        compiler_params=pltpu.CompilerParams(
            dimension_semantics=("arbitrary", "arbitrary"), vmem_limit_bytes=VMEM_LIMIT),
        name="inproj",
    )(x, g, wt, gu, gb)


def _na_bias_rows(rpb):
    n_dc = 2 * NA_KW - 1
    padded = jnp.pad(rpb.astype(F32), ((0, 0), (4, 5), (0, GRID_W - n_dc)))
    return jnp.concatenate([padded[:, :-1], padded[:, 1:]], axis=-1)


def _na_tiles(table, g):
    i_start = (0, g, NA_KEY_ROWS - NA_KH)[table]
    return range(i_start // 2, (i_start + NA_KH + 1) // 2)


def _na_build_tables(rows_ref, tab_ref):
    qc = lax.broadcasted_iota(jnp.int32, (GRID_W, LANES), 0)
    lane = lax.broadcasted_iota(jnp.int32, (GRID_W, LANES), 1)
    kc = lane & (GRID_W - 1)
    col_start = jnp.clip(qc - NA_KW // 2, 0, GRID_W - NA_KW)
    in_win = (kc >= col_start) & (kc < col_start + NA_KW)
    valid = {(True, True): in_win,
             (True, False): in_win & (lane < GRID_W),
             (False, True): in_win & (lane >= GRID_W)}
    row_offset = (0, -(NA_KH // 2), -NA_KH)
    for hh in range(2):
        for t in range(3):
            for g in range(NA_GROUP_ROWS):
                i_start = (0, g, NA_KEY_ROWS - NA_KH)[t]
                for m in _na_tiles(t, g):
                    halves = tuple(i_start <= i < i_start + NA_KH for i in (2 * m, 2 * m + 1))
                    a = 2 * m - g + row_offset[t] + NA_KH - 1 + 4
                    src = jnp.broadcast_to(rows_ref[hh, a:a + 1, :], (GRID_W, LANES))
                    rot = pltpu.roll(src, LANES - (NA_KW - 1), 1, stride=1, stride_axis=0)
                    tab_ref[hh, t, g * GRID_W:(g + 1) * GRID_W, m * LANES:(m + 1) * LANES] = (
                        jnp.where(valid[halves], rot * LOG2E, -jnp.inf))


class _NaRefs:
    def __init__(self, q, kt, v, bias, o, n_groups):
        self.q, self.kt, self.v, self.bias, self.o, self.n_groups = q, kt, v, bias, o, n_groups


def _na_rows(u):
    return pl.ds(pl.multiple_of(u * NA_GROUP, NA_GROUP), NA_GROUP)


def _na_key0(r, u):
    return pl.multiple_of(jnp.clip(u - 1, 0, r.n_groups - 3) * NA_GROUP, NA_GROUP)


def _na_scores(r, u):
    qq = r.q[_na_rows(u), :]
    first_head = lax.broadcasted_iota(jnp.int32, qq.shape, 1) < NA_HEAD_DIM
    zero = jnp.zeros_like(qq)
    q_heads = jnp.concatenate([jnp.where(first_head, qq, zero),
                               jnp.where(first_head, zero, qq)], axis=0)
    return _dot(q_heads, r.kt[:, pl.ds(_na_key0(r, u), NA_KEYS)])


def _na_softmax(r, s, table):
    p_tiles, inv_l = [], []
    for hh in range(2):
        for g in range(NA_GROUP_ROWS):
            rows = slice(hh * NA_GROUP + g * GRID_W, hh * NA_GROUP + (g + 1) * GRID_W)
            sb = [s[rows, c * LANES:(c + 1) * LANES]
                  + r.bias[hh, table, g * GRID_W:(g + 1) * GRID_W, c * LANES:(c + 1) * LANES]
                  for c in _na_tiles(table, g)]
            m = jnp.max(functools.reduce(jnp.maximum, sb), axis=-1, keepdims=True)
            p = [jnp.exp2(x - m) for x in sb]
            inv_l.append(1.0 / jnp.sum(functools.reduce(jnp.add, p), axis=-1, keepdims=True))
            p_tiles.append([x.astype(BF16) for x in p])
    return p_tiles, inv_l


def _na_store_output(r, u, o):
    first_head = lax.broadcasted_iota(jnp.int32, (NA_GROUP, LANES), 1) < NA_HEAD_DIM
    r.o[_na_rows(u), :] = jnp.where(first_head, o[:NA_GROUP], o[NA_GROUP:]).astype(r.o.dtype)


def _na_edge_group(r, u, table):
    tiles = _na_tiles(table, 0)
    p_tiles, inv_l = _na_softmax(r, _na_scores(r, u), table)
    p = jnp.concatenate([jnp.concatenate(t, axis=1) for t in p_tiles], axis=0)
    keys = pl.ds(_na_key0(r, u) + tiles.start * LANES, len(tiles) * LANES)
    _na_store_output(r, u, _dot(p, r.v[keys, :]) * jnp.concatenate(inv_l, axis=0))


def _na_interior_probs(r, s):
    p_tiles, inv_l = _na_softmax(r, s, 1)
    zero_tile = jnp.zeros((GRID_W, LANES), BF16)
    p_rows = []
    for block, tiles in enumerate(p_tiles):
        valid = _na_tiles(1, block % NA_GROUP_ROWS)
        p_rows.append(jnp.concatenate(
            [zero_tile] * valid.start + tiles + [zero_tile] * (NA_KEYS // LANES - valid.stop), axis=1))
    return jnp.concatenate(p_rows, axis=0), jnp.concatenate(inv_l, axis=0)


def _na_interior_output(r, u, p, inv_l):
    _na_store_output(r, u, _dot(p, r.v[pl.ds(_na_key0(r, u), NA_KEYS), :]) * inv_l)


def _na_stage(r, u, carry):
    s_cur, p_prev, l_prev = carry
    s_next = _na_scores(r, jnp.minimum(u + 1, r.n_groups - 2))
    p_cur, l_cur = _na_interior_probs(r, s_cur)
    _na_interior_output(r, u - 1, p_prev, l_prev)
    return s_next, p_cur, l_cur


def _na_kernel(q_ref, kt_ref, v_ref, rows_ref, o_ref, bias_ref, s_ref, p_ref, l_ref, *, n_groups):
    _na_build_tables(rows_ref, bias_ref)
    r = _NaRefs(q_ref, kt_ref, v_ref, bias_ref, o_ref, n_groups)
    last = n_groups - 1
    _na_edge_group(r, 0, 0)
    _na_edge_group(r, last, 2)

    def load_carry():
        return s_ref[...], p_ref[...], l_ref[...]

    def store_carry(carry):
        s_ref[...], p_ref[...], l_ref[...] = carry

    def stages(first, count, carry):
        for k in range(count):
            carry = _na_stage(r, first + k, carry)
        return carry

    def unrolled_stages(i, c):
        store_carry(stages(2 + NA_UNROLL * i, NA_UNROLL, load_carry()))
        return c

    store_carry((_na_scores(r, 2),) + _na_interior_probs(r, _na_scores(r, 1)))
    n_loops, n_tail = divmod(n_groups - 3, NA_UNROLL)
    lax.fori_loop(0, n_loops, unrolled_stages, 0)
    _, p_prev, l_prev = stages(2 + NA_UNROLL * n_loops, n_tail, load_carry())
    _na_interior_output(r, last - 1, p_prev, l_prev)


def _na(qa, kat, va, rows):
    B, T, _ = qa.shape
    n_pairs = NA_HEADS // 2
    return pl.pallas_call(
        functools.partial(_na_kernel, n_groups=T // NA_GROUP),
        grid=(B, n_pairs),
        in_specs=[
            pl.BlockSpec((None, T, LANES), lambda b, p: (b, 0, p)),
            pl.BlockSpec((None, LANES, T), lambda b, p: (b, p, 0)),
            pl.BlockSpec((None, T, LANES), lambda b, p: (b, 0, p)),
            pl.BlockSpec((2,) + rows.shape[1:], lambda b, p: (p, 0, 0)),
        ],
        out_specs=pl.BlockSpec((None, T, LANES), lambda b, p: (b, 0, p)),
        out_shape=jax.ShapeDtypeStruct((B, T, NA_HEADS * NA_HEAD_DIM), BF16),
        scratch_shapes=[pltpu.VMEM((2, 3, NA_GROUP, NA_KEYS), F32),
                        pltpu.VMEM((2 * NA_GROUP, NA_KEYS), F32),
                        pltpu.VMEM((2 * NA_GROUP, NA_KEYS), BF16),
                        pltpu.VMEM((2 * NA_GROUP, 1), F32)],
        compiler_params=pltpu.CompilerParams(
            dimension_semantics=("parallel", "parallel"), vmem_limit_bytes=VMEM_LIMIT),
        name="natten",
    )(qa, kat, va, rows)


class _GlaDir:
    def __init__(self, reverse, ops, v, dec, o, state, a, contrib, prev):
        kw = GLA_HEADS * GLA_DK
        self.reverse = reverse
        self.qd, self.ki, self.kd = (ops.at[:, i * kw:(i + 1) * kw] for i in range(3))
        self.v, self.dec, self.o, self.state = v, dec, o, state
        self.a, self.contrib, self.prev = a, contrib, prev


def _gla_pair_lanes(h):
    lane = lax.broadcasted_iota(jnp.int32, (GLA_CHUNK, LANES), 1)
    return (lane < GLA_DK) if h % 2 == 0 else (lane >= GLA_DK)


def _gla_intra(d, c):
    C = GLA_CHUNK
    rows = slice(c * C, (c + 1) * C)
    qd, ki = d.qd[rows, :], d.ki[rows, :]
    lane = lax.broadcasted_iota(jnp.int32, qd.shape, 1)
    zeros = jnp.zeros_like(ki)
    i = lax.broadcasted_iota(jnp.int32, (2 * C, LANES), 0) & (C - 1)
    col = lax.broadcasted_iota(jnp.int32, (2 * C, LANES), 1)
    for parity in range(2):
        q_heads = jnp.concatenate(
            [jnp.where((lane >= h * GLA_DK) & (lane < (h + 1) * GLA_DK), qd, jnp.zeros_like(qd))
             for h in range(parity, GLA_HEADS, 2)], axis=0)
        keys = jnp.concatenate([zeros, ki] if parity == 0 else [ki, zeros], axis=0)
        scores = _dot_nt(q_heads, keys)
        j = col - C if parity == 0 else col
        in_chunk = (j >= 0) & (j < C)
        keep = in_chunk & ((j > i) if d.reverse else (j <= i))
        d.a[c, parity] = jnp.where(keep, scores, 0.0).astype(BF16)
    for p in range(GLA_HEADS // 2):
        kd_p = d.kd[rows, p * LANES:(p + 1) * LANES]
        v_p = d.v[rows, 2 * p * GLA_DV:(2 * p + 2) * GLA_DV]
        both = _dot_tn(kd_p, v_p)
        d.contrib[c, 2 * p] = both[:GLA_DK, :GLA_DV]
        d.contrib[c, 2 * p + 1] = both[GLA_DK:, GLA_DV:]


def _gla_scan(d, n_chunks):
    order = range(n_chunks - 1, -1, -1) if d.reverse else range(n_chunks)
    pad = jnp.zeros((LANES - n_chunks, GLA_HEADS * GLA_DK), F32)
    dec_t = jnp.concatenate([d.dec[c] for c in range(n_chunks)] + [pad], axis=0).T
    for h in range(GLA_HEADS):
        st = d.state[h]
        for c in order:
            d.prev[c, h] = st.astype(BF16)
            st = st * dec_t[h * GLA_DK:(h + 1) * GLA_DK, c:c + 1] + d.contrib[c, h]
        d.state[h] = st


def _gla_outputs(d, c):
    C = GLA_CHUNK
    rows = slice(c * C, (c + 1) * C)
    for h in range(GLA_HEADS):
        p, parity = divmod(h, 2)
        qd_p = d.qd[rows, p * LANES:(p + 1) * LANES]
        scores = d.a[c, parity, p * C:(p + 1) * C, :]
        lhs = jnp.where(_gla_pair_lanes(h), qd_p, scores)
        cols = slice(h * GLA_DV, (h + 1) * GLA_DV)
        operands = [d.prev[c, h], d.v[rows, cols]]
        rhs = jnp.concatenate(operands if parity == 0 else operands[::-1], axis=0)
        d.o[rows, cols] = _dot(lhs, rhs)


def _gla_kernel(gf_ref, vf_ref, decf_ref, gb_ref, vb_ref, decb_ref,
                of_ref, ob_ref, sf_ref, sb_ref, *scratch, n_chunks):
    @pl.when(pl.program_id(1) == 0)
    def _():
        sf_ref[...] = jnp.zeros_like(sf_ref)
        sb_ref[...] = jnp.zeros_like(sb_ref)

    n = len(scratch) // 2
    dirs = (_GlaDir(False, gf_ref, vf_ref, decf_ref, of_ref, sf_ref, *scratch[:n]),
            _GlaDir(True, gb_ref, vb_ref, decb_ref, ob_ref, sb_ref, *scratch[n:]))

    for c in range(n_chunks):
        for d in dirs:
            _gla_intra(d, c)
    for d in dirs:
        _gla_scan(d, n_chunks)
    for c in range(n_chunks):
        for d in dirs:
            _gla_outputs(d, c)


def _gla(gf, gb, vg, dec, *, tb):
    B, T, vw = vg.shape
    nb = T // tb
    nc = tb // GLA_CHUNK
    kw = GLA_HEADS * GLA_DK
    n_pairs = GLA_HEADS // 2
    fwd = lambda w: pl.BlockSpec((None, tb, w), lambda b, s: (b, s, 0))
    bwd = lambda w: pl.BlockSpec((None, tb, w), lambda b, s: (b, nb - 1 - s, 0))
    dec_fwd = pl.BlockSpec((None, nc, 1, kw), lambda b, s: (b, s, 0, 0))
    dec_bwd = pl.BlockSpec((None, nc, 1, kw), lambda b, s: (b, nb - 1 - s, 0, 1))
    state = pltpu.VMEM((GLA_HEADS, GLA_DK, GLA_DV), F32)
    per_dir = [
        pltpu.VMEM((nc, 2, n_pairs * GLA_CHUNK, 2 * GLA_CHUNK), BF16),
        pltpu.VMEM((nc, GLA_HEADS, GLA_DK, GLA_DV), F32),
        pltpu.VMEM((nc, GLA_HEADS, GLA_DK, GLA_DV), BF16)]
    return pl.pallas_call(
        functools.partial(_gla_kernel, n_chunks=nc),
        grid=(B, nb),
        in_specs=[fwd(3 * kw), fwd(vw), dec_fwd, bwd(3 * kw), bwd(vw), dec_bwd],
        out_specs=[fwd(vw), bwd(vw)],
        out_shape=[jax.ShapeDtypeStruct((B, T, vw), F32)] * 2,
        scratch_shapes=[state, state] + per_dir * 2,
        compiler_params=pltpu.CompilerParams(
            dimension_semantics=("parallel", "arbitrary"), vmem_limit_bytes=VMEM_LIMIT),
        name="gla",
    )(gf, vg, dec, gb, vg, dec)


STAGE_ROWS, STAGE_COLS = 512, 1024


def _stage_bf16(src_hbm, dst, stage, sem):
    rows, cols = src_hbm.shape
    tiles = [(r, c) for r in range(0, rows, STAGE_ROWS) for c in range(0, cols, STAGE_COLS)]

    def copy(i):
        r, c = tiles[i]
        return pltpu.make_async_copy(
            src_hbm.at[pl.ds(r, STAGE_ROWS), pl.ds(c, STAGE_COLS)], stage.at[i % 2], sem.at[i % 2])

    for i in range(min(2, len(tiles))):
        copy(i).start()
    for i, (r, c) in enumerate(tiles):
        copy(i).wait()
        dst[r:r + STAGE_ROWS, c:c + STAGE_COLS] = stage[i % 2].astype(BF16)
        if i + 2 < len(tiles):
            copy(i + 2).start()


def _mix_ffn_kernel(x_ref, yna_ref, of_ref, ob_ref, sr_ref, gng_ref, wout_hbm, gff_ref,
                    w1_hbm, w2_hbm, gfin_ref, out_ref, wout_ref, w1_ref, w2_ref, stage, sem,
                    *, final_norm, ff_chunk):
    @pl.when((pl.program_id(0) == 0) & (pl.program_id(1) == 0))
    def _():
        for src, dst in ((wout_hbm, wout_ref), (w1_hbm, w1_ref), (w2_hbm, w2_ref)):
            _stage_bf16(src, dst, stage, sem)

    o = of_ref[...] + ob_ref[...]
    parts = [_rms(o[:, h * GLA_DV:(h + 1) * GLA_DV], gng_ref[...]) for h in range(GLA_HEADS)]
    y_gla = (jnp.concatenate(parts, axis=-1) * sr_ref[...].astype(F32)).astype(BF16)
    na_w = yna_ref.shape[-1]
    mix = _dot(yna_ref[...], wout_ref[0:na_w, :]) + _dot(y_gla, wout_ref[na_w:, :])
    h1 = x_ref[...] + mix
    n2 = _rms(h1, gff_ref[...]).astype(BF16)
    ffn = None
    for f in range(w1_ref.shape[1] // ff_chunk):
        u = _dot(n2, w1_ref[:, f * ff_chunk:(f + 1) * ff_chunk])
        a = jnp.square(jnp.maximum(u, 0.0)).astype(BF16)
        d = _dot(a, w2_ref[f * ff_chunk:(f + 1) * ff_chunk, :])
        ffn = d if ffn is None else ffn + d
    h2 = h1 + ffn
    out_ref[...] = _rms(h2, gfin_ref[...]) if final_norm else h2


def _mix_ffn(x, yna, of, ob, sr, gng, wout, gff, w1, w2, gfin, *, tm, final_norm):
    B, T, D = x.shape
    tok = lambda w: pl.BlockSpec((None, tm, w), lambda b, i: (b, i, 0))
    const = lambda a: pl.BlockSpec(a.shape, lambda b, i: (0,) * a.ndim,
                                   pipeline_mode=pl.Buffered(1))
    hbm = pl.BlockSpec(memory_space=pl.ANY)
    for w in (wout, w1, w2):
        assert w.dtype == F32 and w.shape[0] % STAGE_ROWS == 0 and w.shape[1] % STAGE_COLS == 0
    return pl.pallas_call(
        functools.partial(_mix_ffn_kernel, final_norm=final_norm, ff_chunk=1024),
        grid=(B, T // tm),
        in_specs=[tok(D), tok(yna.shape[-1]), tok(of.shape[-1]), tok(ob.shape[-1]),
                  tok(sr.shape[-1]), const(gng), hbm, const(gff), hbm, hbm, const(gfin)],
        out_specs=tok(D),
        out_shape=jax.ShapeDtypeStruct((B, T, D), F32),
        scratch_shapes=[pltpu.VMEM(w.shape, BF16) for w in (wout, w1, w2)] + [
            pltpu.VMEM((2, STAGE_ROWS, STAGE_COLS), F32), pltpu.SemaphoreType.DMA((2,))],
        compiler_params=pltpu.CompilerParams(
            dimension_semantics=("arbitrary", "arbitrary"), vmem_limit_bytes=VMEM_LIMIT),
        name="mix_ffn",
    )(x, yna, of, ob, sr, gng, wout, gff, w1, w2, gfin)


def kernel(x, ln_mix_g, w_in, na_rpb, gla_gate_up_fwd, gla_gate_bias_fwd, gla_gate_up_bwd,
           gla_gate_bias_bwd, gla_norm_g, w_out, ln_ff_g, w_ff1, w_ff2, ln_final_g):
    B, T, D = x.shape
    depth = w_in.shape[0]
    assert T % NA_GROUP == 0 and T // NA_GROUP >= 3 and T % 512 == 0
    kw = GLA_HEADS * GLA_DK
    row = lambda v: v.reshape(1, -1).astype(F32)
    h = x
    for l in range(depth):
        zeros = jnp.zeros((GLA_GATE_RANK, kw), F32)
        gu = jnp.concatenate([
            jnp.concatenate([gla_gate_up_fwd[l], zeros], axis=1),
            jnp.concatenate([zeros, gla_gate_up_bwd[l]], axis=1)], axis=0).astype(BF16)
        gb = jnp.concatenate([gla_gate_bias_fwd[l], gla_gate_bias_bwd[l]]).reshape(1, -1)
        qa, kat, va, vg, sr, gla_f, gla_b, dec = _inproj(
            h, row(ln_mix_g[l]), w_in[l].T, gu, gb.astype(F32), tm=512)
        y_na = _na(qa, kat, va, _na_bias_rows(na_rpb[l]))
        o_f, o_b = _gla(gla_f, gla_b, vg, dec, tb=1024 if T % 1024 == 0 else 512)
        h = _mix_ffn(h, y_na, o_f, o_b, sr, row(gla_norm_g[l]), w_out[l], row(ln_ff_g[l]),
                     w_ff1[l], w_ff2[l], row(ln_final_g), tm=512, final_norm=(l == depth - 1))
    return h
```

```python
import functools

import jax
import jax.numpy as jnp
from jax import lax
from jax.experimental import pallas as pl
from jax.experimental.pallas import tpu as pltpu

F32 = jnp.float32
BF16 = jnp.bfloat16

EPS = 1e-6
GRID_W = 64
NA_HEADS = 8
NA_HEAD_DIM = 64
NA_KH = 8
NA_KW = 16
GLA_HEADS = 4
GLA_DK = 64
GLA_DV = 128
GLA_GATE_RANK = 16
GLA_GATE_NORM = 16.0
GLA_CHUNK = 64

LANES = 128
NA_GROUP_ROWS = 4
NA_GROUP = NA_GROUP_ROWS * GRID_W
NA_KEY_ROWS = 12
NA_KEYS = NA_KEY_ROWS * GRID_W
NA_UNROLL = 4
LOG2E = 1.4426950408889634
VMEM_LIMIT = 56 * 1024 * 1024


def _dot(a, b):
    return jnp.dot(a, b, preferred_element_type=F32)


def _dot_nt(a, b):
    return lax.dot_general(a, b, (((1,), (1,)), ((), ())), preferred_element_type=F32)


def _dot_tn(a, b):
    return lax.dot_general(a, b, (((0,), (0,)), ((), ())), preferred_element_type=F32)


def _rms(x, g):
    return x * lax.rsqrt(jnp.mean(x * x, axis=-1, keepdims=True) + EPS) * g


def _segmented_cumsum(x, reverse):
    n = x.shape[0]
    pos = lax.broadcasted_iota(jnp.int32, x.shape, 0) & (GLA_CHUNK - 1)
    step = 1
    while step < GLA_CHUNK:
        if reverse:
            shifted, ok = pltpu.roll(x, n - step, 0), pos < GLA_CHUNK - step
        else:
            shifted, ok = pltpu.roll(x, step, 0), pos >= step
        x = x + jnp.where(ok, shifted, 0.0)
        step *= 2
    return x


def _gla_cumdecay(log_a, reverse):
    b = _segmented_cumsum(log_a, reverse)
    b3 = b.reshape(-1, GLA_CHUNK, b.shape[-1])
    last = 0 if reverse else GLA_CHUNK - 1
    return b, b3[:, last:last + 1, :]


def _gla_operands(q, k, b, b_last):
    b3 = b.reshape(-1, GLA_CHUNK, b.shape[-1])
    k_dec = (k.reshape(b3.shape) * jnp.exp2(b_last - b3)).reshape(b.shape)
    ops = jnp.concatenate([q * jnp.exp2(b), k * jnp.exp2(-b), k_dec], axis=-1)
    return ops.astype(BF16), jnp.exp2(b_last)


def _inproj_kernel(x_ref, g_ref, wt_ref, gu_ref, gb_ref,
                   qa_ref, kat_ref, va_ref, vg_ref, sr_ref, gf_ref, gbw_ref, dec_ref, wb_ref):
    kw = GLA_HEADS * GLA_DK
    n_main = wt_ref.shape[0] - 2 * GLA_GATE_RANK

    @pl.when((pl.program_id(0) == 0) & (pl.program_id(1) == 0))
    def _():
        wb_ref[...] = wt_ref[...].astype(BF16)

    def proj(rows):
        return _dot_nt(n, wb_ref[rows, :])

    n = _rms(x_ref[...], g_ref[...]).astype(BF16)

    def log_decay(cols):
        pre = _dot(z, gu_ref[:, cols]) + gb_ref[:, cols]
        softplus2 = jnp.log2(1.0 + jnp.exp2(jnp.abs(pre) * -LOG2E))
        return jnp.minimum(pre * (LOG2E / GLA_GATE_NORM), 0.0) - softplus2 * (1.0 / GLA_GATE_NORM)

    z = proj(slice(n_main, None)).astype(BF16)
    b_f, last_f = _gla_cumdecay(log_decay(slice(0, kw)), False)
    qa_ref[...] = (proj(slice(0, 512)) * (NA_HEAD_DIM ** -0.5 * LOG2E)).astype(BF16)
    b_b, last_b = _gla_cumdecay(log_decay(slice(kw, 2 * kw)), True)
    kat_ref[...] = _dot_nt(wb_ref[512:1024, :], n).astype(BF16)
    qg = proj(slice(1536, 1792)) * (GLA_DK ** -0.5)
    kg = proj(slice(1792, 2048))
    gf_ref[...], dec_f = _gla_operands(qg, kg, b_f, last_f)
    va_ref[...] = proj(slice(1024, 1536)).astype(BF16)
    gbw_ref[...], dec_b = _gla_operands(qg, kg, b_b, last_b)
    dec_ref[...] = jnp.concatenate([dec_f, dec_b], axis=-1)
    vg_ref[...] = proj(slice(2048, 2560)).astype(BF16)
    r = proj(slice(2560, 3072))
    sr_ref[...] = (r * jax.nn.sigmoid(r)).astype(BF16)


def _inproj(x, g, wt, gu, gb, *, tm):
    B, T, D = x.shape
    kw = GLA_HEADS * GLA_DK
    nc = tm // GLA_CHUNK
    tok = lambda w: pl.BlockSpec((None, tm, w), lambda b, i: (b, i, 0))
    const = lambda a: pl.BlockSpec(a.shape, lambda b, i: (0,) * a.ndim,
                                   pipeline_mode=pl.Buffered(1))
    outs = [
        (jax.ShapeDtypeStruct((B, T, 512), BF16), tok(512)),
        (jax.ShapeDtypeStruct((B, 512, T), BF16),
         pl.BlockSpec((None, 512, tm), lambda b, i: (b, 0, i))),
        (jax.ShapeDtypeStruct((B, T, 512), BF16), tok(512)),
        (jax.ShapeDtypeStruct((B, T, 512), BF16), tok(512)),
        (jax.ShapeDtypeStruct((B, T, 512), BF16), tok(512)),
        (jax.ShapeDtypeStruct((B, T, 3 * kw), BF16), tok(3 * kw)),
        (jax.ShapeDtypeStruct((B, T, 3 * kw), BF16), tok(3 * kw)),
        (jax.ShapeDtypeStruct((B, T // GLA_CHUNK, 1, 2 * kw), F32),
         pl.BlockSpec((None, nc, 1, 2 * kw), lambda b, i: (b, i, 0, 0))),
    ]
    return pl.pallas_call(
        _inproj_kernel,
        grid=(B, T // tm),
        in_specs=[tok(D), const(g), const(wt), const(gu), const(gb)],
        out_specs=[s for _, s in outs],
        out_shape=[s for s, _ in outs],
        scratch_shapes=[pltpu.VMEM(wt.shape, BF16)],
        compiler_params=pltpu.CompilerParams(
            dimension_semantics=("arbitrary", "arbitrary"), vmem_limit_bytes=VMEM_LIMIT),
        name="inproj",
    )(x, g, wt, gu, gb)


def _na_bias_rows(rpb):
    n_dc = 2 * NA_KW - 1
    padded = jnp.pad(rpb.astype(F32), ((0, 0), (4, 5), (0, GRID_W - n_dc)))
    return jnp.concatenate([padded[:, :-1], padded[:, 1:]], axis=-1)


def _na_tiles(table, g):
    i_start = (0, g, NA_KEY_ROWS - NA_KH)[table]
    return range(i_start // 2, (i_start + NA_KH + 1) // 2)


def _na_build_tables(rows_ref, tab_ref):
    qc = lax.broadcasted_iota(jnp.int32, (GRID_W, LANES), 0)
    lane = lax.broadcasted_iota(jnp.int32, (GRID_W, LANES), 1)
    kc = lane & (GRID_W - 1)
    col_start = jnp.clip(qc - NA_KW // 2, 0, GRID_W - NA_KW)
    in_win = (kc >= col_start) & (kc < col_start + NA_KW)
    valid = {(True, True): in_win,
             (True, False): in_win & (lane < GRID_W),
             (False, True): in_win & (lane >= GRID_W)}
    row_offset = (0, -(NA_KH // 2), -NA_KH)
    for hh in range(2):
        for t in range(3):
            for g in range(NA_GROUP_ROWS):
                i_start = (0, g, NA_KEY_ROWS - NA_KH)[t]
                for m in _na_tiles(t, g):
                    halves = tuple(i_start <= i < i_start + NA_KH for i in (2 * m, 2 * m + 1))
                    a = 2 * m - g + row_offset[t] + NA_KH - 1 + 4
                    src = jnp.broadcast_to(rows_ref[hh, a:a + 1, :], (GRID_W, LANES))
                    rot = pltpu.roll(src, LANES - (NA_KW - 1), 1, stride=1, stride_axis=0)
                    tab_ref[hh, t, g * GRID_W:(g + 1) * GRID_W, m * LANES:(m + 1) * LANES] = (
                        jnp.where(valid[halves], rot * LOG2E, -jnp.inf))


class _NaRefs:
    def __init__(self, q, kt, v, bias, o, n_groups):
        self.q, self.kt, self.v, self.bias, self.o, self.n_groups = q, kt, v, bias, o, n_groups


def _na_rows(u):
    return pl.ds(pl.multiple_of(u * NA_GROUP, NA_GROUP), NA_GROUP)


def _na_key0(r, u):
    return pl.multiple_of(jnp.clip(u - 1, 0, r.n_groups - 3) * NA_GROUP, NA_GROUP)


def _na_scores(r, u):
    qq = r.q[_na_rows(u), :]
    first_head = lax.broadcasted_iota(jnp.int32, qq.shape, 1) < NA_HEAD_DIM
    zero = jnp.zeros_like(qq)
    q_heads = jnp.concatenate([jnp.where(first_head, qq, zero),
                               jnp.where(first_head, zero, qq)], axis=0)
    return _dot(q_heads, r.kt[:, pl.ds(_na_key0(r, u), NA_KEYS)])


def _na_softmax(r, s, table):
    p_tiles, inv_l = [], []
    for hh in range(2):
        for g in range(NA_GROUP_ROWS):
            rows = slice(hh * NA_GROUP + g * GRID_W, hh * NA_GROUP + (g + 1) * GRID_W)
            sb = [s[rows, c * LANES:(c + 1) * LANES]
                  + r.bias[hh, table, g * GRID_W:(g + 1) * GRID_W, c * LANES:(c + 1) * LANES]
                  for c in _na_tiles(table, g)]
            m = jnp.max(functools.reduce(jnp.maximum, sb), axis=-1, keepdims=True)
            p = [jnp.exp2(x - m) for x in sb]
            inv_l.append(1.0 / jnp.sum(functools.reduce(jnp.add, p), axis=-1, keepdims=True))
            p_tiles.append([x.astype(BF16) for x in p])
    return p_tiles, inv_l


def _na_store_output(r, u, o):
    first_head = lax.broadcasted_iota(jnp.int32, (NA_GROUP, LANES), 1) < NA_HEAD_DIM
    r.o[_na_rows(u), :] = jnp.where(first_head, o[:NA_GROUP], o[NA_GROUP:]).astype(r.o.dtype)


def _na_edge_group(r, u, table):
    tiles = _na_tiles(table, 0)
    p_tiles, inv_l = _na_softmax(r, _na_scores(r, u), table)
    p = jnp.concatenate([jnp.concatenate(t, axis=1) for t in p_tiles], axis=0)
    keys = pl.ds(_na_key0(r, u) + tiles.start * LANES, len(tiles) * LANES)
    _na_store_output(r, u, _dot(p, r.v[keys, :]) * jnp.concatenate(inv_l, axis=0))


def _na_interior_probs(r, s):
    p_tiles, inv_l = _na_softmax(r, s, 1)
    zero_tile = jnp.zeros((GRID_W, LANES), BF16)
    p_rows = []
    for block, tiles in enumerate(p_tiles):
        valid = _na_tiles(1, block % NA_GROUP_ROWS)
        p_rows.append(jnp.concatenate(
            [zero_tile] * valid.start + tiles + [zero_tile] * (NA_KEYS // LANES - valid.stop), axis=1))
    return jnp.concatenate(p_rows, axis=0), jnp.concatenate(inv_l, axis=0)


def _na_interior_output(r, u, p, inv_l):
    _na_store_output(r, u, _dot(p, r.v[pl.ds(_na_key0(r, u), NA_KEYS), :]) * inv_l)


def _na_stage(r, u, carry):
    s_cur, p_prev, l_prev = carry
    s_next = _na_scores(r, jnp.minimum(u + 1, r.n_groups - 2))
    p_cur, l_cur = _na_interior_probs(r, s_cur)
    _na_interior_output(r, u - 1, p_prev, l_prev)
    return s_next, p_cur, l_cur


def _na_kernel(q_ref, kt_ref, v_ref, rows_ref, o_ref, bias_ref, s_ref, p_ref, l_ref, *, n_groups):
    @pl.when(pl.program_id(1) == 0)
    def _():
        _na_build_tables(rows_ref, bias_ref)

    r = _NaRefs(q_ref, kt_ref, v_ref, bias_ref, o_ref, n_groups)
    last = n_groups - 1
    _na_edge_group(r, 0, 0)
    _na_edge_group(r, last, 2)

    def load_carry():
        return s_ref[...], p_ref[...], l_ref[...]

    def store_carry(carry):
        s_ref[...], p_ref[...], l_ref[...] = carry

    def stages(first, count, carry):
        for k in range(count):
            carry = _na_stage(r, first + k, carry)
        return carry

    def unrolled_stages(i, c):
        store_carry(stages(2 + NA_UNROLL * i, NA_UNROLL, load_carry()))
        return c

    store_carry((_na_scores(r, 2),) + _na_interior_probs(r, _na_scores(r, 1)))
    n_loops, n_tail = divmod(n_groups - 3, NA_UNROLL)
    lax.fori_loop(0, n_loops, unrolled_stages, 0)
    _, p_prev, l_prev = stages(2 + NA_UNROLL * n_loops, n_tail, load_carry())
    _na_interior_output(r, last - 1, p_prev, l_prev)


def _na(qa, kat, va, rows):
    B, T, _ = qa.shape
    n_pairs = NA_HEADS // 2
    return pl.pallas_call(
        functools.partial(_na_kernel, n_groups=T // NA_GROUP),
        grid=(n_pairs, B),
        in_specs=[
            pl.BlockSpec((None, T, LANES), lambda p, b: (b, 0, p)),
            pl.BlockSpec((None, LANES, T), lambda p, b: (b, p, 0)),
            pl.BlockSpec((None, T, LANES), lambda p, b: (b, 0, p)),
            pl.BlockSpec((2,) + rows.shape[1:], lambda p, b: (p, 0, 0)),
        ],
        out_specs=pl.BlockSpec((None, T, LANES), lambda p, b: (b, 0, p)),
        out_shape=jax.ShapeDtypeStruct((B, T, NA_HEADS * NA_HEAD_DIM), BF16),
        scratch_shapes=[pltpu.VMEM((2, 3, NA_GROUP, NA_KEYS), F32),
                        pltpu.VMEM((2 * NA_GROUP, NA_KEYS), F32),
                        pltpu.VMEM((2 * NA_GROUP, NA_KEYS), BF16),
                        pltpu.VMEM((2 * NA_GROUP, 1), F32)],
        compiler_params=pltpu.CompilerParams(
            dimension_semantics=("parallel", "arbitrary"), vmem_limit_bytes=VMEM_LIMIT),
        name="natten",
    )(qa, kat, va, rows)


class _GlaDir:
    def __init__(self, reverse, ops, v, dec, o, state, a, contrib, prev):
        kw = GLA_HEADS * GLA_DK
        self.reverse = reverse
        self.qd, self.ki, self.kd = (ops.at[:, i * kw:(i + 1) * kw] for i in range(3))
        self.v, self.dec, self.o, self.state = v, dec, o, state
        self.a, self.contrib, self.prev = a, contrib, prev


def _gla_pair_lanes(h):
    lane = lax.broadcasted_iota(jnp.int32, (GLA_CHUNK, LANES), 1)
    return (lane < GLA_DK) if h % 2 == 0 else (lane >= GLA_DK)


def _gla_intra(d, c):
    C = GLA_CHUNK
    rows = slice(c * C, (c + 1) * C)
    qd, ki = d.qd[rows, :], d.ki[rows, :]
    lane = lax.broadcasted_iota(jnp.int32, qd.shape, 1)
    zeros = jnp.zeros_like(ki)
    i = lax.broadcasted_iota(jnp.int32, (2 * C, LANES), 0) & (C - 1)
    col = lax.broadcasted_iota(jnp.int32, (2 * C, LANES), 1)
    for parity in range(2):
        q_heads = jnp.concatenate(
            [jnp.where((lane >= h * GLA_DK) & (lane < (h + 1) * GLA_DK), qd, jnp.zeros_like(qd))
             for h in range(parity, GLA_HEADS, 2)], axis=0)
        keys = jnp.concatenate([zeros, ki] if parity == 0 else [ki, zeros], axis=0)
        scores = _dot_nt(q_heads, keys)
        j = col - C if parity == 0 else col
        in_chunk = (j >= 0) & (j < C)
        keep = in_chunk & ((j > i) if d.reverse else (j <= i))
        d.a[c, parity] = jnp.where(keep, scores, 0.0).astype(BF16)
    for p in range(GLA_HEADS // 2):
        kd_p = d.kd[rows, p * LANES:(p + 1) * LANES]
        v_p = d.v[rows, 2 * p * GLA_DV:(2 * p + 2) * GLA_DV]
        both = _dot_tn(kd_p, v_p)
        d.contrib[c, 2 * p] = both[:GLA_DK, :GLA_DV]
        d.contrib[c, 2 * p + 1] = both[GLA_DK:, GLA_DV:]


def _gla_scan(d, n_chunks):
    order = range(n_chunks - 1, -1, -1) if d.reverse else range(n_chunks)
    pad = jnp.zeros((LANES - n_chunks, GLA_HEADS * GLA_DK), F32)
    dec_t = jnp.concatenate([d.dec[c] for c in range(n_chunks)] + [pad], axis=0).T
    for h in range(GLA_HEADS):
        st = d.state[h]
        for c in order:
            d.prev[c, h] = st.astype(BF16)
            st = st * dec_t[h * GLA_DK:(h + 1) * GLA_DK, c:c + 1] + d.contrib[c, h]
        d.state[h] = st


def _gla_outputs(d, c):
    C = GLA_CHUNK
    rows = slice(c * C, (c + 1) * C)
    for h in range(GLA_HEADS):
        p, parity = divmod(h, 2)
        qd_p = d.qd[rows, p * LANES:(p + 1) * LANES]
        scores = d.a[c, parity, p * C:(p + 1) * C, :]
        lhs = jnp.where(_gla_pair_lanes(h), qd_p, scores)
        cols = slice(h * GLA_DV, (h + 1) * GLA_DV)
        operands = [d.prev[c, h], d.v[rows, cols]]
        rhs = jnp.concatenate(operands if parity == 0 else operands[::-1], axis=0)
        d.o[rows, cols] = _dot(lhs, rhs).astype(d.o.dtype)


def _gla_kernel(gf_ref, vf_ref, decf_ref, gb_ref, vb_ref, decb_ref,
                of_ref, ob_ref, sf_ref, sb_ref, *scratch, n_chunks):
    @pl.when(pl.program_id(1) == 0)
    def _():
        sf_ref[...] = jnp.zeros_like(sf_ref)
        sb_ref[...] = jnp.zeros_like(sb_ref)

    n = len(scratch) // 2
    dirs = (_GlaDir(False, gf_ref, vf_ref, decf_ref, of_ref, sf_ref, *scratch[:n]),
            _GlaDir(True, gb_ref, vb_ref, decb_ref, ob_ref, sb_ref, *scratch[n:]))

    for c in range(n_chunks):
        for d in dirs:
            _gla_intra(d, c)
    for d in dirs:
        _gla_scan(d, n_chunks)
    for c in range(n_chunks):
        for d in dirs:
            _gla_outputs(d, c)


def _gla(gf, gb, vg, dec, *, tb):
    B, T, vw = vg.shape
    nb = T // tb
    nc = tb // GLA_CHUNK
    kw = GLA_HEADS * GLA_DK
    n_pairs = GLA_HEADS // 2
    fwd = lambda w: pl.BlockSpec((None, tb, w), lambda b, s: (b, s, 0))
    bwd = lambda w: pl.BlockSpec((None, tb, w), lambda b, s: (b, nb - 1 - s, 0))
    dec_fwd = pl.BlockSpec((None, nc, 1, kw), lambda b, s: (b, s, 0, 0))
    dec_bwd = pl.BlockSpec((None, nc, 1, kw), lambda b, s: (b, nb - 1 - s, 0, 1))
    state = pltpu.VMEM((GLA_HEADS, GLA_DK, GLA_DV), F32)
    per_dir = [
        pltpu.VMEM((nc, 2, n_pairs * GLA_CHUNK, 2 * GLA_CHUNK), BF16),
        pltpu.VMEM((nc, GLA_HEADS, GLA_DK, GLA_DV), F32),
        pltpu.VMEM((nc, GLA_HEADS, GLA_DK, GLA_DV), BF16)]
    return pl.pallas_call(
        functools.partial(_gla_kernel, n_chunks=nc),
        grid=(B, nb),
        in_specs=[fwd(3 * kw), fwd(vw), dec_fwd, bwd(3 * kw), bwd(vw), dec_bwd],
        out_specs=[fwd(vw), bwd(vw)],
        out_shape=[jax.ShapeDtypeStruct((B, T, vw), BF16)] * 2,
        scratch_shapes=[state, state] + per_dir * 2,
        compiler_params=pltpu.CompilerParams(
            dimension_semantics=("parallel", "arbitrary"), vmem_limit_bytes=VMEM_LIMIT),
        name="gla",
    )(gf, vg, dec, gb, vg, dec)


STAGE_ROWS, STAGE_COLS = 512, 1024


def _stage_bf16(src_hbm, dst, stage, sem):
    rows, cols = src_hbm.shape
    tiles = [(r, c) for r in range(0, rows, STAGE_ROWS) for c in range(0, cols, STAGE_COLS)]

    def copy(i):
        r, c = tiles[i]
        return pltpu.make_async_copy(
            src_hbm.at[pl.ds(r, STAGE_ROWS), pl.ds(c, STAGE_COLS)], stage.at[i % 2], sem.at[i % 2])

    for i in range(min(2, len(tiles))):
        copy(i).start()
    for i, (r, c) in enumerate(tiles):
        copy(i).wait()
        dst[r:r + STAGE_ROWS, c:c + STAGE_COLS] = stage[i % 2].astype(BF16)
        if i + 2 < len(tiles):
            copy(i + 2).start()


def _mix_ffn_kernel(x_ref, yna_ref, of_ref, ob_ref, sr_ref, gng_ref, wout_hbm, gff_ref,
                    w1_hbm, w2_hbm, gfin_ref, out_ref, wout_ref, w1_ref, w2_ref, stage, sem,
                    *, final_norm, ff_chunk):
    @pl.when((pl.program_id(0) == 0) & (pl.program_id(1) == 0))
    def _():
        for src, dst in ((wout_hbm, wout_ref), (w1_hbm, w1_ref), (w2_hbm, w2_ref)):
            _stage_bf16(src, dst, stage, sem)

    o = of_ref[...].astype(F32) + ob_ref[...].astype(F32)
    parts = [_rms(o[:, h * GLA_DV:(h + 1) * GLA_DV], gng_ref[...]) for h in range(GLA_HEADS)]
    y_gla = (jnp.concatenate(parts, axis=-1) * sr_ref[...].astype(F32)).astype(BF16)
    na_w = yna_ref.shape[-1]
    mix = _dot(yna_ref[...], wout_ref[0:na_w, :]) + _dot(y_gla, wout_ref[na_w:, :])
    h1 = x_ref[...] + mix
    n2 = _rms(h1, gff_ref[...]).astype(BF16)
    ffn = None
    for f in range(w1_ref.shape[1] // ff_chunk):
        u = _dot(n2, w1_ref[:, f * ff_chunk:(f + 1) * ff_chunk])
        a = jnp.square(jnp.maximum(u, 0.0)).astype(BF16)
        d = _dot(a, w2_ref[f * ff_chunk:(f + 1) * ff_chunk, :])
        ffn = d if ffn is None else ffn + d
    h2 = h1 + ffn
    out_ref[...] = _rms(h2, gfin_ref[...]) if final_norm else h2


def _mix_ffn(x, yna, of, ob, sr, gng, wout, gff, w1, w2, gfin, *, tm, final_norm):
    B, T, D = x.shape
    tok = lambda w: pl.BlockSpec((None, tm, w), lambda b, i: (b, i, 0))
    const = lambda a: pl.BlockSpec(a.shape, lambda b, i: (0,) * a.ndim,
                                   pipeline_mode=pl.Buffered(1))
    hbm = pl.BlockSpec(memory_space=pl.ANY)
    for w in (wout, w1, w2):
        assert w.dtype == F32 and w.shape[0] % STAGE_ROWS == 0 and w.shape[1] % STAGE_COLS == 0
    return pl.pallas_call(
        functools.partial(_mix_ffn_kernel, final_norm=final_norm, ff_chunk=1024),
        grid=(B, T // tm),
        in_specs=[tok(D), tok(yna.shape[-1]), tok(of.shape[-1]), tok(ob.shape[-1]),
                  tok(sr.shape[-1]), const(gng), hbm, const(gff), hbm, hbm, const(gfin)],
        out_specs=tok(D),
        out_shape=jax.ShapeDtypeStruct((B, T, D), F32),
        scratch_shapes=[pltpu.VMEM(w.shape, BF16) for w in (wout, w1, w2)] + [
            pltpu.VMEM((2, STAGE_ROWS, STAGE_COLS), F32), pltpu.SemaphoreType.DMA((2,))],
        compiler_params=pltpu.CompilerParams(
            dimension_semantics=("arbitrary", "arbitrary"), vmem_limit_bytes=VMEM_LIMIT),
        name="mix_ffn",
    )(x, yna, of, ob, sr, gng, wout, gff, w1, w2, gfin)


def kernel(x, ln_mix_g, w_in, na_rpb, gla_gate_up_fwd, gla_gate_bias_fwd, gla_gate_up_bwd,
           gla_gate_bias_bwd, gla_norm_g, w_out, ln_ff_g, w_ff1, w_ff2, ln_final_g):
    B, T, D = x.shape
    depth = w_in.shape[0]
    assert T % NA_GROUP == 0 and T // NA_GROUP >= 3 and T % 512 == 0
    kw = GLA_HEADS * GLA_DK
    row = lambda v: v.reshape(1, -1).astype(F32)
    h = x
    for l in range(depth):
        zeros = jnp.zeros((GLA_GATE_RANK, kw), F32)
        gu = jnp.concatenate([
            jnp.concatenate([gla_gate_up_fwd[l], zeros], axis=1),
            jnp.concatenate([zeros, gla_gate_up_bwd[l]], axis=1)], axis=0).astype(BF16)
        gb = jnp.concatenate([gla_gate_bias_fwd[l], gla_gate_bias_bwd[l]]).reshape(1, -1)
        qa, kat, va, vg, sr, gla_f, gla_b, dec = _inproj(
            h, row(ln_mix_g[l]), w_in[l].T, gu, gb.astype(F32), tm=512)
        y_na = _na(qa, kat, va, _na_bias_rows(na_rpb[l]))
        o_f, o_b = _gla(gla_f, gla_b, vg, dec, tb=1024 if T % 1024 == 0 else 512)
        h = _mix_ffn(h, y_na, o_f, o_b, sr, row(gla_norm_g[l]), w_out[l], row(ln_ff_g[l]),
                     w_ff1[l], w_ff2[l], row(ln_final_g), tm=512, final_norm=(l == depth - 1))
    return h
```

```python
import functools

import jax
import jax.numpy as jnp
from jax import lax
from jax.experimental import pallas as pl
from jax.experimental.pallas import tpu as pltpu

F32 = jnp.float32
BF16 = jnp.bfloat16

EPS = 1e-6
GRID_W = 64
NA_HEADS = 8
NA_HEAD_DIM = 64
NA_KH = 8
NA_KW = 16
GLA_HEADS = 4
GLA_DK = 64
GLA_DV = 128
GLA_GATE_RANK = 16
GLA_GATE_NORM = 16.0
GLA_CHUNK = 64

LANES = 128
NA_GROUP_ROWS = 4
NA_GROUP = NA_GROUP_ROWS * GRID_W
NA_KEY_ROWS = 12
NA_KEYS = NA_KEY_ROWS * GRID_W
NA_UNROLL = 4
LOG2E = 1.4426950408889634
VMEM_LIMIT = 56 * 1024 * 1024


def _dot(a, b):
    return jnp.dot(a, b, preferred_element_type=F32)


def _dot_nt(a, b):
    return lax.dot_general(a, b, (((1,), (1,)), ((), ())), preferred_element_type=F32)


def _dot_tn(a, b):
    return lax.dot_general(a, b, (((0,), (0,)), ((), ())), preferred_element_type=F32)


def _rms(x, g):
    return x * lax.rsqrt(jnp.mean(x * x, axis=-1, keepdims=True) + EPS) * g


def _segmented_cumsum(x, reverse):
    n = x.shape[0]
    pos = lax.broadcasted_iota(jnp.int32, x.shape, 0) & (GLA_CHUNK - 1)
    step = 1
    while step < GLA_CHUNK:
        if reverse:
            shifted, ok = pltpu.roll(x, n - step, 0), pos < GLA_CHUNK - step
        else:
            shifted, ok = pltpu.roll(x, step, 0), pos >= step
        x = x + jnp.where(ok, shifted, 0.0)
        step *= 2
    return x


def _gla_cumdecay(log_a, reverse):
    b = _segmented_cumsum(log_a, reverse)
    b3 = b.reshape(-1, GLA_CHUNK, b.shape[-1])
    last = 0 if reverse else GLA_CHUNK - 1
    return b, b3[:, last:last + 1, :]


def _gla_operands(q, k, b, b_last):
    b3 = b.reshape(-1, GLA_CHUNK, b.shape[-1])
    k_dec = (k.reshape(b3.shape) * jnp.exp2(b_last - b3)).reshape(b.shape)
    ops = jnp.concatenate([q * jnp.exp2(b), k * jnp.exp2(-b), k_dec], axis=-1)
    return ops.astype(BF16), jnp.exp2(b_last)


def _inproj_kernel(x_ref, g_ref, wt_ref, gu_ref, gb_ref,
                   qa_ref, kat_ref, va_ref, vg_ref, sr_ref, gf_ref, gbw_ref, dec_ref, wb_ref):
    kw = GLA_HEADS * GLA_DK
    n_main = wt_ref.shape[0] - 2 * GLA_GATE_RANK

    @pl.when((pl.program_id(0) == 0) & (pl.program_id(1) == 0))
    def _():
        wb_ref[...] = wt_ref[...].astype(BF16)

    def proj(rows):
        return _dot_nt(n, wb_ref[rows, :])

    n = _rms(x_ref[...], g_ref[...]).astype(BF16)

    def log_decay(cols):
        pre = _dot(z, gu_ref[:, cols]) + gb_ref[:, cols]
        softplus2 = jnp.log2(1.0 + jnp.exp2(jnp.abs(pre) * -LOG2E))
        return jnp.minimum(pre * (LOG2E / GLA_GATE_NORM), 0.0) - softplus2 * (1.0 / GLA_GATE_NORM)

    z = proj(slice(n_main, None)).astype(BF16)
    b_f, last_f = _gla_cumdecay(log_decay(slice(0, kw)), False)
    qa_ref[...] = (proj(slice(0, 512)) * (NA_HEAD_DIM ** -0.5 * LOG2E)).astype(BF16)
    b_b, last_b = _gla_cumdecay(log_decay(slice(kw, 2 * kw)), True)
    kat_ref[...] = _dot_nt(wb_ref[512:1024, :], n).astype(BF16)
    qg = proj(slice(1536, 1792)) * (GLA_DK ** -0.5)
    kg = proj(slice(1792, 2048))
    gf_ref[...], dec_f = _gla_operands(qg, kg, b_f, last_f)
    va_ref[...] = proj(slice(1024, 1536)).astype(BF16)
    gbw_ref[...], dec_b = _gla_operands(qg, kg, b_b, last_b)
    dec_ref[...] = jnp.concatenate([dec_f, dec_b], axis=-1)
    vg_ref[...] = proj(slice(2048, 2560)).astype(BF16)
    r = proj(slice(2560, 3072))
    sr_ref[...] = (r * jax.nn.sigmoid(r)).astype(BF16)


def _inproj(x, g, wt, gu, gb, *, tm):
    B, T, D = x.shape
    kw = GLA_HEADS * GLA_DK
    nc = tm // GLA_CHUNK
    tok = lambda w: pl.BlockSpec((None, tm, w), lambda b, i: (b, i, 0))
    const = lambda a: pl.BlockSpec(a.shape, lambda b, i: (0,) * a.ndim,
                                   pipeline_mode=pl.Buffered(1))
    outs = [
        (jax.ShapeDtypeStruct((B, T, 512), BF16), tok(512)),
        (jax.ShapeDtypeStruct((B, 512, T), BF16),
         pl.BlockSpec((None, 512, tm), lambda b, i: (b, 0, i))),
        (jax.ShapeDtypeStruct((B, T, 512), BF16), tok(512)),
        (jax.ShapeDtypeStruct((B, T, 512), BF16), tok(512)),
        (jax.ShapeDtypeStruct((B, T, 512), BF16), tok(512)),
        (jax.ShapeDtypeStruct((B, T, 3 * kw), BF16), tok(3 * kw)),
        (jax.ShapeDtypeStruct((B, T, 3 * kw), BF16), tok(3 * kw)),
        (jax.ShapeDtypeStruct((B, T // GLA_CHUNK, 1, 2 * kw), F32),
         pl.BlockSpec((None, nc, 1, 2 * kw), lambda b, i: (b, i, 0, 0))),
    ]
    return pl.pallas_call(
        _inproj_kernel,
        grid=(B, T // tm),
        in_specs=[tok(D), const(g), const(wt), const(gu), const(gb)],
        out_specs=[s for _, s in outs],
        out_shape=[s for s, _ in outs],
        scratch_shapes=[pltpu.VMEM(wt.shape, BF16)],
        compiler_params=pltpu.CompilerParams(
            dimension_semantics=("arbitrary", "arbitrary"), vmem_limit_bytes=VMEM_LIMIT),
        name="inproj",
    )(x, g, wt, gu, gb)


def _na_bias_rows(rpb):
    n_dc = 2 * NA_KW - 1
    padded = jnp.pad(rpb.astype(F32), ((0, 0), (4, 5), (0, GRID_W - n_dc)))
    return jnp.concatenate([padded[:, :-1], padded[:, 1:]], axis=-1)


def _na_tiles(table, g):
    i_start = (0, g, NA_KEY_ROWS - NA_KH)[table]
    return range(i_start // 2, (i_start + NA_KH + 1) // 2)


def _na_build_tables(rows_ref, tab_ref):
    qc = lax.broadcasted_iota(jnp.int32, (GRID_W, LANES), 0)
    lane = lax.broadcasted_iota(jnp.int32, (GRID_W, LANES), 1)
    kc = lane & (GRID_W - 1)
    col_start = jnp.clip(qc - NA_KW // 2, 0, GRID_W - NA_KW)
    in_win = (kc >= col_start) & (kc < col_start + NA_KW)
    valid = {(True, True): in_win,
             (True, False): in_win & (lane < GRID_W),
             (False, True): in_win & (lane >= GRID_W)}
    row_offset = (0, -(NA_KH // 2), -NA_KH)
    for hh in range(2):
        for t in range(3):
            for g in range(NA_GROUP_ROWS):
                i_start = (0, g, NA_KEY_ROWS - NA_KH)[t]
                for m in _na_tiles(t, g):
                    halves = tuple(i_start <= i < i_start + NA_KH for i in (2 * m, 2 * m + 1))
                    a = 2 * m - g + row_offset[t] + NA_KH - 1 + 4
                    src = jnp.broadcast_to(rows_ref[hh, a:a + 1, :], (GRID_W, LANES))
                    rot = pltpu.roll(src, LANES - (NA_KW - 1), 1, stride=1, stride_axis=0)
                    tab_ref[hh, t, g * GRID_W:(g + 1) * GRID_W, m * LANES:(m + 1) * LANES] = (
                        jnp.where(valid[halves], rot * LOG2E, -jnp.inf))


class _NaRefs:
    def __init__(self, q, kt, v, bias, o, n_groups):
        self.q, self.kt, self.v, self.bias, self.o, self.n_groups = q, kt, v, bias, o, n_groups


def _na_rows(u):
    return pl.ds(pl.multiple_of(u * NA_GROUP, NA_GROUP), NA_GROUP)


def _na_key0(r, u):
    return pl.multiple_of(jnp.clip(u - 1, 0, r.n_groups - 3) * NA_GROUP, NA_GROUP)


def _na_scores(r, u):
    qq = r.q[_na_rows(u), :]
    first_head = lax.broadcasted_iota(jnp.int32, qq.shape, 1) < NA_HEAD_DIM
    zero = jnp.zeros_like(qq)
    q_heads = jnp.concatenate([jnp.where(first_head, qq, zero),
                               jnp.where(first_head, zero, qq)], axis=0)
    return _dot(q_heads, r.kt[:, pl.ds(_na_key0(r, u), NA_KEYS)])


def _na_softmax(r, s, table):
    p_tiles, inv_l = [], []
    for hh in range(2):
        for g in range(NA_GROUP_ROWS):
            rows = slice(hh * NA_GROUP + g * GRID_W, hh * NA_GROUP + (g + 1) * GRID_W)
            sb = [s[rows, c * LANES:(c + 1) * LANES]
                  + r.bias[hh, table, g * GRID_W:(g + 1) * GRID_W, c * LANES:(c + 1) * LANES]
                  for c in _na_tiles(table, g)]
            m = jnp.max(functools.reduce(jnp.maximum, sb), axis=-1, keepdims=True)
            p = [jnp.exp2(x - m) for x in sb]
            inv_l.append(1.0 / jnp.sum(functools.reduce(jnp.add, p), axis=-1, keepdims=True))
            p_tiles.append([x.astype(BF16) for x in p])
    return p_tiles, inv_l


def _na_store_output(r, u, o):
    first_head = lax.broadcasted_iota(jnp.int32, (NA_GROUP, LANES), 1) < NA_HEAD_DIM
    r.o[_na_rows(u), :] = jnp.where(first_head, o[:NA_GROUP], o[NA_GROUP:]).astype(r.o.dtype)


def _na_edge_group(r, u, table):
    tiles = _na_tiles(table, 0)
    p_tiles, inv_l = _na_softmax(r, _na_scores(r, u), table)
    p = jnp.concatenate([jnp.concatenate(t, axis=1) for t in p_tiles], axis=0)
    keys = pl.ds(_na_key0(r, u) + tiles.start * LANES, len(tiles) * LANES)
    _na_store_output(r, u, _dot(p, r.v[keys, :]) * jnp.concatenate(inv_l, axis=0))


def _na_interior_probs(r, s):
    p_tiles, inv_l = _na_softmax(r, s, 1)
    zero_tile = jnp.zeros((GRID_W, LANES), BF16)
    p_rows = []
    for block, tiles in enumerate(p_tiles):
        valid = _na_tiles(1, block % NA_GROUP_ROWS)
        p_rows.append(jnp.concatenate(
            [zero_tile] * valid.start + tiles + [zero_tile] * (NA_KEYS // LANES - valid.stop), axis=1))
    return jnp.concatenate(p_rows, axis=0), jnp.concatenate(inv_l, axis=0)


def _na_interior_output(r, u, p, inv_l):
    _na_store_output(r, u, _dot(p, r.v[pl.ds(_na_key0(r, u), NA_KEYS), :]) * inv_l)


def _na_stage(r, u, carry):
    s_cur, p_prev, l_prev = carry
    s_next = _na_scores(r, jnp.minimum(u + 1, r.n_groups - 2))
    p_cur, l_cur = _na_interior_probs(r, s_cur)
    _na_interior_output(r, u - 1, p_prev, l_prev)
    return s_next, p_cur, l_cur


def _na_kernel(q_ref, kt_ref, v_ref, rows_ref, o_ref, bias_ref, s_ref, p_ref, l_ref, *, n_groups):
    @pl.when(pl.program_id(1) == 0)
    def _():
        _na_build_tables(rows_ref, bias_ref)

    r = _NaRefs(q_ref, kt_ref, v_ref, bias_ref, o_ref, n_groups)
    last = n_groups - 1
    _na_edge_group(r, 0, 0)
    _na_edge_group(r, last, 2)

    def load_carry():
        return s_ref[...], p_ref[...], l_ref[...]

    def store_carry(carry):
        s_ref[...], p_ref[...], l_ref[...] = carry

    def stages(first, count, carry):
        for k in range(count):
            carry = _na_stage(r, first + k, carry)
        return carry

    def unrolled_stages(i, c):
        store_carry(stages(2 + NA_UNROLL * i, NA_UNROLL, load_carry()))
        return c

    store_carry((_na_scores(r, 2),) + _na_interior_probs(r, _na_scores(r, 1)))
    n_loops, n_tail = divmod(n_groups - 3, NA_UNROLL)
    lax.fori_loop(0, n_loops, unrolled_stages, 0)
    _, p_prev, l_prev = stages(2 + NA_UNROLL * n_loops, n_tail, load_carry())
    _na_interior_output(r, last - 1, p_prev, l_prev)


def _na(qa, kat, va, rows):
    B, T, _ = qa.shape
    n_pairs = NA_HEADS // 2
    return pl.pallas_call(
        functools.partial(_na_kernel, n_groups=T // NA_GROUP),
        grid=(n_pairs, B),
        in_specs=[
            pl.BlockSpec((None, T, LANES), lambda p, b: (b, 0, p)),
            pl.BlockSpec((None, LANES, T), lambda p, b: (b, p, 0)),
            pl.BlockSpec((None, T, LANES), lambda p, b: (b, 0, p)),
            pl.BlockSpec((2,) + rows.shape[1:], lambda p, b: (p, 0, 0)),
        ],
        out_specs=pl.BlockSpec((None, T, LANES), lambda p, b: (b, 0, p)),
        out_shape=jax.ShapeDtypeStruct((B, T, NA_HEADS * NA_HEAD_DIM), BF16),
        scratch_shapes=[pltpu.VMEM((2, 3, NA_GROUP, NA_KEYS), F32),
                        pltpu.VMEM((2 * NA_GROUP, NA_KEYS), F32),
                        pltpu.VMEM((2 * NA_GROUP, NA_KEYS), BF16),
                        pltpu.VMEM((2 * NA_GROUP, 1), F32)],
        compiler_params=pltpu.CompilerParams(
            dimension_semantics=("parallel", "arbitrary"), vmem_limit_bytes=VMEM_LIMIT),
        name="natten",
    )(qa, kat, va, rows)


class _GlaDir:
    def __init__(self, reverse, ops, v, dec, o, state, a, contrib, prev):
        kw = GLA_HEADS * GLA_DK
        self.reverse = reverse
        self.qd, self.ki, self.kd = (ops.at[:, i * kw:(i + 1) * kw] for i in range(3))
        self.v, self.dec, self.o, self.state = v, dec, o, state
        self.a, self.contrib, self.prev = a, contrib, prev


def _gla_pair_lanes(h):
    lane = lax.broadcasted_iota(jnp.int32, (GLA_CHUNK, LANES), 1)
    return (lane < GLA_DK) if h % 2 == 0 else (lane >= GLA_DK)


def _gla_intra(d, c):
    C = GLA_CHUNK
    rows = slice(c * C, (c + 1) * C)
    qd, ki = d.qd[rows, :], d.ki[rows, :]
    lane = lax.broadcasted_iota(jnp.int32, qd.shape, 1)
    zeros = jnp.zeros_like(ki)
    i = lax.broadcasted_iota(jnp.int32, (2 * C, LANES), 0) & (C - 1)
    col = lax.broadcasted_iota(jnp.int32, (2 * C, LANES), 1)
    for parity in range(2):
        q_heads = jnp.concatenate(
            [jnp.where((lane >= h * GLA_DK) & (lane < (h + 1) * GLA_DK), qd, jnp.zeros_like(qd))
             for h in range(parity, GLA_HEADS, 2)], axis=0)
        keys = jnp.concatenate([zeros, ki] if parity == 0 else [ki, zeros], axis=0)
        scores = _dot_nt(q_heads, keys)
        j = col - C if parity == 0 else col
        in_chunk = (j >= 0) & (j < C)
        keep = in_chunk & ((j > i) if d.reverse else (j <= i))
        d.a[c, parity] = jnp.where(keep, scores, 0.0).astype(BF16)
    for p in range(GLA_HEADS // 2):
        kd_p = d.kd[rows, p * LANES:(p + 1) * LANES]
        v_p = d.v[rows, 2 * p * GLA_DV:(2 * p + 2) * GLA_DV]
        both = _dot_tn(kd_p, v_p)
        d.contrib[c, 2 * p] = both[:GLA_DK, :GLA_DV]
        d.contrib[c, 2 * p + 1] = both[GLA_DK:, GLA_DV:]


def _gla_scan(d, n_chunks):
    order = range(n_chunks - 1, -1, -1) if d.reverse else range(n_chunks)
    pad = jnp.zeros((LANES - n_chunks, GLA_HEADS * GLA_DK), F32)
    dec_t = jnp.concatenate([d.dec[c] for c in range(n_chunks)] + [pad], axis=0).T
    for h in range(GLA_HEADS):
        st = d.state[h]
        for c in order:
            d.prev[c, h] = st.astype(BF16)
            st = st * dec_t[h * GLA_DK:(h + 1) * GLA_DK, c:c + 1] + d.contrib[c, h]
        d.state[h] = st


def _gla_outputs(d, c):
    C = GLA_CHUNK
    rows = slice(c * C, (c + 1) * C)
    for h in range(GLA_HEADS):
        p, parity = divmod(h, 2)
        qd_p = d.qd[rows, p * LANES:(p + 1) * LANES]
        scores = d.a[c, parity, p * C:(p + 1) * C, :]
        lhs = jnp.where(_gla_pair_lanes(h), qd_p, scores)
        cols = slice(h * GLA_DV, (h + 1) * GLA_DV)
        operands = [d.prev[c, h], d.v[rows, cols]]
        rhs = jnp.concatenate(operands if parity == 0 else operands[::-1], axis=0)
        d.o[rows, cols] = _dot(lhs, rhs).astype(d.o.dtype)


CAST_ROWS, CAST_COLS = 512, 1024


class _WeightCaster:
    def __init__(self, srcs, dsts, stage_in, stage_out, sem_in, sem_out, n_steps):
        self.srcs, self.dsts, self.n_steps = srcs, dsts, n_steps
        self.stage_in, self.stage_out, self.sem_in, self.sem_out = stage_in, stage_out, sem_in, sem_out
        self.tiles = [(w, r, c) for w, src in enumerate(srcs)
                      for r in range(0, src.shape[0], CAST_ROWS)
                      for c in range(0, src.shape[1], CAST_COLS)]

    def _step_tiles(self, k):
        return self.tiles[k::self.n_steps] if k >= 0 else []

    def _window(self, ref, tile):
        w, r, c = tile
        return ref[w].at[pl.ds(r, CAST_ROWS), pl.ds(c, CAST_COLS)]

    def _read(self, tile, slot):
        return pltpu.make_async_copy(self._window(self.srcs, tile), self.stage_in.at[slot],
                                     self.sem_in.at[slot])

    def _write(self, tile, slot):
        return pltpu.make_async_copy(self.stage_out.at[slot], self._window(self.dsts, tile),
                                     self.sem_out.at[slot])

    @staticmethod
    def _tail(tiles):
        return [(t, i % 2) for i, t in enumerate(tiles)][-2:]

    def begin(self, step):
        for k in range(self.n_steps):
            @pl.when(step == k)
            def _(k=k):
                for i, tile in enumerate(self._step_tiles(k)[:2]):
                    self._read(tile, i).start()

    def finish(self, step):
        for k in range(self.n_steps):
            @pl.when(step == k)
            def _(k=k):
                for tile, slot in self._tail(self._step_tiles(k - 1)):
                    self._write(tile, slot).wait()
                tiles = self._step_tiles(k)
                for i, tile in enumerate(tiles):
                    slot = i % 2
                    self._read(tile, slot).wait()
                    if i >= 2:
                        self._write(tiles[i - 2], slot).wait()
                    self.stage_out[slot] = self.stage_in[slot].astype(BF16)
                    self._write(tile, slot).start()
                    if i + 2 < len(tiles):
                        self._read(tiles[i + 2], slot).start()
                if k == self.n_steps - 1:
                    for tile, slot in self._tail(tiles):
                        self._write(tile, slot).wait()


def _gla_kernel(gf_ref, vf_ref, decf_ref, gb_ref, vb_ref, decb_ref, *rest, n_chunks, n_weights, n_steps):
    w_src, rest = rest[:n_weights], rest[n_weights:]
    (of_ref, ob_ref), rest = rest[:2], rest[2:]
    w_dst, rest = rest[:n_weights], rest[n_weights:]
    (sf_ref, sb_ref), rest = rest[:2], rest[2:]
    scratch, cast_scratch = rest[:-4], rest[-4:]
    step = pl.program_id(0) * pl.num_programs(1) + pl.program_id(1)
    caster = _WeightCaster(w_src, w_dst, *cast_scratch, n_steps=n_steps)
    caster.begin(step)

    @pl.when(pl.program_id(1) == 0)
    def _():
        sf_ref[...] = jnp.zeros_like(sf_ref)
        sb_ref[...] = jnp.zeros_like(sb_ref)

    n = len(scratch) // 2
    dirs = (_GlaDir(False, gf_ref, vf_ref, decf_ref, of_ref, sf_ref, *scratch[:n]),
            _GlaDir(True, gb_ref, vb_ref, decb_ref, ob_ref, sb_ref, *scratch[n:]))

    for c in range(n_chunks):
        for d in dirs:
            _gla_intra(d, c)
    for d in dirs:
        _gla_scan(d, n_chunks)
    for c in range(n_chunks):
        for d in dirs:
            _gla_outputs(d, c)
    caster.finish(step)


def _gla(gf, gb, vg, dec, weights, *, tb):
    B, T, vw = vg.shape
    nb = T // tb
    nc = tb // GLA_CHUNK
    kw = GLA_HEADS * GLA_DK
    n_pairs = GLA_HEADS // 2
    fwd = lambda w: pl.BlockSpec((None, tb, w), lambda b, s: (b, s, 0))
    bwd = lambda w: pl.BlockSpec((None, tb, w), lambda b, s: (b, nb - 1 - s, 0))
    dec_fwd = pl.BlockSpec((None, nc, 1, kw), lambda b, s: (b, s, 0, 0))
    dec_bwd = pl.BlockSpec((None, nc, 1, kw), lambda b, s: (b, nb - 1 - s, 0, 1))
    state = pltpu.VMEM((GLA_HEADS, GLA_DK, GLA_DV), F32)
    per_dir = [
        pltpu.VMEM((nc, 2, n_pairs * GLA_CHUNK, 2 * GLA_CHUNK), BF16),
        pltpu.VMEM((nc, GLA_HEADS, GLA_DK, GLA_DV), F32),
        pltpu.VMEM((nc, GLA_HEADS, GLA_DK, GLA_DV), BF16)]
    hbm = pl.BlockSpec(memory_space=pl.ANY)
    for w in weights:
        assert w.dtype == F32 and w.shape[0] % CAST_ROWS == 0 and w.shape[1] % CAST_COLS == 0
    cast_scratch = [pltpu.VMEM((2, CAST_ROWS, CAST_COLS), F32), pltpu.VMEM((2, CAST_ROWS, CAST_COLS), BF16),
                    pltpu.SemaphoreType.DMA((2,)), pltpu.SemaphoreType.DMA((2,))]
    return pl.pallas_call(
        functools.partial(_gla_kernel, n_chunks=nc, n_weights=len(weights), n_steps=B * nb),
        grid=(B, nb),
        in_specs=[fwd(3 * kw), fwd(vw), dec_fwd, bwd(3 * kw), bwd(vw), dec_bwd] + [hbm] * len(weights),
        out_specs=[fwd(vw), bwd(vw)] + [hbm] * len(weights),
        out_shape=[jax.ShapeDtypeStruct((B, T, vw), BF16)] * 2
        + [jax.ShapeDtypeStruct(w.shape, BF16) for w in weights],
        scratch_shapes=[state, state] + per_dir * 2 + cast_scratch,
        compiler_params=pltpu.CompilerParams(
            dimension_semantics=("arbitrary", "arbitrary"), vmem_limit_bytes=VMEM_LIMIT),
        name="gla",
    )(gf, vg, dec, gb, vg, dec, *weights)


def _mix_ffn_kernel(x_ref, yna_ref, of_ref, ob_ref, sr_ref, gng_ref, wout_ref, gff_ref,
                    w1_ref, w2_ref, gfin_ref, out_ref, *, final_norm, ff_chunk):
    o = of_ref[...].astype(F32) + ob_ref[...].astype(F32)
    parts = [_rms(o[:, h * GLA_DV:(h + 1) * GLA_DV], gng_ref[...]) for h in range(GLA_HEADS)]
    y_gla = (jnp.concatenate(parts, axis=-1) * sr_ref[...].astype(F32)).astype(BF16)
    na_w = yna_ref.shape[-1]
    mix = _dot(yna_ref[...], wout_ref[0:na_w, :]) + _dot(y_gla, wout_ref[na_w:, :])
    h1 = x_ref[...] + mix
    n2 = _rms(h1, gff_ref[...]).astype(BF16)
    ffn = None
    for f in range(w1_ref.shape[1] // ff_chunk):
        u = _dot(n2, w1_ref[:, f * ff_chunk:(f + 1) * ff_chunk])
        a = jnp.square(jnp.maximum(u, 0.0)).astype(BF16)
        d = _dot(a, w2_ref[f * ff_chunk:(f + 1) * ff_chunk, :])
        ffn = d if ffn is None else ffn + d
    h2 = h1 + ffn
    out_ref[...] = _rms(h2, gfin_ref[...]) if final_norm else h2


def _mix_ffn(x, yna, of, ob, sr, gng, wout, gff, w1, w2, gfin, *, tm, final_norm):
    B, T, D = x.shape
    tok = lambda w: pl.BlockSpec((None, tm, w), lambda b, i: (b, i, 0))
    const = lambda a: pl.BlockSpec(a.shape, lambda b, i: (0,) * a.ndim,
                                   pipeline_mode=pl.Buffered(1))
    return pl.pallas_call(
        functools.partial(_mix_ffn_kernel, final_norm=final_norm, ff_chunk=1024),
        grid=(B, T // tm),
        in_specs=[tok(D), tok(yna.shape[-1]), tok(of.shape[-1]), tok(ob.shape[-1]),
                  tok(sr.shape[-1]), const(gng), const(wout), const(gff), const(w1),
                  const(w2), const(gfin)],
        out_specs=tok(D),
        out_shape=jax.ShapeDtypeStruct((B, T, D), F32),
        compiler_params=pltpu.CompilerParams(
            dimension_semantics=("parallel", "parallel"), vmem_limit_bytes=VMEM_LIMIT),
        name="mix_ffn",
    )(x, yna, of, ob, sr, gng, wout, gff, w1, w2, gfin)


def kernel(x, ln_mix_g, w_in, na_rpb, gla_gate_up_fwd, gla_gate_bias_fwd, gla_gate_up_bwd,
           gla_gate_bias_bwd, gla_norm_g, w_out, ln_ff_g, w_ff1, w_ff2, ln_final_g):
    B, T, D = x.shape
    depth = w_in.shape[0]
    assert T % NA_GROUP == 0 and T // NA_GROUP >= 3 and T % 512 == 0
    kw = GLA_HEADS * GLA_DK
    row = lambda v: v.reshape(1, -1).astype(F32)
    h = x
    for l in range(depth):
        zeros = jnp.zeros((GLA_GATE_RANK, kw), F32)
        gu = jnp.concatenate([
            jnp.concatenate([gla_gate_up_fwd[l], zeros], axis=1),
            jnp.concatenate([zeros, gla_gate_up_bwd[l]], axis=1)], axis=0).astype(BF16)
        gb = jnp.concatenate([gla_gate_bias_fwd[l], gla_gate_bias_bwd[l]]).reshape(1, -1)
        qa, kat, va, vg, sr, gla_f, gla_b, dec = _inproj(
            h, row(ln_mix_g[l]), w_in[l].T, gu, gb.astype(F32), tm=512)
        y_na = _na(qa, kat, va, _na_bias_rows(na_rpb[l]))
        o_f, o_b, wout, w1, w2 = _gla(gla_f, gla_b, vg, dec, (w_out[l], w_ff1[l], w_ff2[l]),
                                      tb=1024 if T % 1024 == 0 else 512)
        h = _mix_ffn(h, y_na, o_f, o_b, sr, row(gla_norm_g[l]), wout, row(ln_ff_g[l]),
                     w1, w2, row(ln_final_g), tm=512, final_norm=(l == depth - 1))
    return h
```

```python
import functools

import jax
import jax.numpy as jnp
from jax import lax
from jax.experimental import pallas as pl
from jax.experimental.pallas import tpu as pltpu

F32 = jnp.float32
BF16 = jnp.bfloat16

EPS = 1e-6
GRID_W = 64
NA_HEADS = 8
NA_HEAD_DIM = 64
NA_KH = 8
NA_KW = 16
GLA_HEADS = 4
GLA_DK = 64
GLA_DV = 128
GLA_GATE_RANK = 16
GLA_GATE_NORM = 16.0
GLA_CHUNK = 64

NA_WIDTH = NA_HEADS * NA_HEAD_DIM
GLA_KW = GLA_HEADS * GLA_DK
GLA_VW = GLA_HEADS * GLA_DV
COLS, _start = {}, 0
for _name, _width in (("qa", NA_WIDTH), ("ka", NA_WIDTH), ("va", NA_WIDTH), ("qg", GLA_KW),
                      ("kg", GLA_KW), ("vg", GLA_VW), ("rg", GLA_VW), ("z", 2 * GLA_GATE_RANK)):
    COLS[_name] = slice(_start, _start + _width)
    _start += _width
D_IN = _start

LANES = 128
TOKEN_TILE = 512
GLA_BLOCK = 1024
NA_GROUP_ROWS = 4
NA_GROUP = NA_GROUP_ROWS * GRID_W
NA_KEY_ROWS = 12
NA_KEYS = NA_KEY_ROWS * GRID_W
NA_UNROLL = 4
LOG2E = 1.4426950408889634
VMEM_LIMIT = 56 * 1024 * 1024


def _dot(a, b):
    return jnp.dot(a, b, preferred_element_type=F32)


def _dot_nt(a, b):
    return lax.dot_general(a, b, (((1,), (1,)), ((), ())), preferred_element_type=F32)


def _dot_tn(a, b):
    return lax.dot_general(a, b, (((0,), (0,)), ((), ())), preferred_element_type=F32)


def _rms(x, g):
    return x * lax.rsqrt(jnp.mean(x * x, axis=-1, keepdims=True) + EPS) * g


def _segmented_cumsum(x, reverse):
    n = x.shape[0]
    pos = lax.broadcasted_iota(jnp.int32, x.shape, 0) & (GLA_CHUNK - 1)
    step = 1
    while step < GLA_CHUNK:
        if reverse:
            shifted, ok = pltpu.roll(x, n - step, 0), pos < GLA_CHUNK - step
        else:
            shifted, ok = pltpu.roll(x, step, 0), pos >= step
        x = x + jnp.where(ok, shifted, 0.0)
        step *= 2
    return x


def _gla_cumdecay(log_a, reverse):
    b = _segmented_cumsum(log_a, reverse)
    b3 = b.reshape(-1, GLA_CHUNK, b.shape[-1])
    last = 0 if reverse else GLA_CHUNK - 1
    return b, b3[:, last:last + 1, :]


def _gla_operands(q, k, b, b_last):
    b3 = b.reshape(-1, GLA_CHUNK, b.shape[-1])
    k_dec = (k.reshape(b3.shape) * jnp.exp2(b_last - b3)).reshape(b.shape)
    ops = jnp.concatenate([q * jnp.exp2(b), k * jnp.exp2(-b), k_dec], axis=-1)
    return ops.astype(BF16), jnp.exp2(b_last)


def _inproj_kernel(x_ref, g_ref, wt_ref, gu_ref, gb_ref,
                   qa_ref, kat_ref, va_ref, vg_ref, sr_ref, gf_ref, gbw_ref, dec_ref, wb_ref):
    @pl.when((pl.program_id(0) == 0) & (pl.program_id(1) == 0))
    def _():
        wb_ref[...] = wt_ref[...].astype(BF16)

    def proj(name):
        return _dot_nt(n, wb_ref[COLS[name], :])

    n = _rms(x_ref[...], g_ref[...]).astype(BF16)

    def log_decay(cols):
        pre = _dot(z, gu_ref[:, cols]) + gb_ref[:, cols]
        softplus2 = jnp.log2(1.0 + jnp.exp2(jnp.abs(pre) * -LOG2E))
        return jnp.minimum(pre * (LOG2E / GLA_GATE_NORM), 0.0) - softplus2 * (1.0 / GLA_GATE_NORM)

    z = proj("z").astype(BF16)
    b_f, last_f = _gla_cumdecay(log_decay(slice(0, GLA_KW)), False)
    qa_ref[...] = (proj("qa") * (NA_HEAD_DIM ** -0.5 * LOG2E)).astype(BF16)
    b_b, last_b = _gla_cumdecay(log_decay(slice(GLA_KW, 2 * GLA_KW)), True)
    kat_ref[...] = _dot_nt(wb_ref[COLS["ka"], :], n).astype(BF16)
    qg = proj("qg") * (GLA_DK ** -0.5)
    kg = proj("kg")
    gf_ref[...], dec_f = _gla_operands(qg, kg, b_f, last_f)
    va_ref[...] = proj("va").astype(BF16)
    gbw_ref[...], dec_b = _gla_operands(qg, kg, b_b, last_b)
    dec_ref[...] = jnp.concatenate([dec_f, dec_b], axis=-1)
    vg_ref[...] = proj("vg").astype(BF16)
    r = proj("rg")
    sr_ref[...] = (r * jax.nn.sigmoid(r)).astype(BF16)


def _inproj(x, g, wt, gu, gb, *, tm):
    B, T, D = x.shape
    assert wt.shape == (D_IN, D)
    nc = tm // GLA_CHUNK
    tok = lambda w: pl.BlockSpec((None, tm, w), lambda b, i: (b, i, 0))
    const = lambda a: pl.BlockSpec(a.shape, lambda b, i: (0,) * a.ndim,
                                   pipeline_mode=pl.Buffered(1))
    tok_out = lambda w: (jax.ShapeDtypeStruct((B, T, w), BF16), tok(w))
    outs = [
        tok_out(NA_WIDTH),
        (jax.ShapeDtypeStruct((B, NA_WIDTH, T), BF16),
         pl.BlockSpec((None, NA_WIDTH, tm), lambda b, i: (b, 0, i))),
        tok_out(NA_WIDTH),
        tok_out(GLA_VW),
        tok_out(GLA_VW),
        tok_out(3 * GLA_KW),
        tok_out(3 * GLA_KW),
        (jax.ShapeDtypeStruct((B, T // GLA_CHUNK, 1, 2 * GLA_KW), F32),
         pl.BlockSpec((None, nc, 1, 2 * GLA_KW), lambda b, i: (b, i, 0, 0))),
    ]
    return pl.pallas_call(
        _inproj_kernel,
        grid=(B, T // tm),
        in_specs=[tok(D), const(g), const(wt), const(gu), const(gb)],
        out_specs=[s for _, s in outs],
        out_shape=[s for s, _ in outs],
        scratch_shapes=[pltpu.VMEM(wt.shape, BF16)],
        compiler_params=pltpu.CompilerParams(
            dimension_semantics=("arbitrary", "arbitrary"), vmem_limit_bytes=VMEM_LIMIT),
        name="inproj",
    )(x, g, wt, gu, gb)


def _na_bias_rows(rpb):
    n_dc = 2 * NA_KW - 1
    padded = jnp.pad(rpb.astype(F32), ((0, 0), (4, 5), (0, GRID_W - n_dc)))
    return jnp.concatenate([padded[:, :-1], padded[:, 1:]], axis=-1)


def _na_tiles(table, g):
    i_start = (0, g, NA_KEY_ROWS - NA_KH)[table]
    return range(i_start // 2, (i_start + NA_KH + 1) // 2)


def _na_build_tables(rows_ref, tab_ref):
    qc = lax.broadcasted_iota(jnp.int32, (GRID_W, LANES), 0)
    lane = lax.broadcasted_iota(jnp.int32, (GRID_W, LANES), 1)
    kc = lane & (GRID_W - 1)
    col_start = jnp.clip(qc - NA_KW // 2, 0, GRID_W - NA_KW)
    in_win = (kc >= col_start) & (kc < col_start + NA_KW)
    valid = {(True, True): in_win,
             (True, False): in_win & (lane < GRID_W),
             (False, True): in_win & (lane >= GRID_W)}
    row_offset = (0, -(NA_KH // 2), -NA_KH)
    for hh in range(2):
        for t in range(3):
            for g in range(NA_GROUP_ROWS):
                i_start = (0, g, NA_KEY_ROWS - NA_KH)[t]
                for m in _na_tiles(t, g):
                    halves = tuple(i_start <= i < i_start + NA_KH for i in (2 * m, 2 * m + 1))
                    a = 2 * m - g + row_offset[t] + NA_KH - 1 + 4
                    src = jnp.broadcast_to(rows_ref[hh, a:a + 1, :], (GRID_W, LANES))
                    rot = pltpu.roll(src, LANES - (NA_KW - 1), 1, stride=1, stride_axis=0)
                    tab_ref[hh, t, g * GRID_W:(g + 1) * GRID_W, m * LANES:(m + 1) * LANES] = (
                        jnp.where(valid[halves], rot * LOG2E, -jnp.inf))


class _NaRefs:
    def __init__(self, q, kt, v, bias, o, n_groups):
        self.q, self.kt, self.v, self.bias, self.o, self.n_groups = q, kt, v, bias, o, n_groups


def _na_rows(u):
    return pl.ds(pl.multiple_of(u * NA_GROUP, NA_GROUP), NA_GROUP)


def _na_key0(r, u):
    return pl.multiple_of(jnp.clip(u - 1, 0, r.n_groups - 3) * NA_GROUP, NA_GROUP)


def _na_scores(r, u):
    qq = r.q[_na_rows(u), :]
    first_head = lax.broadcasted_iota(jnp.int32, qq.shape, 1) < NA_HEAD_DIM
    zero = jnp.zeros_like(qq)
    q_heads = jnp.concatenate([jnp.where(first_head, qq, zero),
                               jnp.where(first_head, zero, qq)], axis=0)
    return _dot(q_heads, r.kt[:, pl.ds(_na_key0(r, u), NA_KEYS)])


def _na_softmax(r, s, table):
    p_tiles, inv_l = [], []
    for hh in range(2):
        for g in range(NA_GROUP_ROWS):
            rows = slice(hh * NA_GROUP + g * GRID_W, hh * NA_GROUP + (g + 1) * GRID_W)
            sb = [s[rows, c * LANES:(c + 1) * LANES]
                  + r.bias[hh, table, g * GRID_W:(g + 1) * GRID_W, c * LANES:(c + 1) * LANES]
                  for c in _na_tiles(table, g)]
            m = jnp.max(functools.reduce(jnp.maximum, sb), axis=-1, keepdims=True)
            p = [jnp.exp2(x - m) for x in sb]
            inv_l.append(1.0 / jnp.sum(functools.reduce(jnp.add, p), axis=-1, keepdims=True))
            p_tiles.append([x.astype(BF16) for x in p])
    return p_tiles, inv_l


def _na_store_output(r, u, o):
    first_head = lax.broadcasted_iota(jnp.int32, (NA_GROUP, LANES), 1) < NA_HEAD_DIM
    r.o[_na_rows(u), :] = jnp.where(first_head, o[:NA_GROUP], o[NA_GROUP:]).astype(r.o.dtype)


def _na_edge_group(r, u, table):
    tiles = _na_tiles(table, 0)
    p_tiles, inv_l = _na_softmax(r, _na_scores(r, u), table)
    p = jnp.concatenate([jnp.concatenate(t, axis=1) for t in p_tiles], axis=0)
    keys = pl.ds(_na_key0(r, u) + tiles.start * LANES, len(tiles) * LANES)
    _na_store_output(r, u, _dot(p, r.v[keys, :]) * jnp.concatenate(inv_l, axis=0))


def _na_interior_probs(r, s):
    p_tiles, inv_l = _na_softmax(r, s, 1)
    zero_tile = jnp.zeros((GRID_W, LANES), BF16)
    p_rows = []
    for block, tiles in enumerate(p_tiles):
        valid = _na_tiles(1, block % NA_GROUP_ROWS)
        p_rows.append(jnp.concatenate(
            [zero_tile] * valid.start + tiles + [zero_tile] * (NA_KEYS // LANES - valid.stop), axis=1))
    return jnp.concatenate(p_rows, axis=0), jnp.concatenate(inv_l, axis=0)


def _na_interior_output(r, u, p, inv_l):
    _na_store_output(r, u, _dot(p, r.v[pl.ds(_na_key0(r, u), NA_KEYS), :]) * inv_l)


def _na_stage(r, u, carry):
    s_cur, p_prev, l_prev = carry
    s_next = _na_scores(r, jnp.minimum(u + 1, r.n_groups - 2))
    p_cur, l_cur = _na_interior_probs(r, s_cur)
    _na_interior_output(r, u - 1, p_prev, l_prev)
    return s_next, p_cur, l_cur


def _na_kernel(q_ref, kt_ref, v_ref, rows_ref, o_ref, bias_ref, s_ref, p_ref, l_ref, *, n_groups):
    @pl.when(pl.program_id(1) == 0)
    def _():
        _na_build_tables(rows_ref, bias_ref)

    r = _NaRefs(q_ref, kt_ref, v_ref, bias_ref, o_ref, n_groups)
    last = n_groups - 1
    _na_edge_group(r, 0, 0)
    _na_edge_group(r, last, 2)

    def load_carry():
        return s_ref[...], p_ref[...], l_ref[...]

    def store_carry(carry):
        s_ref[...], p_ref[...], l_ref[...] = carry

    def stages(first, count, carry):
        for k in range(count):
            carry = _na_stage(r, first + k, carry)
        return carry

    def unrolled_stages(i, c):
        store_carry(stages(2 + NA_UNROLL * i, NA_UNROLL, load_carry()))
        return c

    store_carry((_na_scores(r, 2),) + _na_interior_probs(r, _na_scores(r, 1)))
    n_loops, n_tail = divmod(n_groups - 3, NA_UNROLL)
    lax.fori_loop(0, n_loops, unrolled_stages, 0)
    _, p_prev, l_prev = stages(2 + NA_UNROLL * n_loops, n_tail, load_carry())
    _na_interior_output(r, last - 1, p_prev, l_prev)


def _na(qa, kat, va, rows):
    B, T, _ = qa.shape
    n_pairs = NA_HEADS // 2
    return pl.pallas_call(
        functools.partial(_na_kernel, n_groups=T // NA_GROUP),
        grid=(n_pairs, B),
        in_specs=[
            pl.BlockSpec((None, T, LANES), lambda p, b: (b, 0, p)),
            pl.BlockSpec((None, LANES, T), lambda p, b: (b, p, 0)),
            pl.BlockSpec((None, T, LANES), lambda p, b: (b, 0, p)),
            pl.BlockSpec((2,) + rows.shape[1:], lambda p, b: (p, 0, 0)),
        ],
        out_specs=pl.BlockSpec((None, T, LANES), lambda p, b: (b, 0, p)),
        out_shape=jax.ShapeDtypeStruct((B, T, NA_WIDTH), BF16),
        scratch_shapes=[pltpu.VMEM((2, 3, NA_GROUP, NA_KEYS), F32),
                        pltpu.VMEM((2 * NA_GROUP, NA_KEYS), F32),
                        pltpu.VMEM((2 * NA_GROUP, NA_KEYS), BF16),
                        pltpu.VMEM((2 * NA_GROUP, 1), F32)],
        compiler_params=pltpu.CompilerParams(
            dimension_semantics=("parallel", "arbitrary"), vmem_limit_bytes=VMEM_LIMIT),
        name="natten",
    )(qa, kat, va, rows)


class _GlaDir:
    def __init__(self, reverse, ops, v, dec, o, state, a, contrib, prev):
        self.reverse = reverse
        self.qd, self.ki, self.kd = (ops.at[:, i * GLA_KW:(i + 1) * GLA_KW] for i in range(3))
        self.v, self.dec, self.o, self.state = v, dec, o, state
        self.a, self.contrib, self.prev = a, contrib, prev


def _gla_pair_lanes(h):
    lane = lax.broadcasted_iota(jnp.int32, (GLA_CHUNK, LANES), 1)
    return (lane < GLA_DK) if h % 2 == 0 else (lane >= GLA_DK)


def _gla_intra(d, c):
    C = GLA_CHUNK
    rows = slice(c * C, (c + 1) * C)
    qd, ki = d.qd[rows, :], d.ki[rows, :]
    lane = lax.broadcasted_iota(jnp.int32, qd.shape, 1)
    zeros = jnp.zeros_like(ki)
    i = lax.broadcasted_iota(jnp.int32, (2 * C, LANES), 0) & (C - 1)
    col = lax.broadcasted_iota(jnp.int32, (2 * C, LANES), 1)
    for parity in range(2):
        q_heads = jnp.concatenate(
            [jnp.where((lane >= h * GLA_DK) & (lane < (h + 1) * GLA_DK), qd, jnp.zeros_like(qd))
             for h in range(parity, GLA_HEADS, 2)], axis=0)
        keys = jnp.concatenate([zeros, ki] if parity == 0 else [ki, zeros], axis=0)
        scores = _dot_nt(q_heads, keys)
        j = col - C if parity == 0 else col
        in_chunk = (j >= 0) & (j < C)
        keep = in_chunk & ((j > i) if d.reverse else (j <= i))
        d.a[c, parity] = jnp.where(keep, scores, 0.0).astype(BF16)
    for p in range(GLA_HEADS // 2):
        kd_p = d.kd[rows, p * LANES:(p + 1) * LANES]
        v_p = d.v[rows, 2 * p * GLA_DV:(2 * p + 2) * GLA_DV]
        both = _dot_tn(kd_p, v_p)
        d.contrib[c, 2 * p] = both[:GLA_DK, :GLA_DV]
        d.contrib[c, 2 * p + 1] = both[GLA_DK:, GLA_DV:]


def _gla_scan(d, n_chunks):
    order = range(n_chunks - 1, -1, -1) if d.reverse else range(n_chunks)
    pad = jnp.zeros((LANES - n_chunks, GLA_KW), F32)
    dec_t = jnp.concatenate([d.dec[c] for c in range(n_chunks)] + [pad], axis=0).T
    for h in range(GLA_HEADS):
        st = d.state[h]
        for c in order:
            d.prev[c, h] = st.astype(BF16)
            st = st * dec_t[h * GLA_DK:(h + 1) * GLA_DK, c:c + 1] + d.contrib[c, h]
        d.state[h] = st


def _gla_outputs(d, c):
    C = GLA_CHUNK
    rows = slice(c * C, (c + 1) * C)
    for h in range(GLA_HEADS):
        p, parity = divmod(h, 2)
        qd_p = d.qd[rows, p * LANES:(p + 1) * LANES]
        scores = d.a[c, parity, p * C:(p + 1) * C, :]
        lhs = jnp.where(_gla_pair_lanes(h), qd_p, scores)
        cols = slice(h * GLA_DV, (h + 1) * GLA_DV)
        operands = [d.prev[c, h], d.v[rows, cols]]
        rhs = jnp.concatenate(operands if parity == 0 else operands[::-1], axis=0)
        d.o[rows, cols] = _dot(lhs, rhs).astype(d.o.dtype)


def _gla_kernel(gf_ref, vf_ref, decf_ref, gb_ref, vb_ref, decb_ref,
                of_ref, ob_ref, sf_ref, sb_ref, *scratch, n_chunks):
    @pl.when(pl.program_id(1) == 0)
    def _():
        sf_ref[...] = jnp.zeros_like(sf_ref)
        sb_ref[...] = jnp.zeros_like(sb_ref)

    n = len(scratch) // 2
    dirs = (_GlaDir(False, gf_ref, vf_ref, decf_ref, of_ref, sf_ref, *scratch[:n]),
            _GlaDir(True, gb_ref, vb_ref, decb_ref, ob_ref, sb_ref, *scratch[n:]))

    for c in range(n_chunks):
        for d in dirs:
            _gla_intra(d, c)
    for d in dirs:
        _gla_scan(d, n_chunks)
    for c in range(n_chunks):
        for d in dirs:
            _gla_outputs(d, c)


def _gla(gf, gb, vg, dec, *, tb):
    B, T, vw = vg.shape
    nb = T // tb
    nc = tb // GLA_CHUNK
    kw = GLA_KW
    n_pairs = GLA_HEADS // 2
    fwd = lambda w: pl.BlockSpec((None, tb, w), lambda b, s: (b, s, 0))
    bwd = lambda w: pl.BlockSpec((None, tb, w), lambda b, s: (b, nb - 1 - s, 0))
    dec_fwd = pl.BlockSpec((None, nc, 1, kw), lambda b, s: (b, s, 0, 0))
    dec_bwd = pl.BlockSpec((None, nc, 1, kw), lambda b, s: (b, nb - 1 - s, 0, 1))
    state = pltpu.VMEM((GLA_HEADS, GLA_DK, GLA_DV), F32)
    per_dir = [
        pltpu.VMEM((nc, 2, n_pairs * GLA_CHUNK, 2 * GLA_CHUNK), BF16),
        pltpu.VMEM((nc, GLA_HEADS, GLA_DK, GLA_DV), F32),
        pltpu.VMEM((nc, GLA_HEADS, GLA_DK, GLA_DV), BF16)]
    return pl.pallas_call(
        functools.partial(_gla_kernel, n_chunks=nc),
        grid=(B, nb),
        in_specs=[fwd(3 * kw), fwd(vw), dec_fwd, bwd(3 * kw), bwd(vw), dec_bwd],
        out_specs=[fwd(vw), bwd(vw)],
        out_shape=[jax.ShapeDtypeStruct((B, T, vw), BF16)] * 2,
        scratch_shapes=[state, state] + per_dir * 2,
        compiler_params=pltpu.CompilerParams(
            dimension_semantics=("parallel", "arbitrary"), vmem_limit_bytes=VMEM_LIMIT),
        name="gla",
    )(gf, vg, dec, gb, vg, dec)


STAGE_ROWS, STAGE_COLS = 512, 1024


def _stage_bf16(src_hbm, dst, stage, sem):
    rows, cols = src_hbm.shape
    tiles = [(r, c) for r in range(0, rows, STAGE_ROWS) for c in range(0, cols, STAGE_COLS)]

    def copy(i):
        r, c = tiles[i]
        return pltpu.make_async_copy(
            src_hbm.at[pl.ds(r, STAGE_ROWS), pl.ds(c, STAGE_COLS)], stage.at[i % 2], sem.at[i % 2])

    for i in range(min(2, len(tiles))):
        copy(i).start()
    for i, (r, c) in enumerate(tiles):
        copy(i).wait()
        dst[r:r + STAGE_ROWS, c:c + STAGE_COLS] = stage[i % 2].astype(BF16)
        if i + 2 < len(tiles):
            copy(i + 2).start()


def _mix_ffn_kernel(x_ref, yna_ref, of_ref, ob_ref, sr_ref, gng_ref, wout_hbm, gff_ref,
                    w1_hbm, w2_hbm, gfin_ref, out_ref, wout_ref, w1_ref, w2_ref, stage, sem,
                    *, final_norm, ff_chunk):
    @pl.when((pl.program_id(0) == 0) & (pl.program_id(1) == 0))
    def _():
        for src, dst in ((wout_hbm, wout_ref), (w1_hbm, w1_ref), (w2_hbm, w2_ref)):
            _stage_bf16(src, dst, stage, sem)

    o = of_ref[...].astype(F32) + ob_ref[...].astype(F32)
    parts = [_rms(o[:, h * GLA_DV:(h + 1) * GLA_DV], gng_ref[...]) for h in range(GLA_HEADS)]
    y_gla = (jnp.concatenate(parts, axis=-1) * sr_ref[...].astype(F32)).astype(BF16)
    na_w = yna_ref.shape[-1]
    mix = _dot(yna_ref[...], wout_ref[0:na_w, :]) + _dot(y_gla, wout_ref[na_w:, :])
    h1 = x_ref[...] + mix
    n2 = _rms(h1, gff_ref[...]).astype(BF16)
    ffn = None
    for f in range(w1_ref.shape[1] // ff_chunk):
        u = _dot(n2, w1_ref[:, f * ff_chunk:(f + 1) * ff_chunk])
        a = jnp.square(jnp.maximum(u, 0.0)).astype(BF16)
        d = _dot(a, w2_ref[f * ff_chunk:(f + 1) * ff_chunk, :])
        ffn = d if ffn is None else ffn + d
    h2 = h1 + ffn
    out_ref[...] = _rms(h2, gfin_ref[...]) if final_norm else h2


def _mix_ffn(x, yna, of, ob, sr, gng, wout, gff, w1, w2, gfin, *, tm, final_norm):
    B, T, D = x.shape
    tok = lambda w: pl.BlockSpec((None, tm, w), lambda b, i: (b, i, 0))
    const = lambda a: pl.BlockSpec(a.shape, lambda b, i: (0,) * a.ndim,
                                   pipeline_mode=pl.Buffered(1))
    hbm = pl.BlockSpec(memory_space=pl.ANY)
    for w in (wout, w1, w2):
        assert w.dtype == F32 and w.shape[0] % STAGE_ROWS == 0 and w.shape[1] % STAGE_COLS == 0
    return pl.pallas_call(
        functools.partial(_mix_ffn_kernel, final_norm=final_norm, ff_chunk=1024),
        grid=(B, T // tm),
        in_specs=[tok(D), tok(yna.shape[-1]), tok(of.shape[-1]), tok(ob.shape[-1]),
                  tok(sr.shape[-1]), const(gng), hbm, const(gff), hbm, hbm, const(gfin)],
        out_specs=tok(D),
        out_shape=jax.ShapeDtypeStruct((B, T, D), F32),
        scratch_shapes=[pltpu.VMEM(w.shape, BF16) for w in (wout, w1, w2)] + [
            pltpu.VMEM((2, STAGE_ROWS, STAGE_COLS), F32), pltpu.SemaphoreType.DMA((2,))],
        compiler_params=pltpu.CompilerParams(
            dimension_semantics=("arbitrary", "arbitrary"), vmem_limit_bytes=VMEM_LIMIT),
        name="mix_ffn",
    )(x, yna, of, ob, sr, gng, wout, gff, w1, w2, gfin)


def kernel(x, ln_mix_g, w_in, na_rpb, gla_gate_up_fwd, gla_gate_bias_fwd, gla_gate_up_bwd,
           gla_gate_bias_bwd, gla_norm_g, w_out, ln_ff_g, w_ff1, w_ff2, ln_final_g):
    B, T, D = x.shape
    depth = w_in.shape[0]
    assert T % TOKEN_TILE == 0 and T // NA_GROUP >= 4
    gla_block = GLA_BLOCK if T % GLA_BLOCK == 0 else TOKEN_TILE
    row = lambda v: v.reshape(1, -1).astype(F32)
    zeros = jnp.zeros((GLA_GATE_RANK, GLA_KW), F32)
    h = x
    for l in range(depth):
        gu = jnp.concatenate([
            jnp.concatenate([gla_gate_up_fwd[l], zeros], axis=1),
            jnp.concatenate([zeros, gla_gate_up_bwd[l]], axis=1)], axis=0).astype(BF16)
        gb = jnp.concatenate([gla_gate_bias_fwd[l], gla_gate_bias_bwd[l]]).reshape(1, -1)
        qa, kat, va, vg, sr, gla_f, gla_b, dec = _inproj(
            h, row(ln_mix_g[l]), w_in[l].T, gu, gb.astype(F32), tm=TOKEN_TILE)
        y_na = _na(qa, kat, va, _na_bias_rows(na_rpb[l]))
        o_f, o_b = _gla(gla_f, gla_b, vg, dec, tb=gla_block)
        h = _mix_ffn(h, y_na, o_f, o_b, sr, row(gla_norm_g[l]), w_out[l], row(ln_ff_g[l]),
                     w_ff1[l], w_ff2[l], row(ln_final_g), tm=TOKEN_TILE,
                     final_norm=(l == depth - 1))
    return h
```

```python
import functools

import jax
import jax.numpy as jnp
from jax import lax
from jax.experimental import pallas as pl
from jax.experimental.pallas import tpu as pltpu

F32 = jnp.float32
BF16 = jnp.bfloat16

EPS = 1e-6
GRID_W = 64
NA_HEADS = 8
NA_HEAD_DIM = 64
NA_KH = 8
NA_KW = 16
GLA_HEADS = 4
GLA_DK = 64
GLA_DV = 128
GLA_GATE_RANK = 16
GLA_GATE_NORM = 16.0
GLA_CHUNK = 64

NA_WIDTH = NA_HEADS * NA_HEAD_DIM
GLA_KW = GLA_HEADS * GLA_DK
GLA_VW = GLA_HEADS * GLA_DV
COLS, _start = {}, 0
for _name, _width in (("qa", NA_WIDTH), ("ka", NA_WIDTH), ("va", NA_WIDTH), ("qg", GLA_KW),
                      ("kg", GLA_KW), ("vg", GLA_VW), ("rg", GLA_VW), ("z", 2 * GLA_GATE_RANK)):
    COLS[_name] = slice(_start, _start + _width)
    _start += _width
D_IN = _start

LANES = 128
TOKEN_TILE = 512
GLA_BLOCK = 1024
NA_GROUP_ROWS = 4
NA_GROUP = NA_GROUP_ROWS * GRID_W
NA_KEY_ROWS = 12
NA_KEYS = NA_KEY_ROWS * GRID_W
NA_UNROLL = 4
LOG2E = 1.4426950408889634
VMEM_LIMIT = 56 * 1024 * 1024


def _dot(a, b):
    return jnp.dot(a, b, preferred_element_type=F32)


def _dot_nt(a, b):
    return lax.dot_general(a, b, (((1,), (1,)), ((), ())), preferred_element_type=F32)


def _dot_tn(a, b):
    return lax.dot_general(a, b, (((0,), (0,)), ((), ())), preferred_element_type=F32)


def _rms(x, g):
    return x * lax.rsqrt(jnp.mean(x * x, axis=-1, keepdims=True) + EPS) * g


def _segmented_cumsum(x, reverse):
    n = x.shape[0]
    pos = lax.broadcasted_iota(jnp.int32, x.shape, 0) & (GLA_CHUNK - 1)
    step = 1
    while step < GLA_CHUNK:
        if reverse:
            shifted, ok = pltpu.roll(x, n - step, 0), pos < GLA_CHUNK - step
        else:
            shifted, ok = pltpu.roll(x, step, 0), pos >= step
        x = x + jnp.where(ok, shifted, 0.0)
        step *= 2
    return x


def _gla_cumdecay(log_a, reverse):
    b = _segmented_cumsum(log_a, reverse)
    b3 = b.reshape(-1, GLA_CHUNK, b.shape[-1])
    last = 0 if reverse else GLA_CHUNK - 1
    return b, b3[:, last:last + 1, :]


def _gla_operands(q, k, b, b_last):
    b3 = b.reshape(-1, GLA_CHUNK, b.shape[-1])
    k_dec = (k.reshape(b3.shape) * jnp.exp2(b_last - b3)).reshape(b.shape)
    ops = jnp.concatenate([q * jnp.exp2(b), k * jnp.exp2(-b), k_dec], axis=-1)
    return ops.astype(BF16), jnp.exp2(b_last)


def _inproj_kernel(x_ref, g_ref, wt_ref, gu_ref, gb_ref,
                   qa_ref, kat_ref, va_ref, vg_ref, sr_ref, gf_ref, gbw_ref, dec_ref, wb_ref):
    @pl.when((pl.program_id(0) == 0) & (pl.program_id(1) == 0))
    def _():
        wb_ref[...] = wt_ref[...].astype(BF16)

    def proj(name):
        return _dot_nt(n, wb_ref[COLS[name], :])

    n = _rms(x_ref[...], g_ref[...]).astype(BF16)

    def log_decay(cols):
        pre = _dot(z, gu_ref[:, cols]) + gb_ref[:, cols]
        softplus2 = jnp.log2(1.0 + jnp.exp2(jnp.abs(pre) * -LOG2E))
        return jnp.minimum(pre * (LOG2E / GLA_GATE_NORM), 0.0) - softplus2 * (1.0 / GLA_GATE_NORM)

    z = proj("z").astype(BF16)
    b_f, last_f = _gla_cumdecay(log_decay(slice(0, GLA_KW)), False)
    qa_ref[...] = (proj("qa") * (NA_HEAD_DIM ** -0.5 * LOG2E)).astype(BF16)
    b_b, last_b = _gla_cumdecay(log_decay(slice(GLA_KW, 2 * GLA_KW)), True)
    kat_ref[...] = _dot_nt(wb_ref[COLS["ka"], :], n).astype(BF16)
    qg = proj("qg") * (GLA_DK ** -0.5)
    kg = proj("kg")
    gf_ref[...], dec_f = _gla_operands(qg, kg, b_f, last_f)
    va_ref[...] = proj("va").astype(BF16)
    gbw_ref[...], dec_b = _gla_operands(qg, kg, b_b, last_b)
    dec_ref[...] = jnp.concatenate([dec_f, dec_b], axis=-1)
    vg_ref[...] = proj("vg").astype(BF16)
    r = proj("rg")
    sr_ref[...] = (r * jax.nn.sigmoid(r)).astype(BF16)


def _inproj(x, g, wt, gu, gb, *, tm):
    B, T, D = x.shape
    assert wt.shape == (D_IN, D)
    nc = tm // GLA_CHUNK
    tok = lambda w: pl.BlockSpec((None, tm, w), lambda b, i: (b, i, 0))
    const = lambda a: pl.BlockSpec(a.shape, lambda b, i: (0,) * a.ndim,
                                   pipeline_mode=pl.Buffered(1))
    tok_out = lambda w: (jax.ShapeDtypeStruct((B, T, w), BF16), tok(w))
    outs = [
        tok_out(NA_WIDTH),
        (jax.ShapeDtypeStruct((B, NA_WIDTH, T), BF16),
         pl.BlockSpec((None, NA_WIDTH, tm), lambda b, i: (b, 0, i))),
        tok_out(NA_WIDTH),
        tok_out(GLA_VW),
        tok_out(GLA_VW),
        tok_out(3 * GLA_KW),
        tok_out(3 * GLA_KW),
        (jax.ShapeDtypeStruct((B, T // GLA_CHUNK, 1, 2 * GLA_KW), F32),
         pl.BlockSpec((None, nc, 1, 2 * GLA_KW), lambda b, i: (b, i, 0, 0))),
    ]
    return pl.pallas_call(
        _inproj_kernel,
        grid=(B, T // tm),
        in_specs=[tok(D), const(g), const(wt), const(gu), const(gb)],
        out_specs=[s for _, s in outs],
        out_shape=[s for s, _ in outs],
        scratch_shapes=[pltpu.VMEM(wt.shape, BF16)],
        compiler_params=pltpu.CompilerParams(
            dimension_semantics=("arbitrary", "arbitrary"), vmem_limit_bytes=VMEM_LIMIT),
        name="inproj",
    )(x, g, wt, gu, gb)


def _na_bias_rows(rpb):
    n_dc = 2 * NA_KW - 1
    padded = jnp.pad(rpb.astype(F32), ((0, 0), (4, 5), (0, GRID_W - n_dc)))
    return jnp.concatenate([padded[:, :-1], padded[:, 1:]], axis=-1)


def _na_tiles(table, g):
    i_start = (0, g, NA_KEY_ROWS - NA_KH)[table]
    return range(i_start // 2, (i_start + NA_KH + 1) // 2)


def _na_build_tables(rows_ref, tab_ref):
    qc = lax.broadcasted_iota(jnp.int32, (GRID_W, LANES), 0)
    lane = lax.broadcasted_iota(jnp.int32, (GRID_W, LANES), 1)
    kc = lane & (GRID_W - 1)
    col_start = jnp.clip(qc - NA_KW // 2, 0, GRID_W - NA_KW)
    in_win = (kc >= col_start) & (kc < col_start + NA_KW)
    valid = {(True, True): in_win,
             (True, False): in_win & (lane < GRID_W),
             (False, True): in_win & (lane >= GRID_W)}
    row_offset = (0, -(NA_KH // 2), -NA_KH)
    for hh in range(2):
        for t in range(3):
            for g in range(NA_GROUP_ROWS):
                i_start = (0, g, NA_KEY_ROWS - NA_KH)[t]
                for m in _na_tiles(t, g):
                    halves = tuple(i_start <= i < i_start + NA_KH for i in (2 * m, 2 * m + 1))
                    a = 2 * m - g + row_offset[t] + NA_KH - 1 + 4
                    src = jnp.broadcast_to(rows_ref[hh, a:a + 1, :], (GRID_W, LANES))
                    rot = pltpu.roll(src, LANES - (NA_KW - 1), 1, stride=1, stride_axis=0)
                    tab_ref[hh, t, g * GRID_W:(g + 1) * GRID_W, m * LANES:(m + 1) * LANES] = (
                        jnp.where(valid[halves], rot * LOG2E, -jnp.inf))


class _NaRefs:
    def __init__(self, q, kt, v, bias, o, n_groups):
        self.q, self.kt, self.v, self.bias, self.o, self.n_groups = q, kt, v, bias, o, n_groups


def _na_rows(u):
    return pl.ds(pl.multiple_of(u * NA_GROUP, NA_GROUP), NA_GROUP)


def _na_key0(r, u):
    return pl.multiple_of(jnp.clip(u - 1, 0, r.n_groups - 3) * NA_GROUP, NA_GROUP)


def _na_scores(r, u):
    qq = r.q[_na_rows(u), :]
    first_head = lax.broadcasted_iota(jnp.int32, qq.shape, 1) < NA_HEAD_DIM
    zero = jnp.zeros_like(qq)
    q_heads = jnp.concatenate([jnp.where(first_head, qq, zero),
                               jnp.where(first_head, zero, qq)], axis=0)
    return _dot(q_heads, r.kt[:, pl.ds(_na_key0(r, u), NA_KEYS)])


def _na_softmax(r, s, table):
    p_tiles, inv_l = [], []
    for hh in range(2):
        for g in range(NA_GROUP_ROWS):
            rows = slice(hh * NA_GROUP + g * GRID_W, hh * NA_GROUP + (g + 1) * GRID_W)
            sb = [s[rows, c * LANES:(c + 1) * LANES]
                  + r.bias[hh, table, g * GRID_W:(g + 1) * GRID_W, c * LANES:(c + 1) * LANES]
                  for c in _na_tiles(table, g)]
            m = jnp.max(functools.reduce(jnp.maximum, sb), axis=-1, keepdims=True)
            p = [jnp.exp2(x - m) for x in sb]
            inv_l.append(1.0 / jnp.sum(functools.reduce(jnp.add, p), axis=-1, keepdims=True))
            p_tiles.append([x.astype(BF16) for x in p])
    return p_tiles, inv_l


def _na_store_output(r, u, o):
    first_head = lax.broadcasted_iota(jnp.int32, (NA_GROUP, LANES), 1) < NA_HEAD_DIM
    r.o[_na_rows(u), :] = jnp.where(first_head, o[:NA_GROUP], o[NA_GROUP:]).astype(r.o.dtype)


def _na_edge_group(r, u, table):
    tiles = _na_tiles(table, 0)
    p_tiles, inv_l = _na_softmax(r, _na_scores(r, u), table)
    p = jnp.concatenate([jnp.concatenate(t, axis=1) for t in p_tiles], axis=0)
    keys = pl.ds(_na_key0(r, u) + tiles.start * LANES, len(tiles) * LANES)
    _na_store_output(r, u, _dot(p, r.v[keys, :]) * jnp.concatenate(inv_l, axis=0))


def _na_interior_probs(r, s):
    p_tiles, inv_l = _na_softmax(r, s, 1)
    zero_tile = jnp.zeros((GRID_W, LANES), BF16)
    p_rows = []
    for block, tiles in enumerate(p_tiles):
        valid = _na_tiles(1, block % NA_GROUP_ROWS)
        p_rows.append(jnp.concatenate(
            [zero_tile] * valid.start + tiles + [zero_tile] * (NA_KEYS // LANES - valid.stop), axis=1))
    return jnp.concatenate(p_rows, axis=0), jnp.concatenate(inv_l, axis=0)


def _na_interior_output(r, u, p, inv_l):
    _na_store_output(r, u, _dot(p, r.v[pl.ds(_na_key0(r, u), NA_KEYS), :]) * inv_l)


def _na_stage(r, u, carry):
    s_cur, p_prev, l_prev = carry
    s_next = _na_scores(r, jnp.minimum(u + 1, r.n_groups - 2))
    p_cur, l_cur = _na_interior_probs(r, s_cur)
    _na_interior_output(r, u - 1, p_prev, l_prev)
    return s_next, p_cur, l_cur


def _na_kernel(q_ref, kt_ref, v_ref, rows_ref, o_ref, bias_ref, s_ref, p_ref, l_ref, *, n_groups):
    @pl.when(pl.program_id(1) == 0)
    def _():
        _na_build_tables(rows_ref, bias_ref)

    r = _NaRefs(q_ref, kt_ref, v_ref, bias_ref, o_ref, n_groups)
    last = n_groups - 1
    _na_edge_group(r, 0, 0)
    _na_edge_group(r, last, 2)

    def load_carry():
        return s_ref[...], p_ref[...], l_ref[...]

    def store_carry(carry):
        s_ref[...], p_ref[...], l_ref[...] = carry

    def stages(first, count, carry):
        for k in range(count):
            carry = _na_stage(r, first + k, carry)
        return carry

    def unrolled_stages(i, c):
        store_carry(stages(2 + NA_UNROLL * i, NA_UNROLL, load_carry()))
        return c

    store_carry((_na_scores(r, 2),) + _na_interior_probs(r, _na_scores(r, 1)))
    n_loops, n_tail = divmod(n_groups - 3, NA_UNROLL)
    lax.fori_loop(0, n_loops, unrolled_stages, 0)
    _, p_prev, l_prev = stages(2 + NA_UNROLL * n_loops, n_tail, load_carry())
    _na_interior_output(r, last - 1, p_prev, l_prev)


def _na(qa, kat, va, rows):
    B, T, _ = qa.shape
    n_pairs = NA_HEADS // 2
    return pl.pallas_call(
        functools.partial(_na_kernel, n_groups=T // NA_GROUP),
        grid=(n_pairs, B),
        in_specs=[
            pl.BlockSpec((None, T, LANES), lambda p, b: (b, 0, p)),
            pl.BlockSpec((None, LANES, T), lambda p, b: (b, p, 0)),
            pl.BlockSpec((None, T, LANES), lambda p, b: (b, 0, p)),
            pl.BlockSpec((2,) + rows.shape[1:], lambda p, b: (p, 0, 0)),
        ],
        out_specs=pl.BlockSpec((None, T, LANES), lambda p, b: (b, 0, p)),
        out_shape=jax.ShapeDtypeStruct((B, T, NA_WIDTH), BF16),
        scratch_shapes=[pltpu.VMEM((2, 3, NA_GROUP, NA_KEYS), F32),
                        pltpu.VMEM((2 * NA_GROUP, NA_KEYS), F32),
                        pltpu.VMEM((2 * NA_GROUP, NA_KEYS), BF16),
                        pltpu.VMEM((2 * NA_GROUP, 1), F32)],
        compiler_params=pltpu.CompilerParams(
            dimension_semantics=("parallel", "arbitrary"), vmem_limit_bytes=VMEM_LIMIT),
        name="natten",
    )(qa, kat, va, rows)


class _GlaDir:
    def __init__(self, reverse, ops, v, dec, o, state, a, contrib, prev):
        self.reverse = reverse
        self.qd, self.ki, self.kd = (ops.at[:, i * GLA_KW:(i + 1) * GLA_KW] for i in range(3))
        self.v, self.dec, self.o, self.state = v, dec, o, state
        self.a, self.contrib, self.prev = a, contrib, prev


def _gla_pair_lanes(h):
    lane = lax.broadcasted_iota(jnp.int32, (GLA_CHUNK, LANES), 1)
    return (lane < GLA_DK) if h % 2 == 0 else (lane >= GLA_DK)


def _gla_intra(d, c):
    C = GLA_CHUNK
    rows = slice(c * C, (c + 1) * C)
    qd, ki = d.qd[rows, :], d.ki[rows, :]
    lane = lax.broadcasted_iota(jnp.int32, qd.shape, 1)
    zeros = jnp.zeros_like(ki)
    i = lax.broadcasted_iota(jnp.int32, (2 * C, LANES), 0) & (C - 1)
    col = lax.broadcasted_iota(jnp.int32, (2 * C, LANES), 1)
    for parity in range(2):
        q_heads = jnp.concatenate(
            [jnp.where((lane >= h * GLA_DK) & (lane < (h + 1) * GLA_DK), qd, jnp.zeros_like(qd))
             for h in range(parity, GLA_HEADS, 2)], axis=0)
        keys = jnp.concatenate([zeros, ki] if parity == 0 else [ki, zeros], axis=0)
        scores = _dot_nt(q_heads, keys)
        j = col - C if parity == 0 else col
        in_chunk = (j >= 0) & (j < C)
        keep = in_chunk & ((j > i) if d.reverse else (j <= i))
        d.a[c, parity] = jnp.where(keep, scores, 0.0).astype(BF16)
    for p in range(GLA_HEADS // 2):
        kd_p = d.kd[rows, p * LANES:(p + 1) * LANES]
        v_p = d.v[rows, 2 * p * GLA_DV:(2 * p + 2) * GLA_DV]
        both = _dot_tn(kd_p, v_p)
        d.contrib[c, 2 * p] = both[:GLA_DK, :GLA_DV]
        d.contrib[c, 2 * p + 1] = both[GLA_DK:, GLA_DV:]


def _gla_scan(d, n_chunks):
    order = range(n_chunks - 1, -1, -1) if d.reverse else range(n_chunks)
    pad = jnp.zeros((LANES - n_chunks, GLA_KW), F32)
    dec_t = jnp.concatenate([d.dec[c] for c in range(n_chunks)] + [pad], axis=0).T
    for h in range(GLA_HEADS):
        st = d.state[h]
        for c in order:
            d.prev[c, h] = st.astype(BF16)
            st = st * dec_t[h * GLA_DK:(h + 1) * GLA_DK, c:c + 1] + d.contrib[c, h]
        d.state[h] = st


def _gla_outputs(d, c):
    C = GLA_CHUNK
    rows = slice(c * C, (c + 1) * C)
    for h in range(GLA_HEADS):
        p, parity = divmod(h, 2)
        qd_p = d.qd[rows, p * LANES:(p + 1) * LANES]
        scores = d.a[c, parity, p * C:(p + 1) * C, :]
        lhs = jnp.where(_gla_pair_lanes(h), qd_p, scores)
        cols = slice(h * GLA_DV, (h + 1) * GLA_DV)
        operands = [d.prev[c, h], d.v[rows, cols]]
        rhs = jnp.concatenate(operands if parity == 0 else operands[::-1], axis=0)
        d.o[rows, cols] = _dot(lhs, rhs).astype(d.o.dtype)


def _gla_kernel(gf_ref, vf_ref, decf_ref, gb_ref, vb_ref, decb_ref,
                of_ref, ob_ref, sf_ref, sb_ref, *scratch, n_chunks):
    @pl.when(pl.program_id(1) == 0)
    def _():
        sf_ref[...] = jnp.zeros_like(sf_ref)
        sb_ref[...] = jnp.zeros_like(sb_ref)

    n = len(scratch) // 2
    dirs = (_GlaDir(False, gf_ref, vf_ref, decf_ref, of_ref, sf_ref, *scratch[:n]),
            _GlaDir(True, gb_ref, vb_ref, decb_ref, ob_ref, sb_ref, *scratch[n:]))

    for c in range(n_chunks):
        for d in dirs:
            _gla_intra(d, c)
    for d in dirs:
        _gla_scan(d, n_chunks)
    for c in range(n_chunks):
        for d in dirs:
            _gla_outputs(d, c)


def _gla(gf, gb, vg, dec, *, tb):
    B, T, vw = vg.shape
    nb = T // tb
    nc = tb // GLA_CHUNK
    kw = GLA_KW
    n_pairs = GLA_HEADS // 2
    fwd = lambda w: pl.BlockSpec((None, tb, w), lambda b, s: (b, s, 0))
    bwd = lambda w: pl.BlockSpec((None, tb, w), lambda b, s: (b, nb - 1 - s, 0))
    dec_fwd = pl.BlockSpec((None, nc, 1, kw), lambda b, s: (b, s, 0, 0))
    dec_bwd = pl.BlockSpec((None, nc, 1, kw), lambda b, s: (b, nb - 1 - s, 0, 1))
    state = pltpu.VMEM((GLA_HEADS, GLA_DK, GLA_DV), F32)
    per_dir = [
        pltpu.VMEM((nc, 2, n_pairs * GLA_CHUNK, 2 * GLA_CHUNK), BF16),
        pltpu.VMEM((nc, GLA_HEADS, GLA_DK, GLA_DV), F32),
        pltpu.VMEM((nc, GLA_HEADS, GLA_DK, GLA_DV), BF16)]
    return pl.pallas_call(
        functools.partial(_gla_kernel, n_chunks=nc),
        grid=(B, nb),
        in_specs=[fwd(3 * kw), fwd(vw), dec_fwd, bwd(3 * kw), bwd(vw), dec_bwd],
        out_specs=[fwd(vw), bwd(vw)],
        out_shape=[jax.ShapeDtypeStruct((B, T, vw), BF16)] * 2,
        scratch_shapes=[state, state] + per_dir * 2,
        compiler_params=pltpu.CompilerParams(
            dimension_semantics=("parallel", "arbitrary"), vmem_limit_bytes=VMEM_LIMIT),
        name="gla",
    )(gf, vg, dec, gb, vg, dec)


STAGE_ROWS, STAGE_COLS = 512, 1024


class _WeightStager:
    def __init__(self, tiles, stage, sem):
        self.tiles, self.stage, self.sem, self.done = tiles, stage, sem, 0
        for i in range(min(2, len(tiles))):
            self._copy(i).start()

    def _copy(self, i):
        src, _, r, c = self.tiles[i]
        return pltpu.make_async_copy(src.at[pl.ds(r, STAGE_ROWS), pl.ds(c, STAGE_COLS)],
                                     self.stage.at[i % 2], self.sem.at[i % 2])

    def need(self, count):
        while self.done < count:
            i = self.done
            _, dst, r, c = self.tiles[i]
            self._copy(i).wait()
            dst[r:r + STAGE_ROWS, c:c + STAGE_COLS] = self.stage[i % 2].astype(BF16)
            if i + 2 < len(self.tiles):
                self._copy(i + 2).start()
            self.done += 1


def _mix_ffn_body(x_ref, yna_ref, of_ref, ob_ref, sr_ref, gng_ref, gff_ref, gfin_ref, out_ref,
                  wout_ref, w1_ref, w2_ref, ready, *, final_norm, ff_chunk):
    o = of_ref[...].astype(F32) + ob_ref[...].astype(F32)
    parts = [_rms(o[:, h * GLA_DV:(h + 1) * GLA_DV], gng_ref[...]) for h in range(GLA_HEADS)]
    y_gla = (jnp.concatenate(parts, axis=-1) * sr_ref[...].astype(F32)).astype(BF16)
    na_w = yna_ref.shape[-1]
    tiles_per = lambda ref: (ref.shape[0] // STAGE_ROWS) * (ref.shape[1] // STAGE_COLS)
    n_tiles = tiles_per(wout_ref)
    ready(n_tiles)
    mix = _dot(yna_ref[...], wout_ref[0:na_w, :]) + _dot(y_gla, wout_ref[na_w:, :])
    h1 = x_ref[...] + mix
    n2 = _rms(h1, gff_ref[...]).astype(BF16)
    ffn = None
    n_chunks = w1_ref.shape[1] // ff_chunk
    for f in range(n_chunks):
        n_tiles += tiles_per(w1_ref) // n_chunks
        ready(n_tiles)
        u = _dot(n2, w1_ref[:, f * ff_chunk:(f + 1) * ff_chunk])
        a = jnp.square(jnp.maximum(u, 0.0)).astype(BF16)
        n_tiles += tiles_per(w2_ref) // n_chunks
        ready(n_tiles)
        d = _dot(a, w2_ref[f * ff_chunk:(f + 1) * ff_chunk, :])
        ffn = d if ffn is None else ffn + d
    h2 = h1 + ffn
    out_ref[...] = _rms(h2, gfin_ref[...]) if final_norm else h2


def _mix_ffn_kernel(x_ref, yna_ref, of_ref, ob_ref, sr_ref, gng_ref, wout_hbm, gff_ref,
                    w1_hbm, w2_hbm, gfin_ref, out_ref, wout_ref, w1_ref, w2_ref, stage, sem,
                    *, final_norm, ff_chunk):
    body = functools.partial(
        _mix_ffn_body, x_ref, yna_ref, of_ref, ob_ref, sr_ref, gng_ref, gff_ref, gfin_ref, out_ref,
        wout_ref, w1_ref, w2_ref, final_norm=final_norm, ff_chunk=ff_chunk)
    first = (pl.program_id(0) == 0) & (pl.program_id(1) == 0)

    @pl.when(first)
    def _():
        def tiles(src, dst, rows, cols):
            return [(src, dst, r, c) for r in range(rows.start, rows.stop, STAGE_ROWS)
                    for c in range(cols.start, cols.stop, STAGE_COLS)]

        order = tiles(wout_hbm, wout_ref, slice(0, wout_ref.shape[0]), slice(0, wout_ref.shape[1]))
        for f in range(w1_ref.shape[1] // ff_chunk):
            chunk = slice(f * ff_chunk, (f + 1) * ff_chunk)
            order += tiles(w1_hbm, w1_ref, slice(0, w1_ref.shape[0]), chunk)
            order += tiles(w2_hbm, w2_ref, chunk, slice(0, w2_ref.shape[1]))
        stager = _WeightStager(order, stage, sem)
        body(stager.need)
        stager.need(len(order))

    @pl.when(jnp.logical_not(first))
    def _():
        body(lambda n: None)


def _mix_ffn(x, yna, of, ob, sr, gng, wout, gff, w1, w2, gfin, *, tm, final_norm):
    B, T, D = x.shape
    tok = lambda w: pl.BlockSpec((None, tm, w), lambda b, i: (b, i, 0))
    const = lambda a: pl.BlockSpec(a.shape, lambda b, i: (0,) * a.ndim,
                                   pipeline_mode=pl.Buffered(1))
    hbm = pl.BlockSpec(memory_space=pl.ANY)
    for w in (wout, w1, w2):
        assert w.dtype == F32 and w.shape[0] % STAGE_ROWS == 0 and w.shape[1] % STAGE_COLS == 0
    assert w1.shape[1] % 1024 == 0 and 1024 % STAGE_ROWS == 0 and 1024 % STAGE_COLS == 0
    return pl.pallas_call(
        functools.partial(_mix_ffn_kernel, final_norm=final_norm, ff_chunk=1024),
        grid=(B, T // tm),
        in_specs=[tok(D), tok(yna.shape[-1]), tok(of.shape[-1]), tok(ob.shape[-1]),
                  tok(sr.shape[-1]), const(gng), hbm, const(gff), hbm, hbm, const(gfin)],
        out_specs=tok(D),
        out_shape=jax.ShapeDtypeStruct((B, T, D), F32),
        scratch_shapes=[pltpu.VMEM(w.shape, BF16) for w in (wout, w1, w2)] + [
            pltpu.VMEM((2, STAGE_ROWS, STAGE_COLS), F32), pltpu.SemaphoreType.DMA((2,))],
        compiler_params=pltpu.CompilerParams(
            dimension_semantics=("arbitrary", "arbitrary"), vmem_limit_bytes=VMEM_LIMIT),
        name="mix_ffn",
    )(x, yna, of, ob, sr, gng, wout, gff, w1, w2, gfin)


def kernel(x, ln_mix_g, w_in, na_rpb, gla_gate_up_fwd, gla_gate_bias_fwd, gla_gate_up_bwd,
           gla_gate_bias_bwd, gla_norm_g, w_out, ln_ff_g, w_ff1, w_ff2, ln_final_g):
    B, T, D = x.shape
    depth = w_in.shape[0]
    assert T % TOKEN_TILE == 0 and T // NA_GROUP >= 4
    gla_block = GLA_BLOCK if T % GLA_BLOCK == 0 else TOKEN_TILE
    row = lambda v: v.reshape(1, -1).astype(F32)
    zeros = jnp.zeros((GLA_GATE_RANK, GLA_KW), F32)
    h = x
    for l in range(depth):
        gu = jnp.concatenate([
            jnp.concatenate([gla_gate_up_fwd[l], zeros], axis=1),
            jnp.concatenate([zeros, gla_gate_up_bwd[l]], axis=1)], axis=0).astype(BF16)
        gb = jnp.concatenate([gla_gate_bias_fwd[l], gla_gate_bias_bwd[l]]).reshape(1, -1)
        qa, kat, va, vg, sr, gla_f, gla_b, dec = _inproj(
            h, row(ln_mix_g[l]), w_in[l].T, gu, gb.astype(F32), tm=TOKEN_TILE)
        y_na = _na(qa, kat, va, _na_bias_rows(na_rpb[l]))
        o_f, o_b = _gla(gla_f, gla_b, vg, dec, tb=gla_block)
        h = _mix_ffn(h, y_na, o_f, o_b, sr, row(gla_norm_g[l]), w_out[l], row(ln_ff_g[l]),
                     w_ff1[l], w_ff2[l], row(ln_final_g), tm=TOKEN_TILE,
                     final_norm=(l == depth - 1))
    return h
```

```python
import functools

import jax
import jax.numpy as jnp
from jax import lax
from jax.experimental import pallas as pl
from jax.experimental.pallas import tpu as pltpu

F32 = jnp.float32
BF16 = jnp.bfloat16

EPS = 1e-6
GRID_W = 64
NA_HEADS = 8
NA_HEAD_DIM = 64
NA_KH = 8
NA_KW = 16
GLA_HEADS = 4
GLA_DK = 64
GLA_DV = 128
GLA_GATE_RANK = 16
GLA_GATE_NORM = 16.0
GLA_CHUNK = 64

NA_WIDTH = NA_HEADS * NA_HEAD_DIM
GLA_KW = GLA_HEADS * GLA_DK
GLA_VW = GLA_HEADS * GLA_DV
COLS, _start = {}, 0
for _name, _width in (("qa", NA_WIDTH), ("ka", NA_WIDTH), ("va", NA_WIDTH), ("qg", GLA_KW),
                      ("kg", GLA_KW), ("vg", GLA_VW), ("rg", GLA_VW), ("z", 2 * GLA_GATE_RANK)):
    COLS[_name] = slice(_start, _start + _width)
    _start += _width
D_IN = _start

LANES = 128
TOKEN_TILE = 512
GLA_BLOCK = 1024
NA_GROUP_ROWS = 4
NA_GROUP = NA_GROUP_ROWS * GRID_W
NA_KEY_ROWS = 12
NA_KEYS = NA_KEY_ROWS * GRID_W
NA_UNROLL = 4
LOG2E = 1.4426950408889634
VMEM_LIMIT = 56 * 1024 * 1024


def _dot(a, b):
    return jnp.dot(a, b, preferred_element_type=F32)


def _dot_nt(a, b):
    return lax.dot_general(a, b, (((1,), (1,)), ((), ())), preferred_element_type=F32)


def _dot_tn(a, b):
    return lax.dot_general(a, b, (((0,), (0,)), ((), ())), preferred_element_type=F32)


def _rms(x, g):
    return x * lax.rsqrt(jnp.mean(x * x, axis=-1, keepdims=True) + EPS) * g


def _segmented_cumsum(x, reverse):
    n = x.shape[0]
    pos = lax.broadcasted_iota(jnp.int32, x.shape, 0) & (GLA_CHUNK - 1)
    step = 1
    while step < GLA_CHUNK:
        if reverse:
            shifted, ok = pltpu.roll(x, n - step, 0), pos < GLA_CHUNK - step
        else:
            shifted, ok = pltpu.roll(x, step, 0), pos >= step
        x = x + jnp.where(ok, shifted, 0.0)
        step *= 2
    return x


def _gla_cumdecay(log_a, reverse):
    b = _segmented_cumsum(log_a, reverse)
    b3 = b.reshape(-1, GLA_CHUNK, b.shape[-1])
    last = 0 if reverse else GLA_CHUNK - 1
    return b, b3[:, last:last + 1, :]


def _gla_operands(q, k, b, b_last):
    b3 = b.reshape(-1, GLA_CHUNK, b.shape[-1])
    k_dec = (k.reshape(b3.shape) * jnp.exp2(b_last - b3)).reshape(b.shape)
    ops = jnp.concatenate([q * jnp.exp2(b), k * jnp.exp2(-b), k_dec], axis=-1)
    return ops.astype(BF16), jnp.exp2(b_last)


def _inproj_kernel(x_ref, g_ref, wt_ref, gu_ref, gb_ref,
                   qa_ref, kat_ref, va_ref, vg_ref, sr_ref, qg_ref, kg_ref, la_ref, wb_ref):
    @pl.when((pl.program_id(0) == 0) & (pl.program_id(1) == 0))
    def _():
        wb_ref[...] = wt_ref[...].astype(BF16)

    def proj(name):
        return _dot_nt(n, wb_ref[COLS[name], :])

    n = _rms(x_ref[...], g_ref[...]).astype(BF16)
    z = proj("z").astype(BF16)
    pre = _dot(z, gu_ref[...]) + gb_ref[...]
    softplus2 = jnp.log2(1.0 + jnp.exp2(jnp.abs(pre) * -LOG2E))
    la_ref[...] = jnp.minimum(pre * (LOG2E / GLA_GATE_NORM), 0.0) - softplus2 * (1.0 / GLA_GATE_NORM)
    qa_ref[...] = (proj("qa") * (NA_HEAD_DIM ** -0.5 * LOG2E)).astype(BF16)
    kat_ref[...] = _dot_nt(wb_ref[COLS["ka"], :], n).astype(BF16)
    qg_ref[...] = (proj("qg") * (GLA_DK ** -0.5)).astype(BF16)
    kg_ref[...] = proj("kg").astype(BF16)
    va_ref[...] = proj("va").astype(BF16)
    vg_ref[...] = proj("vg").astype(BF16)
    r = proj("rg")
    sr_ref[...] = (r * jax.nn.sigmoid(r)).astype(BF16)


def _inproj(x, g, wt, gu, gb, *, tm):
    B, T, D = x.shape
    assert wt.shape == (D_IN, D)
    tok = lambda w: pl.BlockSpec((None, tm, w), lambda b, i: (b, i, 0))
    const = lambda a: pl.BlockSpec(a.shape, lambda b, i: (0,) * a.ndim,
                                   pipeline_mode=pl.Buffered(1))
    tok_out = lambda w: (jax.ShapeDtypeStruct((B, T, w), BF16), tok(w))
    outs = [
        tok_out(NA_WIDTH),
        (jax.ShapeDtypeStruct((B, NA_WIDTH, T), BF16),
         pl.BlockSpec((None, NA_WIDTH, tm), lambda b, i: (b, 0, i))),
        tok_out(NA_WIDTH),
        tok_out(GLA_VW),
        tok_out(GLA_VW),
        tok_out(GLA_KW),
        tok_out(GLA_KW),
        (jax.ShapeDtypeStruct((B, T, 2 * GLA_KW), F32), tok(2 * GLA_KW)),
    ]
    return pl.pallas_call(
        _inproj_kernel,
        grid=(B, T // tm),
        in_specs=[tok(D), const(g), const(wt), const(gu), const(gb)],
        out_specs=[s for _, s in outs],
        out_shape=[s for s, _ in outs],
        scratch_shapes=[pltpu.VMEM(wt.shape, BF16)],
        compiler_params=pltpu.CompilerParams(
            dimension_semantics=("arbitrary", "arbitrary"), vmem_limit_bytes=VMEM_LIMIT),
        name="inproj",
    )(x, g, wt, gu, gb)


def _na_bias_rows(rpb):
    n_dc = 2 * NA_KW - 1
    padded = jnp.pad(rpb.astype(F32), ((0, 0), (4, 5), (0, GRID_W - n_dc)))
    return jnp.concatenate([padded[:, :-1], padded[:, 1:]], axis=-1)


def _na_tiles(table, g):
    i_start = (0, g, NA_KEY_ROWS - NA_KH)[table]
    return range(i_start // 2, (i_start + NA_KH + 1) // 2)


def _na_build_tables(rows_ref, tab_ref):
    qc = lax.broadcasted_iota(jnp.int32, (GRID_W, LANES), 0)
    lane = lax.broadcasted_iota(jnp.int32, (GRID_W, LANES), 1)
    kc = lane & (GRID_W - 1)
    col_start = jnp.clip(qc - NA_KW // 2, 0, GRID_W - NA_KW)
    in_win = (kc >= col_start) & (kc < col_start + NA_KW)
    valid = {(True, True): in_win,
             (True, False): in_win & (lane < GRID_W),
             (False, True): in_win & (lane >= GRID_W)}
    row_offset = (0, -(NA_KH // 2), -NA_KH)
    for hh in range(2):
        for t in range(3):
            for g in range(NA_GROUP_ROWS):
                i_start = (0, g, NA_KEY_ROWS - NA_KH)[t]
                for m in _na_tiles(t, g):
                    halves = tuple(i_start <= i < i_start + NA_KH for i in (2 * m, 2 * m + 1))
                    a = 2 * m - g + row_offset[t] + NA_KH - 1 + 4
                    src = jnp.broadcast_to(rows_ref[hh, a:a + 1, :], (GRID_W, LANES))
                    rot = pltpu.roll(src, LANES - (NA_KW - 1), 1, stride=1, stride_axis=0)
                    tab_ref[hh, t, g * GRID_W:(g + 1) * GRID_W, m * LANES:(m + 1) * LANES] = (
                        jnp.where(valid[halves], rot * LOG2E, -jnp.inf))


class _NaRefs:
    def __init__(self, q, kt, v, bias, o, n_groups):
        self.q, self.kt, self.v, self.bias, self.o, self.n_groups = q, kt, v, bias, o, n_groups


def _na_rows(u):
    return pl.ds(pl.multiple_of(u * NA_GROUP, NA_GROUP), NA_GROUP)


def _na_key0(r, u):
    return pl.multiple_of(jnp.clip(u - 1, 0, r.n_groups - 3) * NA_GROUP, NA_GROUP)


def _na_scores(r, u):
    qq = r.q[_na_rows(u), :]
    first_head = lax.broadcasted_iota(jnp.int32, qq.shape, 1) < NA_HEAD_DIM
    zero = jnp.zeros_like(qq)
    q_heads = jnp.concatenate([jnp.where(first_head, qq, zero),
                               jnp.where(first_head, zero, qq)], axis=0)
    return _dot(q_heads, r.kt[:, pl.ds(_na_key0(r, u), NA_KEYS)])


def _na_softmax(r, s, table):
    p_tiles, inv_l = [], []
    for hh in range(2):
        for g in range(NA_GROUP_ROWS):
            rows = slice(hh * NA_GROUP + g * GRID_W, hh * NA_GROUP + (g + 1) * GRID_W)
            sb = [s[rows, c * LANES:(c + 1) * LANES]
                  + r.bias[hh, table, g * GRID_W:(g + 1) * GRID_W, c * LANES:(c + 1) * LANES]
                  for c in _na_tiles(table, g)]
            m = jnp.max(functools.reduce(jnp.maximum, sb), axis=-1, keepdims=True)
            p = [jnp.exp2(x - m) for x in sb]
            inv_l.append(1.0 / jnp.sum(functools.reduce(jnp.add, p), axis=-1, keepdims=True))
            p_tiles.append([x.astype(BF16) for x in p])
    return p_tiles, inv_l


def _na_store_output(r, u, o):
    first_head = lax.broadcasted_iota(jnp.int32, (NA_GROUP, LANES), 1) < NA_HEAD_DIM
    r.o[_na_rows(u), :] = jnp.where(first_head, o[:NA_GROUP], o[NA_GROUP:]).astype(r.o.dtype)


def _na_edge_group(r, u, table):
    tiles = _na_tiles(table, 0)
    p_tiles, inv_l = _na_softmax(r, _na_scores(r, u), table)
    p = jnp.concatenate([jnp.concatenate(t, axis=1) for t in p_tiles], axis=0)
    keys = pl.ds(_na_key0(r, u) + tiles.start * LANES, len(tiles) * LANES)
    _na_store_output(r, u, _dot(p, r.v[keys, :]) * jnp.concatenate(inv_l, axis=0))


def _na_interior_probs(r, s):
    p_tiles, inv_l = _na_softmax(r, s, 1)
    zero_tile = jnp.zeros((GRID_W, LANES), BF16)
    p_rows = []
    for block, tiles in enumerate(p_tiles):
        valid = _na_tiles(1, block % NA_GROUP_ROWS)
        p_rows.append(jnp.concatenate(
            [zero_tile] * valid.start + tiles + [zero_tile] * (NA_KEYS // LANES - valid.stop), axis=1))
    return jnp.concatenate(p_rows, axis=0), jnp.concatenate(inv_l, axis=0)


def _na_interior_output(r, u, p, inv_l):
    _na_store_output(r, u, _dot(p, r.v[pl.ds(_na_key0(r, u), NA_KEYS), :]) * inv_l)


def _na_stage(r, u, carry):
    s_cur, p_prev, l_prev = carry
    s_next = _na_scores(r, jnp.minimum(u + 1, r.n_groups - 2))
    p_cur, l_cur = _na_interior_probs(r, s_cur)
    _na_interior_output(r, u - 1, p_prev, l_prev)
    return s_next, p_cur, l_cur


def _na_kernel(q_ref, kt_ref, v_ref, rows_ref, o_ref, bias_ref, s_ref, p_ref, l_ref, *, n_groups):
    @pl.when(pl.program_id(1) == 0)
    def _():
        _na_build_tables(rows_ref, bias_ref)

    r = _NaRefs(q_ref, kt_ref, v_ref, bias_ref, o_ref, n_groups)
    last = n_groups - 1
    _na_edge_group(r, 0, 0)
    _na_edge_group(r, last, 2)

    def load_carry():
        return s_ref[...], p_ref[...], l_ref[...]

    def store_carry(carry):
        s_ref[...], p_ref[...], l_ref[...] = carry

    def stages(first, count, carry):
        for k in range(count):
            carry = _na_stage(r, first + k, carry)
        return carry

    def unrolled_stages(i, c):
        store_carry(stages(2 + NA_UNROLL * i, NA_UNROLL, load_carry()))
        return c

    store_carry((_na_scores(r, 2),) + _na_interior_probs(r, _na_scores(r, 1)))
    n_loops, n_tail = divmod(n_groups - 3, NA_UNROLL)
    lax.fori_loop(0, n_loops, unrolled_stages, 0)
    _, p_prev, l_prev = stages(2 + NA_UNROLL * n_loops, n_tail, load_carry())
    _na_interior_output(r, last - 1, p_prev, l_prev)


def _na(qa, kat, va, rows):
    B, T, _ = qa.shape
    n_pairs = NA_HEADS // 2
    return pl.pallas_call(
        functools.partial(_na_kernel, n_groups=T // NA_GROUP),
        grid=(n_pairs, B),
        in_specs=[
            pl.BlockSpec((None, T, LANES), lambda p, b: (b, 0, p)),
            pl.BlockSpec((None, LANES, T), lambda p, b: (b, p, 0)),
            pl.BlockSpec((None, T, LANES), lambda p, b: (b, 0, p)),
            pl.BlockSpec((2,) + rows.shape[1:], lambda p, b: (p, 0, 0)),
        ],
        out_specs=pl.BlockSpec((None, T, LANES), lambda p, b: (b, 0, p)),
        out_shape=jax.ShapeDtypeStruct((B, T, NA_WIDTH), BF16),
        scratch_shapes=[pltpu.VMEM((2, 3, NA_GROUP, NA_KEYS), F32),
                        pltpu.VMEM((2 * NA_GROUP, NA_KEYS), F32),
                        pltpu.VMEM((2 * NA_GROUP, NA_KEYS), BF16),
                        pltpu.VMEM((2 * NA_GROUP, 1), F32)],
        compiler_params=pltpu.CompilerParams(
            dimension_semantics=("parallel", "arbitrary"), vmem_limit_bytes=VMEM_LIMIT),
        name="natten",
    )(qa, kat, va, rows)


class _GlaDir:
    def __init__(self, reverse, q, k, v, log_a, o, state, a, contrib, prev):
        self.reverse = reverse
        b, b_last = _gla_cumdecay(log_a[...], reverse)
        ops, self.dec = _gla_operands(q[...].astype(F32), k[...].astype(F32), b, b_last)
        self.qd, self.ki, self.kd = (ops[:, i * GLA_KW:(i + 1) * GLA_KW] for i in range(3))
        self.v, self.o, self.state = v, o, state
        self.a, self.contrib, self.prev = a, contrib, prev


def _gla_pair_lanes(h):
    lane = lax.broadcasted_iota(jnp.int32, (GLA_CHUNK, LANES), 1)
    return (lane < GLA_DK) if h % 2 == 0 else (lane >= GLA_DK)


def _gla_intra(d, c):
    C = GLA_CHUNK
    rows = slice(c * C, (c + 1) * C)
    qd, ki = d.qd[rows, :], d.ki[rows, :]
    lane = lax.broadcasted_iota(jnp.int32, qd.shape, 1)
    zeros = jnp.zeros_like(ki)
    i = lax.broadcasted_iota(jnp.int32, (2 * C, LANES), 0) & (C - 1)
    col = lax.broadcasted_iota(jnp.int32, (2 * C, LANES), 1)
    for parity in range(2):
        q_heads = jnp.concatenate(
            [jnp.where((lane >= h * GLA_DK) & (lane < (h + 1) * GLA_DK), qd, jnp.zeros_like(qd))
             for h in range(parity, GLA_HEADS, 2)], axis=0)
        keys = jnp.concatenate([zeros, ki] if parity == 0 else [ki, zeros], axis=0)
        scores = _dot_nt(q_heads, keys)
        j = col - C if parity == 0 else col
        in_chunk = (j >= 0) & (j < C)
        keep = in_chunk & ((j > i) if d.reverse else (j <= i))
        d.a[c, parity] = jnp.where(keep, scores, 0.0).astype(BF16)
    for p in range(GLA_HEADS // 2):
        kd_p = d.kd[rows, p * LANES:(p + 1) * LANES]
        v_p = d.v[rows, 2 * p * GLA_DV:(2 * p + 2) * GLA_DV]
        both = _dot_tn(kd_p, v_p)
        d.contrib[c, 2 * p] = both[:GLA_DK, :GLA_DV]
        d.contrib[c, 2 * p + 1] = both[GLA_DK:, GLA_DV:]


def _gla_scan(d, n_chunks):
    order = range(n_chunks - 1, -1, -1) if d.reverse else range(n_chunks)
    pad = jnp.zeros((LANES - n_chunks, GLA_KW), F32)
    dec_t = jnp.concatenate([d.dec[c] for c in range(n_chunks)] + [pad], axis=0).T
    for h in range(GLA_HEADS):
        st = d.state[h]
        for c in order:
            d.prev[c, h] = st.astype(BF16)
            st = st * dec_t[h * GLA_DK:(h + 1) * GLA_DK, c:c + 1] + d.contrib[c, h]
        d.state[h] = st


def _gla_outputs(d, c):
    C = GLA_CHUNK
    rows = slice(c * C, (c + 1) * C)
    for h in range(GLA_HEADS):
        p, parity = divmod(h, 2)
        qd_p = d.qd[rows, p * LANES:(p + 1) * LANES]
        scores = d.a[c, parity, p * C:(p + 1) * C, :]
        lhs = jnp.where(_gla_pair_lanes(h), qd_p, scores)
        cols = slice(h * GLA_DV, (h + 1) * GLA_DV)
        operands = [d.prev[c, h], d.v[rows, cols]]
        rhs = jnp.concatenate(operands if parity == 0 else operands[::-1], axis=0)
        d.o[rows, cols] = _dot(lhs, rhs).astype(d.o.dtype)


def _gla_kernel(qf_ref, kf_ref, vf_ref, laf_ref, qb_ref, kb_ref, vb_ref, lab_ref,
                of_ref, ob_ref, sf_ref, sb_ref, *scratch, n_chunks):
    @pl.when(pl.program_id(1) == 0)
    def _():
        sf_ref[...] = jnp.zeros_like(sf_ref)
        sb_ref[...] = jnp.zeros_like(sb_ref)

    n = len(scratch) // 2
    dirs = (_GlaDir(False, qf_ref, kf_ref, vf_ref, laf_ref, of_ref, sf_ref, *scratch[:n]),
            _GlaDir(True, qb_ref, kb_ref, vb_ref, lab_ref, ob_ref, sb_ref, *scratch[n:]))

    for c in range(n_chunks):
        for d in dirs:
            _gla_intra(d, c)
    for d in dirs:
        _gla_scan(d, n_chunks)
    for c in range(n_chunks):
        for d in dirs:
            _gla_outputs(d, c)


def _gla(qg, kg, vg, la, *, tb):
    B, T, vw = vg.shape
    nb = T // tb
    nc = tb // GLA_CHUNK
    kw = GLA_KW
    n_pairs = GLA_HEADS // 2
    fwd = lambda w, blk=0: pl.BlockSpec((None, tb, w), lambda b, s: (b, s, blk))
    bwd = lambda w, blk=0: pl.BlockSpec((None, tb, w), lambda b, s: (b, nb - 1 - s, blk))
    state = pltpu.VMEM((GLA_HEADS, GLA_DK, GLA_DV), F32)
    per_dir = [
        pltpu.VMEM((nc, 2, n_pairs * GLA_CHUNK, 2 * GLA_CHUNK), BF16),
        pltpu.VMEM((nc, GLA_HEADS, GLA_DK, GLA_DV), F32),
        pltpu.VMEM((nc, GLA_HEADS, GLA_DK, GLA_DV), BF16)]
    return pl.pallas_call(
        functools.partial(_gla_kernel, n_chunks=nc),
        grid=(B, nb),
        in_specs=[fwd(kw), fwd(kw), fwd(vw), fwd(kw, 0), bwd(kw), bwd(kw), bwd(vw), bwd(kw, 1)],
        out_specs=[fwd(vw), bwd(vw)],
        out_shape=[jax.ShapeDtypeStruct((B, T, vw), BF16)] * 2,
        scratch_shapes=[state, state] + per_dir * 2,
        compiler_params=pltpu.CompilerParams(
            dimension_semantics=("parallel", "arbitrary"), vmem_limit_bytes=VMEM_LIMIT),
        name="gla",
    )(qg, kg, vg, la, qg, kg, vg, la)


STAGE_ROWS, STAGE_COLS = 512, 1024


class _WeightStager:
    def __init__(self, tiles, stage, sem):
        self.tiles, self.stage, self.sem, self.done = tiles, stage, sem, 0
        for i in range(min(2, len(tiles))):
            self._copy(i).start()

    def _copy(self, i):
        src, _, r, c = self.tiles[i]
        return pltpu.make_async_copy(src.at[pl.ds(r, STAGE_ROWS), pl.ds(c, STAGE_COLS)],
                                     self.stage.at[i % 2], self.sem.at[i % 2])

    def need(self, count):
        while self.done < count:
            i = self.done
            _, dst, r, c = self.tiles[i]
            self._copy(i).wait()
            dst[r:r + STAGE_ROWS, c:c + STAGE_COLS] = self.stage[i % 2].astype(BF16)
            if i + 2 < len(self.tiles):
                self._copy(i + 2).start()
            self.done += 1


def _mix_ffn_body(x_ref, yna_ref, of_ref, ob_ref, sr_ref, gng_ref, gff_ref, gfin_ref, out_ref,
                  wout_ref, w1_ref, w2_ref, ready, *, final_norm, ff_chunk):
    o = of_ref[...].astype(F32) + ob_ref[...].astype(F32)
    parts = [_rms(o[:, h * GLA_DV:(h + 1) * GLA_DV], gng_ref[...]) for h in range(GLA_HEADS)]
    y_gla = (jnp.concatenate(parts, axis=-1) * sr_ref[...].astype(F32)).astype(BF16)
    na_w = yna_ref.shape[-1]
    tiles_per = lambda ref: (ref.shape[0] // STAGE_ROWS) * (ref.shape[1] // STAGE_COLS)
    n_tiles = tiles_per(wout_ref)
    ready(n_tiles)
    mix = _dot(yna_ref[...], wout_ref[0:na_w, :]) + _dot(y_gla, wout_ref[na_w:, :])
    h1 = x_ref[...] + mix
    n2 = _rms(h1, gff_ref[...]).astype(BF16)
    ffn = None
    n_chunks = w1_ref.shape[1] // ff_chunk
    for f in range(n_chunks):
        n_tiles += tiles_per(w1_ref) // n_chunks
        ready(n_tiles)
        u = _dot(n2, w1_ref[:, f * ff_chunk:(f + 1) * ff_chunk])
        a = jnp.square(jnp.maximum(u, 0.0)).astype(BF16)
        n_tiles += tiles_per(w2_ref) // n_chunks
        ready(n_tiles)
        d = _dot(a, w2_ref[f * ff_chunk:(f + 1) * ff_chunk, :])
        ffn = d if ffn is None else ffn + d
    h2 = h1 + ffn
    out_ref[...] = _rms(h2, gfin_ref[...]) if final_norm else h2


def _mix_ffn_kernel(x_ref, yna_ref, of_ref, ob_ref, sr_ref, gng_ref, wout_hbm, gff_ref,
                    w1_hbm, w2_hbm, gfin_ref, out_ref, wout_ref, w1_ref, w2_ref, stage, sem,
                    *, final_norm, ff_chunk):
    body = functools.partial(
        _mix_ffn_body, x_ref, yna_ref, of_ref, ob_ref, sr_ref, gng_ref, gff_ref, gfin_ref, out_ref,
        wout_ref, w1_ref, w2_ref, final_norm=final_norm, ff_chunk=ff_chunk)
    first = (pl.program_id(0) == 0) & (pl.program_id(1) == 0)

    @pl.when(first)
    def _():
        def tiles(src, dst, rows, cols):
            return [(src, dst, r, c) for r in range(rows.start, rows.stop, STAGE_ROWS)
                    for c in range(cols.start, cols.stop, STAGE_COLS)]

        order = tiles(wout_hbm, wout_ref, slice(0, wout_ref.shape[0]), slice(0, wout_ref.shape[1]))
        for f in range(w1_ref.shape[1] // ff_chunk):
            chunk = slice(f * ff_chunk, (f + 1) * ff_chunk)
            order += tiles(w1_hbm, w1_ref, slice(0, w1_ref.shape[0]), chunk)
            order += tiles(w2_hbm, w2_ref, chunk, slice(0, w2_ref.shape[1]))
        stager = _WeightStager(order, stage, sem)
        body(stager.need)
        stager.need(len(order))

    @pl.when(jnp.logical_not(first))
    def _():
        body(lambda n: None)


def _mix_ffn(x, yna, of, ob, sr, gng, wout, gff, w1, w2, gfin, *, tm, final_norm):
    B, T, D = x.shape
    tok = lambda w: pl.BlockSpec((None, tm, w), lambda b, i: (b, i, 0))
    const = lambda a: pl.BlockSpec(a.shape, lambda b, i: (0,) * a.ndim,
                                   pipeline_mode=pl.Buffered(1))
    hbm = pl.BlockSpec(memory_space=pl.ANY)
    for w in (wout, w1, w2):
        assert w.dtype == F32 and w.shape[0] % STAGE_ROWS == 0 and w.shape[1] % STAGE_COLS == 0
    assert w1.shape[1] % 1024 == 0 and 1024 % STAGE_ROWS == 0 and 1024 % STAGE_COLS == 0
    return pl.pallas_call(
        functools.partial(_mix_ffn_kernel, final_norm=final_norm, ff_chunk=1024),
        grid=(B, T // tm),
        in_specs=[tok(D), tok(yna.shape[-1]), tok(of.shape[-1]), tok(ob.shape[-1]),
                  tok(sr.shape[-1]), const(gng), hbm, const(gff), hbm, hbm, const(gfin)],
        out_specs=tok(D),
        out_shape=jax.ShapeDtypeStruct((B, T, D), F32),
        scratch_shapes=[pltpu.VMEM(w.shape, BF16) for w in (wout, w1, w2)] + [
            pltpu.VMEM((2, STAGE_ROWS, STAGE_COLS), F32), pltpu.SemaphoreType.DMA((2,))],
        compiler_params=pltpu.CompilerParams(
            dimension_semantics=("arbitrary", "arbitrary"), vmem_limit_bytes=VMEM_LIMIT),
        name="mix_ffn",
    )(x, yna, of, ob, sr, gng, wout, gff, w1, w2, gfin)


def kernel(x, ln_mix_g, w_in, na_rpb, gla_gate_up_fwd, gla_gate_bias_fwd, gla_gate_up_bwd,
           gla_gate_bias_bwd, gla_norm_g, w_out, ln_ff_g, w_ff1, w_ff2, ln_final_g):
    B, T, D = x.shape
    depth = w_in.shape[0]
    assert T % TOKEN_TILE == 0 and T // NA_GROUP >= 4
    gla_block = GLA_BLOCK if T % GLA_BLOCK == 0 else TOKEN_TILE
    row = lambda v: v.reshape(1, -1).astype(F32)
    zeros = jnp.zeros((GLA_GATE_RANK, GLA_KW), F32)
    h = x
    for l in range(depth):
        gu = jnp.concatenate([
            jnp.concatenate([gla_gate_up_fwd[l], zeros], axis=1),
            jnp.concatenate([zeros, gla_gate_up_bwd[l]], axis=1)], axis=0).astype(BF16)
        gb = jnp.concatenate([gla_gate_bias_fwd[l], gla_gate_bias_bwd[l]]).reshape(1, -1)
        qa, kat, va, vg, sr, qg, kg, la = _inproj(
            h, row(ln_mix_g[l]), w_in[l].T, gu, gb.astype(F32), tm=TOKEN_TILE)
        y_na = _na(qa, kat, va, _na_bias_rows(na_rpb[l]))
        o_f, o_b = _gla(qg, kg, vg, la, tb=gla_block)
        h = _mix_ffn(h, y_na, o_f, o_b, sr, row(gla_norm_g[l]), w_out[l], row(ln_ff_g[l]),
                     w_ff1[l], w_ff2[l], row(ln_final_g), tm=TOKEN_TILE,
                     final_norm=(l == depth - 1))
    return h
```

```python
import functools

import jax
import jax.numpy as jnp
from jax import lax
from jax.experimental import pallas as pl
from jax.experimental.pallas import tpu as pltpu

F32 = jnp.float32
BF16 = jnp.bfloat16

EPS = 1e-6
GRID_W = 64
NA_HEADS = 8
NA_HEAD_DIM = 64
NA_KH = 8
NA_KW = 16
GLA_HEADS = 4
GLA_DK = 64
GLA_DV = 128
GLA_GATE_RANK = 16
GLA_GATE_NORM = 16.0
GLA_CHUNK = 64

NA_WIDTH = NA_HEADS * NA_HEAD_DIM
GLA_KW = GLA_HEADS * GLA_DK
GLA_VW = GLA_HEADS * GLA_DV
COLS, _start = {}, 0
for _name, _width in (("qa", NA_WIDTH), ("ka", NA_WIDTH), ("va", NA_WIDTH), ("qg", GLA_KW),
                      ("kg", GLA_KW), ("vg", GLA_VW), ("rg", GLA_VW), ("z", 2 * GLA_GATE_RANK)):
    COLS[_name] = slice(_start, _start + _width)
    _start += _width
D_IN = _start

LANES = 128
TOKEN_TILE = 512
GLA_BLOCK = 1024
NA_GROUP_ROWS = 4
NA_GROUP = NA_GROUP_ROWS * GRID_W
NA_KEY_ROWS = 12
NA_KEYS = NA_KEY_ROWS * GRID_W
NA_UNROLL = 4
LOG2E = 1.4426950408889634
VMEM_LIMIT = 60 * 1024 * 1024


def _dot(a, b):
    return jnp.dot(a, b, preferred_element_type=F32)


def _dot_nt(a, b):
    return lax.dot_general(a, b, (((1,), (1,)), ((), ())), preferred_element_type=F32)


def _dot_tn(a, b):
    return lax.dot_general(a, b, (((0,), (0,)), ((), ())), preferred_element_type=F32)


def _rms(x, g):
    return x * lax.rsqrt(jnp.mean(x * x, axis=-1, keepdims=True) + EPS) * g


def _segmented_cumsum(x, reverse):
    n = x.shape[0]
    pos = lax.broadcasted_iota(jnp.int32, x.shape, 0) & (GLA_CHUNK - 1)
    step = 1
    while step < GLA_CHUNK:
        if reverse:
            shifted, ok = pltpu.roll(x, n - step, 0), pos < GLA_CHUNK - step
        else:
            shifted, ok = pltpu.roll(x, step, 0), pos >= step
        x = x + jnp.where(ok, shifted, 0.0)
        step *= 2
    return x


def _gla_cumdecay(log_a, reverse):
    b = _segmented_cumsum(log_a, reverse)
    b3 = b.reshape(-1, GLA_CHUNK, b.shape[-1])
    last = 0 if reverse else GLA_CHUNK - 1
    return b, b3[:, last:last + 1, :]


def _gla_operands(q, k, b, b_last):
    b3 = b.reshape(-1, GLA_CHUNK, b.shape[-1])
    k_dec = (k.reshape(b3.shape) * jnp.exp2(b_last - b3)).reshape(b.shape)
    ops = jnp.concatenate([q * jnp.exp2(b), k * jnp.exp2(-b), k_dec], axis=-1)
    return ops.astype(BF16), jnp.exp2(b_last)


def _inproj_kernel(x_ref, g_ref, wt_ref, gu_ref, gb_ref,
                   qa_ref, kat_ref, va_ref, vg_ref, sr_ref, gf_ref, gbw_ref, dec_ref, wb_ref):
    @pl.when((pl.program_id(0) == 0) & (pl.program_id(1) == 0))
    def _():
        wb_ref[...] = wt_ref[...].astype(BF16)

    def proj(name):
        return _dot_nt(n, wb_ref[COLS[name], :])

    n = _rms(x_ref[...], g_ref[...]).astype(BF16)

    def log_decay(cols):
        pre = _dot(z, gu_ref[:, cols]) + gb_ref[:, cols]
        softplus2 = jnp.log2(1.0 + jnp.exp2(jnp.abs(pre) * -LOG2E))
        return jnp.minimum(pre * (LOG2E / GLA_GATE_NORM), 0.0) - softplus2 * (1.0 / GLA_GATE_NORM)

    z = proj("z").astype(BF16)
    b_f, last_f = _gla_cumdecay(log_decay(slice(0, GLA_KW)), False)
    qa_ref[...] = (proj("qa") * (NA_HEAD_DIM ** -0.5 * LOG2E)).astype(BF16)
    b_b, last_b = _gla_cumdecay(log_decay(slice(GLA_KW, 2 * GLA_KW)), True)
    kat_ref[...] = _dot_nt(wb_ref[COLS["ka"], :], n).astype(BF16)
    qg = proj("qg") * (GLA_DK ** -0.5)
    kg = proj("kg")
    gf_ref[...], dec_f = _gla_operands(qg, kg, b_f, last_f)
    va_ref[...] = proj("va").astype(BF16)
    gbw_ref[...], dec_b = _gla_operands(qg, kg, b_b, last_b)
    dec_ref[...] = jnp.concatenate([dec_f, dec_b], axis=-1)
    vg_ref[...] = proj("vg").astype(BF16)
    r = proj("rg")
    sr_ref[...] = (r * jax.nn.sigmoid(r)).astype(BF16)


def _inproj(x, g, wt, gu, gb, *, tm):
    B, T, D = x.shape
    assert wt.shape == (D_IN, D)
    nc = tm // GLA_CHUNK
    tok = lambda w: pl.BlockSpec((None, tm, w), lambda b, i: (b, i, 0))
    const = lambda a: pl.BlockSpec(a.shape, lambda b, i: (0,) * a.ndim,
                                   pipeline_mode=pl.Buffered(1))
    tok_out = lambda w: (jax.ShapeDtypeStruct((B, T, w), BF16), tok(w))
    outs = [
        tok_out(NA_WIDTH),
        (jax.ShapeDtypeStruct((B, NA_WIDTH, T), BF16),
         pl.BlockSpec((None, NA_WIDTH, tm), lambda b, i: (b, 0, i))),
        tok_out(NA_WIDTH),
        tok_out(GLA_VW),
        tok_out(GLA_VW),
        tok_out(3 * GLA_KW),
        tok_out(3 * GLA_KW),
        (jax.ShapeDtypeStruct((B, T // GLA_CHUNK, 1, 2 * GLA_KW), F32),
         pl.BlockSpec((None, nc, 1, 2 * GLA_KW), lambda b, i: (b, i, 0, 0))),
    ]
    return pl.pallas_call(
        _inproj_kernel,
        grid=(B, T // tm),
        in_specs=[tok(D), const(g), const(wt), const(gu), const(gb)],
        out_specs=[s for _, s in outs],
        out_shape=[s for s, _ in outs],
        scratch_shapes=[pltpu.VMEM(wt.shape, BF16)],
        compiler_params=pltpu.CompilerParams(
            dimension_semantics=("arbitrary", "arbitrary"), vmem_limit_bytes=VMEM_LIMIT),
        name="inproj",
    )(x, g, wt, gu, gb)


def _na_bias_rows(rpb):
    n_dc = 2 * NA_KW - 1
    padded = jnp.pad(rpb.astype(F32), ((0, 0), (4, 5), (0, GRID_W - n_dc)))
    return jnp.concatenate([padded[:, :-1], padded[:, 1:]], axis=-1)


def _na_tiles(table, g):
    i_start = (0, g, NA_KEY_ROWS - NA_KH)[table]
    return range(i_start // 2, (i_start + NA_KH + 1) // 2)


def _na_build_tables(rows_ref, tab_ref):
    qc = lax.broadcasted_iota(jnp.int32, (GRID_W, LANES), 0)
    lane = lax.broadcasted_iota(jnp.int32, (GRID_W, LANES), 1)
    kc = lane & (GRID_W - 1)
    col_start = jnp.clip(qc - NA_KW // 2, 0, GRID_W - NA_KW)
    in_win = (kc >= col_start) & (kc < col_start + NA_KW)
    valid = {(True, True): in_win,
             (True, False): in_win & (lane < GRID_W),
             (False, True): in_win & (lane >= GRID_W)}
    row_offset = (0, -(NA_KH // 2), -NA_KH)
    for hh in range(2):
        for t in range(3):
            for g in range(NA_GROUP_ROWS):
                i_start = (0, g, NA_KEY_ROWS - NA_KH)[t]
                for m in _na_tiles(t, g):
                    halves = tuple(i_start <= i < i_start + NA_KH for i in (2 * m, 2 * m + 1))
                    a = 2 * m - g + row_offset[t] + NA_KH - 1 + 4
                    src = jnp.broadcast_to(rows_ref[hh, a:a + 1, :], (GRID_W, LANES))
                    rot = pltpu.roll(src, LANES - (NA_KW - 1), 1, stride=1, stride_axis=0)
                    tab_ref[hh, t, g * GRID_W:(g + 1) * GRID_W, m * LANES:(m + 1) * LANES] = (
                        jnp.where(valid[halves], rot * LOG2E, -jnp.inf))


class _NaRefs:
    def __init__(self, q, kt, v, bias, o, n_groups):
        self.q, self.kt, self.v, self.bias, self.o, self.n_groups = q, kt, v, bias, o, n_groups


def _na_rows(u):
    return pl.ds(pl.multiple_of(u * NA_GROUP, NA_GROUP), NA_GROUP)


def _na_key0(r, u):
    return pl.multiple_of(jnp.clip(u - 1, 0, r.n_groups - 3) * NA_GROUP, NA_GROUP)


def _na_scores(r, u):
    qq = r.q[_na_rows(u), :]
    first_head = lax.broadcasted_iota(jnp.int32, qq.shape, 1) < NA_HEAD_DIM
    zero = jnp.zeros_like(qq)
    q_heads = jnp.concatenate([jnp.where(first_head, qq, zero),
                               jnp.where(first_head, zero, qq)], axis=0)
    return _dot(q_heads, r.kt[:, pl.ds(_na_key0(r, u), NA_KEYS)])


def _na_softmax(r, s, table):
    p_tiles, inv_l = [], []
    for hh in range(2):
        for g in range(NA_GROUP_ROWS):
            rows = slice(hh * NA_GROUP + g * GRID_W, hh * NA_GROUP + (g + 1) * GRID_W)
            sb = [s[rows, c * LANES:(c + 1) * LANES]
                  + r.bias[hh, table, g * GRID_W:(g + 1) * GRID_W, c * LANES:(c + 1) * LANES]
                  for c in _na_tiles(table, g)]
            m = jnp.max(functools.reduce(jnp.maximum, sb), axis=-1, keepdims=True)
            p = [jnp.exp2(x - m) for x in sb]
            inv_l.append(1.0 / jnp.sum(functools.reduce(jnp.add, p), axis=-1, keepdims=True))
            p_tiles.append([x.astype(BF16) for x in p])
    return p_tiles, inv_l


def _na_store_output(r, u, o):
    first_head = lax.broadcasted_iota(jnp.int32, (NA_GROUP, LANES), 1) < NA_HEAD_DIM
    r.o[_na_rows(u), :] = jnp.where(first_head, o[:NA_GROUP], o[NA_GROUP:]).astype(r.o.dtype)


def _na_edge_group(r, u, table):
    tiles = _na_tiles(table, 0)
    p_tiles, inv_l = _na_softmax(r, _na_scores(r, u), table)
    p = jnp.concatenate([jnp.concatenate(t, axis=1) for t in p_tiles], axis=0)
    keys = pl.ds(_na_key0(r, u) + tiles.start * LANES, len(tiles) * LANES)
    _na_store_output(r, u, _dot(p, r.v[keys, :]) * jnp.concatenate(inv_l, axis=0))


def _na_interior_probs(r, s):
    p_tiles, inv_l = _na_softmax(r, s, 1)
    zero_tile = jnp.zeros((GRID_W, LANES), BF16)
    p_rows = []
    for block, tiles in enumerate(p_tiles):
        valid = _na_tiles(1, block % NA_GROUP_ROWS)
        p_rows.append(jnp.concatenate(
            [zero_tile] * valid.start + tiles + [zero_tile] * (NA_KEYS // LANES - valid.stop), axis=1))
    return jnp.concatenate(p_rows, axis=0), jnp.concatenate(inv_l, axis=0)


def _na_interior_output(r, u, p, inv_l):
    _na_store_output(r, u, _dot(p, r.v[pl.ds(_na_key0(r, u), NA_KEYS), :]) * inv_l)


def _na_stage(r, u, carry):
    s_cur, p_prev, l_prev = carry
    s_next = _na_scores(r, jnp.minimum(u + 1, r.n_groups - 2))
    p_cur, l_cur = _na_interior_probs(r, s_cur)
    _na_interior_output(r, u - 1, p_prev, l_prev)
    return s_next, p_cur, l_cur


def _na_kernel(q_ref, kt_ref, v_ref, rows_ref, o_ref, bias_ref, s_ref, p_ref, l_ref, *, n_groups):
    @pl.when(pl.program_id(1) == 0)
    def _():
        _na_build_tables(rows_ref, bias_ref)

    r = _NaRefs(q_ref, kt_ref, v_ref, bias_ref, o_ref, n_groups)
    last = n_groups - 1
    _na_edge_group(r, 0, 0)
    _na_edge_group(r, last, 2)

    def load_carry():
        return s_ref[...], p_ref[...], l_ref[...]

    def store_carry(carry):
        s_ref[...], p_ref[...], l_ref[...] = carry

    def stages(first, count, carry):
        for k in range(count):
            carry = _na_stage(r, first + k, carry)
        return carry

    def unrolled_stages(i, c):
        store_carry(stages(2 + NA_UNROLL * i, NA_UNROLL, load_carry()))
        return c

    store_carry((_na_scores(r, 2),) + _na_interior_probs(r, _na_scores(r, 1)))
    n_loops, n_tail = divmod(n_groups - 3, NA_UNROLL)
    lax.fori_loop(0, n_loops, unrolled_stages, 0)
    _, p_prev, l_prev = stages(2 + NA_UNROLL * n_loops, n_tail, load_carry())
    _na_interior_output(r, last - 1, p_prev, l_prev)


def _na(qa, kat, va, rows):
    B, T, _ = qa.shape
    n_pairs = NA_HEADS // 2
    return pl.pallas_call(
        functools.partial(_na_kernel, n_groups=T // NA_GROUP),
        grid=(n_pairs, B),
        in_specs=[
            pl.BlockSpec((None, T, LANES), lambda p, b: (b, 0, p)),
            pl.BlockSpec((None, LANES, T), lambda p, b: (b, p, 0)),
            pl.BlockSpec((None, T, LANES), lambda p, b: (b, 0, p)),
            pl.BlockSpec((2,) + rows.shape[1:], lambda p, b: (p, 0, 0)),
        ],
        out_specs=pl.BlockSpec((None, T, LANES), lambda p, b: (b, 0, p)),
        out_shape=jax.ShapeDtypeStruct((B, T, NA_WIDTH), BF16),
        scratch_shapes=[pltpu.VMEM((2, 3, NA_GROUP, NA_KEYS), F32),
                        pltpu.VMEM((2 * NA_GROUP, NA_KEYS), F32),
                        pltpu.VMEM((2 * NA_GROUP, NA_KEYS), BF16),
                        pltpu.VMEM((2 * NA_GROUP, 1), F32)],
        compiler_params=pltpu.CompilerParams(
            dimension_semantics=("parallel", "arbitrary"), vmem_limit_bytes=VMEM_LIMIT),
        name="natten",
    )(qa, kat, va, rows)


class _GlaDir:
    def __init__(self, reverse, ops, v, dec, o, state, a, contrib, prev):
        self.reverse = reverse
        self.qd, self.ki, self.kd = (ops.at[:, i * GLA_KW:(i + 1) * GLA_KW] for i in range(3))
        self.v, self.dec, self.o, self.state = v, dec, o, state
        self.a, self.contrib, self.prev = a, contrib, prev


def _gla_pair_lanes(h):
    lane = lax.broadcasted_iota(jnp.int32, (GLA_CHUNK, LANES), 1)
    return (lane < GLA_DK) if h % 2 == 0 else (lane >= GLA_DK)


def _gla_intra(d, c):
    C = GLA_CHUNK
    rows = slice(c * C, (c + 1) * C)
    qd, ki = d.qd[rows, :], d.ki[rows, :]
    lane = lax.broadcasted_iota(jnp.int32, qd.shape, 1)
    zeros = jnp.zeros_like(ki)
    i = lax.broadcasted_iota(jnp.int32, (2 * C, LANES), 0) & (C - 1)
    col = lax.broadcasted_iota(jnp.int32, (2 * C, LANES), 1)
    for parity in range(2):
        q_heads = jnp.concatenate(
            [jnp.where((lane >= h * GLA_DK) & (lane < (h + 1) * GLA_DK), qd, jnp.zeros_like(qd))
             for h in range(parity, GLA_HEADS, 2)], axis=0)
        keys = jnp.concatenate([zeros, ki] if parity == 0 else [ki, zeros], axis=0)
        scores = _dot_nt(q_heads, keys)
        j = col - C if parity == 0 else col
        in_chunk = (j >= 0) & (j < C)
        keep = in_chunk & ((j > i) if d.reverse else (j <= i))
        d.a[c, parity] = jnp.where(keep, scores, 0.0).astype(BF16)
    for p in range(GLA_HEADS // 2):
        kd_p = d.kd[rows, p * LANES:(p + 1) * LANES]
        v_p = d.v[rows, 2 * p * GLA_DV:(2 * p + 2) * GLA_DV]
        both = _dot_tn(kd_p, v_p)
        d.contrib[c, 2 * p] = both[:GLA_DK, :GLA_DV]
        d.contrib[c, 2 * p + 1] = both[GLA_DK:, GLA_DV:]


def _gla_scan(d, n_chunks):
    order = range(n_chunks - 1, -1, -1) if d.reverse else range(n_chunks)
    pad = jnp.zeros((LANES - n_chunks, GLA_KW), F32)
    dec_t = jnp.concatenate([d.dec[c] for c in range(n_chunks)] + [pad], axis=0).T
    for h in range(GLA_HEADS):
        st = d.state[h]
        for c in order:
            d.prev[c, h] = st.astype(BF16)
            st = st * dec_t[h * GLA_DK:(h + 1) * GLA_DK, c:c + 1] + d.contrib[c, h]
        d.state[h] = st


def _gla_outputs(d, c):
    C = GLA_CHUNK
    rows = slice(c * C, (c + 1) * C)
    for h in range(GLA_HEADS):
        p, parity = divmod(h, 2)
        qd_p = d.qd[rows, p * LANES:(p + 1) * LANES]
        scores = d.a[c, parity, p * C:(p + 1) * C, :]
        lhs = jnp.where(_gla_pair_lanes(h), qd_p, scores)
        cols = slice(h * GLA_DV, (h + 1) * GLA_DV)
        operands = [d.prev[c, h], d.v[rows, cols]]
        rhs = jnp.concatenate(operands if parity == 0 else operands[::-1], axis=0)
        d.o[rows, cols] = _dot(lhs, rhs).astype(d.o.dtype)


def _gla_kernel(gf_ref, vf_ref, decf_ref, gb_ref, vb_ref, decb_ref,
                of_ref, ob_ref, sf_ref, sb_ref, *scratch, n_chunks):
    @pl.when(pl.program_id(1) == 0)
    def _():
        sf_ref[...] = jnp.zeros_like(sf_ref)
        sb_ref[...] = jnp.zeros_like(sb_ref)

    n = len(scratch) // 2
    dirs = (_GlaDir(False, gf_ref, vf_ref, decf_ref, of_ref, sf_ref, *scratch[:n]),
            _GlaDir(True, gb_ref, vb_ref, decb_ref, ob_ref, sb_ref, *scratch[n:]))

    for c in range(n_chunks):
        for d in dirs:
            _gla_intra(d, c)
    for d in dirs:
        _gla_scan(d, n_chunks)
    for c in range(n_chunks):
        for d in dirs:
            _gla_outputs(d, c)


def _gla(gf, gb, vg, dec, *, tb):
    B, T, vw = vg.shape
    nb = T // tb
    nc = tb // GLA_CHUNK
    kw = GLA_KW
    n_pairs = GLA_HEADS // 2
    fwd = lambda w: pl.BlockSpec((None, tb, w), lambda b, s: (b, s, 0))
    bwd = lambda w: pl.BlockSpec((None, tb, w), lambda b, s: (b, nb - 1 - s, 0))
    dec_fwd = pl.BlockSpec((None, nc, 1, kw), lambda b, s: (b, s, 0, 0))
    dec_bwd = pl.BlockSpec((None, nc, 1, kw), lambda b, s: (b, nb - 1 - s, 0, 1))
    state = pltpu.VMEM((GLA_HEADS, GLA_DK, GLA_DV), F32)
    per_dir = [
        pltpu.VMEM((nc, 2, n_pairs * GLA_CHUNK, 2 * GLA_CHUNK), BF16),
        pltpu.VMEM((nc, GLA_HEADS, GLA_DK, GLA_DV), F32),
        pltpu.VMEM((nc, GLA_HEADS, GLA_DK, GLA_DV), BF16)]
    return pl.pallas_call(
        functools.partial(_gla_kernel, n_chunks=nc),
        grid=(B, nb),
        in_specs=[fwd(3 * kw), fwd(vw), dec_fwd, bwd(3 * kw), bwd(vw), dec_bwd],
        out_specs=[fwd(vw), bwd(vw)],
        out_shape=[jax.ShapeDtypeStruct((B, T, vw), BF16)] * 2,
        scratch_shapes=[state, state] + per_dir * 2,
        compiler_params=pltpu.CompilerParams(
            dimension_semantics=("parallel", "arbitrary"), vmem_limit_bytes=VMEM_LIMIT),
        name="gla",
    )(gf, vg, dec, gb, vg, dec)


MIX_SUBTILE = 512
STAGE_ROWS, STAGE_COLS = 512, 1024


class _WeightStager:
    def __init__(self, tiles, stage, sem):
        self.tiles, self.stage, self.sem, self.done = tiles, stage, sem, 0
        for i in range(min(2, len(tiles))):
            self._copy(i).start()

    def _copy(self, i):
        src, _, r, c = self.tiles[i]
        return pltpu.make_async_copy(src.at[pl.ds(r, STAGE_ROWS), pl.ds(c, STAGE_COLS)],
                                     self.stage.at[i % 2], self.sem.at[i % 2])

    def need(self, count):
        while self.done < count:
            i = self.done
            _, dst, r, c = self.tiles[i]
            self._copy(i).wait()
            dst[r:r + STAGE_ROWS, c:c + STAGE_COLS] = self.stage[i % 2].astype(BF16)
            if i + 2 < len(self.tiles):
                self._copy(i + 2).start()
            self.done += 1


def _mix_ffn_body(x_ref, yna_ref, of_ref, ob_ref, sr_ref, gng_ref, gff_ref, gfin_ref, out_ref,
                  wout_ref, w1_ref, w2_ref, ready, *, final_norm, ff_chunk):
    tiles_per = lambda ref: (ref.shape[0] // STAGE_ROWS) * (ref.shape[1] // STAGE_COLS)
    na_w = yna_ref.shape[-1]
    n_chunks = w1_ref.shape[1] // ff_chunk
    for rows in (slice(i, i + MIX_SUBTILE) for i in range(0, x_ref.shape[0], MIX_SUBTILE)):
        o = of_ref[rows, :].astype(F32) + ob_ref[rows, :].astype(F32)
        parts = [_rms(o[:, h * GLA_DV:(h + 1) * GLA_DV], gng_ref[...]) for h in range(GLA_HEADS)]
        y_gla = (jnp.concatenate(parts, axis=-1) * sr_ref[rows, :].astype(F32)).astype(BF16)
        n_tiles = tiles_per(wout_ref)
        ready(n_tiles)
        mix = _dot(yna_ref[rows, :], wout_ref[0:na_w, :]) + _dot(y_gla, wout_ref[na_w:, :])
        h1 = x_ref[rows, :] + mix
        n2 = _rms(h1, gff_ref[...]).astype(BF16)
        ffn = None
        for f in range(n_chunks):
            n_tiles += tiles_per(w1_ref) // n_chunks
            ready(n_tiles)
            u = _dot(n2, w1_ref[:, f * ff_chunk:(f + 1) * ff_chunk])
            a = jnp.square(jnp.maximum(u, 0.0)).astype(BF16)
            n_tiles += tiles_per(w2_ref) // n_chunks
            ready(n_tiles)
            d = _dot(a, w2_ref[f * ff_chunk:(f + 1) * ff_chunk, :])
            ffn = d if ffn is None else ffn + d
        h2 = h1 + ffn
        out_ref[rows, :] = _rms(h2, gfin_ref[...]) if final_norm else h2


def _mix_ffn_kernel(x_ref, yna_ref, of_ref, ob_ref, sr_ref, gng_ref, wout_hbm, gff_ref,
                    w1_hbm, w2_hbm, gfin_ref, out_ref, wout_ref, w1_ref, w2_ref, stage, sem,
                    *, final_norm, ff_chunk):
    body = functools.partial(
        _mix_ffn_body, x_ref, yna_ref, of_ref, ob_ref, sr_ref, gng_ref, gff_ref, gfin_ref, out_ref,
        wout_ref, w1_ref, w2_ref, final_norm=final_norm, ff_chunk=ff_chunk)
    first = (pl.program_id(0) == 0) & (pl.program_id(1) == 0)

    @pl.when(first)
    def _():
        def tiles(src, dst, rows, cols):
            return [(src, dst, r, c) for r in range(rows.start, rows.stop, STAGE_ROWS)
                    for c in range(cols.start, cols.stop, STAGE_COLS)]

        order = tiles(wout_hbm, wout_ref, slice(0, wout_ref.shape[0]), slice(0, wout_ref.shape[1]))
        for f in range(w1_ref.shape[1] // ff_chunk):
            chunk = slice(f * ff_chunk, (f + 1) * ff_chunk)
            order += tiles(w1_hbm, w1_ref, slice(0, w1_ref.shape[0]), chunk)
            order += tiles(w2_hbm, w2_ref, chunk, slice(0, w2_ref.shape[1]))
        stager = _WeightStager(order, stage, sem)
        body(stager.need)
        stager.need(len(order))

    @pl.when(jnp.logical_not(first))
    def _():
        body(lambda n: None)


def _mix_ffn(x, yna, of, ob, sr, gng, wout, gff, w1, w2, gfin, *, tm, final_norm):
    B, T, D = x.shape
    tok = lambda w: pl.BlockSpec((None, tm, w), lambda b, i: (b, i, 0))
    const = lambda a: pl.BlockSpec(a.shape, lambda b, i: (0,) * a.ndim,
                                   pipeline_mode=pl.Buffered(1))
    hbm = pl.BlockSpec(memory_space=pl.ANY)
    for w in (wout, w1, w2):
        assert w.dtype == F32 and w.shape[0] % STAGE_ROWS == 0 and w.shape[1] % STAGE_COLS == 0
    assert w1.shape[1] % 1024 == 0 and 1024 % STAGE_ROWS == 0 and 1024 % STAGE_COLS == 0
    return pl.pallas_call(
        functools.partial(_mix_ffn_kernel, final_norm=final_norm, ff_chunk=1024),
        grid=(B, T // tm),
        in_specs=[tok(D), tok(yna.shape[-1]), tok(of.shape[-1]), tok(ob.shape[-1]),
                  tok(sr.shape[-1]), const(gng), hbm, const(gff), hbm, hbm, const(gfin)],
        out_specs=tok(D),
        out_shape=jax.ShapeDtypeStruct((B, T, D), F32),
        scratch_shapes=[pltpu.VMEM(w.shape, BF16) for w in (wout, w1, w2)] + [
            pltpu.VMEM((2, STAGE_ROWS, STAGE_COLS), F32), pltpu.SemaphoreType.DMA((2,))],
        compiler_params=pltpu.CompilerParams(
            dimension_semantics=("arbitrary", "arbitrary"), vmem_limit_bytes=VMEM_LIMIT),
        name="mix_ffn",
    )(x, yna, of, ob, sr, gng, wout, gff, w1, w2, gfin)


def kernel(x, ln_mix_g, w_in, na_rpb, gla_gate_up_fwd, gla_gate_bias_fwd, gla_gate_up_bwd,
           gla_gate_bias_bwd, gla_norm_g, w_out, ln_ff_g, w_ff1, w_ff2, ln_final_g):
    B, T, D = x.shape
    depth = w_in.shape[0]
    assert T % TOKEN_TILE == 0 and T // NA_GROUP >= 4
    gla_block = GLA_BLOCK if T % GLA_BLOCK == 0 else TOKEN_TILE
    row = lambda v: v.reshape(1, -1).astype(F32)
    zeros = jnp.zeros((GLA_GATE_RANK, GLA_KW), F32)
    h = x
    for l in range(depth):
        gu = jnp.concatenate([
            jnp.concatenate([gla_gate_up_fwd[l], zeros], axis=1),
            jnp.concatenate([zeros, gla_gate_up_bwd[l]], axis=1)], axis=0).astype(BF16)
        gb = jnp.concatenate([gla_gate_bias_fwd[l], gla_gate_bias_bwd[l]]).reshape(1, -1)
        qa, kat, va, vg, sr, gla_f, gla_b, dec = _inproj(
            h, row(ln_mix_g[l]), w_in[l].T, gu, gb.astype(F32), tm=TOKEN_TILE)
        y_na = _na(qa, kat, va, _na_bias_rows(na_rpb[l]))
        o_f, o_b = _gla(gla_f, gla_b, vg, dec, tb=gla_block)
        h = _mix_ffn(h, y_na, o_f, o_b, sr, row(gla_norm_g[l]), w_out[l], row(ln_ff_g[l]),
                     w_ff1[l], w_ff2[l], row(ln_final_g), tm=2 * TOKEN_TILE,
                     final_norm=(l == depth - 1))
    return h
```

```python
import functools

import jax
import jax.numpy as jnp
from jax import lax
from jax.experimental import pallas as pl
from jax.experimental.pallas import tpu as pltpu

F32 = jnp.float32
BF16 = jnp.bfloat16

EPS = 1e-6
GRID_W = 64
NA_HEADS = 8
NA_HEAD_DIM = 64
NA_KH = 8
NA_KW = 16
GLA_HEADS = 4
GLA_DK = 64
GLA_DV = 128
GLA_GATE_RANK = 16
GLA_GATE_NORM = 16.0
GLA_CHUNK = 64

NA_WIDTH = NA_HEADS * NA_HEAD_DIM
GLA_KW = GLA_HEADS * GLA_DK
GLA_VW = GLA_HEADS * GLA_DV
COLS, _start = {}, 0
for _name, _width in (("qa", NA_WIDTH), ("ka", NA_WIDTH), ("va", NA_WIDTH), ("qg", GLA_KW),
                      ("kg", GLA_KW), ("vg", GLA_VW), ("rg", GLA_VW), ("z", 2 * GLA_GATE_RANK)):
    COLS[_name] = slice(_start, _start + _width)
    _start += _width
D_IN = _start

LANES = 128
TOKEN_TILE = 512
GLA_BLOCK = 1024
NA_GROUP_ROWS = 4
NA_GROUP = NA_GROUP_ROWS * GRID_W
NA_KEY_ROWS = 12
NA_KEYS = NA_KEY_ROWS * GRID_W
NA_UNROLL = 4
LOG2E = 1.4426950408889634
VMEM_LIMIT = 56 * 1024 * 1024


def _dot(a, b):
    return jnp.dot(a, b, preferred_element_type=F32)


def _dot_nt(a, b):
    return lax.dot_general(a, b, (((1,), (1,)), ((), ())), preferred_element_type=F32)


def _dot_tn(a, b):
    return lax.dot_general(a, b, (((0,), (0,)), ((), ())), preferred_element_type=F32)


def _rms(x, g):
    return x * lax.rsqrt(jnp.mean(x * x, axis=-1, keepdims=True) + EPS) * g


def _segmented_cumsum(x, reverse):
    n = x.shape[0]
    pos = lax.broadcasted_iota(jnp.int32, x.shape, 0) & (GLA_CHUNK - 1)
    step = 1
    while step < GLA_CHUNK:
        if reverse:
            shifted, ok = pltpu.roll(x, n - step, 0), pos < GLA_CHUNK - step
        else:
            shifted, ok = pltpu.roll(x, step, 0), pos >= step
        x = x + jnp.where(ok, shifted, 0.0)
        step *= 2
    return x


def _gla_cumdecay(log_a, reverse):
    b = _segmented_cumsum(log_a, reverse)
    b3 = b.reshape(-1, GLA_CHUNK, b.shape[-1])
    last = 0 if reverse else GLA_CHUNK - 1
    return b, b3[:, last:last + 1, :]


def _gla_operands(q, k, b, b_last):
    b3 = b.reshape(-1, GLA_CHUNK, b.shape[-1])
    k_dec = (k.reshape(b3.shape) * jnp.exp2(b_last - b3)).reshape(b.shape)
    ops = jnp.concatenate([q * jnp.exp2(b), k * jnp.exp2(-b), k_dec], axis=-1)
    return ops.astype(BF16), jnp.exp2(b_last)


def _inproj_kernel(x_ref, g_ref, wt_ref, gu_ref, gb_ref,
                   qa_ref, kat_ref, va_ref, vg_ref, sr_ref, gf_ref, gbw_ref, dec_ref, wb_ref):
    @pl.when((pl.program_id(0) == 0) & (pl.program_id(1) == 0))
    def _():
        wb_ref[...] = wt_ref[...].astype(BF16)

    def proj(name):
        return _dot_nt(n, wb_ref[COLS[name], :])

    n = _rms(x_ref[...], g_ref[...]).astype(BF16)

    def log_decay(cols):
        pre = _dot(z, gu_ref[:, cols]) + gb_ref[:, cols]
        softplus2 = jnp.log2(1.0 + jnp.exp2(jnp.abs(pre) * -LOG2E))
        return jnp.minimum(pre * (LOG2E / GLA_GATE_NORM), 0.0) - softplus2 * (1.0 / GLA_GATE_NORM)

    z = proj("z").astype(BF16)
    b_f, last_f = _gla_cumdecay(log_decay(slice(0, GLA_KW)), False)
    qa_ref[...] = (proj("qa") * (NA_HEAD_DIM ** -0.5 * LOG2E)).astype(BF16)
    b_b, last_b = _gla_cumdecay(log_decay(slice(GLA_KW, 2 * GLA_KW)), True)
    kat_ref[...] = _dot_nt(wb_ref[COLS["ka"], :], n).astype(BF16)
    qg = proj("qg") * (GLA_DK ** -0.5)
    kg = proj("kg")
    gf_ref[...], dec_f = _gla_operands(qg, kg, b_f, last_f)
    va_ref[...] = proj("va").astype(BF16)
    gbw_ref[...], dec_b = _gla_operands(qg, kg, b_b, last_b)
    dec_ref[...] = jnp.concatenate([dec_f, dec_b], axis=-1)
    vg_ref[...] = proj("vg").astype(BF16)
    r = proj("rg")
    sr_ref[...] = (r * jax.nn.sigmoid(r)).astype(BF16)


def _inproj(x, g, wt, gu, gb, *, tm):
    B, T, D = x.shape
    assert wt.shape == (D_IN, D)
    nc = tm // GLA_CHUNK
    tok = lambda w: pl.BlockSpec((None, tm, w), lambda b, i: (b, i, 0))
    const = lambda a: pl.BlockSpec(a.shape, lambda b, i: (0,) * a.ndim,
                                   pipeline_mode=pl.Buffered(1))
    tok_out = lambda w: (jax.ShapeDtypeStruct((B, T, w), BF16), tok(w))
    outs = [
        tok_out(NA_WIDTH),
        (jax.ShapeDtypeStruct((B, NA_WIDTH, T), BF16),
         pl.BlockSpec((None, NA_WIDTH, tm), lambda b, i: (b, 0, i))),
        tok_out(NA_WIDTH),
        tok_out(GLA_VW),
        tok_out(GLA_VW),
        tok_out(3 * GLA_KW),
        tok_out(3 * GLA_KW),
        (jax.ShapeDtypeStruct((B, T // GLA_CHUNK, 1, 2 * GLA_KW), F32),
         pl.BlockSpec((None, nc, 1, 2 * GLA_KW), lambda b, i: (b, i, 0, 0))),
    ]
    return pl.pallas_call(
        _inproj_kernel,
        grid=(B, T // tm),
        in_specs=[tok(D), const(g), const(wt), const(gu), const(gb)],
        out_specs=[s for _, s in outs],
        out_shape=[s for s, _ in outs],
        scratch_shapes=[pltpu.VMEM(wt.shape, BF16)],
        compiler_params=pltpu.CompilerParams(
            dimension_semantics=("arbitrary", "arbitrary"), vmem_limit_bytes=VMEM_LIMIT),
        name="inproj",
    )(x, g, wt, gu, gb)


def _na_bias_rows(rpb):
    n_dc = 2 * NA_KW - 1
    padded = jnp.pad(rpb.astype(F32), ((0, 0), (4, 5), (0, GRID_W - n_dc)))
    return jnp.concatenate([padded[:, :-1], padded[:, 1:]], axis=-1)


def _na_tiles(table, g):
    i_start = (0, g, NA_KEY_ROWS - NA_KH)[table]
    return range(i_start // 2, (i_start + NA_KH + 1) // 2)


def _na_build_tables(rows_ref, tab_ref):
    qc = lax.broadcasted_iota(jnp.int32, (GRID_W, LANES), 0)
    lane = lax.broadcasted_iota(jnp.int32, (GRID_W, LANES), 1)
    kc = lane & (GRID_W - 1)
    col_start = jnp.clip(qc - NA_KW // 2, 0, GRID_W - NA_KW)
    in_win = (kc >= col_start) & (kc < col_start + NA_KW)
    valid = {(True, True): in_win,
             (True, False): in_win & (lane < GRID_W),
             (False, True): in_win & (lane >= GRID_W)}
    row_offset = (0, -(NA_KH // 2), -NA_KH)
    for hh in range(2):
        for t in range(3):
            for g in range(NA_GROUP_ROWS):
                i_start = (0, g, NA_KEY_ROWS - NA_KH)[t]
                for m in _na_tiles(t, g):
                    halves = tuple(i_start <= i < i_start + NA_KH for i in (2 * m, 2 * m + 1))
                    a = 2 * m - g + row_offset[t] + NA_KH - 1 + 4
                    src = jnp.broadcast_to(rows_ref[hh, a:a + 1, :], (GRID_W, LANES))
                    rot = pltpu.roll(src, LANES - (NA_KW - 1), 1, stride=1, stride_axis=0)
                    tab_ref[hh, t, g * GRID_W:(g + 1) * GRID_W, m * LANES:(m + 1) * LANES] = (
                        jnp.where(valid[halves], rot * LOG2E, -jnp.inf))


class _NaRefs:
    def __init__(self, q, kt, v, bias, o, n_groups):
        self.q, self.kt, self.v, self.bias, self.o, self.n_groups = q, kt, v, bias, o, n_groups


def _na_rows(u):
    return pl.ds(pl.multiple_of(u * NA_GROUP, NA_GROUP), NA_GROUP)


def _na_key0(r, u):
    return pl.multiple_of(jnp.clip(u - 1, 0, r.n_groups - 3) * NA_GROUP, NA_GROUP)


def _na_scores(r, u):
    qq = r.q[_na_rows(u), :]
    first_head = lax.broadcasted_iota(jnp.int32, qq.shape, 1) < NA_HEAD_DIM
    zero = jnp.zeros_like(qq)
    q_heads = jnp.concatenate([jnp.where(first_head, qq, zero),
                               jnp.where(first_head, zero, qq)], axis=0)
    return _dot(q_heads, r.kt[:, pl.ds(_na_key0(r, u), NA_KEYS)])


def _na_softmax(r, s, table):
    p_tiles, inv_l = [], []
    for hh in range(2):
        for g in range(NA_GROUP_ROWS):
            rows = slice(hh * NA_GROUP + g * GRID_W, hh * NA_GROUP + (g + 1) * GRID_W)
            sb = [s[rows, c * LANES:(c + 1) * LANES]
                  + r.bias[hh, table, g * GRID_W:(g + 1) * GRID_W, c * LANES:(c + 1) * LANES]
                  for c in _na_tiles(table, g)]
            m = jnp.max(functools.reduce(jnp.maximum, sb), axis=-1, keepdims=True)
            p = [jnp.exp2(x - m) for x in sb]
            inv_l.append(1.0 / jnp.sum(functools.reduce(jnp.add, p), axis=-1, keepdims=True))
            p_tiles.append([x.astype(BF16) for x in p])
    return p_tiles, inv_l


def _na_store_output(r, u, o):
    first_head = lax.broadcasted_iota(jnp.int32, (NA_GROUP, LANES), 1) < NA_HEAD_DIM
    r.o[_na_rows(u), :] = jnp.where(first_head, o[:NA_GROUP], o[NA_GROUP:]).astype(r.o.dtype)


def _na_edge_group(r, u, table):
    tiles = _na_tiles(table, 0)
    p_tiles, inv_l = _na_softmax(r, _na_scores(r, u), table)
    p = jnp.concatenate([jnp.concatenate(t, axis=1) for t in p_tiles], axis=0)
    keys = pl.ds(_na_key0(r, u) + tiles.start * LANES, len(tiles) * LANES)
    _na_store_output(r, u, _dot(p, r.v[keys, :]) * jnp.concatenate(inv_l, axis=0))


def _na_interior_probs(r, s):
    p_tiles, inv_l = _na_softmax(r, s, 1)
    zero_tile = jnp.zeros((GRID_W, LANES), BF16)
    p_rows = []
    for block, tiles in enumerate(p_tiles):
        valid = _na_tiles(1, block % NA_GROUP_ROWS)
        p_rows.append(jnp.concatenate(
            [zero_tile] * valid.start + tiles + [zero_tile] * (NA_KEYS // LANES - valid.stop), axis=1))
    return jnp.concatenate(p_rows, axis=0), jnp.concatenate(inv_l, axis=0)


def _na_interior_output(r, u, p, inv_l):
    _na_store_output(r, u, _dot(p, r.v[pl.ds(_na_key0(r, u), NA_KEYS), :]) * inv_l)


def _na_stage(r, u, carry):
    s_cur, p_prev, l_prev = carry
    s_next = _na_scores(r, jnp.minimum(u + 1, r.n_groups - 2))
    p_cur, l_cur = _na_interior_probs(r, s_cur)
    _na_interior_output(r, u - 1, p_prev, l_prev)
    return s_next, p_cur, l_cur


def _na_kernel(q_ref, kt_ref, v_ref, rows_ref, o_ref, bias_ref, s_ref, p_ref, l_ref, *, n_groups):
    @pl.when(pl.program_id(1) == 0)
    def _():
        _na_build_tables(rows_ref, bias_ref)

    r = _NaRefs(q_ref, kt_ref, v_ref, bias_ref, o_ref, n_groups)
    last = n_groups - 1
    _na_edge_group(r, 0, 0)
    _na_edge_group(r, last, 2)

    def load_carry():
        return s_ref[...], p_ref[...], l_ref[...]

    def store_carry(carry):
        s_ref[...], p_ref[...], l_ref[...] = carry

    def stages(first, count, carry):
        for k in range(count):
            carry = _na_stage(r, first + k, carry)
        return carry

    def unrolled_stages(i, c):
        store_carry(stages(2 + NA_UNROLL * i, NA_UNROLL, load_carry()))
        return c

    store_carry((_na_scores(r, 2),) + _na_interior_probs(r, _na_scores(r, 1)))
    n_loops, n_tail = divmod(n_groups - 3, NA_UNROLL)
    lax.fori_loop(0, n_loops, unrolled_stages, 0)
    _, p_prev, l_prev = stages(2 + NA_UNROLL * n_loops, n_tail, load_carry())
    _na_interior_output(r, last - 1, p_prev, l_prev)


def _na(qa, kat, va, rows):
    B, T, _ = qa.shape
    n_pairs = NA_HEADS // 2
    return pl.pallas_call(
        functools.partial(_na_kernel, n_groups=T // NA_GROUP),
        grid=(n_pairs, B),
        in_specs=[
            pl.BlockSpec((None, T, LANES), lambda p, b: (b, 0, p)),
            pl.BlockSpec((None, LANES, T), lambda p, b: (b, p, 0)),
            pl.BlockSpec((None, T, LANES), lambda p, b: (b, 0, p)),
            pl.BlockSpec((2,) + rows.shape[1:], lambda p, b: (p, 0, 0)),
        ],
        out_specs=pl.BlockSpec((None, T, LANES), lambda p, b: (b, 0, p)),
        out_shape=jax.ShapeDtypeStruct((B, T, NA_WIDTH), BF16),
        scratch_shapes=[pltpu.VMEM((2, 3, NA_GROUP, NA_KEYS), F32),
                        pltpu.VMEM((2 * NA_GROUP, NA_KEYS), F32),
                        pltpu.VMEM((2 * NA_GROUP, NA_KEYS), BF16),
                        pltpu.VMEM((2 * NA_GROUP, 1), F32)],
        compiler_params=pltpu.CompilerParams(
            dimension_semantics=("parallel", "arbitrary"), vmem_limit_bytes=VMEM_LIMIT),
        name="natten",
    )(qa, kat, va, rows)


class _GlaDir:
    def __init__(self, reverse, ops, v, dec, o, state, a, contrib, prev):
        self.reverse = reverse
        self.qd, self.ki, self.kd = (ops.at[:, i * GLA_KW:(i + 1) * GLA_KW] for i in range(3))
        self.v, self.dec, self.o, self.state = v, dec, o, state
        self.a, self.contrib, self.prev = a, contrib, prev


def _gla_pair_lanes(h):
    lane = lax.broadcasted_iota(jnp.int32, (GLA_CHUNK, LANES), 1)
    return (lane < GLA_DK) if h % 2 == 0 else (lane >= GLA_DK)


def _gla_intra(d, c):
    C = GLA_CHUNK
    rows = slice(c * C, (c + 1) * C)
    qd, ki = d.qd[rows, :], d.ki[rows, :]
    lane = lax.broadcasted_iota(jnp.int32, qd.shape, 1)
    zeros = jnp.zeros_like(ki)
    i = lax.broadcasted_iota(jnp.int32, (2 * C, LANES), 0) & (C - 1)
    col = lax.broadcasted_iota(jnp.int32, (2 * C, LANES), 1)
    for parity in range(2):
        q_heads = jnp.concatenate(
            [jnp.where((lane >= h * GLA_DK) & (lane < (h + 1) * GLA_DK), qd, jnp.zeros_like(qd))
             for h in range(parity, GLA_HEADS, 2)], axis=0)
        keys = jnp.concatenate([zeros, ki] if parity == 0 else [ki, zeros], axis=0)
        scores = _dot_nt(q_heads, keys)
        j = col - C if parity == 0 else col
        in_chunk = (j >= 0) & (j < C)
        keep = in_chunk & ((j > i) if d.reverse else (j <= i))
        d.a[c, parity] = jnp.where(keep, scores, 0.0).astype(BF16)
    for p in range(GLA_HEADS // 2):
        kd_p = d.kd[rows, p * LANES:(p + 1) * LANES]
        v_p = d.v[rows, 2 * p * GLA_DV:(2 * p + 2) * GLA_DV]
        both = _dot_tn(kd_p, v_p)
        d.contrib[c, 2 * p] = both[:GLA_DK, :GLA_DV]
        d.contrib[c, 2 * p + 1] = both[GLA_DK:, GLA_DV:]


def _gla_scan(d, n_chunks):
    order = range(n_chunks - 1, -1, -1) if d.reverse else range(n_chunks)
    pad = jnp.zeros((LANES - n_chunks, GLA_KW), F32)
    dec_t = jnp.concatenate([d.dec[c] for c in range(n_chunks)] + [pad], axis=0).T
    for h in range(GLA_HEADS):
        st = d.state[h]
        for c in order:
            d.prev[c, h] = st.astype(BF16)
            st = st * dec_t[h * GLA_DK:(h + 1) * GLA_DK, c:c + 1] + d.contrib[c, h]
        d.state[h] = st


def _gla_outputs(d, c):
    C = GLA_CHUNK
    rows = slice(c * C, (c + 1) * C)
    for h in range(GLA_HEADS):
        p, parity = divmod(h, 2)
        qd_p = d.qd[rows, p * LANES:(p + 1) * LANES]
        scores = d.a[c, parity, p * C:(p + 1) * C, :]
        lhs = jnp.where(_gla_pair_lanes(h), qd_p, scores)
        cols = slice(h * GLA_DV, (h + 1) * GLA_DV)
        operands = [d.prev[c, h], d.v[rows, cols]]
        rhs = jnp.concatenate(operands if parity == 0 else operands[::-1], axis=0)
        d.o[rows, cols] = _dot(lhs, rhs).astype(d.o.dtype)


def _gla_kernel(gf_ref, vf_ref, decf_ref, gb_ref, vb_ref, decb_ref,
                of_ref, ob_ref, sf_ref, sb_ref, *scratch, n_chunks):
    @pl.when(pl.program_id(1) == 0)
    def _():
        sf_ref[...] = jnp.zeros_like(sf_ref)
        sb_ref[...] = jnp.zeros_like(sb_ref)

    n = len(scratch) // 2
    dirs = (_GlaDir(False, gf_ref, vf_ref, decf_ref, of_ref, sf_ref, *scratch[:n]),
            _GlaDir(True, gb_ref, vb_ref, decb_ref, ob_ref, sb_ref, *scratch[n:]))

    for c in range(n_chunks):
        for d in dirs:
            _gla_intra(d, c)
    for d in dirs:
        _gla_scan(d, n_chunks)
    for c in range(n_chunks):
        for d in dirs:
            _gla_outputs(d, c)


def _gla(gf, gb, vg, dec, *, tb):
    B, T, vw = vg.shape
    nb = T // tb
    nc = tb // GLA_CHUNK
    kw = GLA_KW
    n_pairs = GLA_HEADS // 2
    fwd = lambda w: pl.BlockSpec((None, tb, w), lambda b, s: (b, s, 0))
    bwd = lambda w: pl.BlockSpec((None, tb, w), lambda b, s: (b, nb - 1 - s, 0))
    dec_fwd = pl.BlockSpec((None, nc, 1, kw), lambda b, s: (b, s, 0, 0))
    dec_bwd = pl.BlockSpec((None, nc, 1, kw), lambda b, s: (b, nb - 1 - s, 0, 1))
    state = pltpu.VMEM((GLA_HEADS, GLA_DK, GLA_DV), F32)
    per_dir = [
        pltpu.VMEM((nc, 2, n_pairs * GLA_CHUNK, 2 * GLA_CHUNK), BF16),
        pltpu.VMEM((nc, GLA_HEADS, GLA_DK, GLA_DV), F32),
        pltpu.VMEM((nc, GLA_HEADS, GLA_DK, GLA_DV), BF16)]
    return pl.pallas_call(
        functools.partial(_gla_kernel, n_chunks=nc),
        grid=(B, nb),
        in_specs=[fwd(3 * kw), fwd(vw), dec_fwd, bwd(3 * kw), bwd(vw), dec_bwd],
        out_specs=[fwd(vw), bwd(vw)],
        out_shape=[jax.ShapeDtypeStruct((B, T, vw), BF16)] * 2,
        scratch_shapes=[state, state] + per_dir * 2,
        compiler_params=pltpu.CompilerParams(
            dimension_semantics=("parallel", "arbitrary"), vmem_limit_bytes=VMEM_LIMIT),
        name="gla",
    )(gf, vg, dec, gb, vg, dec)


STAGE_ROWS, STAGE_COLS = 512, 1024
FF_CHUNK = 2048


class _WeightStager:
    def __init__(self, tiles, stage, sem):
        self.tiles, self.stage, self.sem, self.done = tiles, stage, sem, 0
        for i in range(min(2, len(tiles))):
            self._copy(i).start()

    def _copy(self, i):
        src, _, r, c = self.tiles[i]
        return pltpu.make_async_copy(src.at[pl.ds(r, STAGE_ROWS), pl.ds(c, STAGE_COLS)],
                                     self.stage.at[i % 2], self.sem.at[i % 2])

    def need(self, count):
        while self.done < count:
            i = self.done
            _, dst, r, c = self.tiles[i]
            self._copy(i).wait()
            dst[r:r + STAGE_ROWS, c:c + STAGE_COLS] = self.stage[i % 2].astype(BF16)
            if i + 2 < len(self.tiles):
                self._copy(i + 2).start()
            self.done += 1


def _mix_ffn_body(x_ref, yna_ref, of_ref, ob_ref, sr_ref, gng_ref, gff_ref, gfin_ref, out_ref,
                  wout_ref, w1_ref, w2_ref, ready, *, final_norm, ff_chunk):
    o = of_ref[...].astype(F32) + ob_ref[...].astype(F32)
    parts = [_rms(o[:, h * GLA_DV:(h + 1) * GLA_DV], gng_ref[...]) for h in range(GLA_HEADS)]
    y_gla = (jnp.concatenate(parts, axis=-1) * sr_ref[...].astype(F32)).astype(BF16)
    na_w = yna_ref.shape[-1]
    tiles_per = lambda ref: (ref.shape[0] // STAGE_ROWS) * (ref.shape[1] // STAGE_COLS)
    n_tiles = tiles_per(wout_ref)
    ready(n_tiles)
    mix = _dot(yna_ref[...], wout_ref[0:na_w, :]) + _dot(y_gla, wout_ref[na_w:, :])
    h1 = x_ref[...] + mix
    n2 = _rms(h1, gff_ref[...]).astype(BF16)
    ffn = None
    n_chunks = w1_ref.shape[1] // ff_chunk
    for f in range(n_chunks):
        n_tiles += tiles_per(w1_ref) // n_chunks
        ready(n_tiles)
        u = _dot(n2, w1_ref[:, f * ff_chunk:(f + 1) * ff_chunk])
        a = jnp.square(jnp.maximum(u, 0.0)).astype(BF16)
        n_tiles += tiles_per(w2_ref) // n_chunks
        ready(n_tiles)
        d = _dot(a, w2_ref[f * ff_chunk:(f + 1) * ff_chunk, :])
        ffn = d if ffn is None else ffn + d
    h2 = h1 + ffn
    out_ref[...] = _rms(h2, gfin_ref[...]) if final_norm else h2


def _mix_ffn_kernel(x_ref, yna_ref, of_ref, ob_ref, sr_ref, gng_ref, wout_hbm, gff_ref,
                    w1_hbm, w2_hbm, gfin_ref, out_ref, wout_ref, w1_ref, w2_ref, stage, sem,
                    *, final_norm, ff_chunk):
    body = functools.partial(
        _mix_ffn_body, x_ref, yna_ref, of_ref, ob_ref, sr_ref, gng_ref, gff_ref, gfin_ref, out_ref,
        wout_ref, w1_ref, w2_ref, final_norm=final_norm, ff_chunk=ff_chunk)
    first = (pl.program_id(0) == 0) & (pl.program_id(1) == 0)

    @pl.when(first)
    def _():
        def tiles(src, dst, rows, cols):
            return [(src, dst, r, c) for r in range(rows.start, rows.stop, STAGE_ROWS)
                    for c in range(cols.start, cols.stop, STAGE_COLS)]

        order = tiles(wout_hbm, wout_ref, slice(0, wout_ref.shape[0]), slice(0, wout_ref.shape[1]))
        for f in range(w1_ref.shape[1] // ff_chunk):
            chunk = slice(f * ff_chunk, (f + 1) * ff_chunk)
            order += tiles(w1_hbm, w1_ref, slice(0, w1_ref.shape[0]), chunk)
            order += tiles(w2_hbm, w2_ref, chunk, slice(0, w2_ref.shape[1]))
        stager = _WeightStager(order, stage, sem)
        body(stager.need)
        stager.need(len(order))

    @pl.when(jnp.logical_not(first))
    def _():
        body(lambda n: None)


def _mix_ffn(x, yna, of, ob, sr, gng, wout, gff, w1, w2, gfin, *, tm, final_norm):
    B, T, D = x.shape
    tok = lambda w: pl.BlockSpec((None, tm, w), lambda b, i: (b, i, 0))
    const = lambda a: pl.BlockSpec(a.shape, lambda b, i: (0,) * a.ndim,
                                   pipeline_mode=pl.Buffered(1))
    hbm = pl.BlockSpec(memory_space=pl.ANY)
    for w in (wout, w1, w2):
        assert w.dtype == F32 and w.shape[0] % STAGE_ROWS == 0 and w.shape[1] % STAGE_COLS == 0
    assert w1.shape[1] % FF_CHUNK == 0 and FF_CHUNK % STAGE_ROWS == 0 and FF_CHUNK % STAGE_COLS == 0
    return pl.pallas_call(
        functools.partial(_mix_ffn_kernel, final_norm=final_norm, ff_chunk=FF_CHUNK),
        grid=(B, T // tm),
        in_specs=[tok(D), tok(yna.shape[-1]), tok(of.shape[-1]), tok(ob.shape[-1]),
                  tok(sr.shape[-1]), const(gng), hbm, const(gff), hbm, hbm, const(gfin)],
        out_specs=tok(D),
        out_shape=jax.ShapeDtypeStruct((B, T, D), F32),
        scratch_shapes=[pltpu.VMEM(w.shape, BF16) for w in (wout, w1, w2)] + [
            pltpu.VMEM((2, STAGE_ROWS, STAGE_COLS), F32), pltpu.SemaphoreType.DMA((2,))],
        compiler_params=pltpu.CompilerParams(
            dimension_semantics=("arbitrary", "arbitrary"), vmem_limit_bytes=VMEM_LIMIT),
        name="mix_ffn",
    )(x, yna, of, ob, sr, gng, wout, gff, w1, w2, gfin)


def kernel(x, ln_mix_g, w_in, na_rpb, gla_gate_up_fwd, gla_gate_bias_fwd, gla_gate_up_bwd,
           gla_gate_bias_bwd, gla_norm_g, w_out, ln_ff_g, w_ff1, w_ff2, ln_final_g):
    B, T, D = x.shape
    depth = w_in.shape[0]
    assert T % TOKEN_TILE == 0 and T // NA_GROUP >= 4
    gla_block = GLA_BLOCK if T % GLA_BLOCK == 0 else TOKEN_TILE
    row = lambda v: v.reshape(1, -1).astype(F32)
    zeros = jnp.zeros((GLA_GATE_RANK, GLA_KW), F32)
    h = x
    for l in range(depth):
        gu = jnp.concatenate([
            jnp.concatenate([gla_gate_up_fwd[l], zeros], axis=1),
            jnp.concatenate([zeros, gla_gate_up_bwd[l]], axis=1)], axis=0).astype(BF16)
        gb = jnp.concatenate([gla_gate_bias_fwd[l], gla_gate_bias_bwd[l]]).reshape(1, -1)
        qa, kat, va, vg, sr, gla_f, gla_b, dec = _inproj(
            h, row(ln_mix_g[l]), w_in[l].T, gu, gb.astype(F32), tm=TOKEN_TILE)
        y_na = _na(qa, kat, va, _na_bias_rows(na_rpb[l]))
        o_f, o_b = _gla(gla_f, gla_b, vg, dec, tb=gla_block)
        h = _mix_ffn(h, y_na, o_f, o_b, sr, row(gla_norm_g[l]), w_out[l], row(ln_ff_g[l]),
                     w_ff1[l], w_ff2[l], row(ln_final_g), tm=TOKEN_TILE,
                     final_norm=(l == depth - 1))
    return h
```

```python
import functools

import jax
import jax.numpy as jnp
from jax import lax
from jax.experimental import pallas as pl
from jax.experimental.pallas import tpu as pltpu

F32 = jnp.float32
BF16 = jnp.bfloat16

EPS = 1e-6
GRID_W = 64
NA_HEADS = 8
NA_HEAD_DIM = 64
NA_KH = 8
NA_KW = 16
GLA_HEADS = 4
GLA_DK = 64
GLA_DV = 128
GLA_GATE_RANK = 16
GLA_GATE_NORM = 16.0
GLA_CHUNK = 64

NA_WIDTH = NA_HEADS * NA_HEAD_DIM
GLA_KW = GLA_HEADS * GLA_DK
GLA_VW = GLA_HEADS * GLA_DV
COLS, _start = {}, 0
for _name, _width in (("qa", NA_WIDTH), ("ka", NA_WIDTH), ("va", NA_WIDTH), ("qg", GLA_KW),
                      ("kg", GLA_KW), ("vg", GLA_VW), ("rg", GLA_VW), ("z", 2 * GLA_GATE_RANK)):
    COLS[_name] = slice(_start, _start + _width)
    _start += _width
D_IN = _start

LANES = 128
TOKEN_TILE = 512
GLA_BLOCK = 1024
NA_GROUP_ROWS = 4
NA_GROUP = NA_GROUP_ROWS * GRID_W
NA_KEY_ROWS = 12
NA_KEYS = NA_KEY_ROWS * GRID_W
NA_UNROLL = 4
LOG2E = 1.4426950408889634
VMEM_LIMIT = 56 * 1024 * 1024


def _dot(a, b):
    return jnp.dot(a, b, preferred_element_type=F32)


def _dot_nt(a, b):
    return lax.dot_general(a, b, (((1,), (1,)), ((), ())), preferred_element_type=F32)


def _dot_tn(a, b):
    return lax.dot_general(a, b, (((0,), (0,)), ((), ())), preferred_element_type=F32)


def _rms(x, g):
    return x * lax.rsqrt(jnp.mean(x * x, axis=-1, keepdims=True) + EPS) * g


def _segmented_cumsum(x, reverse):
    n = x.shape[0]
    pos = lax.broadcasted_iota(jnp.int32, x.shape, 0) & (GLA_CHUNK - 1)
    step = 1
    while step < GLA_CHUNK:
        if reverse:
            shifted, ok = pltpu.roll(x, n - step, 0), pos < GLA_CHUNK - step
        else:
            shifted, ok = pltpu.roll(x, step, 0), pos >= step
        x = x + jnp.where(ok, shifted, 0.0)
        step *= 2
    return x


def _gla_cumdecay(log_a, reverse):
    b = _segmented_cumsum(log_a, reverse)
    b3 = b.reshape(-1, GLA_CHUNK, b.shape[-1])
    last = 0 if reverse else GLA_CHUNK - 1
    return b, b3[:, last:last + 1, :]


def _gla_operands(q, k, b, b_last):
    b3 = b.reshape(-1, GLA_CHUNK, b.shape[-1])
    k_dec = (k.reshape(b3.shape) * jnp.exp2(b_last - b3)).reshape(b.shape)
    ops = jnp.concatenate([q * jnp.exp2(b), k * jnp.exp2(-b), k_dec], axis=-1)
    return ops.astype(BF16), jnp.exp2(b_last)


def _inproj_kernel(x_ref, g_ref, wt_ref, gu_ref, gb_ref,
                   qa_ref, ka_ref, vat_ref, vg_ref, sr_ref, gf_ref, gbw_ref, dec_ref, wb_ref):
    @pl.when((pl.program_id(0) == 0) & (pl.program_id(1) == 0))
    def _():
        wb_ref[...] = wt_ref[...].astype(BF16)

    def proj(name):
        return _dot_nt(n, wb_ref[COLS[name], :])

    n = _rms(x_ref[...], g_ref[...]).astype(BF16)

    def log_decay(cols):
        pre = _dot(z, gu_ref[:, cols]) + gb_ref[:, cols]
        softplus2 = jnp.log2(1.0 + jnp.exp2(jnp.abs(pre) * -LOG2E))
        return jnp.minimum(pre * (LOG2E / GLA_GATE_NORM), 0.0) - softplus2 * (1.0 / GLA_GATE_NORM)

    z = proj("z").astype(BF16)
    b_f, last_f = _gla_cumdecay(log_decay(slice(0, GLA_KW)), False)
    qa_ref[...] = (proj("qa") * (NA_HEAD_DIM ** -0.5 * LOG2E)).astype(BF16)
    b_b, last_b = _gla_cumdecay(log_decay(slice(GLA_KW, 2 * GLA_KW)), True)
    ka_ref[...] = proj("ka").astype(BF16)
    qg = proj("qg") * (GLA_DK ** -0.5)
    kg = proj("kg")
    gf_ref[...], dec_f = _gla_operands(qg, kg, b_f, last_f)
    vat_ref[...] = _dot_nt(wb_ref[COLS["va"], :], n).astype(BF16)
    gbw_ref[...], dec_b = _gla_operands(qg, kg, b_b, last_b)
    dec_ref[...] = jnp.concatenate([dec_f, dec_b], axis=-1)
    vg_ref[...] = proj("vg").astype(BF16)
    r = proj("rg")
    sr_ref[...] = (r * jax.nn.sigmoid(r)).astype(BF16)


def _inproj(x, g, wt, gu, gb, *, tm):
    B, T, D = x.shape
    assert wt.shape == (D_IN, D)
    nc = tm // GLA_CHUNK
    tok = lambda w: pl.BlockSpec((None, tm, w), lambda b, i: (b, i, 0))
    const = lambda a: pl.BlockSpec(a.shape, lambda b, i: (0,) * a.ndim,
                                   pipeline_mode=pl.Buffered(1))
    tok_out = lambda w: (jax.ShapeDtypeStruct((B, T, w), BF16), tok(w))
    outs = [
        tok_out(NA_WIDTH),
        tok_out(NA_WIDTH),
        (jax.ShapeDtypeStruct((B, NA_WIDTH, T), BF16),
         pl.BlockSpec((None, NA_WIDTH, tm), lambda b, i: (b, 0, i))),
        tok_out(GLA_VW),
        tok_out(GLA_VW),
        tok_out(3 * GLA_KW),
        tok_out(3 * GLA_KW),
        (jax.ShapeDtypeStruct((B, T // GLA_CHUNK, 1, 2 * GLA_KW), F32),
         pl.BlockSpec((None, nc, 1, 2 * GLA_KW), lambda b, i: (b, i, 0, 0))),
    ]
    return pl.pallas_call(
        _inproj_kernel,
        grid=(B, T // tm),
        in_specs=[tok(D), const(g), const(wt), const(gu), const(gb)],
        out_specs=[s for _, s in outs],
        out_shape=[s for s, _ in outs],
        scratch_shapes=[pltpu.VMEM(wt.shape, BF16)],
        compiler_params=pltpu.CompilerParams(
            dimension_semantics=("arbitrary", "arbitrary"), vmem_limit_bytes=VMEM_LIMIT),
        name="inproj",
    )(x, g, wt, gu, gb)


def _na_bias_rows(rpb):
    n_dc = 2 * NA_KW - 1
    padded = jnp.pad(rpb.astype(F32), ((0, 0), (4, 5), (0, GRID_W - n_dc)))
    return jnp.concatenate([padded[:, :-1], padded[:, 1:]], axis=-1)


def _na_key_rows(table, g):
    return (0, g, NA_KEY_ROWS - NA_KH)[table]


def _na_build_tables(rows_ref, tab_ref, tabt_ref):
    qc = lax.broadcasted_iota(jnp.int32, (GRID_W, LANES), 0)
    lane = lax.broadcasted_iota(jnp.int32, (GRID_W, LANES), 1)
    kc = lane & (GRID_W - 1)
    col_start = jnp.clip(qc - NA_KW // 2, 0, GRID_W - NA_KW)
    in_win = (kc >= col_start) & (kc < col_start + NA_KW)
    valid = {(True, True): in_win,
             (True, False): in_win & (lane < GRID_W),
             (False, True): in_win & (lane >= GRID_W)}
    row_offset = (0, -(NA_KH // 2), -NA_KH)
    for hh in range(2):
        for t in range(3):
            for g in range(NA_GROUP_ROWS):
                i_start = _na_key_rows(t, g)
                for m in range(NA_KEY_ROWS // 2):
                    halves = tuple(i_start <= i < i_start + NA_KH for i in (2 * m, 2 * m + 1))
                    if any(halves):
                        a = 2 * m - g + row_offset[t] + NA_KH - 1 + 4
                        src = jnp.broadcast_to(rows_ref[hh, a:a + 1, :], (GRID_W, LANES))
                        rot = pltpu.roll(src, LANES - (NA_KW - 1), 1, stride=1, stride_axis=0)
                        block = jnp.where(valid[halves], rot * LOG2E, -jnp.inf)
                    else:
                        block = jnp.full((GRID_W, LANES), -jnp.inf, F32)
                    tab_ref[hh, t, g * GRID_W:(g + 1) * GRID_W, m * LANES:(m + 1) * LANES] = block
            tabt_ref[hh, t] = tab_ref[hh, t].T


class _NaRefs:
    def __init__(self, q, k, vt, bias, o, n_groups):
        self.q, self.k, self.vt, self.bias, self.o, self.n_groups = q, k, vt, bias, o, n_groups


def _na_rows(u):
    return pl.ds(pl.multiple_of(u * NA_GROUP, NA_GROUP), NA_GROUP)


def _na_key0(r, u):
    return pl.multiple_of(jnp.clip(u - 1, 0, r.n_groups - 3) * NA_GROUP, NA_GROUP)


def _na_scores(r, u):
    qq = r.q[_na_rows(u), :]
    first_head = lax.broadcasted_iota(jnp.int32, qq.shape, 1) < NA_HEAD_DIM
    zero = jnp.zeros_like(qq)
    q_heads = jnp.concatenate([jnp.where(first_head, qq, zero),
                               jnp.where(first_head, zero, qq)], axis=0)
    return _dot_nt(r.k[pl.ds(_na_key0(r, u), NA_KEYS), :], q_heads)


def _na_probs(r, s, table):
    p_cols, inv_l = [], []
    for hh in range(2):
        for gp in range(NA_GROUP_ROWS // 2):
            lo = GRID_W * _na_key_rows(table, 2 * gp)
            hi = GRID_W * (_na_key_rows(table, 2 * gp + 1) + NA_KH)
            cols = slice(hh * NA_GROUP + gp * LANES, hh * NA_GROUP + (gp + 1) * LANES)
            sb = s[lo:hi, cols] + r.bias[hh, table, lo:hi, gp * LANES:(gp + 1) * LANES]
            p = jnp.exp2(sb - jnp.max(sb, axis=0, keepdims=True))
            inv_l.append(1.0 / jnp.sum(p, axis=0, keepdims=True))
            pieces = [jnp.zeros((lo, LANES), BF16), p.astype(BF16), jnp.zeros((NA_KEYS - hi, LANES), BF16)]
            p_cols.append(jnp.concatenate([x for x in pieces if x.shape[0]], axis=0))
    return jnp.concatenate(p_cols, axis=1), jnp.concatenate(inv_l, axis=1)


def _na_output(r, u, p, inv_l):
    ot = _dot(r.vt[:, pl.ds(_na_key0(r, u), NA_KEYS)], p) * inv_l
    first_head = lax.broadcasted_iota(jnp.int32, (LANES, NA_GROUP), 0) < NA_HEAD_DIM
    both = jnp.where(first_head, ot[:, :NA_GROUP], ot[:, NA_GROUP:])
    r.o[_na_rows(u), :] = both.T.astype(r.o.dtype)


def _na_stage(r, u, carry):
    s_cur, p_prev, l_prev = carry
    s_next = _na_scores(r, jnp.minimum(u + 1, r.n_groups - 2))
    p_cur, l_cur = _na_probs(r, s_cur, 1)
    _na_output(r, u - 1, p_prev, l_prev)
    return s_next, p_cur, l_cur


def _na_kernel(q_ref, k_ref, vt_ref, rows_ref, o_ref, bias_ref, tab_ref, s_ref, p_ref, l_ref,
               *, n_groups):
    @pl.when(pl.program_id(1) == 0)
    def _():
        _na_build_tables(rows_ref, tab_ref, bias_ref)

    r = _NaRefs(q_ref, k_ref, vt_ref, bias_ref, o_ref, n_groups)
    last = n_groups - 1
    for u, table in ((0, 0), (last, 2)):
        _na_output(r, u, *_na_probs(r, _na_scores(r, u), table))

    def load_carry():
        return s_ref[...], p_ref[...], l_ref[...]

    def store_carry(carry):
        s_ref[...], p_ref[...], l_ref[...] = carry

    def stages(first, count, carry):
        for k in range(count):
            carry = _na_stage(r, first + k, carry)
        return carry

    def unrolled_stages(i, c):
        store_carry(stages(2 + NA_UNROLL * i, NA_UNROLL, load_carry()))
        return c

    store_carry((_na_scores(r, 2),) + _na_probs(r, _na_scores(r, 1), 1))
    n_loops, n_tail = divmod(n_groups - 3, NA_UNROLL)
    lax.fori_loop(0, n_loops, unrolled_stages, 0)
    _, p_prev, l_prev = stages(2 + NA_UNROLL * n_loops, n_tail, load_carry())
    _na_output(r, last - 1, p_prev, l_prev)


def _na(qa, ka, vat, rows):
    B, T, _ = qa.shape
    n_pairs = NA_HEADS // 2
    return pl.pallas_call(
        functools.partial(_na_kernel, n_groups=T // NA_GROUP),
        grid=(n_pairs, B),
        in_specs=[
            pl.BlockSpec((None, T, LANES), lambda p, b: (b, 0, p)),
            pl.BlockSpec((None, T, LANES), lambda p, b: (b, 0, p)),
            pl.BlockSpec((None, LANES, T), lambda p, b: (b, p, 0)),
            pl.BlockSpec((2,) + rows.shape[1:], lambda p, b: (p, 0, 0)),
        ],
        out_specs=pl.BlockSpec((None, T, LANES), lambda p, b: (b, 0, p)),
        out_shape=jax.ShapeDtypeStruct((B, T, NA_WIDTH), BF16),
        scratch_shapes=[pltpu.VMEM((2, 3, NA_KEYS, NA_GROUP), F32),
                        pltpu.VMEM((2, 3, NA_GROUP, NA_KEYS), F32),
                        pltpu.VMEM((NA_KEYS, 2 * NA_GROUP), F32),
                        pltpu.VMEM((NA_KEYS, 2 * NA_GROUP), BF16),
                        pltpu.VMEM((1, 2 * NA_GROUP), F32)],
        compiler_params=pltpu.CompilerParams(
            dimension_semantics=("parallel", "arbitrary"), vmem_limit_bytes=VMEM_LIMIT),
        name="natten",
    )(qa, ka, vat, rows)


class _GlaDir:
    def __init__(self, reverse, ops, v, dec, o, state, a, contrib, prev):
        self.reverse = reverse
        self.qd, self.ki, self.kd = (ops.at[:, i * GLA_KW:(i + 1) * GLA_KW] for i in range(3))
        self.v, self.dec, self.o, self.state = v, dec, o, state
        self.a, self.contrib, self.prev = a, contrib, prev


def _gla_pair_lanes(h):
    lane = lax.broadcasted_iota(jnp.int32, (GLA_CHUNK, LANES), 1)
    return (lane < GLA_DK) if h % 2 == 0 else (lane >= GLA_DK)


def _gla_intra(d, c):
    C = GLA_CHUNK
    rows = slice(c * C, (c + 1) * C)
    qd, ki = d.qd[rows, :], d.ki[rows, :]
    lane = lax.broadcasted_iota(jnp.int32, qd.shape, 1)
    zeros = jnp.zeros_like(ki)
    i = lax.broadcasted_iota(jnp.int32, (2 * C, LANES), 0) & (C - 1)
    col = lax.broadcasted_iota(jnp.int32, (2 * C, LANES), 1)
    for parity in range(2):
        q_heads = jnp.concatenate(
            [jnp.where((lane >= h * GLA_DK) & (lane < (h + 1) * GLA_DK), qd, jnp.zeros_like(qd))
             for h in range(parity, GLA_HEADS, 2)], axis=0)
        keys = jnp.concatenate([zeros, ki] if parity == 0 else [ki, zeros], axis=0)
        scores = _dot_nt(q_heads, keys)
        j = col - C if parity == 0 else col
        in_chunk = (j >= 0) & (j < C)
        keep = in_chunk & ((j > i) if d.reverse else (j <= i))
        d.a[c, parity] = jnp.where(keep, scores, 0.0).astype(BF16)
    for p in range(GLA_HEADS // 2):
        kd_p = d.kd[rows, p * LANES:(p + 1) * LANES]
        v_p = d.v[rows, 2 * p * GLA_DV:(2 * p + 2) * GLA_DV]
        both = _dot_tn(kd_p, v_p)
        d.contrib[c, 2 * p] = both[:GLA_DK, :GLA_DV]
        d.contrib[c, 2 * p + 1] = both[GLA_DK:, GLA_DV:]


def _gla_scan(d, n_chunks):
    order = range(n_chunks - 1, -1, -1) if d.reverse else range(n_chunks)
    pad = jnp.zeros((LANES - n_chunks, GLA_KW), F32)
    dec_t = jnp.concatenate([d.dec[c] for c in range(n_chunks)] + [pad], axis=0).T
    for h in range(GLA_HEADS):
        st = d.state[h]
        for c in order:
            d.prev[c, h] = st.astype(BF16)
            st = st * dec_t[h * GLA_DK:(h + 1) * GLA_DK, c:c + 1] + d.contrib[c, h]
        d.state[h] = st


def _gla_outputs(d, c):
    C = GLA_CHUNK
    rows = slice(c * C, (c + 1) * C)
    for h in range(GLA_HEADS):
        p, parity = divmod(h, 2)
        qd_p = d.qd[rows, p * LANES:(p + 1) * LANES]
        scores = d.a[c, parity, p * C:(p + 1) * C, :]
        lhs = jnp.where(_gla_pair_lanes(h), qd_p, scores)
        cols = slice(h * GLA_DV, (h + 1) * GLA_DV)
        operands = [d.prev[c, h], d.v[rows, cols]]
        rhs = jnp.concatenate(operands if parity == 0 else operands[::-1], axis=0)
        d.o[rows, cols] = _dot(lhs, rhs).astype(d.o.dtype)


def _gla_kernel(gf_ref, vf_ref, decf_ref, gb_ref, vb_ref, decb_ref,
                of_ref, ob_ref, sf_ref, sb_ref, *scratch, n_chunks):
    @pl.when(pl.program_id(1) == 0)
    def _():
        sf_ref[...] = jnp.zeros_like(sf_ref)
        sb_ref[...] = jnp.zeros_like(sb_ref)

    n = len(scratch) // 2
    dirs = (_GlaDir(False, gf_ref, vf_ref, decf_ref, of_ref, sf_ref, *scratch[:n]),
            _GlaDir(True, gb_ref, vb_ref, decb_ref, ob_ref, sb_ref, *scratch[n:]))

    for c in range(n_chunks):
        for d in dirs:
            _gla_intra(d, c)
    for d in dirs:
        _gla_scan(d, n_chunks)
    for c in range(n_chunks):
        for d in dirs:
            _gla_outputs(d, c)


def _gla(gf, gb, vg, dec, *, tb):
    B, T, vw = vg.shape
    nb = T // tb
    nc = tb // GLA_CHUNK
    kw = GLA_KW
    n_pairs = GLA_HEADS // 2
    fwd = lambda w: pl.BlockSpec((None, tb, w), lambda b, s: (b, s, 0))
    bwd = lambda w: pl.BlockSpec((None, tb, w), lambda b, s: (b, nb - 1 - s, 0))
    dec_fwd = pl.BlockSpec((None, nc, 1, kw), lambda b, s: (b, s, 0, 0))
    dec_bwd = pl.BlockSpec((None, nc, 1, kw), lambda b, s: (b, nb - 1 - s, 0, 1))
    state = pltpu.VMEM((GLA_HEADS, GLA_DK, GLA_DV), F32)
    per_dir = [
        pltpu.VMEM((nc, 2, n_pairs * GLA_CHUNK, 2 * GLA_CHUNK), BF16),
        pltpu.VMEM((nc, GLA_HEADS, GLA_DK, GLA_DV), F32),
        pltpu.VMEM((nc, GLA_HEADS, GLA_DK, GLA_DV), BF16)]
    return pl.pallas_call(
        functools.partial(_gla_kernel, n_chunks=nc),
        grid=(B, nb),
        in_specs=[fwd(3 * kw), fwd(vw), dec_fwd, bwd(3 * kw), bwd(vw), dec_bwd],
        out_specs=[fwd(vw), bwd(vw)],
        out_shape=[jax.ShapeDtypeStruct((B, T, vw), BF16)] * 2,
        scratch_shapes=[state, state] + per_dir * 2,
        compiler_params=pltpu.CompilerParams(
            dimension_semantics=("parallel", "arbitrary"), vmem_limit_bytes=VMEM_LIMIT),
        name="gla",
    )(gf, vg, dec, gb, vg, dec)


STAGE_ROWS, STAGE_COLS = 512, 1024


class _WeightStager:
    def __init__(self, tiles, stage, sem):
        self.tiles, self.stage, self.sem, self.done = tiles, stage, sem, 0
        for i in range(min(2, len(tiles))):
            self._copy(i).start()

    def _copy(self, i):
        src, _, r, c = self.tiles[i]
        return pltpu.make_async_copy(src.at[pl.ds(r, STAGE_ROWS), pl.ds(c, STAGE_COLS)],
                                     self.stage.at[i % 2], self.sem.at[i % 2])

    def need(self, count):
        while self.done < count:
            i = self.done
            _, dst, r, c = self.tiles[i]
            self._copy(i).wait()
            dst[r:r + STAGE_ROWS, c:c + STAGE_COLS] = self.stage[i % 2].astype(BF16)
            if i + 2 < len(self.tiles):
                self._copy(i + 2).start()
            self.done += 1


def _mix_ffn_body(x_ref, yna_ref, of_ref, ob_ref, sr_ref, gng_ref, gff_ref, gfin_ref, out_ref,
                  wout_ref, w1_ref, w2_ref, ready, *, final_norm, ff_chunk):
    o = of_ref[...].astype(F32) + ob_ref[...].astype(F32)
    parts = [_rms(o[:, h * GLA_DV:(h + 1) * GLA_DV], gng_ref[...]) for h in range(GLA_HEADS)]
    y_gla = (jnp.concatenate(parts, axis=-1) * sr_ref[...].astype(F32)).astype(BF16)
    na_w = yna_ref.shape[-1]
    tiles_per = lambda ref: (ref.shape[0] // STAGE_ROWS) * (ref.shape[1] // STAGE_COLS)
    n_tiles = tiles_per(wout_ref)
    ready(n_tiles)
    mix = _dot(yna_ref[...], wout_ref[0:na_w, :]) + _dot(y_gla, wout_ref[na_w:, :])
    h1 = x_ref[...] + mix
    n2 = _rms(h1, gff_ref[...]).astype(BF16)
    ffn = None
    n_chunks = w1_ref.shape[1] // ff_chunk
    for f in range(n_chunks):
        n_tiles += tiles_per(w1_ref) // n_chunks
        ready(n_tiles)
        u = _dot(n2, w1_ref[:, f * ff_chunk:(f + 1) * ff_chunk])
        a = jnp.square(jnp.maximum(u, 0.0)).astype(BF16)
        n_tiles += tiles_per(w2_ref) // n_chunks
        ready(n_tiles)
        d = _dot(a, w2_ref[f * ff_chunk:(f + 1) * ff_chunk, :])
        ffn = d if ffn is None else ffn + d
    h2 = h1 + ffn
    out_ref[...] = _rms(h2, gfin_ref[...]) if final_norm else h2


def _mix_ffn_kernel(x_ref, yna_ref, of_ref, ob_ref, sr_ref, gng_ref, wout_hbm, gff_ref,
                    w1_hbm, w2_hbm, gfin_ref, out_ref, wout_ref, w1_ref, w2_ref, stage, sem,
                    *, final_norm, ff_chunk):
    body = functools.partial(
        _mix_ffn_body, x_ref, yna_ref, of_ref, ob_ref, sr_ref, gng_ref, gff_ref, gfin_ref, out_ref,
        wout_ref, w1_ref, w2_ref, final_norm=final_norm, ff_chunk=ff_chunk)
    first = (pl.program_id(0) == 0) & (pl.program_id(1) == 0)

    @pl.when(first)
    def _():
        def tiles(src, dst, rows, cols):
            return [(src, dst, r, c) for r in range(rows.start, rows.stop, STAGE_ROWS)
                    for c in range(cols.start, cols.stop, STAGE_COLS)]

        order = tiles(wout_hbm, wout_ref, slice(0, wout_ref.shape[0]), slice(0, wout_ref.shape[1]))
        for f in range(w1_ref.shape[1] // ff_chunk):
            chunk = slice(f * ff_chunk, (f + 1) * ff_chunk)
            order += tiles(w1_hbm, w1_ref, slice(0, w1_ref.shape[0]), chunk)
            order += tiles(w2_hbm, w2_ref, chunk, slice(0, w2_ref.shape[1]))
        stager = _WeightStager(order, stage, sem)
        body(stager.need)
        stager.need(len(order))

    @pl.when(jnp.logical_not(first))
    def _():
        body(lambda n: None)


def _mix_ffn(x, yna, of, ob, sr, gng, wout, gff, w1, w2, gfin, *, tm, final_norm):
    B, T, D = x.shape
    tok = lambda w: pl.BlockSpec((None, tm, w), lambda b, i: (b, i, 0))
    const = lambda a: pl.BlockSpec(a.shape, lambda b, i: (0,) * a.ndim,
                                   pipeline_mode=pl.Buffered(1))
    hbm = pl.BlockSpec(memory_space=pl.ANY)
    for w in (wout, w1, w2):
        assert w.dtype == F32 and w.shape[0] % STAGE_ROWS == 0 and w.shape[1] % STAGE_COLS == 0
    assert w1.shape[1] % 1024 == 0 and 1024 % STAGE_ROWS == 0 and 1024 % STAGE_COLS == 0
    return pl.pallas_call(
        functools.partial(_mix_ffn_kernel, final_norm=final_norm, ff_chunk=1024),
        grid=(B, T // tm),
        in_specs=[tok(D), tok(yna.shape[-1]), tok(of.shape[-1]), tok(ob.shape[-1]),
                  tok(sr.shape[-1]), const(gng), hbm, const(gff), hbm, hbm, const(gfin)],
        out_specs=tok(D),
        out_shape=jax.ShapeDtypeStruct((B, T, D), F32),
        scratch_shapes=[pltpu.VMEM(w.shape, BF16) for w in (wout, w1, w2)] + [
            pltpu.VMEM((2, STAGE_ROWS, STAGE_COLS), F32), pltpu.SemaphoreType.DMA((2,))],
        compiler_params=pltpu.CompilerParams(
            dimension_semantics=("arbitrary", "arbitrary"), vmem_limit_bytes=VMEM_LIMIT),
        name="mix_ffn",
    )(x, yna, of, ob, sr, gng, wout, gff, w1, w2, gfin)


def kernel(x, ln_mix_g, w_in, na_rpb, gla_gate_up_fwd, gla_gate_bias_fwd, gla_gate_up_bwd,
           gla_gate_bias_bwd, gla_norm_g, w_out, ln_ff_g, w_ff1, w_ff2, ln_final_g):
    B, T, D = x.shape
    depth = w_in.shape[0]
    assert T % TOKEN_TILE == 0 and T // NA_GROUP >= 4
    gla_block = GLA_BLOCK if T % GLA_BLOCK == 0 else TOKEN_TILE
    row = lambda v: v.reshape(1, -1).astype(F32)
    zeros = jnp.zeros((GLA_GATE_RANK, GLA_KW), F32)
    h = x
    for l in range(depth):
        gu = jnp.concatenate([
            jnp.concatenate([gla_gate_up_fwd[l], zeros], axis=1),
            jnp.concatenate([zeros, gla_gate_up_bwd[l]], axis=1)], axis=0).astype(BF16)
        gb = jnp.concatenate([gla_gate_bias_fwd[l], gla_gate_bias_bwd[l]]).reshape(1, -1)
        qa, ka, vat, vg, sr, gla_f, gla_b, dec = _inproj(
            h, row(ln_mix_g[l]), w_in[l].T, gu, gb.astype(F32), tm=TOKEN_TILE)
        y_na = _na(qa, ka, vat, _na_bias_rows(na_rpb[l]))
        o_f, o_b = _gla(gla_f, gla_b, vg, dec, tb=gla_block)
        h = _mix_ffn(h, y_na, o_f, o_b, sr, row(gla_norm_g[l]), w_out[l], row(ln_ff_g[l]),
                     w_ff1[l], w_ff2[l], row(ln_final_g), tm=TOKEN_TILE,
                     final_norm=(l == depth - 1))
    return h
```

```python
import functools

import jax
import jax.numpy as jnp
from jax import lax
from jax.experimental import pallas as pl
from jax.experimental.pallas import tpu as pltpu

F32 = jnp.float32
BF16 = jnp.bfloat16

EPS = 1e-6
GRID_W = 64
NA_HEADS = 8
NA_HEAD_DIM = 64
NA_KH = 8
NA_KW = 16
GLA_HEADS = 4
GLA_DK = 64
GLA_DV = 128
GLA_GATE_RANK = 16
GLA_GATE_NORM = 16.0
GLA_CHUNK = 64

NA_WIDTH = NA_HEADS * NA_HEAD_DIM
GLA_KW = GLA_HEADS * GLA_DK
GLA_VW = GLA_HEADS * GLA_DV
COLS, _start = {}, 0
for _name, _width in (("qa", NA_WIDTH), ("ka", NA_WIDTH), ("va", NA_WIDTH), ("qg", GLA_KW),
                      ("kg", GLA_KW), ("vg", GLA_VW), ("rg", GLA_VW), ("z", 2 * GLA_GATE_RANK)):
    COLS[_name] = slice(_start, _start + _width)
    _start += _width
D_IN = _start

LANES = 128
TOKEN_TILE = 512
GLA_BLOCK = 1024
NA_GROUP_ROWS = 4
NA_GROUP = NA_GROUP_ROWS * GRID_W
NA_KEY_ROWS = 12
NA_KEYS = NA_KEY_ROWS * GRID_W
NA_UNROLL = 4
LOG2E = 1.4426950408889634
VMEM_LIMIT = 56 * 1024 * 1024


def _dot(a, b):
    return jnp.dot(a, b, preferred_element_type=F32)


def _dot_nt(a, b):
    return lax.dot_general(a, b, (((1,), (1,)), ((), ())), preferred_element_type=F32)


def _dot_tn(a, b):
    return lax.dot_general(a, b, (((0,), (0,)), ((), ())), preferred_element_type=F32)


def _rms(x, g):
    return x * lax.rsqrt(jnp.mean(x * x, axis=-1, keepdims=True) + EPS) * g


def _segmented_cumsum(x, reverse):
    n = x.shape[0]
    pos = lax.broadcasted_iota(jnp.int32, x.shape, 0) & (GLA_CHUNK - 1)
    step = 1
    while step < GLA_CHUNK:
        if reverse:
            shifted, ok = pltpu.roll(x, n - step, 0), pos < GLA_CHUNK - step
        else:
            shifted, ok = pltpu.roll(x, step, 0), pos >= step
        x = x + jnp.where(ok, shifted, 0.0)
        step *= 2
    return x


def _gla_cumdecay(log_a, reverse):
    b = _segmented_cumsum(log_a, reverse)
    b3 = b.reshape(-1, GLA_CHUNK, b.shape[-1])
    last = 0 if reverse else GLA_CHUNK - 1
    return b, b3[:, last:last + 1, :]


def _gla_operands(q, k, b, b_last):
    b3 = b.reshape(-1, GLA_CHUNK, b.shape[-1])
    k_dec = (k.reshape(b3.shape) * jnp.exp2(b_last - b3)).reshape(b.shape)
    ops = jnp.concatenate([q * jnp.exp2(b), k * jnp.exp2(-b), k_dec], axis=-1)
    return ops.astype(BF16), jnp.exp2(b_last)


def _inproj_kernel(x_ref, g_ref, wt_ref, gu_ref, gb_ref,
                   qa_ref, kat_ref, va_ref, vg_ref, sr_ref, gf_ref, gbw_ref, dec_ref, wb_ref):
    @pl.when((pl.program_id(0) == 0) & (pl.program_id(1) == 0))
    def _():
        wb_ref[...] = wt_ref[...].astype(BF16)

    def proj(name):
        return _dot_nt(n, wb_ref[COLS[name], :])

    n = _rms(x_ref[...], g_ref[...]).astype(BF16)

    def log_decay(cols):
        pre = _dot(z, gu_ref[:, cols]) + gb_ref[:, cols]
        softplus2 = jnp.log2(1.0 + jnp.exp2(jnp.abs(pre) * -LOG2E))
        return jnp.minimum(pre * (LOG2E / GLA_GATE_NORM), 0.0) - softplus2 * (1.0 / GLA_GATE_NORM)

    z = proj("z").astype(BF16)
    b_f, last_f = _gla_cumdecay(log_decay(slice(0, GLA_KW)), False)
    qa_ref[...] = (proj("qa") * (NA_HEAD_DIM ** -0.5 * LOG2E)).astype(BF16)
    b_b, last_b = _gla_cumdecay(log_decay(slice(GLA_KW, 2 * GLA_KW)), True)
    kat_ref[...] = _dot_nt(wb_ref[COLS["ka"], :], n).astype(BF16)
    qg = proj("qg") * (GLA_DK ** -0.5)
    kg = proj("kg")
    gf_ref[...], dec_f = _gla_operands(qg, kg, b_f, last_f)
    va_ref[...] = proj("va").astype(BF16)
    gbw_ref[...], dec_b = _gla_operands(qg, kg, b_b, last_b)
    dec_ref[...] = jnp.concatenate([dec_f, dec_b], axis=-1)
    vg_ref[...] = proj("vg").astype(BF16)
    r = proj("rg")
    sr_ref[...] = (r * jax.nn.sigmoid(r)).astype(BF16)


def _inproj(x, g, wt, gu, gb, *, tm):
    B, T, D = x.shape
    assert wt.shape == (D_IN, D)
    nc = tm // GLA_CHUNK
    tok = lambda w: pl.BlockSpec((None, tm, w), lambda b, i: (b, i, 0))
    const = lambda a: pl.BlockSpec(a.shape, lambda b, i: (0,) * a.ndim,
                                   pipeline_mode=pl.Buffered(1))
    tok_out = lambda w: (jax.ShapeDtypeStruct((B, T, w), BF16), tok(w))
    outs = [
        tok_out(NA_WIDTH),
        (jax.ShapeDtypeStruct((B, NA_WIDTH, T), BF16),
         pl.BlockSpec((None, NA_WIDTH, tm), lambda b, i: (b, 0, i))),
        tok_out(NA_WIDTH),
        tok_out(GLA_VW),
        tok_out(GLA_VW),
        tok_out(3 * GLA_KW),
        tok_out(3 * GLA_KW),
        (jax.ShapeDtypeStruct((B, T // GLA_CHUNK, 1, 2 * GLA_KW), F32),
         pl.BlockSpec((None, nc, 1, 2 * GLA_KW), lambda b, i: (b, i, 0, 0))),
    ]
    return pl.pallas_call(
        _inproj_kernel,
        grid=(B, T // tm),
        in_specs=[tok(D), const(g), const(wt), const(gu), const(gb)],
        out_specs=[s for _, s in outs],
        out_shape=[s for s, _ in outs],
        scratch_shapes=[pltpu.VMEM(wt.shape, BF16)],
        compiler_params=pltpu.CompilerParams(
            dimension_semantics=("arbitrary", "arbitrary"), vmem_limit_bytes=VMEM_LIMIT),
        name="inproj",
    )(x, g, wt, gu, gb)


def _na_bias_rows(rpb):
    n_dc = 2 * NA_KW - 1
    padded = jnp.pad(rpb.astype(F32), ((0, 0), (4, 5), (0, GRID_W - n_dc)))
    return jnp.concatenate([padded[:, :-1], padded[:, 1:]], axis=-1)


def _na_tiles(table, g):
    i_start = (0, g, NA_KEY_ROWS - NA_KH)[table]
    return range(i_start // 2, (i_start + NA_KH + 1) // 2)


def _na_build_tables(rows_ref, tab_ref):
    qc = lax.broadcasted_iota(jnp.int32, (GRID_W, LANES), 0)
    lane = lax.broadcasted_iota(jnp.int32, (GRID_W, LANES), 1)
    kc = lane & (GRID_W - 1)
    col_start = jnp.clip(qc - NA_KW // 2, 0, GRID_W - NA_KW)
    in_win = (kc >= col_start) & (kc < col_start + NA_KW)
    valid = {(True, True): in_win,
             (True, False): in_win & (lane < GRID_W),
             (False, True): in_win & (lane >= GRID_W)}
    row_offset = (0, -(NA_KH // 2), -NA_KH)
    for hh in range(2):
        for t in range(3):
            for g in range(NA_GROUP_ROWS):
                i_start = (0, g, NA_KEY_ROWS - NA_KH)[t]
                for m in _na_tiles(t, g):
                    halves = tuple(i_start <= i < i_start + NA_KH for i in (2 * m, 2 * m + 1))
                    a = 2 * m - g + row_offset[t] + NA_KH - 1 + 4
                    src = jnp.broadcast_to(rows_ref[hh, a:a + 1, :], (GRID_W, LANES))
                    rot = pltpu.roll(src, LANES - (NA_KW - 1), 1, stride=1, stride_axis=0)
                    tab_ref[hh, t, g * GRID_W:(g + 1) * GRID_W, m * LANES:(m + 1) * LANES] = (
                        jnp.where(valid[halves], rot * LOG2E, -jnp.inf))


class _NaRefs:
    def __init__(self, q, kt, v, bias, o, n_groups):
        self.q, self.kt, self.v, self.bias, self.o, self.n_groups = q, kt, v, bias, o, n_groups


def _na_rows(u):
    return pl.ds(pl.multiple_of(u * NA_GROUP, NA_GROUP), NA_GROUP)


def _na_key0(r, u):
    return pl.multiple_of(jnp.clip(u - 1, 0, r.n_groups - 3) * NA_GROUP, NA_GROUP)


def _na_scores(r, u):
    qq = r.q[_na_rows(u), :]
    first_head = lax.broadcasted_iota(jnp.int32, qq.shape, 1) < NA_HEAD_DIM
    zero = jnp.zeros_like(qq)
    q_heads = jnp.concatenate([jnp.where(first_head, qq, zero),
                               jnp.where(first_head, zero, qq)], axis=0)
    return _dot(q_heads, r.kt[:, pl.ds(_na_key0(r, u), NA_KEYS)])


def _na_softmax(r, s, table):
    p_tiles, inv_l = [], []
    for hh in range(2):
        for g in range(NA_GROUP_ROWS):
            rows = slice(hh * NA_GROUP + g * GRID_W, hh * NA_GROUP + (g + 1) * GRID_W)
            sb = [s[rows, c * LANES:(c + 1) * LANES]
                  + r.bias[hh, table, g * GRID_W:(g + 1) * GRID_W, c * LANES:(c + 1) * LANES]
                  for c in _na_tiles(table, g)]
            m = jnp.max(functools.reduce(jnp.maximum, sb), axis=-1, keepdims=True)
            p = [jnp.exp2(x - m) for x in sb]
            inv_l.append(1.0 / jnp.sum(functools.reduce(jnp.add, p), axis=-1, keepdims=True))
            p_tiles.append([x.astype(BF16) for x in p])
    return p_tiles, inv_l


def _na_store_output(r, u, o):
    first_head = lax.broadcasted_iota(jnp.int32, (NA_GROUP, LANES), 1) < NA_HEAD_DIM
    r.o[_na_rows(u), :] = jnp.where(first_head, o[:NA_GROUP], o[NA_GROUP:]).astype(r.o.dtype)


def _na_edge_group(r, u, table):
    tiles = _na_tiles(table, 0)
    p_tiles, inv_l = _na_softmax(r, _na_scores(r, u), table)
    p = jnp.concatenate([jnp.concatenate(t, axis=1) for t in p_tiles], axis=0)
    keys = pl.ds(_na_key0(r, u) + tiles.start * LANES, len(tiles) * LANES)
    _na_store_output(r, u, _dot(p, r.v[keys, :]) * jnp.concatenate(inv_l, axis=0))


def _na_interior_probs(r, s):
    p_tiles, inv_l = _na_softmax(r, s, 1)
    zero_tile = jnp.zeros((GRID_W, LANES), BF16)
    p_rows = []
    for block, tiles in enumerate(p_tiles):
        valid = _na_tiles(1, block % NA_GROUP_ROWS)
        p_rows.append(jnp.concatenate(
            [zero_tile] * valid.start + tiles + [zero_tile] * (NA_KEYS // LANES - valid.stop), axis=1))
    return jnp.concatenate(p_rows, axis=0), jnp.concatenate(inv_l, axis=0)


def _na_interior_output(r, u, p, inv_l):
    _na_store_output(r, u, _dot(p, r.v[pl.ds(_na_key0(r, u), NA_KEYS), :]) * inv_l)


def _na_stage(r, u, carry):
    s_cur, p_prev, l_prev = carry
    s_next = _na_scores(r, jnp.minimum(u + 1, r.n_groups - 2))
    p_cur, l_cur = _na_interior_probs(r, s_cur)
    _na_interior_output(r, u - 1, p_prev, l_prev)
    return s_next, p_cur, l_cur


def _na_kernel(q_ref, kt_ref, v_ref, rows_ref, o_ref, bias_ref, s_ref, p_ref, l_ref, *, n_groups):
    @pl.when(pl.program_id(1) == 0)
    def _():
        _na_build_tables(rows_ref, bias_ref)

    r = _NaRefs(q_ref, kt_ref, v_ref, bias_ref, o_ref, n_groups)
    last = n_groups - 1
    _na_edge_group(r, 0, 0)
    _na_edge_group(r, last, 2)

    def load_carry():
        return s_ref[...], p_ref[...], l_ref[...]

    def store_carry(carry):
        s_ref[...], p_ref[...], l_ref[...] = carry

    def stages(first, count, carry):
        for k in range(count):
            carry = _na_stage(r, first + k, carry)
        return carry

    def unrolled_stages(i, c):
        store_carry(stages(2 + NA_UNROLL * i, NA_UNROLL, load_carry()))
        return c

    store_carry((_na_scores(r, 2),) + _na_interior_probs(r, _na_scores(r, 1)))
    n_loops, n_tail = divmod(n_groups - 3, NA_UNROLL)
    lax.fori_loop(0, n_loops, unrolled_stages, 0)
    _, p_prev, l_prev = stages(2 + NA_UNROLL * n_loops, n_tail, load_carry())
    _na_interior_output(r, last - 1, p_prev, l_prev)


def _na(qa, kat, va, rows):
    B, T, _ = qa.shape
    n_pairs = NA_HEADS // 2
    return pl.pallas_call(
        functools.partial(_na_kernel, n_groups=T // NA_GROUP),
        grid=(n_pairs, B),
        in_specs=[
            pl.BlockSpec((None, T, LANES), lambda p, b: (b, 0, p)),
            pl.BlockSpec((None, LANES, T), lambda p, b: (b, p, 0)),
            pl.BlockSpec((None, T, LANES), lambda p, b: (b, 0, p)),
            pl.BlockSpec((2,) + rows.shape[1:], lambda p, b: (p, 0, 0)),
        ],
        out_specs=pl.BlockSpec((None, T, LANES), lambda p, b: (b, 0, p)),
        out_shape=jax.ShapeDtypeStruct((B, T, NA_WIDTH), BF16),
        scratch_shapes=[pltpu.VMEM((2, 3, NA_GROUP, NA_KEYS), F32),
                        pltpu.VMEM((2 * NA_GROUP, NA_KEYS), F32),
                        pltpu.VMEM((2 * NA_GROUP, NA_KEYS), BF16),
                        pltpu.VMEM((2 * NA_GROUP, 1), F32)],
        compiler_params=pltpu.CompilerParams(
            dimension_semantics=("parallel", "arbitrary"), vmem_limit_bytes=VMEM_LIMIT),
        name="natten",
    )(qa, kat, va, rows)


class _GlaDir:
    def __init__(self, reverse, ops, v, dec, o, state, a, contrib, prev):
        self.reverse = reverse
        self.qd, self.ki, self.kd = (ops.at[:, i * GLA_KW:(i + 1) * GLA_KW] for i in range(3))
        self.v, self.dec, self.o, self.state = v, dec, o, state
        self.a, self.contrib, self.prev = a, contrib, prev


def _gla_pair_lanes(h):
    lane = lax.broadcasted_iota(jnp.int32, (GLA_CHUNK, LANES), 1)
    return (lane < GLA_DK) if h % 2 == 0 else (lane >= GLA_DK)


def _gla_intra(d, c):
    C = GLA_CHUNK
    rows = slice(c * C, (c + 1) * C)
    qd, ki = d.qd[rows, :], d.ki[rows, :]
    lane = lax.broadcasted_iota(jnp.int32, qd.shape, 1)
    zeros = jnp.zeros_like(ki)
    i = lax.broadcasted_iota(jnp.int32, (2 * C, LANES), 0) & (C - 1)
    col = lax.broadcasted_iota(jnp.int32, (2 * C, LANES), 1)
    for parity in range(2):
        q_heads = jnp.concatenate(
            [jnp.where((lane >= h * GLA_DK) & (lane < (h + 1) * GLA_DK), qd, jnp.zeros_like(qd))
             for h in range(parity, GLA_HEADS, 2)], axis=0)
        keys = jnp.concatenate([zeros, ki] if parity == 0 else [ki, zeros], axis=0)
        scores = _dot_nt(q_heads, keys)
        j = col - C if parity == 0 else col
        in_chunk = (j >= 0) & (j < C)
        keep = in_chunk & ((j > i) if d.reverse else (j <= i))
        d.a[c, parity] = jnp.where(keep, scores, 0.0).astype(BF16)
    for p in range(GLA_HEADS // 2):
        kd_p = d.kd[rows, p * LANES:(p + 1) * LANES]
        v_p = d.v[rows, 2 * p * GLA_DV:(2 * p + 2) * GLA_DV]
        both = _dot_tn(kd_p, v_p)
        d.contrib[c, 2 * p] = both[:GLA_DK, :GLA_DV]
        d.contrib[c, 2 * p + 1] = both[GLA_DK:, GLA_DV:]


def _gla_scan(d, n_chunks):
    order = range(n_chunks - 1, -1, -1) if d.reverse else range(n_chunks)
    pad = jnp.zeros((LANES - n_chunks, GLA_KW), F32)
    dec_t = jnp.concatenate([d.dec[c] for c in range(n_chunks)] + [pad], axis=0).T
    for h in range(GLA_HEADS):
        st = d.state[h]
        for c in order:
            d.prev[c, h] = st.astype(BF16)
            st = st * dec_t[h * GLA_DK:(h + 1) * GLA_DK, c:c + 1] + d.contrib[c, h]
        d.state[h] = st


def _gla_outputs(d, c):
    C = GLA_CHUNK
    rows = slice(c * C, (c + 1) * C)
    for h in range(GLA_HEADS):
        p, parity = divmod(h, 2)
        qd_p = d.qd[rows, p * LANES:(p + 1) * LANES]
        scores = d.a[c, parity, p * C:(p + 1) * C, :]
        lhs = jnp.where(_gla_pair_lanes(h), qd_p, scores)
        cols = slice(h * GLA_DV, (h + 1) * GLA_DV)
        operands = [d.prev[c, h], d.v[rows, cols]]
        rhs = jnp.concatenate(operands if parity == 0 else operands[::-1], axis=0)
        d.o[rows, cols] = _dot(lhs, rhs).astype(d.o.dtype)


def _gla_kernel(gf_ref, vf_ref, decf_ref, gb_ref, vb_ref, decb_ref,
                of_ref, ob_ref, sf_ref, sb_ref, *scratch, n_chunks):
    @pl.when(pl.program_id(1) == 0)
    def _():
        sf_ref[...] = jnp.zeros_like(sf_ref)
        sb_ref[...] = jnp.zeros_like(sb_ref)

    n = len(scratch) // 2
    dirs = (_GlaDir(False, gf_ref, vf_ref, decf_ref, of_ref, sf_ref, *scratch[:n]),
            _GlaDir(True, gb_ref, vb_ref, decb_ref, ob_ref, sb_ref, *scratch[n:]))

    for c in range(n_chunks):
        for d in dirs:
            _gla_intra(d, c)
    for d in dirs:
        _gla_scan(d, n_chunks)
    for c in range(n_chunks):
        for d in dirs:
            _gla_outputs(d, c)


def _gla(gf, gb, vg, dec, *, tb):
    B, T, vw = vg.shape
    nb = T // tb
    nc = tb // GLA_CHUNK
    kw = GLA_KW
    n_pairs = GLA_HEADS // 2
    fwd = lambda w: pl.BlockSpec((None, tb, w), lambda b, s: (b, s, 0))
    bwd = lambda w: pl.BlockSpec((None, tb, w), lambda b, s: (b, nb - 1 - s, 0))
    dec_fwd = pl.BlockSpec((None, nc, 1, kw), lambda b, s: (b, s, 0, 0))
    dec_bwd = pl.BlockSpec((None, nc, 1, kw), lambda b, s: (b, nb - 1 - s, 0, 1))
    state = pltpu.VMEM((GLA_HEADS, GLA_DK, GLA_DV), F32)
    per_dir = [
        pltpu.VMEM((nc, 2, n_pairs * GLA_CHUNK, 2 * GLA_CHUNK), BF16),
        pltpu.VMEM((nc, GLA_HEADS, GLA_DK, GLA_DV), F32),
        pltpu.VMEM((nc, GLA_HEADS, GLA_DK, GLA_DV), BF16)]
    return pl.pallas_call(
        functools.partial(_gla_kernel, n_chunks=nc),
        grid=(B, nb),
        in_specs=[fwd(3 * kw), fwd(vw), dec_fwd, bwd(3 * kw), bwd(vw), dec_bwd],
        out_specs=[fwd(vw), bwd(vw)],
        out_shape=[jax.ShapeDtypeStruct((B, T, vw), BF16)] * 2,
        scratch_shapes=[state, state] + per_dir * 2,
        compiler_params=pltpu.CompilerParams(
            dimension_semantics=("parallel", "arbitrary"), vmem_limit_bytes=VMEM_LIMIT),
        name="gla",
    )(gf, vg, dec, gb, vg, dec)


MIX_SUBTILES = 2
STAGE_ROWS, STAGE_COLS = 512, 1024


class _WeightStager:
    def __init__(self, tiles, stage, sem):
        self.tiles, self.stage, self.sem, self.done = tiles, stage, sem, 0
        for i in range(min(2, len(tiles))):
            self._copy(i).start()

    def _copy(self, i):
        src, _, r, c = self.tiles[i]
        return pltpu.make_async_copy(src.at[pl.ds(r, STAGE_ROWS), pl.ds(c, STAGE_COLS)],
                                     self.stage.at[i % 2], self.sem.at[i % 2])

    def need(self, count):
        while self.done < count:
            i = self.done
            _, dst, r, c = self.tiles[i]
            self._copy(i).wait()
            dst[r:r + STAGE_ROWS, c:c + STAGE_COLS] = self.stage[i % 2].astype(BF16)
            if i + 2 < len(self.tiles):
                self._copy(i + 2).start()
            self.done += 1


def _mix_ffn_body(x_ref, yna_ref, of_ref, ob_ref, sr_ref, gng_ref, gff_ref, gfin_ref, out_ref,
                  wout_ref, w1_ref, w2_ref, ready, *, final_norm, ff_chunk):
    na_w = yna_ref.shape[-1]
    tiles_per = lambda ref: (ref.shape[0] // STAGE_ROWS) * (ref.shape[1] // STAGE_COLS)
    n_chunks = w1_ref.shape[1] // ff_chunk
    sub = x_ref.shape[0] // MIX_SUBTILES
    row_sets = [slice(i * sub, (i + 1) * sub) for i in range(MIX_SUBTILES)]
    h1s, n2s = [], []
    for rows in row_sets:
        o = of_ref[rows, :].astype(F32) + ob_ref[rows, :].astype(F32)
        parts = [_rms(o[:, h * GLA_DV:(h + 1) * GLA_DV], gng_ref[...]) for h in range(GLA_HEADS)]
        y_gla = (jnp.concatenate(parts, axis=-1) * sr_ref[rows, :].astype(F32)).astype(BF16)
        ready(tiles_per(wout_ref))
        mix = _dot(yna_ref[rows, :], wout_ref[0:na_w, :]) + _dot(y_gla, wout_ref[na_w:, :])
        h1 = x_ref[rows, :] + mix
        h1s.append(h1)
        n2s.append(_rms(h1, gff_ref[...]).astype(BF16))
    for rows, h1, n2 in zip(row_sets, h1s, n2s):
        n_tiles = tiles_per(wout_ref)
        ffn = None
        for f in range(n_chunks):
            n_tiles += tiles_per(w1_ref) // n_chunks
            ready(n_tiles)
            u = _dot(n2, w1_ref[:, f * ff_chunk:(f + 1) * ff_chunk])
            a = jnp.square(jnp.maximum(u, 0.0)).astype(BF16)
            n_tiles += tiles_per(w2_ref) // n_chunks
            ready(n_tiles)
            d = _dot(a, w2_ref[f * ff_chunk:(f + 1) * ff_chunk, :])
            ffn = d if ffn is None else ffn + d
        h2 = h1 + ffn
        out_ref[rows, :] = _rms(h2, gfin_ref[...]) if final_norm else h2


def _mix_ffn_kernel(x_ref, yna_ref, of_ref, ob_ref, sr_ref, gng_ref, wout_hbm, gff_ref,
                    w1_hbm, w2_hbm, gfin_ref, out_ref, wout_ref, w1_ref, w2_ref, stage, sem,
                    *, final_norm, ff_chunk):
    body = functools.partial(
        _mix_ffn_body, x_ref, yna_ref, of_ref, ob_ref, sr_ref, gng_ref, gff_ref, gfin_ref, out_ref,
        wout_ref, w1_ref, w2_ref, final_norm=final_norm, ff_chunk=ff_chunk)
    first = (pl.program_id(0) == 0) & (pl.program_id(1) == 0)

    @pl.when(first)
    def _():
        def tiles(src, dst, rows, cols):
            return [(src, dst, r, c) for r in range(rows.start, rows.stop, STAGE_ROWS)
                    for c in range(cols.start, cols.stop, STAGE_COLS)]

        order = tiles(wout_hbm, wout_ref, slice(0, wout_ref.shape[0]), slice(0, wout_ref.shape[1]))
        for f in range(w1_ref.shape[1] // ff_chunk):
            chunk = slice(f * ff_chunk, (f + 1) * ff_chunk)
            order += tiles(w1_hbm, w1_ref, slice(0, w1_ref.shape[0]), chunk)
            order += tiles(w2_hbm, w2_ref, chunk, slice(0, w2_ref.shape[1]))
        stager = _WeightStager(order, stage, sem)
        body(stager.need)
        stager.need(len(order))

    @pl.when(jnp.logical_not(first))
    def _():
        body(lambda n: None)


def _mix_ffn(x, yna, of, ob, sr, gng, wout, gff, w1, w2, gfin, *, tm, final_norm):
    B, T, D = x.shape
    tok = lambda w: pl.BlockSpec((None, tm, w), lambda b, i: (b, i, 0))
    const = lambda a: pl.BlockSpec(a.shape, lambda b, i: (0,) * a.ndim,
                                   pipeline_mode=pl.Buffered(1))
    hbm = pl.BlockSpec(memory_space=pl.ANY)
    for w in (wout, w1, w2):
        assert w.dtype == F32 and w.shape[0] % STAGE_ROWS == 0 and w.shape[1] % STAGE_COLS == 0
    assert w1.shape[1] % 1024 == 0 and 1024 % STAGE_ROWS == 0 and 1024 % STAGE_COLS == 0
    return pl.pallas_call(
        functools.partial(_mix_ffn_kernel, final_norm=final_norm, ff_chunk=1024),
        grid=(B, T // tm),
        in_specs=[tok(D), tok(yna.shape[-1]), tok(of.shape[-1]), tok(ob.shape[-1]),
                  tok(sr.shape[-1]), const(gng), hbm, const(gff), hbm, hbm, const(gfin)],
        out_specs=tok(D),
        out_shape=jax.ShapeDtypeStruct((B, T, D), F32),
        scratch_shapes=[pltpu.VMEM(w.shape, BF16) for w in (wout, w1, w2)] + [
            pltpu.VMEM((2, STAGE_ROWS, STAGE_COLS), F32), pltpu.SemaphoreType.DMA((2,))],
        compiler_params=pltpu.CompilerParams(
            dimension_semantics=("arbitrary", "arbitrary"), vmem_limit_bytes=VMEM_LIMIT),
        name="mix_ffn",
    )(x, yna, of, ob, sr, gng, wout, gff, w1, w2, gfin)


def kernel(x, ln_mix_g, w_in, na_rpb, gla_gate_up_fwd, gla_gate_bias_fwd, gla_gate_up_bwd,
           gla_gate_bias_bwd, gla_norm_g, w_out, ln_ff_g, w_ff1, w_ff2, ln_final_g):
    B, T, D = x.shape
    depth = w_in.shape[0]
    assert T % TOKEN_TILE == 0 and T // NA_GROUP >= 4
    gla_block = GLA_BLOCK if T % GLA_BLOCK == 0 else TOKEN_TILE
    row = lambda v: v.reshape(1, -1).astype(F32)
    zeros = jnp.zeros((GLA_GATE_RANK, GLA_KW), F32)
    h = x
    for l in range(depth):
        gu = jnp.concatenate([
            jnp.concatenate([gla_gate_up_fwd[l], zeros], axis=1),
            jnp.concatenate([zeros, gla_gate_up_bwd[l]], axis=1)], axis=0).astype(BF16)
        gb = jnp.concatenate([gla_gate_bias_fwd[l], gla_gate_bias_bwd[l]]).reshape(1, -1)
        qa, kat, va, vg, sr, gla_f, gla_b, dec = _inproj(
            h, row(ln_mix_g[l]), w_in[l].T, gu, gb.astype(F32), tm=TOKEN_TILE)
        y_na = _na(qa, kat, va, _na_bias_rows(na_rpb[l]))
        o_f, o_b = _gla(gla_f, gla_b, vg, dec, tb=gla_block)
        h = _mix_ffn(h, y_na, o_f, o_b, sr, row(gla_norm_g[l]), w_out[l], row(ln_ff_g[l]),
                     w_ff1[l], w_ff2[l], row(ln_final_g), tm=TOKEN_TILE,
                     final_norm=(l == depth - 1))
    return h
```

```python
import functools
import itertools

import jax
import jax.numpy as jnp
from jax import lax
from jax.experimental import pallas as pl
from jax.experimental.pallas import tpu as pltpu

F32 = jnp.float32
BF16 = jnp.bfloat16

EPS = 1e-6
GRID_W = 64
NA_HEADS = 8
NA_HEAD_DIM = 64
NA_KH = 8
NA_KW = 16
GLA_HEADS = 4
GLA_DK = 64
GLA_DV = 128
GLA_GATE_RANK = 16
GLA_GATE_NORM = 16.0
GLA_CHUNK = 64

NA_WIDTH = NA_HEADS * NA_HEAD_DIM
GLA_KW = GLA_HEADS * GLA_DK
GLA_VW = GLA_HEADS * GLA_DV
COLS, _start = {}, 0
for _name, _width in (("qa", NA_WIDTH), ("ka", NA_WIDTH), ("va", NA_WIDTH), ("qg", GLA_KW),
                      ("kg", GLA_KW), ("vg", GLA_VW), ("rg", GLA_VW), ("z", 2 * GLA_GATE_RANK)):
    COLS[_name] = slice(_start, _start + _width)
    _start += _width
D_IN = _start

LANES = 128
INPROJ_SUBTILES = 2
TOKEN_TILE = 512
GLA_BLOCK = 1024
NA_GROUP_ROWS = 4
NA_GROUP = NA_GROUP_ROWS * GRID_W
NA_KEY_ROWS = 12
NA_KEYS = NA_KEY_ROWS * GRID_W
NA_UNROLL = 4
LOG2E = 1.4426950408889634
VMEM_LIMIT = 56 * 1024 * 1024


def _dot(a, b):
    return jnp.dot(a, b, preferred_element_type=F32)


def _dot_nt(a, b):
    return lax.dot_general(a, b, (((1,), (1,)), ((), ())), preferred_element_type=F32)


def _dot_tn(a, b):
    return lax.dot_general(a, b, (((0,), (0,)), ((), ())), preferred_element_type=F32)


def _rms(x, g):
    return x * lax.rsqrt(jnp.mean(x * x, axis=-1, keepdims=True) + EPS) * g


def _segmented_cumsum(x, reverse):
    n = x.shape[0]
    pos = lax.broadcasted_iota(jnp.int32, x.shape, 0) & (GLA_CHUNK - 1)
    step = 1
    while step < GLA_CHUNK:
        if reverse:
            shifted, ok = pltpu.roll(x, n - step, 0), pos < GLA_CHUNK - step
        else:
            shifted, ok = pltpu.roll(x, step, 0), pos >= step
        x = x + jnp.where(ok, shifted, 0.0)
        step *= 2
    return x


def _gla_cumdecay(log_a, reverse):
    b = _segmented_cumsum(log_a, reverse)
    b3 = b.reshape(-1, GLA_CHUNK, b.shape[-1])
    last = 0 if reverse else GLA_CHUNK - 1
    return b, b3[:, last:last + 1, :]


def _gla_operands(q, k, b, b_last):
    b3 = b.reshape(-1, GLA_CHUNK, b.shape[-1])
    k_dec = (k.reshape(b3.shape) * jnp.exp2(b_last - b3)).reshape(b.shape)
    ops = jnp.concatenate([q * jnp.exp2(b), k * jnp.exp2(-b), k_dec], axis=-1)
    return ops.astype(BF16), jnp.exp2(b_last)


def _inproj_kernel(x_ref, g_ref, wt_ref, gu_ref, gb_ref,
                   qa_ref, kat_ref, va_ref, vg_ref, sr_ref, gf_ref, gbw_ref, dec_ref, wb_ref):
    @pl.when((pl.program_id(0) == 0) & (pl.program_id(1) == 0))
    def _():
        wb_ref[...] = wt_ref[...].astype(BF16)

    def chain(rows):
        n = _rms(x_ref[rows, :], g_ref[...]).astype(BF16)
        chunks = slice(rows.start // GLA_CHUNK, rows.stop // GLA_CHUNK)

        def proj(name):
            return _dot_nt(n, wb_ref[COLS[name], :])

        def log_decay(cols):
            pre = _dot(z, gu_ref[:, cols]) + gb_ref[:, cols]
            softplus2 = jnp.log2(1.0 + jnp.exp2(jnp.abs(pre) * -LOG2E))
            return jnp.minimum(pre * (LOG2E / GLA_GATE_NORM), 0.0) - softplus2 * (1.0 / GLA_GATE_NORM)

        z = proj("z").astype(BF16)
        yield
        b_f, last_f = _gla_cumdecay(log_decay(slice(0, GLA_KW)), False)
        qa_ref[rows, :] = (proj("qa") * (NA_HEAD_DIM ** -0.5 * LOG2E)).astype(BF16)
        yield
        b_b, last_b = _gla_cumdecay(log_decay(slice(GLA_KW, 2 * GLA_KW)), True)
        kat_ref[:, rows] = _dot_nt(wb_ref[COLS["ka"], :], n).astype(BF16)
        yield
        qg = proj("qg") * (GLA_DK ** -0.5)
        kg = proj("kg")
        yield
        gf_ref[rows, :], dec_f = _gla_operands(qg, kg, b_f, last_f)
        va_ref[rows, :] = proj("va").astype(BF16)
        yield
        gbw_ref[rows, :], dec_b = _gla_operands(qg, kg, b_b, last_b)
        dec_ref[chunks] = jnp.concatenate([dec_f, dec_b], axis=-1)
        vg_ref[rows, :] = proj("vg").astype(BF16)
        yield
        r = proj("rg")
        sr_ref[rows, :] = (r * jax.nn.sigmoid(r)).astype(BF16)

    sub = x_ref.shape[0] // INPROJ_SUBTILES
    chains = [chain(slice(i * sub, (i + 1) * sub)) for i in range(INPROJ_SUBTILES)]
    for _ in itertools.zip_longest(*chains):
        pass


def _inproj(x, g, wt, gu, gb, *, tm):
    B, T, D = x.shape
    assert wt.shape == (D_IN, D)
    nc = tm // GLA_CHUNK
    tok = lambda w: pl.BlockSpec((None, tm, w), lambda b, i: (b, i, 0))
    const = lambda a: pl.BlockSpec(a.shape, lambda b, i: (0,) * a.ndim,
                                   pipeline_mode=pl.Buffered(1))
    tok_out = lambda w: (jax.ShapeDtypeStruct((B, T, w), BF16), tok(w))
    outs = [
        tok_out(NA_WIDTH),
        (jax.ShapeDtypeStruct((B, NA_WIDTH, T), BF16),
         pl.BlockSpec((None, NA_WIDTH, tm), lambda b, i: (b, 0, i))),
        tok_out(NA_WIDTH),
        tok_out(GLA_VW),
        tok_out(GLA_VW),
        tok_out(3 * GLA_KW),
        tok_out(3 * GLA_KW),
        (jax.ShapeDtypeStruct((B, T // GLA_CHUNK, 1, 2 * GLA_KW), F32),
         pl.BlockSpec((None, nc, 1, 2 * GLA_KW), lambda b, i: (b, i, 0, 0))),
    ]
    return pl.pallas_call(
        _inproj_kernel,
        grid=(B, T // tm),
        in_specs=[tok(D), const(g), const(wt), const(gu), const(gb)],
        out_specs=[s for _, s in outs],
        out_shape=[s for s, _ in outs],
        scratch_shapes=[pltpu.VMEM(wt.shape, BF16)],
        compiler_params=pltpu.CompilerParams(
            dimension_semantics=("arbitrary", "arbitrary"), vmem_limit_bytes=VMEM_LIMIT),
        name="inproj",
    )(x, g, wt, gu, gb)


def _na_bias_rows(rpb):
    n_dc = 2 * NA_KW - 1
    padded = jnp.pad(rpb.astype(F32), ((0, 0), (4, 5), (0, GRID_W - n_dc)))
    return jnp.concatenate([padded[:, :-1], padded[:, 1:]], axis=-1)


def _na_tiles(table, g):
    i_start = (0, g, NA_KEY_ROWS - NA_KH)[table]
    return range(i_start // 2, (i_start + NA_KH + 1) // 2)


def _na_build_tables(rows_ref, tab_ref):
    qc = lax.broadcasted_iota(jnp.int32, (GRID_W, LANES), 0)
    lane = lax.broadcasted_iota(jnp.int32, (GRID_W, LANES), 1)
    kc = lane & (GRID_W - 1)
    col_start = jnp.clip(qc - NA_KW // 2, 0, GRID_W - NA_KW)
    in_win = (kc >= col_start) & (kc < col_start + NA_KW)
    valid = {(True, True): in_win,
             (True, False): in_win & (lane < GRID_W),
             (False, True): in_win & (lane >= GRID_W)}
    row_offset = (0, -(NA_KH // 2), -NA_KH)
    for hh in range(2):
        for t in range(3):
            for g in range(NA_GROUP_ROWS):
                i_start = (0, g, NA_KEY_ROWS - NA_KH)[t]
                for m in _na_tiles(t, g):
                    halves = tuple(i_start <= i < i_start + NA_KH for i in (2 * m, 2 * m + 1))
                    a = 2 * m - g + row_offset[t] + NA_KH - 1 + 4
                    src = jnp.broadcast_to(rows_ref[hh, a:a + 1, :], (GRID_W, LANES))
                    rot = pltpu.roll(src, LANES - (NA_KW - 1), 1, stride=1, stride_axis=0)
                    tab_ref[hh, t, g * GRID_W:(g + 1) * GRID_W, m * LANES:(m + 1) * LANES] = (
                        jnp.where(valid[halves], rot * LOG2E, -jnp.inf))


class _NaRefs:
    def __init__(self, q, kt, v, bias, o, n_groups):
        self.q, self.kt, self.v, self.bias, self.o, self.n_groups = q, kt, v, bias, o, n_groups


def _na_rows(u):
    return pl.ds(pl.multiple_of(u * NA_GROUP, NA_GROUP), NA_GROUP)


def _na_key0(r, u):
    return pl.multiple_of(jnp.clip(u - 1, 0, r.n_groups - 3) * NA_GROUP, NA_GROUP)


def _na_scores(r, u):
    qq = r.q[_na_rows(u), :]
    first_head = lax.broadcasted_iota(jnp.int32, qq.shape, 1) < NA_HEAD_DIM
    zero = jnp.zeros_like(qq)
    q_heads = jnp.concatenate([jnp.where(first_head, qq, zero),
                               jnp.where(first_head, zero, qq)], axis=0)
    return _dot(q_heads, r.kt[:, pl.ds(_na_key0(r, u), NA_KEYS)])


def _na_softmax(r, s, table):
    p_tiles, inv_l = [], []
    for hh in range(2):
        for g in range(NA_GROUP_ROWS):
            rows = slice(hh * NA_GROUP + g * GRID_W, hh * NA_GROUP + (g + 1) * GRID_W)
            sb = [s[rows, c * LANES:(c + 1) * LANES]
                  + r.bias[hh, table, g * GRID_W:(g + 1) * GRID_W, c * LANES:(c + 1) * LANES]
                  for c in _na_tiles(table, g)]
            m = jnp.max(functools.reduce(jnp.maximum, sb), axis=-1, keepdims=True)
            p = [jnp.exp2(x - m) for x in sb]
            inv_l.append(1.0 / jnp.sum(functools.reduce(jnp.add, p), axis=-1, keepdims=True))
            p_tiles.append([x.astype(BF16) for x in p])
    return p_tiles, inv_l


def _na_store_output(r, u, o):
    first_head = lax.broadcasted_iota(jnp.int32, (NA_GROUP, LANES), 1) < NA_HEAD_DIM
    r.o[_na_rows(u), :] = jnp.where(first_head, o[:NA_GROUP], o[NA_GROUP:]).astype(r.o.dtype)


def _na_edge_group(r, u, table):
    tiles = _na_tiles(table, 0)
    p_tiles, inv_l = _na_softmax(r, _na_scores(r, u), table)
    p = jnp.concatenate([jnp.concatenate(t, axis=1) for t in p_tiles], axis=0)
    keys = pl.ds(_na_key0(r, u) + tiles.start * LANES, len(tiles) * LANES)
    _na_store_output(r, u, _dot(p, r.v[keys, :]) * jnp.concatenate(inv_l, axis=0))


def _na_interior_probs(r, s):
    p_tiles, inv_l = _na_softmax(r, s, 1)
    zero_tile = jnp.zeros((GRID_W, LANES), BF16)
    p_rows = []
    for block, tiles in enumerate(p_tiles):
        valid = _na_tiles(1, block % NA_GROUP_ROWS)
        p_rows.append(jnp.concatenate(
            [zero_tile] * valid.start + tiles + [zero_tile] * (NA_KEYS // LANES - valid.stop), axis=1))
    return jnp.concatenate(p_rows, axis=0), jnp.concatenate(inv_l, axis=0)


def _na_interior_output(r, u, p, inv_l):
    _na_store_output(r, u, _dot(p, r.v[pl.ds(_na_key0(r, u), NA_KEYS), :]) * inv_l)


def _na_stage(r, u, carry):
    s_cur, p_prev, l_prev = carry
    s_next = _na_scores(r, jnp.minimum(u + 1, r.n_groups - 2))
    p_cur, l_cur = _na_interior_probs(r, s_cur)
    _na_interior_output(r, u - 1, p_prev, l_prev)
    return s_next, p_cur, l_cur


def _na_kernel(q_ref, kt_ref, v_ref, rows_ref, o_ref, bias_ref, s_ref, p_ref, l_ref, *, n_groups):
    @pl.when(pl.program_id(1) == 0)
    def _():
        _na_build_tables(rows_ref, bias_ref)

    r = _NaRefs(q_ref, kt_ref, v_ref, bias_ref, o_ref, n_groups)
    last = n_groups - 1
    _na_edge_group(r, 0, 0)
    _na_edge_group(r, last, 2)

    def load_carry():
        return s_ref[...], p_ref[...], l_ref[...]

    def store_carry(carry):
        s_ref[...], p_ref[...], l_ref[...] = carry

    def stages(first, count, carry):
        for k in range(count):
            carry = _na_stage(r, first + k, carry)
        return carry

    def unrolled_stages(i, c):
        store_carry(stages(2 + NA_UNROLL * i, NA_UNROLL, load_carry()))
        return c

    store_carry((_na_scores(r, 2),) + _na_interior_probs(r, _na_scores(r, 1)))
    n_loops, n_tail = divmod(n_groups - 3, NA_UNROLL)
    lax.fori_loop(0, n_loops, unrolled_stages, 0)
    _, p_prev, l_prev = stages(2 + NA_UNROLL * n_loops, n_tail, load_carry())
    _na_interior_output(r, last - 1, p_prev, l_prev)


def _na(qa, kat, va, rows):
    B, T, _ = qa.shape
    n_pairs = NA_HEADS // 2
    return pl.pallas_call(
        functools.partial(_na_kernel, n_groups=T // NA_GROUP),
        grid=(n_pairs, B),
        in_specs=[
            pl.BlockSpec((None, T, LANES), lambda p, b: (b, 0, p)),
            pl.BlockSpec((None, LANES, T), lambda p, b: (b, p, 0)),
            pl.BlockSpec((None, T, LANES), lambda p, b: (b, 0, p)),
            pl.BlockSpec((2,) + rows.shape[1:], lambda p, b: (p, 0, 0)),
        ],
        out_specs=pl.BlockSpec((None, T, LANES), lambda p, b: (b, 0, p)),
        out_shape=jax.ShapeDtypeStruct((B, T, NA_WIDTH), BF16),
        scratch_shapes=[pltpu.VMEM((2, 3, NA_GROUP, NA_KEYS), F32),
                        pltpu.VMEM((2 * NA_GROUP, NA_KEYS), F32),
                        pltpu.VMEM((2 * NA_GROUP, NA_KEYS), BF16),
                        pltpu.VMEM((2 * NA_GROUP, 1), F32)],
        compiler_params=pltpu.CompilerParams(
            dimension_semantics=("parallel", "arbitrary"), vmem_limit_bytes=VMEM_LIMIT),
        name="natten",
    )(qa, kat, va, rows)


class _GlaDir:
    def __init__(self, reverse, ops, v, dec, o, state, a, contrib, prev):
        self.reverse = reverse
        self.qd, self.ki, self.kd = (ops.at[:, i * GLA_KW:(i + 1) * GLA_KW] for i in range(3))
        self.v, self.dec, self.o, self.state = v, dec, o, state
        self.a, self.contrib, self.prev = a, contrib, prev


def _gla_pair_lanes(h):
    lane = lax.broadcasted_iota(jnp.int32, (GLA_CHUNK, LANES), 1)
    return (lane < GLA_DK) if h % 2 == 0 else (lane >= GLA_DK)


def _gla_intra(d, c):
    C = GLA_CHUNK
    rows = slice(c * C, (c + 1) * C)
    qd, ki = d.qd[rows, :], d.ki[rows, :]
    lane = lax.broadcasted_iota(jnp.int32, qd.shape, 1)
    zeros = jnp.zeros_like(ki)
    i = lax.broadcasted_iota(jnp.int32, (2 * C, LANES), 0) & (C - 1)
    col = lax.broadcasted_iota(jnp.int32, (2 * C, LANES), 1)
    for parity in range(2):
        q_heads = jnp.concatenate(
            [jnp.where((lane >= h * GLA_DK) & (lane < (h + 1) * GLA_DK), qd, jnp.zeros_like(qd))
             for h in range(parity, GLA_HEADS, 2)], axis=0)
        keys = jnp.concatenate([zeros, ki] if parity == 0 else [ki, zeros], axis=0)
        scores = _dot_nt(q_heads, keys)
        j = col - C if parity == 0 else col
        in_chunk = (j >= 0) & (j < C)
        keep = in_chunk & ((j > i) if d.reverse else (j <= i))
        d.a[c, parity] = jnp.where(keep, scores, 0.0).astype(BF16)
    for p in range(GLA_HEADS // 2):
        kd_p = d.kd[rows, p * LANES:(p + 1) * LANES]
        v_p = d.v[rows, 2 * p * GLA_DV:(2 * p + 2) * GLA_DV]
        both = _dot_tn(kd_p, v_p)
        d.contrib[c, 2 * p] = both[:GLA_DK, :GLA_DV]
        d.contrib[c, 2 * p + 1] = both[GLA_DK:, GLA_DV:]


def _gla_scan(d, n_chunks):
    order = range(n_chunks - 1, -1, -1) if d.reverse else range(n_chunks)
    pad = jnp.zeros((LANES - n_chunks, GLA_KW), F32)
    dec_t = jnp.concatenate([d.dec[c] for c in range(n_chunks)] + [pad], axis=0).T
    for h in range(GLA_HEADS):
        st = d.state[h]
        for c in order:
            d.prev[c, h] = st.astype(BF16)
            st = st * dec_t[h * GLA_DK:(h + 1) * GLA_DK, c:c + 1] + d.contrib[c, h]
        d.state[h] = st


def _gla_outputs(d, c):
    C = GLA_CHUNK
    rows = slice(c * C, (c + 1) * C)
    for h in range(GLA_HEADS):
        p, parity = divmod(h, 2)
        qd_p = d.qd[rows, p * LANES:(p + 1) * LANES]
        scores = d.a[c, parity, p * C:(p + 1) * C, :]
        lhs = jnp.where(_gla_pair_lanes(h), qd_p, scores)
        cols = slice(h * GLA_DV, (h + 1) * GLA_DV)
        operands = [d.prev[c, h], d.v[rows, cols]]
        rhs = jnp.concatenate(operands if parity == 0 else operands[::-1], axis=0)
        d.o[rows, cols] = _dot(lhs, rhs).astype(d.o.dtype)


def _gla_kernel(gf_ref, vf_ref, decf_ref, gb_ref, vb_ref, decb_ref,
                of_ref, ob_ref, sf_ref, sb_ref, *scratch, n_chunks):
    @pl.when(pl.program_id(1) == 0)
    def _():
        sf_ref[...] = jnp.zeros_like(sf_ref)
        sb_ref[...] = jnp.zeros_like(sb_ref)

    n = len(scratch) // 2
    dirs = (_GlaDir(False, gf_ref, vf_ref, decf_ref, of_ref, sf_ref, *scratch[:n]),
            _GlaDir(True, gb_ref, vb_ref, decb_ref, ob_ref, sb_ref, *scratch[n:]))

    for c in range(n_chunks):
        for d in dirs:
            _gla_intra(d, c)
    for d in dirs:
        _gla_scan(d, n_chunks)
    for c in range(n_chunks):
        for d in dirs:
            _gla_outputs(d, c)


def _gla(gf, gb, vg, dec, *, tb):
    B, T, vw = vg.shape
    nb = T // tb
    nc = tb // GLA_CHUNK
    kw = GLA_KW
    n_pairs = GLA_HEADS // 2
    fwd = lambda w: pl.BlockSpec((None, tb, w), lambda b, s: (b, s, 0))
    bwd = lambda w: pl.BlockSpec((None, tb, w), lambda b, s: (b, nb - 1 - s, 0))
    dec_fwd = pl.BlockSpec((None, nc, 1, kw), lambda b, s: (b, s, 0, 0))
    dec_bwd = pl.BlockSpec((None, nc, 1, kw), lambda b, s: (b, nb - 1 - s, 0, 1))
    state = pltpu.VMEM((GLA_HEADS, GLA_DK, GLA_DV), F32)
    per_dir = [
        pltpu.VMEM((nc, 2, n_pairs * GLA_CHUNK, 2 * GLA_CHUNK), BF16),
        pltpu.VMEM((nc, GLA_HEADS, GLA_DK, GLA_DV), F32),
        pltpu.VMEM((nc, GLA_HEADS, GLA_DK, GLA_DV), BF16)]
    return pl.pallas_call(
        functools.partial(_gla_kernel, n_chunks=nc),
        grid=(B, nb),
        in_specs=[fwd(3 * kw), fwd(vw), dec_fwd, bwd(3 * kw), bwd(vw), dec_bwd],
        out_specs=[fwd(vw), bwd(vw)],
        out_shape=[jax.ShapeDtypeStruct((B, T, vw), BF16)] * 2,
        scratch_shapes=[state, state] + per_dir * 2,
        compiler_params=pltpu.CompilerParams(
            dimension_semantics=("parallel", "arbitrary"), vmem_limit_bytes=VMEM_LIMIT),
        name="gla",
    )(gf, vg, dec, gb, vg, dec)


MIX_SUBTILES = 2
STAGE_ROWS, STAGE_COLS = 512, 1024


class _WeightStager:
    def __init__(self, tiles, stage, sem):
        self.tiles, self.stage, self.sem, self.done = tiles, stage, sem, 0
        for i in range(min(2, len(tiles))):
            self._copy(i).start()

    def _copy(self, i):
        src, _, r, c = self.tiles[i]
        return pltpu.make_async_copy(src.at[pl.ds(r, STAGE_ROWS), pl.ds(c, STAGE_COLS)],
                                     self.stage.at[i % 2], self.sem.at[i % 2])

    def need(self, count):
        while self.done < count:
            i = self.done
            _, dst, r, c = self.tiles[i]
            self._copy(i).wait()
            dst[r:r + STAGE_ROWS, c:c + STAGE_COLS] = self.stage[i % 2].astype(BF16)
            if i + 2 < len(self.tiles):
                self._copy(i + 2).start()
            self.done += 1


def _mix_ffn_body(x_ref, yna_ref, of_ref, ob_ref, sr_ref, gng_ref, gff_ref, gfin_ref, out_ref,
                  wout_ref, w1_ref, w2_ref, ready, *, final_norm, ff_chunk):
    na_w = yna_ref.shape[-1]
    tiles_per = lambda ref: (ref.shape[0] // STAGE_ROWS) * (ref.shape[1] // STAGE_COLS)
    n_chunks = w1_ref.shape[1] // ff_chunk
    sub = x_ref.shape[0] // MIX_SUBTILES
    row_sets = [slice(i * sub, (i + 1) * sub) for i in range(MIX_SUBTILES)]
    h1s, n2s = [], []
    for rows in row_sets:
        o = of_ref[rows, :].astype(F32) + ob_ref[rows, :].astype(F32)
        parts = [_rms(o[:, h * GLA_DV:(h + 1) * GLA_DV], gng_ref[...]) for h in range(GLA_HEADS)]
        y_gla = (jnp.concatenate(parts, axis=-1) * sr_ref[rows, :].astype(F32)).astype(BF16)
        ready(tiles_per(wout_ref))
        mix = _dot(yna_ref[rows, :], wout_ref[0:na_w, :]) + _dot(y_gla, wout_ref[na_w:, :])
        h1 = x_ref[rows, :] + mix
        h1s.append(h1)
        n2s.append(_rms(h1, gff_ref[...]).astype(BF16))
    for rows, h1, n2 in zip(row_sets, h1s, n2s):
        n_tiles = tiles_per(wout_ref)
        ffn = None
        for f in range(n_chunks):
            n_tiles += tiles_per(w1_ref) // n_chunks
            ready(n_tiles)
            u = _dot(n2, w1_ref[:, f * ff_chunk:(f + 1) * ff_chunk])
            a = jnp.square(jnp.maximum(u, 0.0)).astype(BF16)
            n_tiles += tiles_per(w2_ref) // n_chunks
            ready(n_tiles)
            d = _dot(a, w2_ref[f * ff_chunk:(f + 1) * ff_chunk, :])
            ffn = d if ffn is None else ffn + d
        h2 = h1 + ffn
        out_ref[rows, :] = _rms(h2, gfin_ref[...]) if final_norm else h2


def _mix_ffn_kernel(x_ref, yna_ref, of_ref, ob_ref, sr_ref, gng_ref, wout_hbm, gff_ref,
                    w1_hbm, w2_hbm, gfin_ref, out_ref, wout_ref, w1_ref, w2_ref, stage, sem,
                    *, final_norm, ff_chunk):
    body = functools.partial(
        _mix_ffn_body, x_ref, yna_ref, of_ref, ob_ref, sr_ref, gng_ref, gff_ref, gfin_ref, out_ref,
        wout_ref, w1_ref, w2_ref, final_norm=final_norm, ff_chunk=ff_chunk)
    first = (pl.program_id(0) == 0) & (pl.program_id(1) == 0)

    @pl.when(first)
    def _():
        def tiles(src, dst, rows, cols):
            return [(src, dst, r, c) for r in range(rows.start, rows.stop, STAGE_ROWS)
                    for c in range(cols.start, cols.stop, STAGE_COLS)]

        order = tiles(wout_hbm, wout_ref, slice(0, wout_ref.shape[0]), slice(0, wout_ref.shape[1]))
        for f in range(w1_ref.shape[1] // ff_chunk):
            chunk = slice(f * ff_chunk, (f + 1) * ff_chunk)
            order += tiles(w1_hbm, w1_ref, slice(0, w1_ref.shape[0]), chunk)
            order += tiles(w2_hbm, w2_ref, chunk, slice(0, w2_ref.shape[1]))
        stager = _WeightStager(order, stage, sem)
        body(stager.need)
        stager.need(len(order))

    @pl.when(jnp.logical_not(first))
    def _():
        body(lambda n: None)


def _mix_ffn(x, yna, of, ob, sr, gng, wout, gff, w1, w2, gfin, *, tm, final_norm):
    B, T, D = x.shape
    tok = lambda w: pl.BlockSpec((None, tm, w), lambda b, i: (b, i, 0))
    const = lambda a: pl.BlockSpec(a.shape, lambda b, i: (0,) * a.ndim,
                                   pipeline_mode=pl.Buffered(1))
    hbm = pl.BlockSpec(memory_space=pl.ANY)
    for w in (wout, w1, w2):
        assert w.dtype == F32 and w.shape[0] % STAGE_ROWS == 0 and w.shape[1] % STAGE_COLS == 0
    assert w1.shape[1] % 1024 == 0 and 1024 % STAGE_ROWS == 0 and 1024 % STAGE_COLS == 0
    return pl.pallas_call(
        functools.partial(_mix_ffn_kernel, final_norm=final_norm, ff_chunk=1024),
        grid=(B, T // tm),
        in_specs=[tok(D), tok(yna.shape[-1]), tok(of.shape[-1]), tok(ob.shape[-1]),
                  tok(sr.shape[-1]), const(gng), hbm, const(gff), hbm, hbm, const(gfin)],
        out_specs=tok(D),
        out_shape=jax.ShapeDtypeStruct((B, T, D), F32),
        scratch_shapes=[pltpu.VMEM(w.shape, BF16) for w in (wout, w1, w2)] + [
            pltpu.VMEM((2, STAGE_ROWS, STAGE_COLS), F32), pltpu.SemaphoreType.DMA((2,))],
        compiler_params=pltpu.CompilerParams(
            dimension_semantics=("arbitrary", "arbitrary"), vmem_limit_bytes=VMEM_LIMIT),
        name="mix_ffn",
    )(x, yna, of, ob, sr, gng, wout, gff, w1, w2, gfin)


def kernel(x, ln_mix_g, w_in, na_rpb, gla_gate_up_fwd, gla_gate_bias_fwd, gla_gate_up_bwd,
           gla_gate_bias_bwd, gla_norm_g, w_out, ln_ff_g, w_ff1, w_ff2, ln_final_g):
    B, T, D = x.shape
    depth = w_in.shape[0]
    assert T % TOKEN_TILE == 0 and T // NA_GROUP >= 4
    gla_block = GLA_BLOCK if T % GLA_BLOCK == 0 else TOKEN_TILE
    row = lambda v: v.reshape(1, -1).astype(F32)
    zeros = jnp.zeros((GLA_GATE_RANK, GLA_KW), F32)
    h = x
    for l in range(depth):
        gu = jnp.concatenate([
            jnp.concatenate([gla_gate_up_fwd[l], zeros], axis=1),
            jnp.concatenate([zeros, gla_gate_up_bwd[l]], axis=1)], axis=0).astype(BF16)
        gb = jnp.concatenate([gla_gate_bias_fwd[l], gla_gate_bias_bwd[l]]).reshape(1, -1)
        qa, kat, va, vg, sr, gla_f, gla_b, dec = _inproj(
            h, row(ln_mix_g[l]), w_in[l].T, gu, gb.astype(F32), tm=TOKEN_TILE)
        y_na = _na(qa, kat, va, _na_bias_rows(na_rpb[l]))
        o_f, o_b = _gla(gla_f, gla_b, vg, dec, tb=gla_block)
        h = _mix_ffn(h, y_na, o_f, o_b, sr, row(gla_norm_g[l]), w_out[l], row(ln_ff_g[l]),
                     w_ff1[l], w_ff2[l], row(ln_final_g), tm=TOKEN_TILE,
                     final_norm=(l == depth - 1))
    return h
```

```python
import functools

import jax
import jax.numpy as jnp
from jax import lax
from jax.experimental import pallas as pl
from jax.experimental.pallas import tpu as pltpu

F32 = jnp.float32
BF16 = jnp.bfloat16

EPS = 1e-6
GRID_W = 64
NA_HEADS = 8
NA_HEAD_DIM = 64
NA_KH = 8
NA_KW = 16
GLA_HEADS = 4
GLA_DK = 64
GLA_DV = 128
GLA_GATE_RANK = 16
GLA_GATE_NORM = 16.0
GLA_CHUNK = 64

NA_WIDTH = NA_HEADS * NA_HEAD_DIM
GLA_KW = GLA_HEADS * GLA_DK
GLA_VW = GLA_HEADS * GLA_DV
COLS, _start = {}, 0
for _name, _width in (("qa", NA_WIDTH), ("ka", NA_WIDTH), ("va", NA_WIDTH), ("qg", GLA_KW),
                      ("kg", GLA_KW), ("vg", GLA_VW), ("rg", GLA_VW), ("z", 2 * GLA_GATE_RANK)):
    COLS[_name] = slice(_start, _start + _width)
    _start += _width
D_IN = _start

LANES = 128
TOKEN_TILE = 512
GLA_BLOCK = 2048
NA_GROUP_ROWS = 4
NA_GROUP = NA_GROUP_ROWS * GRID_W
NA_KEY_ROWS = 12
NA_KEYS = NA_KEY_ROWS * GRID_W
NA_UNROLL = 4
LOG2E = 1.4426950408889634
VMEM_LIMIT = 56 * 1024 * 1024


def _dot(a, b):
    return jnp.dot(a, b, preferred_element_type=F32)


def _dot_nt(a, b):
    return lax.dot_general(a, b, (((1,), (1,)), ((), ())), preferred_element_type=F32)


def _dot_tn(a, b):
    return lax.dot_general(a, b, (((0,), (0,)), ((), ())), preferred_element_type=F32)


def _rms(x, g):
    return x * lax.rsqrt(jnp.mean(x * x, axis=-1, keepdims=True) + EPS) * g


def _segmented_cumsum(x, reverse):
    n = x.shape[0]
    pos = lax.broadcasted_iota(jnp.int32, x.shape, 0) & (GLA_CHUNK - 1)
    step = 1
    while step < GLA_CHUNK:
        if reverse:
            shifted, ok = pltpu.roll(x, n - step, 0), pos < GLA_CHUNK - step
        else:
            shifted, ok = pltpu.roll(x, step, 0), pos >= step
        x = x + jnp.where(ok, shifted, 0.0)
        step *= 2
    return x


def _gla_cumdecay(log_a, reverse):
    b = _segmented_cumsum(log_a, reverse)
    b3 = b.reshape(-1, GLA_CHUNK, b.shape[-1])
    last = 0 if reverse else GLA_CHUNK - 1
    return b, b3[:, last:last + 1, :]


def _gla_operands(q, k, b, b_last):
    b3 = b.reshape(-1, GLA_CHUNK, b.shape[-1])
    k_dec = (k.reshape(b3.shape) * jnp.exp2(b_last - b3)).reshape(b.shape)
    ops = jnp.concatenate([q * jnp.exp2(b), k * jnp.exp2(-b), k_dec], axis=-1)
    return ops.astype(BF16), jnp.exp2(b_last)


def _inproj_kernel(x_ref, g_ref, wt_ref, gu_ref, gb_ref,
                   qa_ref, kat_ref, va_ref, vg_ref, sr_ref, gf_ref, gbw_ref, dec_ref, wb_ref):
    @pl.when((pl.program_id(0) == 0) & (pl.program_id(1) == 0))
    def _():
        wb_ref[...] = wt_ref[...].astype(BF16)

    def proj(name):
        return _dot_nt(n, wb_ref[COLS[name], :])

    n = _rms(x_ref[...], g_ref[...]).astype(BF16)

    def log_decay(cols):
        pre = _dot(z, gu_ref[:, cols]) + gb_ref[:, cols]
        softplus2 = jnp.log2(1.0 + jnp.exp2(jnp.abs(pre) * -LOG2E))
        return jnp.minimum(pre * (LOG2E / GLA_GATE_NORM), 0.0) - softplus2 * (1.0 / GLA_GATE_NORM)

    z = proj("z").astype(BF16)
    b_f, last_f = _gla_cumdecay(log_decay(slice(0, GLA_KW)), False)
    qa_ref[...] = (proj("qa") * (NA_HEAD_DIM ** -0.5 * LOG2E)).astype(BF16)
    b_b, last_b = _gla_cumdecay(log_decay(slice(GLA_KW, 2 * GLA_KW)), True)
    kat_ref[...] = _dot_nt(wb_ref[COLS["ka"], :], n).astype(BF16)
    qg = proj("qg") * (GLA_DK ** -0.5)
    kg = proj("kg")
    gf_ref[...], dec_f = _gla_operands(qg, kg, b_f, last_f)
    va_ref[...] = proj("va").astype(BF16)
    gbw_ref[...], dec_b = _gla_operands(qg, kg, b_b, last_b)
    dec_ref[...] = jnp.concatenate([dec_f, dec_b], axis=-1)
    vg_ref[...] = proj("vg").astype(BF16)
    r = proj("rg")
    sr_ref[...] = (r * jax.nn.sigmoid(r)).astype(BF16)


def _inproj(x, g, wt, gu, gb, *, tm):
    B, T, D = x.shape
    assert wt.shape == (D_IN, D)
    nc = tm // GLA_CHUNK
    tok = lambda w: pl.BlockSpec((None, tm, w), lambda b, i: (b, i, 0))
    const = lambda a: pl.BlockSpec(a.shape, lambda b, i: (0,) * a.ndim,
                                   pipeline_mode=pl.Buffered(1))
    tok_out = lambda w: (jax.ShapeDtypeStruct((B, T, w), BF16), tok(w))
    outs = [
        tok_out(NA_WIDTH),
        (jax.ShapeDtypeStruct((B, NA_WIDTH, T), BF16),
         pl.BlockSpec((None, NA_WIDTH, tm), lambda b, i: (b, 0, i))),
        tok_out(NA_WIDTH),
        tok_out(GLA_VW),
        tok_out(GLA_VW),
        tok_out(3 * GLA_KW),
        tok_out(3 * GLA_KW),
        (jax.ShapeDtypeStruct((B, T // GLA_CHUNK, 1, 2 * GLA_KW), F32),
         pl.BlockSpec((None, nc, 1, 2 * GLA_KW), lambda b, i: (b, i, 0, 0))),
    ]
    return pl.pallas_call(
        _inproj_kernel,
        grid=(B, T // tm),
        in_specs=[tok(D), const(g), const(wt), const(gu), const(gb)],
        out_specs=[s for _, s in outs],
        out_shape=[s for s, _ in outs],
        scratch_shapes=[pltpu.VMEM(wt.shape, BF16)],
        compiler_params=pltpu.CompilerParams(
            dimension_semantics=("arbitrary", "arbitrary"), vmem_limit_bytes=VMEM_LIMIT),
        name="inproj",
    )(x, g, wt, gu, gb)


def _na_bias_rows(rpb):
    n_dc = 2 * NA_KW - 1
    padded = jnp.pad(rpb.astype(F32), ((0, 0), (4, 5), (0, GRID_W - n_dc)))
    return jnp.concatenate([padded[:, :-1], padded[:, 1:]], axis=-1)


def _na_tiles(table, g):
    i_start = (0, g, NA_KEY_ROWS - NA_KH)[table]
    return range(i_start // 2, (i_start + NA_KH + 1) // 2)


def _na_build_tables(rows_ref, tab_ref):
    qc = lax.broadcasted_iota(jnp.int32, (GRID_W, LANES), 0)
    lane = lax.broadcasted_iota(jnp.int32, (GRID_W, LANES), 1)
    kc = lane & (GRID_W - 1)
    col_start = jnp.clip(qc - NA_KW // 2, 0, GRID_W - NA_KW)
    in_win = (kc >= col_start) & (kc < col_start + NA_KW)
    valid = {(True, True): in_win,
             (True, False): in_win & (lane < GRID_W),
             (False, True): in_win & (lane >= GRID_W)}
    row_offset = (0, -(NA_KH // 2), -NA_KH)
    for hh in range(2):
        for t in range(3):
            for g in range(NA_GROUP_ROWS):
                i_start = (0, g, NA_KEY_ROWS - NA_KH)[t]
                for m in _na_tiles(t, g):
                    halves = tuple(i_start <= i < i_start + NA_KH for i in (2 * m, 2 * m + 1))
                    a = 2 * m - g + row_offset[t] + NA_KH - 1 + 4
                    src = jnp.broadcast_to(rows_ref[hh, a:a + 1, :], (GRID_W, LANES))
                    rot = pltpu.roll(src, LANES - (NA_KW - 1), 1, stride=1, stride_axis=0)
                    tab_ref[hh, t, g * GRID_W:(g + 1) * GRID_W, m * LANES:(m + 1) * LANES] = (
                        jnp.where(valid[halves], rot * LOG2E, -jnp.inf))


class _NaRefs:
    def __init__(self, q, kt, v, bias, o, n_groups):
        self.q, self.kt, self.v, self.bias, self.o, self.n_groups = q, kt, v, bias, o, n_groups


def _na_rows(u):
    return pl.ds(pl.multiple_of(u * NA_GROUP, NA_GROUP), NA_GROUP)


def _na_key0(r, u):
    return pl.multiple_of(jnp.clip(u - 1, 0, r.n_groups - 3) * NA_GROUP, NA_GROUP)


def _na_scores(r, u):
    qq = r.q[_na_rows(u), :]
    first_head = lax.broadcasted_iota(jnp.int32, qq.shape, 1) < NA_HEAD_DIM
    zero = jnp.zeros_like(qq)
    q_heads = jnp.concatenate([jnp.where(first_head, qq, zero),
                               jnp.where(first_head, zero, qq)], axis=0)
    return _dot(q_heads, r.kt[:, pl.ds(_na_key0(r, u), NA_KEYS)])


def _na_softmax(r, s, table):
    p_tiles, inv_l = [], []
    for hh in range(2):
        for g in range(NA_GROUP_ROWS):
            rows = slice(hh * NA_GROUP + g * GRID_W, hh * NA_GROUP + (g + 1) * GRID_W)
            sb = [s[rows, c * LANES:(c + 1) * LANES]
                  + r.bias[hh, table, g * GRID_W:(g + 1) * GRID_W, c * LANES:(c + 1) * LANES]
                  for c in _na_tiles(table, g)]
            m = jnp.max(functools.reduce(jnp.maximum, sb), axis=-1, keepdims=True)
            p = [jnp.exp2(x - m) for x in sb]
            inv_l.append(1.0 / jnp.sum(functools.reduce(jnp.add, p), axis=-1, keepdims=True))
            p_tiles.append([x.astype(BF16) for x in p])
    return p_tiles, inv_l


def _na_store_output(r, u, o):
    first_head = lax.broadcasted_iota(jnp.int32, (NA_GROUP, LANES), 1) < NA_HEAD_DIM
    r.o[_na_rows(u), :] = jnp.where(first_head, o[:NA_GROUP], o[NA_GROUP:]).astype(r.o.dtype)


def _na_edge_group(r, u, table):
    tiles = _na_tiles(table, 0)
    p_tiles, inv_l = _na_softmax(r, _na_scores(r, u), table)
    p = jnp.concatenate([jnp.concatenate(t, axis=1) for t in p_tiles], axis=0)
    keys = pl.ds(_na_key0(r, u) + tiles.start * LANES, len(tiles) * LANES)
    _na_store_output(r, u, _dot(p, r.v[keys, :]) * jnp.concatenate(inv_l, axis=0))


def _na_interior_probs(r, s):
    p_tiles, inv_l = _na_softmax(r, s, 1)
    zero_tile = jnp.zeros((GRID_W, LANES), BF16)
    p_rows = []
    for block, tiles in enumerate(p_tiles):
        valid = _na_tiles(1, block % NA_GROUP_ROWS)
        p_rows.append(jnp.concatenate(
            [zero_tile] * valid.start + tiles + [zero_tile] * (NA_KEYS // LANES - valid.stop), axis=1))
    return jnp.concatenate(p_rows, axis=0), jnp.concatenate(inv_l, axis=0)


def _na_interior_output(r, u, p, inv_l):
    _na_store_output(r, u, _dot(p, r.v[pl.ds(_na_key0(r, u), NA_KEYS), :]) * inv_l)


def _na_stage(r, u, carry):
    s_cur, p_prev, l_prev = carry
    s_next = _na_scores(r, jnp.minimum(u + 1, r.n_groups - 2))
    p_cur, l_cur = _na_interior_probs(r, s_cur)
    _na_interior_output(r, u - 1, p_prev, l_prev)
    return s_next, p_cur, l_cur


def _na_kernel(q_ref, kt_ref, v_ref, rows_ref, o_ref, bias_ref, s_ref, p_ref, l_ref, *, n_groups):
    @pl.when(pl.program_id(1) == 0)
    def _():
        _na_build_tables(rows_ref, bias_ref)

    r = _NaRefs(q_ref, kt_ref, v_ref, bias_ref, o_ref, n_groups)
    last = n_groups - 1
    _na_edge_group(r, 0, 0)
    _na_edge_group(r, last, 2)

    def load_carry():
        return s_ref[...], p_ref[...], l_ref[...]

    def store_carry(carry):
        s_ref[...], p_ref[...], l_ref[...] = carry

    def stages(first, count, carry):
        for k in range(count):
            carry = _na_stage(r, first + k, carry)
        return carry

    def unrolled_stages(i, c):
        store_carry(stages(2 + NA_UNROLL * i, NA_UNROLL, load_carry()))
        return c

    store_carry((_na_scores(r, 2),) + _na_interior_probs(r, _na_scores(r, 1)))
    n_loops, n_tail = divmod(n_groups - 3, NA_UNROLL)
    lax.fori_loop(0, n_loops, unrolled_stages, 0)
    _, p_prev, l_prev = stages(2 + NA_UNROLL * n_loops, n_tail, load_carry())
    _na_interior_output(r, last - 1, p_prev, l_prev)


def _na(qa, kat, va, rows):
    B, T, _ = qa.shape
    n_pairs = NA_HEADS // 2
    return pl.pallas_call(
        functools.partial(_na_kernel, n_groups=T // NA_GROUP),
        grid=(n_pairs, B),
        in_specs=[
            pl.BlockSpec((None, T, LANES), lambda p, b: (b, 0, p)),
            pl.BlockSpec((None, LANES, T), lambda p, b: (b, p, 0)),
            pl.BlockSpec((None, T, LANES), lambda p, b: (b, 0, p)),
            pl.BlockSpec((2,) + rows.shape[1:], lambda p, b: (p, 0, 0)),
        ],
        out_specs=pl.BlockSpec((None, T, LANES), lambda p, b: (b, 0, p)),
        out_shape=jax.ShapeDtypeStruct((B, T, NA_WIDTH), BF16),
        scratch_shapes=[pltpu.VMEM((2, 3, NA_GROUP, NA_KEYS), F32),
                        pltpu.VMEM((2 * NA_GROUP, NA_KEYS), F32),
                        pltpu.VMEM((2 * NA_GROUP, NA_KEYS), BF16),
                        pltpu.VMEM((2 * NA_GROUP, 1), F32)],
        compiler_params=pltpu.CompilerParams(
            dimension_semantics=("parallel", "arbitrary"), vmem_limit_bytes=VMEM_LIMIT),
        name="natten",
    )(qa, kat, va, rows)


class _GlaDir:
    def __init__(self, reverse, ops, v, dec, o, state, a, contrib, prev):
        self.reverse = reverse
        self.qd, self.ki, self.kd = (ops.at[:, i * GLA_KW:(i + 1) * GLA_KW] for i in range(3))
        self.v, self.dec, self.o, self.state = v, dec, o, state
        self.a, self.contrib, self.prev = a, contrib, prev


def _gla_pair_lanes(h):
    lane = lax.broadcasted_iota(jnp.int32, (GLA_CHUNK, LANES), 1)
    return (lane < GLA_DK) if h % 2 == 0 else (lane >= GLA_DK)


def _gla_intra(d, c):
    C = GLA_CHUNK
    rows = slice(c * C, (c + 1) * C)
    qd, ki = d.qd[rows, :], d.ki[rows, :]
    lane = lax.broadcasted_iota(jnp.int32, qd.shape, 1)
    zeros = jnp.zeros_like(ki)
    i = lax.broadcasted_iota(jnp.int32, (2 * C, LANES), 0) & (C - 1)
    col = lax.broadcasted_iota(jnp.int32, (2 * C, LANES), 1)
    for parity in range(2):
        q_heads = jnp.concatenate(
            [jnp.where((lane >= h * GLA_DK) & (lane < (h + 1) * GLA_DK), qd, jnp.zeros_like(qd))
             for h in range(parity, GLA_HEADS, 2)], axis=0)
        keys = jnp.concatenate([zeros, ki] if parity == 0 else [ki, zeros], axis=0)
        scores = _dot_nt(q_heads, keys)
        j = col - C if parity == 0 else col
        in_chunk = (j >= 0) & (j < C)
        keep = in_chunk & ((j > i) if d.reverse else (j <= i))
        d.a[c, parity] = jnp.where(keep, scores, 0.0).astype(BF16)
    for p in range(GLA_HEADS // 2):
        kd_p = d.kd[rows, p * LANES:(p + 1) * LANES]
        v_p = d.v[rows, 2 * p * GLA_DV:(2 * p + 2) * GLA_DV]
        both = _dot_tn(kd_p, v_p)
        d.contrib[c, 2 * p] = both[:GLA_DK, :GLA_DV]
        d.contrib[c, 2 * p + 1] = both[GLA_DK:, GLA_DV:]


def _gla_scan(d, n_chunks):
    order = range(n_chunks - 1, -1, -1) if d.reverse else range(n_chunks)
    pad = jnp.zeros((LANES - n_chunks, GLA_KW), F32)
    dec_t = jnp.concatenate([d.dec[c] for c in range(n_chunks)] + [pad], axis=0).T
    for h in range(GLA_HEADS):
        st = d.state[h]
        for c in order:
            d.prev[c, h] = st.astype(BF16)
            st = st * dec_t[h * GLA_DK:(h + 1) * GLA_DK, c:c + 1] + d.contrib[c, h]
        d.state[h] = st


def _gla_outputs(d, c):
    C = GLA_CHUNK
    rows = slice(c * C, (c + 1) * C)
    for h in range(GLA_HEADS):
        p, parity = divmod(h, 2)
        qd_p = d.qd[rows, p * LANES:(p + 1) * LANES]
        scores = d.a[c, parity, p * C:(p + 1) * C, :]
        lhs = jnp.where(_gla_pair_lanes(h), qd_p, scores)
        cols = slice(h * GLA_DV, (h + 1) * GLA_DV)
        operands = [d.prev[c, h], d.v[rows, cols]]
        rhs = jnp.concatenate(operands if parity == 0 else operands[::-1], axis=0)
        d.o[rows, cols] = _dot(lhs, rhs).astype(d.o.dtype)


def _gla_kernel(gf_ref, vf_ref, decf_ref, gb_ref, vb_ref, decb_ref,
                of_ref, ob_ref, sf_ref, sb_ref, *scratch, n_chunks):
    @pl.when(pl.program_id(1) == 0)
    def _():
        sf_ref[...] = jnp.zeros_like(sf_ref)
        sb_ref[...] = jnp.zeros_like(sb_ref)

    n = len(scratch) // 2
    dirs = (_GlaDir(False, gf_ref, vf_ref, decf_ref, of_ref, sf_ref, *scratch[:n]),
            _GlaDir(True, gb_ref, vb_ref, decb_ref, ob_ref, sb_ref, *scratch[n:]))

    for c in range(n_chunks):
        for d in dirs:
            _gla_intra(d, c)
    for d in dirs:
        _gla_scan(d, n_chunks)
    for c in range(n_chunks):
        for d in dirs:
            _gla_outputs(d, c)


def _gla(gf, gb, vg, dec, *, tb):
    B, T, vw = vg.shape
    nb = T // tb
    nc = tb // GLA_CHUNK
    kw = GLA_KW
    n_pairs = GLA_HEADS // 2
    fwd = lambda w: pl.BlockSpec((None, tb, w), lambda b, s: (b, s, 0))
    bwd = lambda w: pl.BlockSpec((None, tb, w), lambda b, s: (b, nb - 1 - s, 0))
    dec_fwd = pl.BlockSpec((None, nc, 1, kw), lambda b, s: (b, s, 0, 0))
    dec_bwd = pl.BlockSpec((None, nc, 1, kw), lambda b, s: (b, nb - 1 - s, 0, 1))
    state = pltpu.VMEM((GLA_HEADS, GLA_DK, GLA_DV), F32)
    per_dir = [
        pltpu.VMEM((nc, 2, n_pairs * GLA_CHUNK, 2 * GLA_CHUNK), BF16),
        pltpu.VMEM((nc, GLA_HEADS, GLA_DK, GLA_DV), F32),
        pltpu.VMEM((nc, GLA_HEADS, GLA_DK, GLA_DV), BF16)]
    return pl.pallas_call(
        functools.partial(_gla_kernel, n_chunks=nc),
        grid=(B, nb),
        in_specs=[fwd(3 * kw), fwd(vw), dec_fwd, bwd(3 * kw), bwd(vw), dec_bwd],
        out_specs=[fwd(vw), bwd(vw)],
        out_shape=[jax.ShapeDtypeStruct((B, T, vw), BF16)] * 2,
        scratch_shapes=[state, state] + per_dir * 2,
        compiler_params=pltpu.CompilerParams(
            dimension_semantics=("parallel", "arbitrary"), vmem_limit_bytes=VMEM_LIMIT),
        name="gla",
    )(gf, vg, dec, gb, vg, dec)


MIX_SUBTILES = 2
STAGE_ROWS, STAGE_COLS = 512, 1024


class _WeightStager:
    def __init__(self, tiles, stage, sem):
        self.tiles, self.stage, self.sem, self.done = tiles, stage, sem, 0
        for i in range(min(2, len(tiles))):
            self._copy(i).start()

    def _copy(self, i):
        src, _, r, c = self.tiles[i]
        return pltpu.make_async_copy(src.at[pl.ds(r, STAGE_ROWS), pl.ds(c, STAGE_COLS)],
                                     self.stage.at[i % 2], self.sem.at[i % 2])

    def need(self, count):
        while self.done < count:
            i = self.done
            _, dst, r, c = self.tiles[i]
            self._copy(i).wait()
            dst[r:r + STAGE_ROWS, c:c + STAGE_COLS] = self.stage[i % 2].astype(BF16)
            if i + 2 < len(self.tiles):
                self._copy(i + 2).start()
            self.done += 1


def _mix_ffn_body(x_ref, yna_ref, of_ref, ob_ref, sr_ref, gng_ref, gff_ref, gfin_ref, out_ref,
                  wout_ref, w1_ref, w2_ref, ready, *, final_norm, ff_chunk):
    na_w = yna_ref.shape[-1]
    tiles_per = lambda ref: (ref.shape[0] // STAGE_ROWS) * (ref.shape[1] // STAGE_COLS)
    n_chunks = w1_ref.shape[1] // ff_chunk
    sub = x_ref.shape[0] // MIX_SUBTILES
    row_sets = [slice(i * sub, (i + 1) * sub) for i in range(MIX_SUBTILES)]
    h1s, n2s = [], []
    for rows in row_sets:
        o = of_ref[rows, :].astype(F32) + ob_ref[rows, :].astype(F32)
        parts = [_rms(o[:, h * GLA_DV:(h + 1) * GLA_DV], gng_ref[...]) for h in range(GLA_HEADS)]
        y_gla = (jnp.concatenate(parts, axis=-1) * sr_ref[rows, :].astype(F32)).astype(BF16)
        ready(tiles_per(wout_ref))
        mix = _dot(yna_ref[rows, :], wout_ref[0:na_w, :]) + _dot(y_gla, wout_ref[na_w:, :])
        h1 = x_ref[rows, :] + mix
        h1s.append(h1)
        n2s.append(_rms(h1, gff_ref[...]).astype(BF16))
    for rows, h1, n2 in zip(row_sets, h1s, n2s):
        n_tiles = tiles_per(wout_ref)
        ffn = None
        for f in range(n_chunks):
            n_tiles += tiles_per(w1_ref) // n_chunks
            ready(n_tiles)
            u = _dot(n2, w1_ref[:, f * ff_chunk:(f + 1) * ff_chunk])
            a = jnp.square(jnp.maximum(u, 0.0)).astype(BF16)
            n_tiles += tiles_per(w2_ref) // n_chunks
            ready(n_tiles)
            d = _dot(a, w2_ref[f * ff_chunk:(f + 1) * ff_chunk, :])
            ffn = d if ffn is None else ffn + d
        h2 = h1 + ffn
        out_ref[rows, :] = _rms(h2, gfin_ref[...]) if final_norm else h2


def _mix_ffn_kernel(x_ref, yna_ref, of_ref, ob_ref, sr_ref, gng_ref, wout_hbm, gff_ref,
                    w1_hbm, w2_hbm, gfin_ref, out_ref, wout_ref, w1_ref, w2_ref, stage, sem,
                    *, final_norm, ff_chunk):
    body = functools.partial(
        _mix_ffn_body, x_ref, yna_ref, of_ref, ob_ref, sr_ref, gng_ref, gff_ref, gfin_ref, out_ref,
        wout_ref, w1_ref, w2_ref, final_norm=final_norm, ff_chunk=ff_chunk)
    first = (pl.program_id(0) == 0) & (pl.program_id(1) == 0)

    @pl.when(first)
    def _():
        def tiles(src, dst, rows, cols):
            return [(src, dst, r, c) for r in range(rows.start, rows.stop, STAGE_ROWS)
                    for c in range(cols.start, cols.stop, STAGE_COLS)]

        order = tiles(wout_hbm, wout_ref, slice(0, wout_ref.shape[0]), slice(0, wout_ref.shape[1]))
        for f in range(w1_ref.shape[1] // ff_chunk):
            chunk = slice(f * ff_chunk, (f + 1) * ff_chunk)
            order += tiles(w1_hbm, w1_ref, slice(0, w1_ref.shape[0]), chunk)
            order += tiles(w2_hbm, w2_ref, chunk, slice(0, w2_ref.shape[1]))
        stager = _WeightStager(order, stage, sem)
        body(stager.need)
        stager.need(len(order))

    @pl.when(jnp.logical_not(first))
    def _():
        body(lambda n: None)


def _mix_ffn(x, yna, of, ob, sr, gng, wout, gff, w1, w2, gfin, *, tm, final_norm):
    B, T, D = x.shape
    tok = lambda w: pl.BlockSpec((None, tm, w), lambda b, i: (b, i, 0))
    const = lambda a: pl.BlockSpec(a.shape, lambda b, i: (0,) * a.ndim,
                                   pipeline_mode=pl.Buffered(1))
    hbm = pl.BlockSpec(memory_space=pl.ANY)
    for w in (wout, w1, w2):
        assert w.dtype == F32 and w.shape[0] % STAGE_ROWS == 0 and w.shape[1] % STAGE_COLS == 0
    assert w1.shape[1] % 1024 == 0 and 1024 % STAGE_ROWS == 0 and 1024 % STAGE_COLS == 0
    return pl.pallas_call(
        functools.partial(_mix_ffn_kernel, final_norm=final_norm, ff_chunk=1024),
        grid=(B, T // tm),
        in_specs=[tok(D), tok(yna.shape[-1]), tok(of.shape[-1]), tok(ob.shape[-1]),
                  tok(sr.shape[-1]), const(gng), hbm, const(gff), hbm, hbm, const(gfin)],
        out_specs=tok(D),
        out_shape=jax.ShapeDtypeStruct((B, T, D), F32),
        scratch_shapes=[pltpu.VMEM(w.shape, BF16) for w in (wout, w1, w2)] + [
            pltpu.VMEM((2, STAGE_ROWS, STAGE_COLS), F32), pltpu.SemaphoreType.DMA((2,))],
        compiler_params=pltpu.CompilerParams(
            dimension_semantics=("arbitrary", "arbitrary"), vmem_limit_bytes=VMEM_LIMIT),
        name="mix_ffn",
    )(x, yna, of, ob, sr, gng, wout, gff, w1, w2, gfin)


def kernel(x, ln_mix_g, w_in, na_rpb, gla_gate_up_fwd, gla_gate_bias_fwd, gla_gate_up_bwd,
           gla_gate_bias_bwd, gla_norm_g, w_out, ln_ff_g, w_ff1, w_ff2, ln_final_g):
    B, T, D = x.shape
    depth = w_in.shape[0]
    assert T % TOKEN_TILE == 0 and T // NA_GROUP >= 4
    gla_block = GLA_BLOCK if T % GLA_BLOCK == 0 else TOKEN_TILE
    row = lambda v: v.reshape(1, -1).astype(F32)
    zeros = jnp.zeros((GLA_GATE_RANK, GLA_KW), F32)
    h = x
    for l in range(depth):
        gu = jnp.concatenate([
            jnp.concatenate([gla_gate_up_fwd[l], zeros], axis=1),
            jnp.concatenate([zeros, gla_gate_up_bwd[l]], axis=1)], axis=0).astype(BF16)
        gb = jnp.concatenate([gla_gate_bias_fwd[l], gla_gate_bias_bwd[l]]).reshape(1, -1)
        qa, kat, va, vg, sr, gla_f, gla_b, dec = _inproj(
            h, row(ln_mix_g[l]), w_in[l].T, gu, gb.astype(F32), tm=TOKEN_TILE)
        y_na = _na(qa, kat, va, _na_bias_rows(na_rpb[l]))
        o_f, o_b = _gla(gla_f, gla_b, vg, dec, tb=gla_block)
        h = _mix_ffn(h, y_na, o_f, o_b, sr, row(gla_norm_g[l]), w_out[l], row(ln_ff_g[l]),
                     w_ff1[l], w_ff2[l], row(ln_final_g), tm=TOKEN_TILE,
                     final_norm=(l == depth - 1))
    return h
```

```python
import functools

import jax
import jax.numpy as jnp
from jax import lax
from jax.experimental import pallas as pl
from jax.experimental.pallas import tpu as pltpu

F32 = jnp.float32
BF16 = jnp.bfloat16

EPS = 1e-6
GRID_W = 64
NA_HEADS = 8
NA_HEAD_DIM = 64
NA_KH = 8
NA_KW = 16
GLA_HEADS = 4
GLA_DK = 64
GLA_DV = 128
GLA_GATE_RANK = 16
GLA_GATE_NORM = 16.0
GLA_CHUNK = 64

NA_WIDTH = NA_HEADS * NA_HEAD_DIM
GLA_KW = GLA_HEADS * GLA_DK
GLA_VW = GLA_HEADS * GLA_DV
COLS, _start = {}, 0
for _name, _width in (("qa", NA_WIDTH), ("ka", NA_WIDTH), ("va", NA_WIDTH), ("qg", GLA_KW),
                      ("kg", GLA_KW), ("vg", GLA_VW), ("rg", GLA_VW), ("z", 2 * GLA_GATE_RANK)):
    COLS[_name] = slice(_start, _start + _width)
    _start += _width
D_IN = _start

LANES = 128
TOKEN_TILE = 512
MIX_TILE = 1024
GLA_BLOCK = 1024
NA_GROUP_ROWS = 4
NA_GROUP = NA_GROUP_ROWS * GRID_W
NA_KEY_ROWS = 12
NA_KEYS = NA_KEY_ROWS * GRID_W
NA_UNROLL = 4
LOG2E = 1.4426950408889634
VMEM_LIMIT = 60 * 1024 * 1024


def _dot(a, b):
    return jnp.dot(a, b, preferred_element_type=F32)


def _dot_nt(a, b):
    return lax.dot_general(a, b, (((1,), (1,)), ((), ())), preferred_element_type=F32)


def _dot_tn(a, b):
    return lax.dot_general(a, b, (((0,), (0,)), ((), ())), preferred_element_type=F32)


def _rms(x, g):
    return x * lax.rsqrt(jnp.mean(x * x, axis=-1, keepdims=True) + EPS) * g


def _segmented_cumsum(x, reverse):
    n = x.shape[0]
    pos = lax.broadcasted_iota(jnp.int32, x.shape, 0) & (GLA_CHUNK - 1)
    step = 1
    while step < GLA_CHUNK:
        if reverse:
            shifted, ok = pltpu.roll(x, n - step, 0), pos < GLA_CHUNK - step
        else:
            shifted, ok = pltpu.roll(x, step, 0), pos >= step
        x = x + jnp.where(ok, shifted, 0.0)
        step *= 2
    return x


def _gla_cumdecay(log_a, reverse):
    b = _segmented_cumsum(log_a, reverse)
    b3 = b.reshape(-1, GLA_CHUNK, b.shape[-1])
    last = 0 if reverse else GLA_CHUNK - 1
    return b, b3[:, last:last + 1, :]


def _gla_operands(q, k, b, b_last):
    b3 = b.reshape(-1, GLA_CHUNK, b.shape[-1])
    k_dec = (k.reshape(b3.shape) * jnp.exp2(b_last - b3)).reshape(b.shape)
    ops = jnp.concatenate([q * jnp.exp2(b), k * jnp.exp2(-b), k_dec], axis=-1)
    return ops.astype(BF16), jnp.exp2(b_last)


def _inproj_kernel(x_ref, g_ref, wt_ref, gu_ref, gb_ref,
                   qa_ref, kat_ref, va_ref, vg_ref, sr_ref, gf_ref, gbw_ref, dec_ref, wb_ref):
    @pl.when((pl.program_id(0) == 0) & (pl.program_id(1) == 0))
    def _():
        wb_ref[...] = wt_ref[...].astype(BF16)

    def proj(name):
        return _dot_nt(n, wb_ref[COLS[name], :])

    n = _rms(x_ref[...], g_ref[...]).astype(BF16)

    def log_decay(cols):
        pre = _dot(z, gu_ref[:, cols]) + gb_ref[:, cols]
        softplus2 = jnp.log2(1.0 + jnp.exp2(jnp.abs(pre) * -LOG2E))
        return jnp.minimum(pre * (LOG2E / GLA_GATE_NORM), 0.0) - softplus2 * (1.0 / GLA_GATE_NORM)

    z = proj("z").astype(BF16)
    b_f, last_f = _gla_cumdecay(log_decay(slice(0, GLA_KW)), False)
    qa_ref[...] = (proj("qa") * (NA_HEAD_DIM ** -0.5 * LOG2E)).astype(BF16)
    b_b, last_b = _gla_cumdecay(log_decay(slice(GLA_KW, 2 * GLA_KW)), True)
    kat_ref[...] = _dot_nt(wb_ref[COLS["ka"], :], n).astype(BF16)
    qg = proj("qg") * (GLA_DK ** -0.5)
    kg = proj("kg")
    gf_ref[...], dec_f = _gla_operands(qg, kg, b_f, last_f)
    va_ref[...] = proj("va").astype(BF16)
    gbw_ref[...], dec_b = _gla_operands(qg, kg, b_b, last_b)
    dec_ref[...] = jnp.concatenate([dec_f, dec_b], axis=-1)
    vg_ref[...] = proj("vg").astype(BF16)
    r = proj("rg")
    sr_ref[...] = (r * jax.nn.sigmoid(r)).astype(BF16)


def _inproj(x, g, wt, gu, gb, *, tm):
    B, T, D = x.shape
    assert wt.shape == (D_IN, D)
    nc = tm // GLA_CHUNK
    tok = lambda w: pl.BlockSpec((None, tm, w), lambda b, i: (b, i, 0))
    const = lambda a: pl.BlockSpec(a.shape, lambda b, i: (0,) * a.ndim,
                                   pipeline_mode=pl.Buffered(1))
    tok_out = lambda w: (jax.ShapeDtypeStruct((B, T, w), BF16), tok(w))
    outs = [
        tok_out(NA_WIDTH),
        (jax.ShapeDtypeStruct((B, NA_WIDTH, T), BF16),
         pl.BlockSpec((None, NA_WIDTH, tm), lambda b, i: (b, 0, i))),
        tok_out(NA_WIDTH),
        tok_out(GLA_VW),
        tok_out(GLA_VW),
        tok_out(3 * GLA_KW),
        tok_out(3 * GLA_KW),
        (jax.ShapeDtypeStruct((B, T // GLA_CHUNK, 1, 2 * GLA_KW), F32),
         pl.BlockSpec((None, nc, 1, 2 * GLA_KW), lambda b, i: (b, i, 0, 0))),
    ]
    return pl.pallas_call(
        _inproj_kernel,
        grid=(B, T // tm),
        in_specs=[tok(D), const(g), const(wt), const(gu), const(gb)],
        out_specs=[s for _, s in outs],
        out_shape=[s for s, _ in outs],
        scratch_shapes=[pltpu.VMEM(wt.shape, BF16)],
        compiler_params=pltpu.CompilerParams(
            dimension_semantics=("arbitrary", "arbitrary"), vmem_limit_bytes=VMEM_LIMIT),
        name="inproj",
    )(x, g, wt, gu, gb)


def _na_bias_rows(rpb):
    n_dc = 2 * NA_KW - 1
    padded = jnp.pad(rpb.astype(F32), ((0, 0), (4, 5), (0, GRID_W - n_dc)))
    return jnp.concatenate([padded[:, :-1], padded[:, 1:]], axis=-1)


def _na_tiles(table, g):
    i_start = (0, g, NA_KEY_ROWS - NA_KH)[table]
    return range(i_start // 2, (i_start + NA_KH + 1) // 2)


def _na_build_tables(rows_ref, tab_ref):
    qc = lax.broadcasted_iota(jnp.int32, (GRID_W, LANES), 0)
    lane = lax.broadcasted_iota(jnp.int32, (GRID_W, LANES), 1)
    kc = lane & (GRID_W - 1)
    col_start = jnp.clip(qc - NA_KW // 2, 0, GRID_W - NA_KW)
    in_win = (kc >= col_start) & (kc < col_start + NA_KW)
    valid = {(True, True): in_win,
             (True, False): in_win & (lane < GRID_W),
             (False, True): in_win & (lane >= GRID_W)}
    row_offset = (0, -(NA_KH // 2), -NA_KH)
    for hh in range(2):
        for t in range(3):
            for g in range(NA_GROUP_ROWS):
                i_start = (0, g, NA_KEY_ROWS - NA_KH)[t]
                for m in _na_tiles(t, g):
                    halves = tuple(i_start <= i < i_start + NA_KH for i in (2 * m, 2 * m + 1))
                    a = 2 * m - g + row_offset[t] + NA_KH - 1 + 4
                    src = jnp.broadcast_to(rows_ref[hh, a:a + 1, :], (GRID_W, LANES))
                    rot = pltpu.roll(src, LANES - (NA_KW - 1), 1, stride=1, stride_axis=0)
                    tab_ref[hh, t, g * GRID_W:(g + 1) * GRID_W, m * LANES:(m + 1) * LANES] = (
                        jnp.where(valid[halves], rot * LOG2E, -jnp.inf))


class _NaRefs:
    def __init__(self, q, kt, v, bias, o, n_groups):
        self.q, self.kt, self.v, self.bias, self.o, self.n_groups = q, kt, v, bias, o, n_groups


def _na_rows(u):
    return pl.ds(pl.multiple_of(u * NA_GROUP, NA_GROUP), NA_GROUP)


def _na_key0(r, u):
    return pl.multiple_of(jnp.clip(u - 1, 0, r.n_groups - 3) * NA_GROUP, NA_GROUP)


def _na_scores(r, u):
    qq = r.q[_na_rows(u), :]
    first_head = lax.broadcasted_iota(jnp.int32, qq.shape, 1) < NA_HEAD_DIM
    zero = jnp.zeros_like(qq)
    q_heads = jnp.concatenate([jnp.where(first_head, qq, zero),
                               jnp.where(first_head, zero, qq)], axis=0)
    return _dot(q_heads, r.kt[:, pl.ds(_na_key0(r, u), NA_KEYS)])


def _na_softmax(r, s, table):
    p_tiles, inv_l = [], []
    for hh in range(2):
        for g in range(NA_GROUP_ROWS):
            rows = slice(hh * NA_GROUP + g * GRID_W, hh * NA_GROUP + (g + 1) * GRID_W)
            sb = [s[rows, c * LANES:(c + 1) * LANES]
                  + r.bias[hh, table, g * GRID_W:(g + 1) * GRID_W, c * LANES:(c + 1) * LANES]
                  for c in _na_tiles(table, g)]
            m = jnp.max(functools.reduce(jnp.maximum, sb), axis=-1, keepdims=True)
            p = [jnp.exp2(x - m) for x in sb]
            inv_l.append(1.0 / jnp.sum(functools.reduce(jnp.add, p), axis=-1, keepdims=True))
            p_tiles.append([x.astype(BF16) for x in p])
    return p_tiles, inv_l


def _na_store_output(r, u, o):
    first_head = lax.broadcasted_iota(jnp.int32, (NA_GROUP, LANES), 1) < NA_HEAD_DIM
    r.o[_na_rows(u), :] = jnp.where(first_head, o[:NA_GROUP], o[NA_GROUP:]).astype(r.o.dtype)


def _na_edge_group(r, u, table):
    tiles = _na_tiles(table, 0)
    p_tiles, inv_l = _na_softmax(r, _na_scores(r, u), table)
    p = jnp.concatenate([jnp.concatenate(t, axis=1) for t in p_tiles], axis=0)
    keys = pl.ds(_na_key0(r, u) + tiles.start * LANES, len(tiles) * LANES)
    _na_store_output(r, u, _dot(p, r.v[keys, :]) * jnp.concatenate(inv_l, axis=0))


def _na_interior_probs(r, s):
    p_tiles, inv_l = _na_softmax(r, s, 1)
    zero_tile = jnp.zeros((GRID_W, LANES), BF16)
    p_rows = []
    for block, tiles in enumerate(p_tiles):
        valid = _na_tiles(1, block % NA_GROUP_ROWS)
        p_rows.append(jnp.concatenate(
            [zero_tile] * valid.start + tiles + [zero_tile] * (NA_KEYS // LANES - valid.stop), axis=1))
    return jnp.concatenate(p_rows, axis=0), jnp.concatenate(inv_l, axis=0)


def _na_interior_output(r, u, p, inv_l):
    _na_store_output(r, u, _dot(p, r.v[pl.ds(_na_key0(r, u), NA_KEYS), :]) * inv_l)


def _na_stage(r, u, carry):
    s_cur, p_prev, l_prev = carry
    s_next = _na_scores(r, jnp.minimum(u + 1, r.n_groups - 2))
    p_cur, l_cur = _na_interior_probs(r, s_cur)
    _na_interior_output(r, u - 1, p_prev, l_prev)
    return s_next, p_cur, l_cur


def _na_kernel(q_ref, kt_ref, v_ref, rows_ref, o_ref, bias_ref, s_ref, p_ref, l_ref, *, n_groups):
    @pl.when(pl.program_id(1) == 0)
    def _():
        _na_build_tables(rows_ref, bias_ref)

    r = _NaRefs(q_ref, kt_ref, v_ref, bias_ref, o_ref, n_groups)
    last = n_groups - 1
    _na_edge_group(r, 0, 0)
    _na_edge_group(r, last, 2)

    def load_carry():
        return s_ref[...], p_ref[...], l_ref[...]

    def store_carry(carry):
        s_ref[...], p_ref[...], l_ref[...] = carry

    def stages(first, count, carry):
        for k in range(count):
            carry = _na_stage(r, first + k, carry)
        return carry

    def unrolled_stages(i, c):
        store_carry(stages(2 + NA_UNROLL * i, NA_UNROLL, load_carry()))
        return c

    store_carry((_na_scores(r, 2),) + _na_interior_probs(r, _na_scores(r, 1)))
    n_loops, n_tail = divmod(n_groups - 3, NA_UNROLL)
    lax.fori_loop(0, n_loops, unrolled_stages, 0)
    _, p_prev, l_prev = stages(2 + NA_UNROLL * n_loops, n_tail, load_carry())
    _na_interior_output(r, last - 1, p_prev, l_prev)


def _na(qa, kat, va, rows):
    B, T, _ = qa.shape
    n_pairs = NA_HEADS // 2
    return pl.pallas_call(
        functools.partial(_na_kernel, n_groups=T // NA_GROUP),
        grid=(n_pairs, B),
        in_specs=[
            pl.BlockSpec((None, T, LANES), lambda p, b: (b, 0, p)),
            pl.BlockSpec((None, LANES, T), lambda p, b: (b, p, 0)),
            pl.BlockSpec((None, T, LANES), lambda p, b: (b, 0, p)),
            pl.BlockSpec((2,) + rows.shape[1:], lambda p, b: (p, 0, 0)),
        ],
        out_specs=pl.BlockSpec((None, T, LANES), lambda p, b: (b, 0, p)),
        out_shape=jax.ShapeDtypeStruct((B, T, NA_WIDTH), BF16),
        scratch_shapes=[pltpu.VMEM((2, 3, NA_GROUP, NA_KEYS), F32),
                        pltpu.VMEM((2 * NA_GROUP, NA_KEYS), F32),
                        pltpu.VMEM((2 * NA_GROUP, NA_KEYS), BF16),
                        pltpu.VMEM((2 * NA_GROUP, 1), F32)],
        compiler_params=pltpu.CompilerParams(
            dimension_semantics=("parallel", "arbitrary"), vmem_limit_bytes=VMEM_LIMIT),
        name="natten",
    )(qa, kat, va, rows)


class _GlaDir:
    def __init__(self, reverse, ops, v, dec, o, state, a, contrib, prev):
        self.reverse = reverse
        self.qd, self.ki, self.kd = (ops.at[:, i * GLA_KW:(i + 1) * GLA_KW] for i in range(3))
        self.v, self.dec, self.o, self.state = v, dec, o, state
        self.a, self.contrib, self.prev = a, contrib, prev


def _gla_pair_lanes(h):
    lane = lax.broadcasted_iota(jnp.int32, (GLA_CHUNK, LANES), 1)
    return (lane < GLA_DK) if h % 2 == 0 else (lane >= GLA_DK)


def _gla_intra(d, c):
    C = GLA_CHUNK
    rows = slice(c * C, (c + 1) * C)
    qd, ki = d.qd[rows, :], d.ki[rows, :]
    lane = lax.broadcasted_iota(jnp.int32, qd.shape, 1)
    zeros = jnp.zeros_like(ki)
    i = lax.broadcasted_iota(jnp.int32, (2 * C, LANES), 0) & (C - 1)
    col = lax.broadcasted_iota(jnp.int32, (2 * C, LANES), 1)
    for parity in range(2):
        q_heads = jnp.concatenate(
            [jnp.where((lane >= h * GLA_DK) & (lane < (h + 1) * GLA_DK), qd, jnp.zeros_like(qd))
             for h in range(parity, GLA_HEADS, 2)], axis=0)
        keys = jnp.concatenate([zeros, ki] if parity == 0 else [ki, zeros], axis=0)
        scores = _dot_nt(q_heads, keys)
        j = col - C if parity == 0 else col
        in_chunk = (j >= 0) & (j < C)
        keep = in_chunk & ((j > i) if d.reverse else (j <= i))
        d.a[c, parity] = jnp.where(keep, scores, 0.0).astype(BF16)
    for p in range(GLA_HEADS // 2):
        kd_p = d.kd[rows, p * LANES:(p + 1) * LANES]
        v_p = d.v[rows, 2 * p * GLA_DV:(2 * p + 2) * GLA_DV]
        both = _dot_tn(kd_p, v_p)
        d.contrib[c, 2 * p] = both[:GLA_DK, :GLA_DV]
        d.contrib[c, 2 * p + 1] = both[GLA_DK:, GLA_DV:]


def _gla_scan(d, n_chunks):
    order = range(n_chunks - 1, -1, -1) if d.reverse else range(n_chunks)
    pad = jnp.zeros((LANES - n_chunks, GLA_KW), F32)
    dec_t = jnp.concatenate([d.dec[c] for c in range(n_chunks)] + [pad], axis=0).T
    for h in range(GLA_HEADS):
        st = d.state[h]
        for c in order:
            d.prev[c, h] = st.astype(BF16)
            st = st * dec_t[h * GLA_DK:(h + 1) * GLA_DK, c:c + 1] + d.contrib[c, h]
        d.state[h] = st


def _gla_outputs(d, c):
    C = GLA_CHUNK
    rows = slice(c * C, (c + 1) * C)
    for h in range(GLA_HEADS):
        p, parity = divmod(h, 2)
        qd_p = d.qd[rows, p * LANES:(p + 1) * LANES]
        scores = d.a[c, parity, p * C:(p + 1) * C, :]
        lhs = jnp.where(_gla_pair_lanes(h), qd_p, scores)
        cols = slice(h * GLA_DV, (h + 1) * GLA_DV)
        operands = [d.prev[c, h], d.v[rows, cols]]
        rhs = jnp.concatenate(operands if parity == 0 else operands[::-1], axis=0)
        d.o[rows, cols] = _dot(lhs, rhs).astype(d.o.dtype)


def _gla_kernel(gf_ref, vf_ref, decf_ref, gb_ref, vb_ref, decb_ref,
                of_ref, ob_ref, sf_ref, sb_ref, *scratch, n_chunks):
    @pl.when(pl.program_id(1) == 0)
    def _():
        sf_ref[...] = jnp.zeros_like(sf_ref)
        sb_ref[...] = jnp.zeros_like(sb_ref)

    n = len(scratch) // 2
    dirs = (_GlaDir(False, gf_ref, vf_ref, decf_ref, of_ref, sf_ref, *scratch[:n]),
            _GlaDir(True, gb_ref, vb_ref, decb_ref, ob_ref, sb_ref, *scratch[n:]))

    for c in range(n_chunks):
        for d in dirs:
            _gla_intra(d, c)
    for d in dirs:
        _gla_scan(d, n_chunks)
    for c in range(n_chunks):
        for d in dirs:
            _gla_outputs(d, c)


def _gla(gf, gb, vg, dec, *, tb):
    B, T, vw = vg.shape
    nb = T // tb
    nc = tb // GLA_CHUNK
    kw = GLA_KW
    n_pairs = GLA_HEADS // 2
    fwd = lambda w: pl.BlockSpec((None, tb, w), lambda b, s: (b, s, 0))
    bwd = lambda w: pl.BlockSpec((None, tb, w), lambda b, s: (b, nb - 1 - s, 0))
    dec_fwd = pl.BlockSpec((None, nc, 1, kw), lambda b, s: (b, s, 0, 0))
    dec_bwd = pl.BlockSpec((None, nc, 1, kw), lambda b, s: (b, nb - 1 - s, 0, 1))
    state = pltpu.VMEM((GLA_HEADS, GLA_DK, GLA_DV), F32)
    per_dir = [
        pltpu.VMEM((nc, 2, n_pairs * GLA_CHUNK, 2 * GLA_CHUNK), BF16),
        pltpu.VMEM((nc, GLA_HEADS, GLA_DK, GLA_DV), F32),
        pltpu.VMEM((nc, GLA_HEADS, GLA_DK, GLA_DV), BF16)]
    return pl.pallas_call(
        functools.partial(_gla_kernel, n_chunks=nc),
        grid=(B, nb),
        in_specs=[fwd(3 * kw), fwd(vw), dec_fwd, bwd(3 * kw), bwd(vw), dec_bwd],
        out_specs=[fwd(vw), bwd(vw)],
        out_shape=[jax.ShapeDtypeStruct((B, T, vw), BF16)] * 2,
        scratch_shapes=[state, state] + per_dir * 2,
        compiler_params=pltpu.CompilerParams(
            dimension_semantics=("parallel", "arbitrary"), vmem_limit_bytes=VMEM_LIMIT),
        name="gla",
    )(gf, vg, dec, gb, vg, dec)


MIX_SUBTILES = 4
STAGE_ROWS, STAGE_COLS = 512, 1024


class _WeightStager:
    def __init__(self, tiles, stage, sem):
        self.tiles, self.stage, self.sem, self.done = tiles, stage, sem, 0
        for i in range(min(2, len(tiles))):
            self._copy(i).start()

    def _copy(self, i):
        src, _, r, c = self.tiles[i]
        return pltpu.make_async_copy(src.at[pl.ds(r, STAGE_ROWS), pl.ds(c, STAGE_COLS)],
                                     self.stage.at[i % 2], self.sem.at[i % 2])

    def need(self, count):
        while self.done < count:
            i = self.done
            _, dst, r, c = self.tiles[i]
            self._copy(i).wait()
            dst[r:r + STAGE_ROWS, c:c + STAGE_COLS] = self.stage[i % 2].astype(BF16)
            if i + 2 < len(self.tiles):
                self._copy(i + 2).start()
            self.done += 1


def _mix_ffn_body(x_ref, yna_ref, of_ref, ob_ref, sr_ref, gng_ref, gff_ref, gfin_ref, out_ref,
                  wout_ref, w1_ref, w2_ref, ready, *, final_norm, ff_chunk):
    na_w = yna_ref.shape[-1]
    tiles_per = lambda ref: (ref.shape[0] // STAGE_ROWS) * (ref.shape[1] // STAGE_COLS)
    n_chunks = w1_ref.shape[1] // ff_chunk
    sub = x_ref.shape[0] // MIX_SUBTILES
    row_sets = [slice(i * sub, (i + 1) * sub) for i in range(MIX_SUBTILES)]
    h1s, n2s = [], []
    for rows in row_sets:
        o = of_ref[rows, :].astype(F32) + ob_ref[rows, :].astype(F32)
        parts = [_rms(o[:, h * GLA_DV:(h + 1) * GLA_DV], gng_ref[...]) for h in range(GLA_HEADS)]
        y_gla = (jnp.concatenate(parts, axis=-1) * sr_ref[rows, :].astype(F32)).astype(BF16)
        ready(tiles_per(wout_ref))
        mix = _dot(yna_ref[rows, :], wout_ref[0:na_w, :]) + _dot(y_gla, wout_ref[na_w:, :])
        h1 = x_ref[rows, :] + mix
        h1s.append(h1)
        n2s.append(_rms(h1, gff_ref[...]).astype(BF16))
    for rows, h1, n2 in zip(row_sets, h1s, n2s):
        n_tiles = tiles_per(wout_ref)
        ffn = None
        for f in range(n_chunks):
            n_tiles += tiles_per(w1_ref) // n_chunks
            ready(n_tiles)
            u = _dot(n2, w1_ref[:, f * ff_chunk:(f + 1) * ff_chunk])
            a = jnp.square(jnp.maximum(u, 0.0)).astype(BF16)
            n_tiles += tiles_per(w2_ref) // n_chunks
            ready(n_tiles)
            d = _dot(a, w2_ref[f * ff_chunk:(f + 1) * ff_chunk, :])
            ffn = d if ffn is None else ffn + d
        h2 = h1 + ffn
        out_ref[rows, :] = _rms(h2, gfin_ref[...]) if final_norm else h2


def _mix_ffn_kernel(x_ref, yna_ref, of_ref, ob_ref, sr_ref, gng_ref, wout_hbm, gff_ref,
                    w1_hbm, w2_hbm, gfin_ref, out_ref, wout_ref, w1_ref, w2_ref, stage, sem,
                    *, final_norm, ff_chunk):
    body = functools.partial(
        _mix_ffn_body, x_ref, yna_ref, of_ref, ob_ref, sr_ref, gng_ref, gff_ref, gfin_ref, out_ref,
        wout_ref, w1_ref, w2_ref, final_norm=final_norm, ff_chunk=ff_chunk)
    first = (pl.program_id(0) == 0) & (pl.program_id(1) == 0)

    @pl.when(first)
    def _():
        def tiles(src, dst, rows, cols):
            return [(src, dst, r, c) for r in range(rows.start, rows.stop, STAGE_ROWS)
                    for c in range(cols.start, cols.stop, STAGE_COLS)]

        order = tiles(wout_hbm, wout_ref, slice(0, wout_ref.shape[0]), slice(0, wout_ref.shape[1]))
        for f in range(w1_ref.shape[1] // ff_chunk):
            chunk = slice(f * ff_chunk, (f + 1) * ff_chunk)
            order += tiles(w1_hbm, w1_ref, slice(0, w1_ref.shape[0]), chunk)
            order += tiles(w2_hbm, w2_ref, chunk, slice(0, w2_ref.shape[1]))
        stager = _WeightStager(order, stage, sem)
        body(stager.need)
        stager.need(len(order))

    @pl.when(jnp.logical_not(first))
    def _():
        body(lambda n: None)


def _mix_ffn(x, yna, of, ob, sr, gng, wout, gff, w1, w2, gfin, *, tm, final_norm):
    B, T, D = x.shape
    tok = lambda w: pl.BlockSpec((None, tm, w), lambda b, i: (b, i, 0))
    const = lambda a: pl.BlockSpec(a.shape, lambda b, i: (0,) * a.ndim,
                                   pipeline_mode=pl.Buffered(1))
    hbm = pl.BlockSpec(memory_space=pl.ANY)
    for w in (wout, w1, w2):
        assert w.dtype == F32 and w.shape[0] % STAGE_ROWS == 0 and w.shape[1] % STAGE_COLS == 0
    assert w1.shape[1] % 1024 == 0 and 1024 % STAGE_ROWS == 0 and 1024 % STAGE_COLS == 0
    return pl.pallas_call(
        functools.partial(_mix_ffn_kernel, final_norm=final_norm, ff_chunk=1024),
        grid=(B, T // tm),
        in_specs=[tok(D), tok(yna.shape[-1]), tok(of.shape[-1]), tok(ob.shape[-1]),
                  tok(sr.shape[-1]), const(gng), hbm, const(gff), hbm, hbm, const(gfin)],
        out_specs=tok(D),
        out_shape=jax.ShapeDtypeStruct((B, T, D), F32),
        scratch_shapes=[pltpu.VMEM(w.shape, BF16) for w in (wout, w1, w2)] + [
            pltpu.VMEM((2, STAGE_ROWS, STAGE_COLS), F32), pltpu.SemaphoreType.DMA((2,))],
        compiler_params=pltpu.CompilerParams(
            dimension_semantics=("arbitrary", "arbitrary"), vmem_limit_bytes=VMEM_LIMIT),
        name="mix_ffn",
    )(x, yna, of, ob, sr, gng, wout, gff, w1, w2, gfin)


def kernel(x, ln_mix_g, w_in, na_rpb, gla_gate_up_fwd, gla_gate_bias_fwd, gla_gate_up_bwd,
           gla_gate_bias_bwd, gla_norm_g, w_out, ln_ff_g, w_ff1, w_ff2, ln_final_g):
    B, T, D = x.shape
    depth = w_in.shape[0]
    assert T % TOKEN_TILE == 0 and T % MIX_TILE == 0 and T // NA_GROUP >= 4
    gla_block = GLA_BLOCK if T % GLA_BLOCK == 0 else TOKEN_TILE
    row = lambda v: v.reshape(1, -1).astype(F32)
    zeros = jnp.zeros((GLA_GATE_RANK, GLA_KW), F32)
    h = x
    for l in range(depth):
        gu = jnp.concatenate([
            jnp.concatenate([gla_gate_up_fwd[l], zeros], axis=1),
            jnp.concatenate([zeros, gla_gate_up_bwd[l]], axis=1)], axis=0).astype(BF16)
        gb = jnp.concatenate([gla_gate_bias_fwd[l], gla_gate_bias_bwd[l]]).reshape(1, -1)
        qa, kat, va, vg, sr, gla_f, gla_b, dec = _inproj(
            h, row(ln_mix_g[l]), w_in[l].T, gu, gb.astype(F32), tm=TOKEN_TILE)
        y_na = _na(qa, kat, va, _na_bias_rows(na_rpb[l]))
        o_f, o_b = _gla(gla_f, gla_b, vg, dec, tb=gla_block)
        h = _mix_ffn(h, y_na, o_f, o_b, sr, row(gla_norm_g[l]), w_out[l], row(ln_ff_g[l]),
                     w_ff1[l], w_ff2[l], row(ln_final_g), tm=MIX_TILE,
                     final_norm=(l == depth - 1))
    return h
```

```python
import functools

import jax
import jax.numpy as jnp
from jax import lax
from jax.experimental import pallas as pl
from jax.experimental.pallas import tpu as pltpu

F32 = jnp.float32
BF16 = jnp.bfloat16

EPS = 1e-6
GRID_W = 64
NA_HEADS = 8
NA_HEAD_DIM = 64
NA_KH = 8
NA_KW = 16
GLA_HEADS = 4
GLA_DK = 64
GLA_DV = 128
GLA_GATE_RANK = 16
GLA_GATE_NORM = 16.0
GLA_CHUNK = 64

NA_WIDTH = NA_HEADS * NA_HEAD_DIM
GLA_KW = GLA_HEADS * GLA_DK
GLA_VW = GLA_HEADS * GLA_DV
COLS, _start = {}, 0
for _name, _width in (("qa", NA_WIDTH), ("ka", NA_WIDTH), ("va", NA_WIDTH), ("qg", GLA_KW),
                      ("kg", GLA_KW), ("vg", GLA_VW), ("rg", GLA_VW), ("z", 2 * GLA_GATE_RANK)):
    COLS[_name] = slice(_start, _start + _width)
    _start += _width
D_IN = _start

LANES = 128
TOKEN_TILE = 512
GLA_BLOCK = 1024
NA_GROUP_ROWS = 4
NA_GROUP = NA_GROUP_ROWS * GRID_W
NA_KEY_ROWS = 12
NA_KEYS = NA_KEY_ROWS * GRID_W
NA_UNROLL = 4
LOG2E = 1.4426950408889634
VMEM_LIMIT = 56 * 1024 * 1024


def _dot(a, b):
    return jnp.dot(a, b, preferred_element_type=F32)


def _dot_nt(a, b):
    return lax.dot_general(a, b, (((1,), (1,)), ((), ())), preferred_element_type=F32)


def _dot_tn(a, b):
    return lax.dot_general(a, b, (((0,), (0,)), ((), ())), preferred_element_type=F32)


def _rms(x, g):
    return x * lax.rsqrt(jnp.mean(x * x, axis=-1, keepdims=True) + EPS) * g


def _segmented_cumsum(x, reverse):
    n = x.shape[0]
    pos = lax.broadcasted_iota(jnp.int32, x.shape, 0) & (GLA_CHUNK - 1)
    step = 1
    while step < GLA_CHUNK:
        if reverse:
            shifted, ok = pltpu.roll(x, n - step, 0), pos < GLA_CHUNK - step
        else:
            shifted, ok = pltpu.roll(x, step, 0), pos >= step
        x = x + jnp.where(ok, shifted, 0.0)
        step *= 2
    return x


def _gla_cumdecay(log_a, reverse):
    b = _segmented_cumsum(log_a, reverse)
    b3 = b.reshape(-1, GLA_CHUNK, b.shape[-1])
    last = 0 if reverse else GLA_CHUNK - 1
    return b, b3[:, last:last + 1, :]


def _gla_operands(q, k, b, b_last):
    b3 = b.reshape(-1, GLA_CHUNK, b.shape[-1])
    k_dec = (k.reshape(b3.shape) * jnp.exp2(b_last - b3)).reshape(b.shape)
    ops = jnp.concatenate([q * jnp.exp2(b), k * jnp.exp2(-b), k_dec], axis=-1)
    return ops.astype(BF16), jnp.exp2(b_last)


def _cast_weight_rows(wt_ref, wb_ref, names):
    @pl.when((pl.program_id(0) == 0) & (pl.program_id(1) == 0))
    def _():
        for name in names:
            wb_ref[COLS[name], :] = wt_ref[COLS[name], :].astype(BF16)


def _inproj_dense_kernel(x_ref, g_ref, wt_ref, qa_ref, kat_ref, va_ref, vg_ref, sr_ref, wb_ref):
    _cast_weight_rows(wt_ref, wb_ref, ("qa", "ka", "va", "vg", "rg"))
    n = _rms(x_ref[...], g_ref[...]).astype(BF16)
    proj = lambda name: _dot_nt(n, wb_ref[COLS[name], :])
    qa_ref[...] = (proj("qa") * (NA_HEAD_DIM ** -0.5 * LOG2E)).astype(BF16)
    kat_ref[...] = _dot_nt(wb_ref[COLS["ka"], :], n).astype(BF16)
    va_ref[...] = proj("va").astype(BF16)
    vg_ref[...] = proj("vg").astype(BF16)
    r = proj("rg")
    sr_ref[...] = (r * jax.nn.sigmoid(r)).astype(BF16)


def _inproj_gla_kernel(x_ref, g_ref, wt_ref, gu_ref, gb_ref, gf_ref, gbw_ref, dec_ref, wb_ref):
    _cast_weight_rows(wt_ref, wb_ref, ("qg", "kg", "z"))
    n = _rms(x_ref[...], g_ref[...]).astype(BF16)
    proj = lambda name: _dot_nt(n, wb_ref[COLS[name], :])

    def log_decay(cols):
        pre = _dot(z, gu_ref[:, cols]) + gb_ref[:, cols]
        softplus2 = jnp.log2(1.0 + jnp.exp2(jnp.abs(pre) * -LOG2E))
        return jnp.minimum(pre * (LOG2E / GLA_GATE_NORM), 0.0) - softplus2 * (1.0 / GLA_GATE_NORM)

    z = proj("z").astype(BF16)
    qg = proj("qg") * (GLA_DK ** -0.5)
    kg = proj("kg")
    b_f, last_f = _gla_cumdecay(log_decay(slice(0, GLA_KW)), False)
    gf_ref[...], dec_f = _gla_operands(qg, kg, b_f, last_f)
    b_b, last_b = _gla_cumdecay(log_decay(slice(GLA_KW, 2 * GLA_KW)), True)
    gbw_ref[...], dec_b = _gla_operands(qg, kg, b_b, last_b)
    dec_ref[...] = jnp.concatenate([dec_f, dec_b], axis=-1)


def _inproj(x, g, wt, gu, gb, *, tm):
    B, T, D = x.shape
    assert wt.shape == (D_IN, D)
    nc = tm // GLA_CHUNK
    tok = lambda w: pl.BlockSpec((None, tm, w), lambda b, i: (b, i, 0))
    const = lambda a: pl.BlockSpec(a.shape, lambda b, i: (0,) * a.ndim,
                                   pipeline_mode=pl.Buffered(1))
    tok_out = lambda w: (jax.ShapeDtypeStruct((B, T, w), BF16), tok(w))

    def call(body, name, inputs, outs):
        return pl.pallas_call(
            body,
            grid=(B, T // tm),
            in_specs=[tok(D)] + [const(a) for a in inputs[1:]],
            out_specs=[s for _, s in outs],
            out_shape=[s for s, _ in outs],
            scratch_shapes=[pltpu.VMEM(wt.shape, BF16)],
            compiler_params=pltpu.CompilerParams(
                dimension_semantics=("arbitrary", "arbitrary"), vmem_limit_bytes=VMEM_LIMIT),
            name=name,
        )(*inputs)

    dense = call(_inproj_dense_kernel, "inproj_dense", (x, g, wt), [
        tok_out(NA_WIDTH),
        (jax.ShapeDtypeStruct((B, NA_WIDTH, T), BF16),
         pl.BlockSpec((None, NA_WIDTH, tm), lambda b, i: (b, 0, i))),
        tok_out(NA_WIDTH),
        tok_out(GLA_VW),
        tok_out(GLA_VW),
    ])
    gla_ops = call(_inproj_gla_kernel, "inproj_gla", (x, g, wt, gu, gb), [
        tok_out(3 * GLA_KW),
        tok_out(3 * GLA_KW),
        (jax.ShapeDtypeStruct((B, T // GLA_CHUNK, 1, 2 * GLA_KW), F32),
         pl.BlockSpec((None, nc, 1, 2 * GLA_KW), lambda b, i: (b, i, 0, 0))),
    ])
    return tuple(dense) + tuple(gla_ops)


def _na_bias_rows(rpb):
    n_dc = 2 * NA_KW - 1
    padded = jnp.pad(rpb.astype(F32), ((0, 0), (4, 5), (0, GRID_W - n_dc)))
    return jnp.concatenate([padded[:, :-1], padded[:, 1:]], axis=-1)


def _na_tiles(table, g):
    i_start = (0, g, NA_KEY_ROWS - NA_KH)[table]
    return range(i_start // 2, (i_start + NA_KH + 1) // 2)


def _na_build_tables(rows_ref, tab_ref):
    qc = lax.broadcasted_iota(jnp.int32, (GRID_W, LANES), 0)
    lane = lax.broadcasted_iota(jnp.int32, (GRID_W, LANES), 1)
    kc = lane & (GRID_W - 1)
    col_start = jnp.clip(qc - NA_KW // 2, 0, GRID_W - NA_KW)
    in_win = (kc >= col_start) & (kc < col_start + NA_KW)
    valid = {(True, True): in_win,
             (True, False): in_win & (lane < GRID_W),
             (False, True): in_win & (lane >= GRID_W)}
    row_offset = (0, -(NA_KH // 2), -NA_KH)
    for hh in range(2):
        for t in range(3):
            for g in range(NA_GROUP_ROWS):
                i_start = (0, g, NA_KEY_ROWS - NA_KH)[t]
                for m in _na_tiles(t, g):
                    halves = tuple(i_start <= i < i_start + NA_KH for i in (2 * m, 2 * m + 1))
                    a = 2 * m - g + row_offset[t] + NA_KH - 1 + 4
                    src = jnp.broadcast_to(rows_ref[hh, a:a + 1, :], (GRID_W, LANES))
                    rot = pltpu.roll(src, LANES - (NA_KW - 1), 1, stride=1, stride_axis=0)
                    tab_ref[hh, t, g * GRID_W:(g + 1) * GRID_W, m * LANES:(m + 1) * LANES] = (
                        jnp.where(valid[halves], rot * LOG2E, -jnp.inf))


class _NaRefs:
    def __init__(self, q, kt, v, bias, o, n_groups):
        self.q, self.kt, self.v, self.bias, self.o, self.n_groups = q, kt, v, bias, o, n_groups


def _na_rows(u):
    return pl.ds(pl.multiple_of(u * NA_GROUP, NA_GROUP), NA_GROUP)


def _na_key0(r, u):
    return pl.multiple_of(jnp.clip(u - 1, 0, r.n_groups - 3) * NA_GROUP, NA_GROUP)


def _na_scores(r, u):
    qq = r.q[_na_rows(u), :]
    first_head = lax.broadcasted_iota(jnp.int32, qq.shape, 1) < NA_HEAD_DIM
    zero = jnp.zeros_like(qq)
    q_heads = jnp.concatenate([jnp.where(first_head, qq, zero),
                               jnp.where(first_head, zero, qq)], axis=0)
    return _dot(q_heads, r.kt[:, pl.ds(_na_key0(r, u), NA_KEYS)])


def _na_softmax(r, s, table):
    p_tiles, inv_l = [], []
    for hh in range(2):
        for g in range(NA_GROUP_ROWS):
            rows = slice(hh * NA_GROUP + g * GRID_W, hh * NA_GROUP + (g + 1) * GRID_W)
            sb = [s[rows, c * LANES:(c + 1) * LANES]
                  + r.bias[hh, table, g * GRID_W:(g + 1) * GRID_W, c * LANES:(c + 1) * LANES]
                  for c in _na_tiles(table, g)]
            m = jnp.max(functools.reduce(jnp.maximum, sb), axis=-1, keepdims=True)
            p = [jnp.exp2(x - m) for x in sb]
            inv_l.append(1.0 / jnp.sum(functools.reduce(jnp.add, p), axis=-1, keepdims=True))
            p_tiles.append([x.astype(BF16) for x in p])
    return p_tiles, inv_l


def _na_store_output(r, u, o):
    first_head = lax.broadcasted_iota(jnp.int32, (NA_GROUP, LANES), 1) < NA_HEAD_DIM
    r.o[_na_rows(u), :] = jnp.where(first_head, o[:NA_GROUP], o[NA_GROUP:]).astype(r.o.dtype)


def _na_edge_group(r, u, table):
    tiles = _na_tiles(table, 0)
    p_tiles, inv_l = _na_softmax(r, _na_scores(r, u), table)
    p = jnp.concatenate([jnp.concatenate(t, axis=1) for t in p_tiles], axis=0)
    keys = pl.ds(_na_key0(r, u) + tiles.start * LANES, len(tiles) * LANES)
    _na_store_output(r, u, _dot(p, r.v[keys, :]) * jnp.concatenate(inv_l, axis=0))


def _na_interior_probs(r, s):
    p_tiles, inv_l = _na_softmax(r, s, 1)
    zero_tile = jnp.zeros((GRID_W, LANES), BF16)
    p_rows = []
    for block, tiles in enumerate(p_tiles):
        valid = _na_tiles(1, block % NA_GROUP_ROWS)
        p_rows.append(jnp.concatenate(
            [zero_tile] * valid.start + tiles + [zero_tile] * (NA_KEYS // LANES - valid.stop), axis=1))
    return jnp.concatenate(p_rows, axis=0), jnp.concatenate(inv_l, axis=0)


def _na_interior_output(r, u, p, inv_l):
    _na_store_output(r, u, _dot(p, r.v[pl.ds(_na_key0(r, u), NA_KEYS), :]) * inv_l)


def _na_stage(r, u, carry):
    s_cur, p_prev, l_prev = carry
    s_next = _na_scores(r, jnp.minimum(u + 1, r.n_groups - 2))
    p_cur, l_cur = _na_interior_probs(r, s_cur)
    _na_interior_output(r, u - 1, p_prev, l_prev)
    return s_next, p_cur, l_cur


def _na_kernel(q_ref, kt_ref, v_ref, rows_ref, o_ref, bias_ref, s_ref, p_ref, l_ref, *, n_groups):
    @pl.when(pl.program_id(1) == 0)
    def _():
        _na_build_tables(rows_ref, bias_ref)

    r = _NaRefs(q_ref, kt_ref, v_ref, bias_ref, o_ref, n_groups)
    last = n_groups - 1
    _na_edge_group(r, 0, 0)
    _na_edge_group(r, last, 2)

    def load_carry():
        return s_ref[...], p_ref[...], l_ref[...]

    def store_carry(carry):
        s_ref[...], p_ref[...], l_ref[...] = carry

    def stages(first, count, carry):
        for k in range(count):
            carry = _na_stage(r, first + k, carry)
        return carry

    def unrolled_stages(i, c):
        store_carry(stages(2 + NA_UNROLL * i, NA_UNROLL, load_carry()))
        return c

    store_carry((_na_scores(r, 2),) + _na_interior_probs(r, _na_scores(r, 1)))
    n_loops, n_tail = divmod(n_groups - 3, NA_UNROLL)
    lax.fori_loop(0, n_loops, unrolled_stages, 0)
    _, p_prev, l_prev = stages(2 + NA_UNROLL * n_loops, n_tail, load_carry())
    _na_interior_output(r, last - 1, p_prev, l_prev)


def _na(qa, kat, va, rows):
    B, T, _ = qa.shape
    n_pairs = NA_HEADS // 2
    return pl.pallas_call(
        functools.partial(_na_kernel, n_groups=T // NA_GROUP),
        grid=(n_pairs, B),
        in_specs=[
            pl.BlockSpec((None, T, LANES), lambda p, b: (b, 0, p)),
            pl.BlockSpec((None, LANES, T), lambda p, b: (b, p, 0)),
            pl.BlockSpec((None, T, LANES), lambda p, b: (b, 0, p)),
            pl.BlockSpec((2,) + rows.shape[1:], lambda p, b: (p, 0, 0)),
        ],
        out_specs=pl.BlockSpec((None, T, LANES), lambda p, b: (b, 0, p)),
        out_shape=jax.ShapeDtypeStruct((B, T, NA_WIDTH), BF16),
        scratch_shapes=[pltpu.VMEM((2, 3, NA_GROUP, NA_KEYS), F32),
                        pltpu.VMEM((2 * NA_GROUP, NA_KEYS), F32),
                        pltpu.VMEM((2 * NA_GROUP, NA_KEYS), BF16),
                        pltpu.VMEM((2 * NA_GROUP, 1), F32)],
        compiler_params=pltpu.CompilerParams(
            dimension_semantics=("parallel", "arbitrary"), vmem_limit_bytes=VMEM_LIMIT),
        name="natten",
    )(qa, kat, va, rows)


class _GlaDir:
    def __init__(self, reverse, ops, v, dec, o, state, a, contrib, prev):
        self.reverse = reverse
        self.qd, self.ki, self.kd = (ops.at[:, i * GLA_KW:(i + 1) * GLA_KW] for i in range(3))
        self.v, self.dec, self.o, self.state = v, dec, o, state
        self.a, self.contrib, self.prev = a, contrib, prev


def _gla_pair_lanes(h):
    lane = lax.broadcasted_iota(jnp.int32, (GLA_CHUNK, LANES), 1)
    return (lane < GLA_DK) if h % 2 == 0 else (lane >= GLA_DK)


def _gla_intra(d, c):
    C = GLA_CHUNK
    rows = slice(c * C, (c + 1) * C)
    qd, ki = d.qd[rows, :], d.ki[rows, :]
    lane = lax.broadcasted_iota(jnp.int32, qd.shape, 1)
    zeros = jnp.zeros_like(ki)
    i = lax.broadcasted_iota(jnp.int32, (2 * C, LANES), 0) & (C - 1)
    col = lax.broadcasted_iota(jnp.int32, (2 * C, LANES), 1)
    for parity in range(2):
        q_heads = jnp.concatenate(
            [jnp.where((lane >= h * GLA_DK) & (lane < (h + 1) * GLA_DK), qd, jnp.zeros_like(qd))
             for h in range(parity, GLA_HEADS, 2)], axis=0)
        keys = jnp.concatenate([zeros, ki] if parity == 0 else [ki, zeros], axis=0)
        scores = _dot_nt(q_heads, keys)
        j = col - C if parity == 0 else col
        in_chunk = (j >= 0) & (j < C)
        keep = in_chunk & ((j > i) if d.reverse else (j <= i))
        d.a[c, parity] = jnp.where(keep, scores, 0.0).astype(BF16)
    for p in range(GLA_HEADS // 2):
        kd_p = d.kd[rows, p * LANES:(p + 1) * LANES]
        v_p = d.v[rows, 2 * p * GLA_DV:(2 * p + 2) * GLA_DV]
        both = _dot_tn(kd_p, v_p)
        d.contrib[c, 2 * p] = both[:GLA_DK, :GLA_DV]
        d.contrib[c, 2 * p + 1] = both[GLA_DK:, GLA_DV:]


def _gla_scan(d, n_chunks):
    order = range(n_chunks - 1, -1, -1) if d.reverse else range(n_chunks)
    pad = jnp.zeros((LANES - n_chunks, GLA_KW), F32)
    dec_t = jnp.concatenate([d.dec[c] for c in range(n_chunks)] + [pad], axis=0).T
    for h in range(GLA_HEADS):
        st = d.state[h]
        for c in order:
            d.prev[c, h] = st.astype(BF16)
            st = st * dec_t[h * GLA_DK:(h + 1) * GLA_DK, c:c + 1] + d.contrib[c, h]
        d.state[h] = st


def _gla_outputs(d, c):
    C = GLA_CHUNK
    rows = slice(c * C, (c + 1) * C)
    for h in range(GLA_HEADS):
        p, parity = divmod(h, 2)
        qd_p = d.qd[rows, p * LANES:(p + 1) * LANES]
        scores = d.a[c, parity, p * C:(p + 1) * C, :]
        lhs = jnp.where(_gla_pair_lanes(h), qd_p, scores)
        cols = slice(h * GLA_DV, (h + 1) * GLA_DV)
        operands = [d.prev[c, h], d.v[rows, cols]]
        rhs = jnp.concatenate(operands if parity == 0 else operands[::-1], axis=0)
        d.o[rows, cols] = _dot(lhs, rhs).astype(d.o.dtype)


def _gla_kernel(gf_ref, vf_ref, decf_ref, gb_ref, vb_ref, decb_ref,
                of_ref, ob_ref, sf_ref, sb_ref, *scratch, n_chunks):
    @pl.when(pl.program_id(1) == 0)
    def _():
        sf_ref[...] = jnp.zeros_like(sf_ref)
        sb_ref[...] = jnp.zeros_like(sb_ref)

    n = len(scratch) // 2
    dirs = (_GlaDir(False, gf_ref, vf_ref, decf_ref, of_ref, sf_ref, *scratch[:n]),
            _GlaDir(True, gb_ref, vb_ref, decb_ref, ob_ref, sb_ref, *scratch[n:]))

    for c in range(n_chunks):
        for d in dirs:
            _gla_intra(d, c)
    for d in dirs:
        _gla_scan(d, n_chunks)
    for c in range(n_chunks):
        for d in dirs:
            _gla_outputs(d, c)


def _gla(gf, gb, vg, dec, *, tb):
    B, T, vw = vg.shape
    nb = T // tb
    nc = tb // GLA_CHUNK
    kw = GLA_KW
    n_pairs = GLA_HEADS // 2
    fwd = lambda w: pl.BlockSpec((None, tb, w), lambda b, s: (b, s, 0))
    bwd = lambda w: pl.BlockSpec((None, tb, w), lambda b, s: (b, nb - 1 - s, 0))
    dec_fwd = pl.BlockSpec((None, nc, 1, kw), lambda b, s: (b, s, 0, 0))
    dec_bwd = pl.BlockSpec((None, nc, 1, kw), lambda b, s: (b, nb - 1 - s, 0, 1))
    state = pltpu.VMEM((GLA_HEADS, GLA_DK, GLA_DV), F32)
    per_dir = [
        pltpu.VMEM((nc, 2, n_pairs * GLA_CHUNK, 2 * GLA_CHUNK), BF16),
        pltpu.VMEM((nc, GLA_HEADS, GLA_DK, GLA_DV), F32),
        pltpu.VMEM((nc, GLA_HEADS, GLA_DK, GLA_DV), BF16)]
    return pl.pallas_call(
        functools.partial(_gla_kernel, n_chunks=nc),
        grid=(B, nb),
        in_specs=[fwd(3 * kw), fwd(vw), dec_fwd, bwd(3 * kw), bwd(vw), dec_bwd],
        out_specs=[fwd(vw), bwd(vw)],
        out_shape=[jax.ShapeDtypeStruct((B, T, vw), BF16)] * 2,
        scratch_shapes=[state, state] + per_dir * 2,
        compiler_params=pltpu.CompilerParams(
            dimension_semantics=("parallel", "arbitrary"), vmem_limit_bytes=VMEM_LIMIT),
        name="gla",
    )(gf, vg, dec, gb, vg, dec)


MIX_SUBTILES = 2
STAGE_ROWS, STAGE_COLS = 512, 1024


class _WeightStager:
    def __init__(self, tiles, stage, sem):
        self.tiles, self.stage, self.sem, self.done = tiles, stage, sem, 0
        for i in range(min(2, len(tiles))):
            self._copy(i).start()

    def _copy(self, i):
        src, _, r, c = self.tiles[i]
        return pltpu.make_async_copy(src.at[pl.ds(r, STAGE_ROWS), pl.ds(c, STAGE_COLS)],
                                     self.stage.at[i % 2], self.sem.at[i % 2])

    def need(self, count):
        while self.done < count:
            i = self.done
            _, dst, r, c = self.tiles[i]
            self._copy(i).wait()
            dst[r:r + STAGE_ROWS, c:c + STAGE_COLS] = self.stage[i % 2].astype(BF16)
            if i + 2 < len(self.tiles):
                self._copy(i + 2).start()
            self.done += 1


def _mix_ffn_body(x_ref, yna_ref, of_ref, ob_ref, sr_ref, gng_ref, gff_ref, gfin_ref, out_ref,
                  wout_ref, w1_ref, w2_ref, ready, *, final_norm, ff_chunk):
    na_w = yna_ref.shape[-1]
    tiles_per = lambda ref: (ref.shape[0] // STAGE_ROWS) * (ref.shape[1] // STAGE_COLS)
    n_chunks = w1_ref.shape[1] // ff_chunk
    sub = x_ref.shape[0] // MIX_SUBTILES
    row_sets = [slice(i * sub, (i + 1) * sub) for i in range(MIX_SUBTILES)]
    h1s, n2s = [], []
    for rows in row_sets:
        o = of_ref[rows, :].astype(F32) + ob_ref[rows, :].astype(F32)
        parts = [_rms(o[:, h * GLA_DV:(h + 1) * GLA_DV], gng_ref[...]) for h in range(GLA_HEADS)]
        y_gla = (jnp.concatenate(parts, axis=-1) * sr_ref[rows, :].astype(F32)).astype(BF16)
        ready(tiles_per(wout_ref))
        mix = _dot(yna_ref[rows, :], wout_ref[0:na_w, :]) + _dot(y_gla, wout_ref[na_w:, :])
        h1 = x_ref[rows, :] + mix
        h1s.append(h1)
        n2s.append(_rms(h1, gff_ref[...]).astype(BF16))
    for rows, h1, n2 in zip(row_sets, h1s, n2s):
        n_tiles = tiles_per(wout_ref)
        ffn = None
        for f in range(n_chunks):
            n_tiles += tiles_per(w1_ref) // n_chunks
            ready(n_tiles)
            u = _dot(n2, w1_ref[:, f * ff_chunk:(f + 1) * ff_chunk])
            a = jnp.square(jnp.maximum(u, 0.0)).astype(BF16)
            n_tiles += tiles_per(w2_ref) // n_chunks
            ready(n_tiles)
            d = _dot(a, w2_ref[f * ff_chunk:(f + 1) * ff_chunk, :])
            ffn = d if ffn is None else ffn + d
        h2 = h1 + ffn
        out_ref[rows, :] = _rms(h2, gfin_ref[...]) if final_norm else h2


def _mix_ffn_kernel(x_ref, yna_ref, of_ref, ob_ref, sr_ref, gng_ref, wout_hbm, gff_ref,
                    w1_hbm, w2_hbm, gfin_ref, out_ref, wout_ref, w1_ref, w2_ref, stage, sem,
                    *, final_norm, ff_chunk):
    body = functools.partial(
        _mix_ffn_body, x_ref, yna_ref, of_ref, ob_ref, sr_ref, gng_ref, gff_ref, gfin_ref, out_ref,
        wout_ref, w1_ref, w2_ref, final_norm=final_norm, ff_chunk=ff_chunk)
    first = (pl.program_id(0) == 0) & (pl.program_id(1) == 0)

    @pl.when(first)
    def _():
        def tiles(src, dst, rows, cols):
            return [(src, dst, r, c) for r in range(rows.start, rows.stop, STAGE_ROWS)
                    for c in range(cols.start, cols.stop, STAGE_COLS)]

        order = tiles(wout_hbm, wout_ref, slice(0, wout_ref.shape[0]), slice(0, wout_ref.shape[1]))
        for f in range(w1_ref.shape[1] // ff_chunk):
            chunk = slice(f * ff_chunk, (f + 1) * ff_chunk)
            order += tiles(w1_hbm, w1_ref, slice(0, w1_ref.shape[0]), chunk)
            order += tiles(w2_hbm, w2_ref, chunk, slice(0, w2_ref.shape[1]))
        stager = _WeightStager(order, stage, sem)
        body(stager.need)
        stager.need(len(order))

    @pl.when(jnp.logical_not(first))
    def _():
        body(lambda n: None)


def _mix_ffn(x, yna, of, ob, sr, gng, wout, gff, w1, w2, gfin, *, tm, final_norm):
    B, T, D = x.shape
    tok = lambda w: pl.BlockSpec((None, tm, w), lambda b, i: (b, i, 0))
    const = lambda a: pl.BlockSpec(a.shape, lambda b, i: (0,) * a.ndim,
                                   pipeline_mode=pl.Buffered(1))
    hbm = pl.BlockSpec(memory_space=pl.ANY)
    for w in (wout, w1, w2):
        assert w.dtype == F32 and w.shape[0] % STAGE_ROWS == 0 and w.shape[1] % STAGE_COLS == 0
    assert w1.shape[1] % 1024 == 0 and 1024 % STAGE_ROWS == 0 and 1024 % STAGE_COLS == 0
    return pl.pallas_call(
        functools.partial(_mix_ffn_kernel, final_norm=final_norm, ff_chunk=1024),
        grid=(B, T // tm),
        in_specs=[tok(D), tok(yna.shape[-1]), tok(of.shape[-1]), tok(ob.shape[-1]),
                  tok(sr.shape[-1]), const(gng), hbm, const(gff), hbm, hbm, const(gfin)],
        out_specs=tok(D),
        out_shape=jax.ShapeDtypeStruct((B, T, D), F32),
        scratch_shapes=[pltpu.VMEM(w.shape, BF16) for w in (wout, w1, w2)] + [
            pltpu.VMEM((2, STAGE_ROWS, STAGE_COLS), F32), pltpu.SemaphoreType.DMA((2,))],
        compiler_params=pltpu.CompilerParams(
            dimension_semantics=("arbitrary", "arbitrary"), vmem_limit_bytes=VMEM_LIMIT),
        name="mix_ffn",
    )(x, yna, of, ob, sr, gng, wout, gff, w1, w2, gfin)


def kernel(x, ln_mix_g, w_in, na_rpb, gla_gate_up_fwd, gla_gate_bias_fwd, gla_gate_up_bwd,
           gla_gate_bias_bwd, gla_norm_g, w_out, ln_ff_g, w_ff1, w_ff2, ln_final_g):
    B, T, D = x.shape
    depth = w_in.shape[0]
    assert T % TOKEN_TILE == 0 and T // NA_GROUP >= 4
    gla_block = GLA_BLOCK if T % GLA_BLOCK == 0 else TOKEN_TILE
    row = lambda v: v.reshape(1, -1).astype(F32)
    zeros = jnp.zeros((GLA_GATE_RANK, GLA_KW), F32)
    h = x
    for l in range(depth):
        gu = jnp.concatenate([
            jnp.concatenate([gla_gate_up_fwd[l], zeros], axis=1),
            jnp.concatenate([zeros, gla_gate_up_bwd[l]], axis=1)], axis=0).astype(BF16)
        gb = jnp.concatenate([gla_gate_bias_fwd[l], gla_gate_bias_bwd[l]]).reshape(1, -1)
        qa, kat, va, vg, sr, gla_f, gla_b, dec = _inproj(
            h, row(ln_mix_g[l]), w_in[l].T, gu, gb.astype(F32), tm=TOKEN_TILE)
        y_na = _na(qa, kat, va, _na_bias_rows(na_rpb[l]))
        o_f, o_b = _gla(gla_f, gla_b, vg, dec, tb=gla_block)
        h = _mix_ffn(h, y_na, o_f, o_b, sr, row(gla_norm_g[l]), w_out[l], row(ln_ff_g[l]),
                     w_ff1[l], w_ff2[l], row(ln_final_g), tm=TOKEN_TILE,
                     final_norm=(l == depth - 1))
    return h
```
